```python
import math
import jax, jax.numpy as jnp
from jax import lax
import numpy as np

D_MODEL = 1024
BATCH = 1
SEQ = 16384
DEPTH = 2
DEC_BATCH = 4
DEC_SEQ = 4096
PAST_LEN = 128

GRID_W = 64
HEAD_DIM = 64
N_Q_HEADS = D_MODEL // 128
N_KV_HEADS = N_Q_HEADS // 4
Q_PER_KV = N_Q_HEADS // N_KV_HEADS
ATTN_DIM = N_Q_HEADS * HEAD_DIM
KV_DIM = N_KV_HEADS * HEAD_DIM
ROPE_THETA = 10000.0
Q_BLOCK = 128
POOL_WINDOWS = (2, 4, 8, 16)
POOL_DIM = D_MODEL // 4
POOL_GROUP = POOL_DIM // len(POOL_WINDOWS)
SSM_DIM = D_MODEL // 4
SSM_P = 16
SSM_G = SSM_DIM // SSM_P
SSM_N = 64
N_BRANCH = 3
SPLIT_POINTS = (ATTN_DIM, ATTN_DIM + KV_DIM, ATTN_DIM + 2 * KV_DIM, ATTN_DIM + 2 * KV_DIM + POOL_DIM, ATTN_DIM + 2 * KV_DIM + POOL_DIM + SSM_DIM)
IN_DIM = SPLIT_POINTS[-1] + N_BRANCH * D_MODEL
N_EXPERTS = 256
TOP_K = 8
N_EXPERT_GROUPS = 8
TOPK_GROUPS = 4
D_EXPERT = D_MODEL // 4
D_SHARED = D_MODEL // 4
ROUTED_SCALE = 2.5
MOE_BLOCK = 128
N_ADA = 6
EPS = 1e-6

kernel_name = 'hybrid_bidir_encoder_two_groups'


def _rmsnorm(x, g):
    xf = x.astype(jnp.float32)
    y = xf * lax.rsqrt(jnp.mean(xf * xf, axis=-1, keepdims=True) + EPS)
    return (y * g.astype(jnp.float32)).astype(x.dtype)


def _axial_rope_tables(seq):
    rows = seq // GRID_W
    row = jnp.repeat(jnp.arange(rows, dtype=jnp.float32), GRID_W)
    col = jnp.tile(jnp.arange(GRID_W, dtype=jnp.float32), rows)
    quarter = HEAD_DIM // 4
    freqs = ROPE_THETA ** (-jnp.arange(quarter, dtype=jnp.float32) / quarter)
    ar = row[:, None] * freqs
    ac = col[:, None] * freqs
    ang = jnp.concatenate([ar, ar, ac, ac], axis=-1)
    return jnp.cos(ang), jnp.sin(ang)


def _apply_rope(x, cos, sin):
    a, b, c, d = jnp.split(x, 4, axis=-1)
    rot = jnp.concatenate([-b, a, -d, c], axis=-1)
    return x * cos[None, :, None, :] + rot * sin[None, :, None, :]


def _attention(q, k, v):
    bsz, seq = q.shape[0], q.shape[1]
    nb = seq // Q_BLOCK
    qb = q.reshape(bsz, nb, Q_BLOCK, N_KV_HEADS, Q_PER_KV, HEAD_DIM).transpose(1, 0, 2, 3, 4, 5)

    def block(qi):
        s = jnp.einsum('bqkgd,bskd->bkgqs', qi, k).astype(jnp.float32)
        p = jax.nn.softmax(s, axis=-1).astype(v.dtype)
        return jnp.einsum('bkgqs,bskd->bqkgd', p, v)

    o = lax.map(block, qb)
    return o.transpose(1, 0, 2, 3, 4, 5).reshape(bsz, seq, ATTN_DIM)


def _multiscale_pool(u, pool_w, pool_scale):
    bsz, seq, _ = u.shape
    uf = u.astype(jnp.float32)
    pos = np.arange(seq)
    outs = []
    for gi, w in enumerate(POOL_WINDOWS):
        ug = uf[..., gi * POOL_GROUP:(gi + 1) * POOL_GROUP]
        cs = jnp.concatenate([jnp.zeros((bsz, 1, POOL_GROUP), jnp.float32), jnp.cumsum(ug, axis=1)], axis=1)
        lo = w // 2
        hi = w - lo - 1
        start = np.maximum(pos - lo, 0).astype(np.int32)
        end = (np.minimum(pos + hi, seq - 1) + 1).astype(np.int32)
        cnt = jnp.asarray((end - start).astype(np.float32))
        mean = (cs[:, end] - cs[:, start]) / cnt[None, :, None]
        outs.append(jnp.einsum('blc,cd->bld', mean - ug, pool_w[gi].astype(jnp.float32)))
    y = jnp.concatenate(outs, axis=-1) * pool_scale.astype(jnp.float32)
    return y.astype(u.dtype)


def _ssm_combine(e1, e2):
    a1, b1 = e1
    a2, b2 = e2
    return a2 * a1, a2 * b1 + b2


def _ssm_direction(u, a_re, a_im, log_dt, b_re, b_im, c_re, c_im):
    a = lax.complex(a_re.astype(jnp.float32), a_im.astype(jnp.float32))
    dt = jnp.exp(log_dt.astype(jnp.float32))[:, None]
    a_bar = jnp.exp(a * dt)
    b_mat = lax.complex(b_re.astype(jnp.float32), b_im.astype(jnp.float32))
    b_bar = ((a_bar - 1.0) / a)[..., None] * b_mat
    bu = jnp.einsum('gnp,blgp->blgn', b_bar, u.astype(jnp.complex64))
    a_seq = jnp.broadcast_to(a_bar, bu.shape)
    _, states = lax.associative_scan(_ssm_combine, (a_seq, bu), axis=1)
    c_mat = lax.complex(c_re.astype(jnp.float32), c_im.astype(jnp.float32))
    return jnp.real(jnp.einsum('gpn,blgn->blgp', c_mat, states))


def _bidir_ssm(u, a_re, a_im, log_dt, b_re, b_im, c_re, c_im, d_skip):
    bsz, seq, _ = u.shape
    uf = u.astype(jnp.float32)
    ug = uf.reshape(bsz, seq, SSM_G, SSM_P)
    y_f = _ssm_direction(ug, a_re[0], a_im[0], log_dt[0], b_re[0], b_im[0], c_re[0], c_im[0])
    y_b = _ssm_direction(ug[:, ::-1], a_re[1], a_im[1], log_dt[1], b_re[1], b_im[1], c_re[1], c_im[1])[:, ::-1]
    y = (y_f + y_b).reshape(bsz, seq, SSM_DIM) + d_skip.astype(jnp.float32) * uf
    return y.astype(u.dtype)


def _mixer(h, cos, sin, w_in, q_norm_g, k_norm_g, w_attn_o, pool_w, pool_scale, w_pool_o,
           ssm_a_re, ssm_a_im, ssm_log_dt, ssm_b_re, ssm_b_im, ssm_c_re, ssm_c_im, ssm_d, w_glu, w_out):
    bsz, seq, _ = h.shape
    q, k, v, u_pool, u_ssm, gates = jnp.split(h @ w_in, SPLIT_POINTS, axis=-1)
    q = _rmsnorm(q.reshape(bsz, seq, N_Q_HEADS, HEAD_DIM), q_norm_g).astype(jnp.float32)
    q = _apply_rope(q, cos, sin) * (HEAD_DIM ** -0.5)
    k = _rmsnorm(k.reshape(bsz, seq, N_KV_HEADS, HEAD_DIM), k_norm_g).astype(jnp.float32)
    k = _apply_rope(k, cos, sin)
    q = q.reshape(bsz, seq, N_KV_HEADS, Q_PER_KV, HEAD_DIM).astype(h.dtype)
    attn = _attention(q, k.astype(h.dtype), v.reshape(bsz, seq, N_KV_HEADS, HEAD_DIM)) @ w_attn_o
    pool = _multiscale_pool(u_pool, pool_w, pool_scale) @ w_pool_o
    y = _bidir_ssm(u_ssm, ssm_a_re, ssm_a_im, ssm_log_dt, ssm_b_re, ssm_b_im, ssm_c_re, ssm_c_im, ssm_d)
    za, zb = jnp.split(jax.nn.gelu(y) @ w_glu, 2, axis=-1)
    ssm = za * jax.nn.sigmoid(zb)
    g_a, g_p, g_s = jnp.split(jax.nn.sigmoid(gates), N_BRANCH, axis=-1)
    merged = g_a * attn + g_p * pool + g_s * ssm
    return merged @ w_out


def _moe(h, w_router, b_router, w_gate, w_up, w_down, ws_gate, ws_up, ws_down):
    bsz, seq, d = h.shape
    n_tok = bsz * seq
    hf = h.reshape(n_tok, d)
    scores = jax.nn.sigmoid(hf.astype(jnp.float32) @ w_router.astype(jnp.float32))
    choice = scores + b_router.astype(jnp.float32)
    per_group = N_EXPERTS // N_EXPERT_GROUPS
    group_score = lax.top_k(choice.reshape(n_tok, N_EXPERT_GROUPS, per_group), 2)[0].sum(-1)
    _, group_idx = lax.top_k(group_score, TOPK_GROUPS)
    group_mask = jnp.any(group_idx[:, :, None] == jnp.arange(N_EXPERT_GROUPS)[None, None, :], axis=1)
    expert_mask = jnp.repeat(group_mask, per_group, axis=1)
    _, expert_idx = lax.top_k(jnp.where(expert_mask, choice, -jnp.inf), TOP_K)
    gate_w = jnp.take_along_axis(scores, expert_idx, axis=1)
    gate_w = gate_w / jnp.sum(gate_w, axis=-1, keepdims=True) * ROUTED_SCALE
    n_assign = n_tok * TOP_K
    e_flat = expert_idx.reshape(n_assign)
    tok_flat = jnp.arange(n_assign, dtype=jnp.int32) // TOP_K
    order = jnp.argsort(e_flat)
    e_sorted = e_flat[order]
    tok_sorted = tok_flat[order]
    w_sorted = gate_w.reshape(n_assign)[order]
    counts = jnp.zeros((N_EXPERTS,), jnp.int32).at[e_flat].add(1)
    starts = jnp.cumsum(counts) - counts
    padded = (counts + MOE_BLOCK - 1) // MOE_BLOCK * MOE_BLOCK
    padded_end = jnp.cumsum(padded)
    padded_start = padded_end - padded
    dest = padded_start[e_sorted] + jnp.arange(n_assign, dtype=jnp.int32) - starts[e_sorted]
    n_blocks = (n_assign + N_EXPERTS * (MOE_BLOCK - 1)) // MOE_BLOCK
    n_rows = n_blocks * MOE_BLOCK
    x_buf = jnp.zeros((n_rows, d), h.dtype).at[dest].set(hf[tok_sorted])
    w_buf = jnp.zeros((n_rows,), jnp.float32).at[dest].set(w_sorted)
    tok_buf = jnp.full((n_rows,), n_tok, jnp.int32).at[dest].set(tok_sorted)
    block_start = jnp.arange(n_blocks, dtype=jnp.int32) * MOE_BLOCK
    block_expert = jnp.minimum(jnp.searchsorted(padded_end, block_start, side='right'), N_EXPERTS - 1)

    def expert_block(args):
        xb, e = args
        return (jax.nn.silu(xb @ w_gate[e]) * (xb @ w_up[e])) @ w_down[e]

    y_buf = lax.map(expert_block, (x_buf.reshape(n_blocks, MOE_BLOCK, d), block_expert)).reshape(n_rows, d)
    routed = jax.ops.segment_sum(y_buf * w_buf[:, None].astype(y_buf.dtype), tok_buf, num_segments=n_tok + 1)[:n_tok]
    shared = (jax.nn.silu(hf @ ws_gate) * (hf @ ws_up)) @ ws_down
    return (routed + shared).reshape(bsz, seq, d)


def _trunk(x, c, w_ada, b_ada, norm1_g, w_in, q_norm_g, k_norm_g, w_attn_o, pool_w, pool_scale, w_pool_o,
           ssm_a_re, ssm_a_im, ssm_log_dt, ssm_b_re, ssm_b_im, ssm_c_re, ssm_c_im, ssm_d, w_glu, w_out,
           norm2_g, w_router, b_router, w_exp_gate, w_exp_up, w_exp_down, w_sh_gate, w_sh_up, w_sh_down):
    cos, sin = _axial_rope_tables(x.shape[1])
    for l in range(DEPTH):
        mod = (jax.nn.silu(c) @ w_ada[l] + b_ada[l])[:, None, :]
        sh1, sc1, g1, sh2, sc2, g2 = jnp.split(mod, N_ADA, axis=-1)
        h = _rmsnorm(x, norm1_g[l]) * (1.0 + sc1) + sh1
        x = x + g1 * _mixer(h, cos, sin, w_in[l], q_norm_g[l], k_norm_g[l], w_attn_o[l], pool_w[l], pool_scale[l],
                            w_pool_o[l], ssm_a_re[l], ssm_a_im[l], ssm_log_dt[l], ssm_b_re[l], ssm_b_im[l],
                            ssm_c_re[l], ssm_c_im[l], ssm_d[l], w_glu[l], w_out[l])
        h = _rmsnorm(x, norm2_g[l]) * (1.0 + sc2) + sh2
        x = x + g2 * _moe(h, w_router[l], b_router[l], w_exp_gate[l], w_exp_up[l], w_exp_down[l],
                          w_sh_gate[l], w_sh_up[l], w_sh_down[l])
    return x


def setup_inputs(seed: int = 0) -> dict:
    key = jax.random.key(seed)
    ks = iter(jax.random.split(key, 40))

    def nrm(shape, scale):
        return jax.random.normal(next(ks), shape, jnp.float32) * scale

    d = D_MODEL
    n_pool = len(POOL_WINDOWS)
    return {
        'x_prompt': nrm((BATCH, SEQ, d), 1.0),
        'x_sample': nrm((DEC_BATCH, DEC_SEQ, d), 1.0),
        'c_prompt': nrm((BATCH, d), 1.0),
        'c_sample': nrm((DEC_BATCH, d), 1.0),
        'w_ada': nrm((DEPTH, d, N_ADA * d), 0.5 * d ** -0.5),
        'b_ada': nrm((DEPTH, N_ADA * d), 0.01),
        'norm1_g': 1.0 + nrm((DEPTH, d), 0.02),
        'w_in': nrm((DEPTH, d, IN_DIM), d ** -0.5),
        'q_norm_g': 1.0 + nrm((DEPTH, HEAD_DIM), 0.02),
        'k_norm_g': 1.0 + nrm((DEPTH, HEAD_DIM), 0.02),
        'w_attn_o': nrm((DEPTH, ATTN_DIM, d), ATTN_DIM ** -0.5),
        'pool_w': nrm((DEPTH, n_pool, POOL_GROUP, POOL_GROUP), POOL_GROUP ** -0.5),
        'pool_scale': 1.0 + nrm((DEPTH, POOL_DIM), 0.1),
        'w_pool_o': nrm((DEPTH, POOL_DIM, d), POOL_DIM ** -0.5),
        'ssm_a_re': -0.5 + nrm((DEPTH, 2, SSM_G, SSM_N), 0.01),
        'ssm_a_im': math.pi * jnp.arange(SSM_N, dtype=jnp.float32) + nrm((DEPTH, 2, SSM_G, SSM_N), 0.01),
        'ssm_log_dt': jax.random.uniform(next(ks), (DEPTH, 2, SSM_G), jnp.float32, math.log(1e-3), math.log(1e-1)),
        'ssm_b_re': nrm((DEPTH, 2, SSM_G, SSM_N, SSM_P), (2 * SSM_P) ** -0.5),
        'ssm_b_im': nrm((DEPTH, 2, SSM_G, SSM_N, SSM_P), (2 * SSM_P) ** -0.5),
        'ssm_c_re': nrm((DEPTH, 2, SSM_G, SSM_P, SSM_N), SSM_N ** -0.5),
        'ssm_c_im': nrm((DEPTH, 2, SSM_G, SSM_P, SSM_N), SSM_N ** -0.5),
        'ssm_d': nrm((DEPTH, SSM_DIM), 1.0),
        'w_glu': nrm((DEPTH, SSM_DIM, 2 * d), SSM_DIM ** -0.5),
        'w_out': nrm((DEPTH, d, d), d ** -0.5),
        'norm2_g': 1.0 + nrm((DEPTH, d), 0.02),
        'w_router': nrm((DEPTH, d, N_EXPERTS), d ** -0.5),
        'b_router': nrm((DEPTH, N_EXPERTS), 0.01),
        'w_exp_gate': nrm((DEPTH, N_EXPERTS, d, D_EXPERT), d ** -0.5),
        'w_exp_up': nrm((DEPTH, N_EXPERTS, d, D_EXPERT), d ** -0.5),
        'w_exp_down': nrm((DEPTH, N_EXPERTS, D_EXPERT, d), D_EXPERT ** -0.5),
        'w_sh_gate': nrm((DEPTH, d, D_SHARED), d ** -0.5),
        'w_sh_up': nrm((DEPTH, d, D_SHARED), d ** -0.5),
        'w_sh_down': nrm((DEPTH, D_SHARED, d), D_SHARED ** -0.5),
    }


def reference(x_prompt, x_sample, c_prompt, c_sample, w_ada, b_ada, norm1_g, w_in, q_norm_g, k_norm_g, w_attn_o,
              pool_w, pool_scale, w_pool_o, ssm_a_re, ssm_a_im, ssm_log_dt, ssm_b_re, ssm_b_im, ssm_c_re, ssm_c_im,
              ssm_d, w_glu, w_out, norm2_g, w_router, b_router, w_exp_gate, w_exp_up, w_exp_down,
              w_sh_gate, w_sh_up, w_sh_down):
    weights = (w_ada, b_ada, norm1_g, w_in, q_norm_g, k_norm_g, w_attn_o, pool_w, pool_scale, w_pool_o,
               ssm_a_re, ssm_a_im, ssm_log_dt, ssm_b_re, ssm_b_im, ssm_c_re, ssm_c_im, ssm_d, w_glu, w_out,
               norm2_g, w_router, b_router, w_exp_gate, w_exp_up, w_exp_down, w_sh_gate, w_sh_up, w_sh_down)
    y_prompt = _trunk(x_prompt, c_prompt, *weights)
    y_sample = _trunk(x_sample, c_sample, *weights)
    return (y_prompt, y_sample)
```

```python
import functools
import math

import jax
import jax.numpy as jnp
import numpy as np
from jax import lax
from jax.experimental import pallas as pl
from jax.experimental.pallas import tpu as pltpu

F32 = jnp.float32
BF16 = jnp.bfloat16
I32 = jnp.int32
HIGHEST = lax.Precision.HIGHEST

D_MODEL = 1024
GRID_W = 64
HEAD_DIM = 64
N_Q_HEADS = 8
N_KV_HEADS = 2
Q_PER_KV = N_Q_HEADS // N_KV_HEADS
ATTN_DIM = N_Q_HEADS * HEAD_DIM
KV_DIM = N_KV_HEADS * HEAD_DIM
ROPE_THETA = 10000.0
POOL_WINDOWS = (2, 4, 8, 16)
POOL_DIM = 256
POOL_GROUP = 64
POOL_HALO = 8
SSM_DIM = 256
SSM_P = 16
SSM_G = 16
SSM_N = 64
SSM_CHUNK = 16
SSM_ROW = SSM_CHUNK * SSM_P
SSM_STATE = 2 * SSM_N
N_EXPERTS = 256
TOP_K = 8
N_EXPERT_GROUPS = 8
GROUP_SIZE = N_EXPERTS // N_EXPERT_GROUPS
TOPK_GROUPS = 4
D_EXPERT = 256
ROUTED_SCALE = 2.5
N_ADA = 6
EPS = 1e-6
IN_DIM = ATTN_DIM + 2 * KV_DIM + POOL_DIM + SSM_DIM + 3 * D_MODEL
OFF_K = ATTN_DIM
OFF_V = OFF_K + KV_DIM
OFF_POOL = OFF_V + KV_DIM
OFF_SSM = OFF_POOL + POOL_DIM
OFF_GATES = OFF_SSM + SSM_DIM
MOD_ROWS = 8

V7X_VMEM_BYTES = 64 * 1024 * 1024
VMEM_LIMIT = V7X_VMEM_BYTES - 8 * 1024 * 1024
LANES = 128


def _cparams(sem):
    return pltpu.CompilerParams(dimension_semantics=sem, vmem_limit_bytes=VMEM_LIMIT)


def _pick(n, pref):
    t = min(n, pref)
    while n % t:
        t //= 2
    return t


def _mod_kernel(c_ref, w_ref, b_ref, o_ref):
    c = c_ref[...]
    a = c * jax.nn.sigmoid(c)
    o_ref[0] = jnp.dot(a, w_ref[0], preferred_element_type=F32, precision=HIGHEST) + b_ref[0]


def _modulation(c_all, w_ada, b_ada):
    depth, d, n = w_ada.shape
    bn = _pick(n, 1536)
    return pl.pallas_call(
        _mod_kernel,
        grid=(depth, n // bn),
        in_specs=[
            pl.BlockSpec((MOD_ROWS, d), lambda l, j: (0, 0)),
            pl.BlockSpec((1, d, bn), lambda l, j: (l, 0, j)),
            pl.BlockSpec((1, 1, bn), lambda l, j: (l, 0, j)),
        ],
        out_specs=pl.BlockSpec((1, MOD_ROWS, bn), lambda l, j: (l, 0, j)),
        out_shape=jax.ShapeDtypeStruct((depth, MOD_ROWS, n), F32),
        compiler_params=_cparams(("arbitrary", "arbitrary")),
        name="adaln_mod",
    )(c_all, w_ada, b_ada.reshape(depth, 1, n))


def _rmsnorm_mod(x, g, scale, shift):
    y = x * lax.rsqrt(jnp.mean(x * x, axis=-1, keepdims=True) + EPS)
    return (y * g) * (1.0 + scale) + shift


def _head_norm_rope(z, gain, cos, sin_signed, ones_bd):
    z2 = z * z
    hi = z2.astype(BF16)
    lo = (z2 - hi.astype(F32)).astype(BF16)
    ss = (jnp.dot(hi, ones_bd, preferred_element_type=F32)
          + jnp.dot(lo, ones_bd, preferred_element_type=F32))
    y = (z * lax.rsqrt(ss * (1.0 / HEAD_DIM) + EPS)) * gain
    w = z.shape[1]
    quarter = HEAD_DIM // 4
    from_right = pltpu.roll(y, w - quarter, 1)
    from_left = pltpu.roll(y, quarter, 1)
    lane = lax.broadcasted_iota(I32, y.shape, 1)
    rot = jnp.where((lane & quarter) == 0, from_right, from_left)
    return y * cos + rot * sin_signed


def _inproj_kernel(ts_ref, x_ref, mod_ref, g_ref, w_ref, gq_ref, gk_ref, cos_ref, sin_ref, ones_ref,
                   q_ref, k_ref, v_ref, up_ref, us_ref, gt_ref):
    del ts_ref
    x = x_ref[...]
    mod = mod_ref[0]
    h = _rmsnorm_mod(x, g_ref[...], mod[1:2], mod[0:1]).astype(BF16)

    cos = cos_ref[...]
    sin = sin_ref[...]
    zq = jnp.dot(h, w_ref[:, 0:ATTN_DIM], preferred_element_type=F32)
    reps = ATTN_DIM // LANES
    yq = _head_norm_rope(zq, gq_ref[...], jnp.concatenate([cos] * reps, axis=1),
                         jnp.concatenate([sin] * reps, axis=1), ones_ref[...]) * (HEAD_DIM ** -0.5)
    for hh in range(N_Q_HEADS):
        q_ref[hh] = yq[:, hh * HEAD_DIM:(hh + 1) * HEAD_DIM].astype(BF16)

    zkv = jnp.dot(h, w_ref[:, OFF_K:OFF_POOL], preferred_element_type=F32)
    yk = _head_norm_rope(zkv[:, :KV_DIM], gk_ref[...], cos, sin, ones_ref[0:KV_DIM, 0:KV_DIM])
    zv = zkv[:, KV_DIM:]
    for j in range(N_KV_HEADS):
        k_ref[j] = yk[:, j * HEAD_DIM:(j + 1) * HEAD_DIM].astype(BF16)
        v_ref[j] = zv[:, j * HEAD_DIM:(j + 1) * HEAD_DIM].astype(BF16)

    zps = jnp.dot(h, w_ref[:, OFF_POOL:OFF_GATES], preferred_element_type=F32)
    up_ref[...] = zps[:, :POOL_DIM]
    us_ref[...] = zps[:, POOL_DIM:]
    for c in range(3):
        zg = jnp.dot(h, w_ref[:, OFF_GATES + c * D_MODEL:OFF_GATES + (c + 1) * D_MODEL],
                     preferred_element_type=F32)
        gt_ref[:, c * D_MODEL:(c + 1) * D_MODEL] = jax.nn.sigmoid(zg).astype(BF16)


def _inproj(x, mod_l, g1, w_in_bf, gq_t, gk_t, cos_t, sin_t, ones_bd, tile_seq, tm):
    t = x.shape[0]
    const = lambda i, ts: (0, 0)
    row = lambda i, ts: (i, 0)
    grid_spec = pltpu.PrefetchScalarGridSpec(
        num_scalar_prefetch=1,
        grid=(t // tm,),
        in_specs=[
            pl.BlockSpec((tm, D_MODEL), row),
            pl.BlockSpec((1, N_ADA, D_MODEL), lambda i, ts: (ts[i], 0, 0)),
            pl.BlockSpec((1, D_MODEL), const),
            pl.BlockSpec((D_MODEL, IN_DIM), const),
            pl.BlockSpec((1, ATTN_DIM), const),
            pl.BlockSpec((1, KV_DIM), const),
            pl.BlockSpec((tm, LANES), row),
            pl.BlockSpec((tm, LANES), row),
            pl.BlockSpec((ATTN_DIM, ATTN_DIM), const),
        ],
        out_specs=[
            pl.BlockSpec((N_Q_HEADS, tm, HEAD_DIM), lambda i, ts: (0, i, 0)),
            pl.BlockSpec((N_KV_HEADS, tm, HEAD_DIM), lambda i, ts: (0, i, 0)),
            pl.BlockSpec((N_KV_HEADS, tm, HEAD_DIM), lambda i, ts: (0, i, 0)),
            pl.BlockSpec((tm, POOL_DIM), row),
            pl.BlockSpec((tm, SSM_DIM), row),
            pl.BlockSpec((tm, 3 * D_MODEL), row),
        ],
    )
    return pl.pallas_call(
        _inproj_kernel,
        grid_spec=grid_spec,
        out_shape=[
            jax.ShapeDtypeStruct((N_Q_HEADS, t, HEAD_DIM), BF16),
            jax.ShapeDtypeStruct((N_KV_HEADS, t, HEAD_DIM), BF16),
            jax.ShapeDtypeStruct((N_KV_HEADS, t, HEAD_DIM), BF16),
            jax.ShapeDtypeStruct((t, POOL_DIM), F32),
            jax.ShapeDtypeStruct((t, SSM_DIM), F32),
            jax.ShapeDtypeStruct((t, 3 * D_MODEL), BF16),
        ],
        compiler_params=_cparams(("arbitrary",)),
        name="norm1_inproj",
    )(tile_seq, x, mod_l, g1, w_in_bf, gq_t, gk_t, cos_t, sin_t, ones_bd)


def _flash_kernel(q_ref, k_ref, v_ref, o_ref, m_sc, l_sc, acc_sc, *, tq, tk):
    kb = pl.program_id(3)

    @pl.when(kb == 0)
    def _():
        m_sc[...] = jnp.full(m_sc.shape, -jnp.inf, F32)
        l_sc[...] = jnp.zeros(l_sc.shape, F32)
        acc_sc[...] = jnp.zeros(acc_sc.shape, F32)

    q = q_ref[...].reshape(Q_PER_KV * tq, HEAD_DIM)
    s = lax.dot_general(q, k_ref[0], (((1,), (1,)), ((), ())), preferred_element_type=F32)
    m_prev = m_sc[...]
    m_new = jnp.maximum(m_prev, jnp.max(s, axis=1, keepdims=True))
    alpha = jnp.exp(m_prev - m_new)
    p = jnp.exp(s - jnp.concatenate([m_new] * (tk // LANES), axis=1))
    l_sc[...] = alpha * l_sc[...] + jnp.sum(p, axis=1, keepdims=True)
    m_sc[...] = m_new
    acc_sc[...] = acc_sc[...] * alpha[:, :HEAD_DIM] + jnp.dot(
        p.astype(BF16), v_ref[0], preferred_element_type=F32)

    @pl.when(kb == pl.num_programs(3) - 1)
    def _():
        o = acc_sc[...] / l_sc[:, :HEAD_DIM]
        for hh in range(Q_PER_KV):
            o_ref[:, hh * HEAD_DIM:(hh + 1) * HEAD_DIM] = o[hh * tq:(hh + 1) * tq].astype(BF16)


def _flash(q8, k2, v2, off, bsz, seq):
    tq = _pick(seq, 256)
    tk = _pick(seq, 512)
    nq, nk = seq // tq, seq // tk
    oq, ok = off // tq, off // tk
    rows = Q_PER_KV * tq
    return pl.pallas_call(
        functools.partial(_flash_kernel, tq=tq, tk=tk),
        grid=(N_KV_HEADS, bsz, nq, nk),
        in_specs=[
            pl.BlockSpec((Q_PER_KV, tq, HEAD_DIM), lambda j, b, i, kb: (j, oq + b * nq + i, 0)),
            pl.BlockSpec((1, tk, HEAD_DIM), lambda j, b, i, kb: (j, ok + b * nk + kb, 0)),
            pl.BlockSpec((1, tk, HEAD_DIM), lambda j, b, i, kb: (j, ok + b * nk + kb, 0)),
        ],
        out_specs=pl.BlockSpec((tq, Q_PER_KV * HEAD_DIM), lambda j, b, i, kb: (b * nq + i, j)),
        out_shape=jax.ShapeDtypeStruct((bsz * seq, ATTN_DIM), BF16),
        scratch_shapes=[
            pltpu.VMEM((rows, LANES), F32),
            pltpu.VMEM((rows, LANES), F32),
            pltpu.VMEM((rows, HEAD_DIM), F32),
        ],
        compiler_params=_cparams(("parallel", "parallel", "parallel", "arbitrary")),
        name="flash_gqa",
    )(q8, k2, v2)


def _ssm_tables(a_re, a_im, log_dt, b_re, b_im, c_re, c_im, d_skip):
    tc = SSM_CHUNK
    a = lax.complex(a_re.astype(F32), a_im.astype(F32))
    dt = jnp.exp(log_dt.astype(F32))[..., None]
    adt = a * dt
    a_bar = jnp.exp(adt)
    b_bar = ((a_bar - 1.0) / a)[..., None] * lax.complex(b_re.astype(F32), b_im.astype(F32))
    c_mat = lax.complex(c_re.astype(F32), c_im.astype(F32))
    taus = jnp.arange(tc + 1, dtype=F32)
    pw = jnp.exp(adt[None] * taus[:, None, None, None])

    kern = jnp.real(jnp.einsum('dgpn,tdgn,dgnq->dgtpq', c_mat, pw[:tc], b_bar))
    s_idx = jnp.arange(tc)[:, None]
    t_idx = jnp.arange(tc)[None, :]
    lag_f = jnp.clip(t_idx - s_idx, 0, tc - 1)
    lag_b = jnp.clip(s_idx - t_idx, 0, tc - 1)
    kf = jnp.where((t_idx >= s_idx)[None, :, :, None, None], kern[0][:, lag_f], 0.0)
    kb = jnp.where((s_idx >= t_idx)[None, :, :, None, None], kern[1][:, lag_b], 0.0)
    w_loc = (kf + kb).transpose(0, 1, 4, 2, 3).reshape(SSM_G, SSM_ROW, SSM_ROW)
    w_loc = w_loc + jnp.eye(SSM_ROW, dtype=F32)[None] * jnp.tile(
        d_skip.astype(F32).reshape(SSM_G, 1, SSM_P), (1, tc, 1)).reshape(SSM_G, 1, SSM_ROW)

    def ri(z, axis):
        return jnp.concatenate([jnp.real(z), jnp.imag(z)], axis=axis)

    pf = jnp.einsum('sgn,gnq->gsqn', pw[:tc][::-1, 0], b_bar[0])
    pb = jnp.einsum('sgn,gnq->gsqn', pw[:tc, 1], b_bar[1])
    p_st = jnp.concatenate([ri(pf, -1), ri(pb, -1)], axis=-1).reshape(SSM_G, SSM_ROW, 2 * SSM_STATE)
    qf = jnp.einsum('gpn,tgn->gntp', c_mat[0], pw[1:tc + 1, 0])
    qb = jnp.einsum('gpn,tgn->gntp', c_mat[1], pw[1:tc + 1, 1][::-1])
    q_f = jnp.concatenate([jnp.real(qf), -jnp.imag(qf)], axis=1).reshape(SSM_G, SSM_STATE, SSM_ROW)
    q_b = jnp.concatenate([jnp.real(qb), -jnp.imag(qb)], axis=1).reshape(SSM_G, SSM_STATE, SSM_ROW)
    a16 = pw[tc]
    a1 = jnp.concatenate([jnp.real(a16), jnp.real(a16)], axis=-1).reshape(2, SSM_G * SSM_STATE)
    a2 = jnp.concatenate([-jnp.imag(a16), jnp.imag(a16)], axis=-1).reshape(2, SSM_G * SSM_STATE)
    a_mul = jnp.stack([a1[0], a2[0], a1[1], a2[1]], axis=0)
    return w_loc.astype(BF16), p_st.astype(BF16), q_f.astype(BF16), q_b.astype(BF16), a_mul


def _ssm_state_kernel(u_ref, p_ref, s_ref):
    s = jnp.dot(u_ref[0].astype(BF16), p_ref[0], preferred_element_type=F32)
    s_ref[0] = s[:, :SSM_STATE]
    s_ref[1] = s[:, SSM_STATE:]


def _ssm_states(ug, p_st, tc):
    g, nc, _ = ug.shape
    return pl.pallas_call(
        _ssm_state_kernel,
        grid=(g, nc // tc),
        in_specs=[
            pl.BlockSpec((1, tc, SSM_ROW), lambda gi, i: (gi, i, 0)),
            pl.BlockSpec((1, SSM_ROW, 2 * SSM_STATE), lambda gi, i: (gi, 0, 0)),
        ],
        out_specs=pl.BlockSpec((2, tc, SSM_STATE), lambda gi, i: (0, i, gi)),
        out_shape=jax.ShapeDtypeStruct((2, nc, g * SSM_STATE), F32),
        compiler_params=_cparams(("arbitrary", "arbitrary")),
        name="ssm_chunk_states",
    )(ug, p_st)


def _swap_halves(x):
    parts = [pltpu.roll(x[:, g * LANES:(g + 1) * LANES], LANES // 2, 1) for g in range(x.shape[1] // LANES)]
    return jnp.concatenate(parts, axis=1)


def _ssm_scan_kernel(bt_ref, first_ref, sf_ref, sb_ref, am_ref, xf_ref, zb_ref, st_sc, sfw_sc, sbw_sc, *, tr):
    n = pl.program_id(0)
    del bt_ref

    @pl.when(first_ref[n] == 1)
    def _():
        st_sc[...] = jnp.zeros(st_sc.shape, F32)

    sfw_sc[...] = _swap_halves(sf_ref[0])
    sbw_sc[...] = _swap_halves(sb_ref[0])
    a1f, a2f, a1b, a2b = am_ref[0:1], am_ref[1:2], am_ref[2:3], am_ref[3:4]

    def body(r, carry):
        xf, xfw, zb, zbw = carry
        xf_ref[pl.ds(r, 1), :] = xf
        sf = sf_ref[0, pl.ds(r, 1), :]
        sfw = sfw_sc[pl.ds(r, 1), :]
        nxf = a1f * xf + a2f * xfw + sf
        nxfw = a1f * xfw - a2f * xf + sfw
        rb = tr - 1 - r
        zb_ref[pl.ds(rb, 1), :] = zb
        sb = sb_ref[0, pl.ds(rb, 1), :]
        sbw = sbw_sc[pl.ds(rb, 1), :]
        nzb = a1b * zb + a2b * zbw + sb
        nzbw = a1b * zbw - a2b * zb + sbw
        return nxf, nxfw, nzb, nzbw

    init = (st_sc[0:1], st_sc[1:2], st_sc[2:3], st_sc[3:4])
    xf, xfw, zb, zbw = lax.fori_loop(0, tr, body, init)
    st_sc[0:1] = xf
    st_sc[1:2] = xfw
    st_sc[2:3] = zb
    st_sc[3:4] = zbw


def _ssm_scan(s_st, a_mul, bwd_tile, first_tile, tr):
    _, nc, w = s_st.shape
    grid_spec = pltpu.PrefetchScalarGridSpec(
        num_scalar_prefetch=2,
        grid=(nc // tr,),
        in_specs=[
            pl.BlockSpec((1, tr, w), lambda n, bt, ft: (0, n, 0)),
            pl.BlockSpec((1, tr, w), lambda n, bt, ft: (1, bt[n], 0)),
            pl.BlockSpec((4, w), lambda n, bt, ft: (0, 0)),
        ],
        out_specs=[
            pl.BlockSpec((tr, w), lambda n, bt, ft: (n, 0)),
            pl.BlockSpec((tr, w), lambda n, bt, ft: (bt[n], 0)),
        ],
        scratch_shapes=[pltpu.VMEM((4, w), F32), pltpu.VMEM((tr, w), F32), pltpu.VMEM((tr, w), F32)],
    )
    return pl.pallas_call(
        functools.partial(_ssm_scan_kernel, tr=tr),
        grid_spec=grid_spec,
        out_shape=[jax.ShapeDtypeStruct((nc, w), F32), jax.ShapeDtypeStruct((nc, w), F32)],
        compiler_params=_cparams(("arbitrary",)),
        name="ssm_chunk_scan",
    )(bwd_tile, first_tile, s_st, s_st, a_mul)


def _ssm_out_kernel(u_ref, xf_ref, zb_ref, w_ref, qf_ref, qb_ref, y_ref):
    y = jnp.dot(u_ref[0].astype(BF16), w_ref[0], preferred_element_type=F32)
    y += jnp.dot(xf_ref[...].astype(BF16), qf_ref[0], preferred_element_type=F32)
    y += jnp.dot(zb_ref[...].astype(BF16), qb_ref[0], preferred_element_type=F32)
    y_ref[0] = y


def _ssm_outputs(ug, xf, zb, w_loc, q_f, q_b, tc):
    g, nc, _ = ug.shape
    return pl.pallas_call(
        _ssm_out_kernel,
        grid=(g, nc // tc),
        in_specs=[
            pl.BlockSpec((1, tc, SSM_ROW), lambda gi, i: (gi, i, 0)),
            pl.BlockSpec((tc, SSM_STATE), lambda gi, i: (i, gi)),
            pl.BlockSpec((tc, SSM_STATE), lambda gi, i: (i, gi)),
            pl.BlockSpec((1, SSM_ROW, SSM_ROW), lambda gi, i: (gi, 0, 0)),
            pl.BlockSpec((1, SSM_STATE, SSM_ROW), lambda gi, i: (gi, 0, 0)),
            pl.BlockSpec((1, SSM_STATE, SSM_ROW), lambda gi, i: (gi, 0, 0)),
        ],
        out_specs=pl.BlockSpec((1, tc, SSM_ROW), lambda gi, i: (gi, i, 0)),
        out_shape=jax.ShapeDtypeStruct((g, nc, SSM_ROW), F32),
        compiler_params=_cparams(("arbitrary", "arbitrary")),
        name="ssm_chunk_outputs",
    )(ug, xf, zb, w_loc, q_f, q_b)


def _bidir_ssm(u_ssm, tables, bwd_tile, first_tile, tr):
    w_loc, p_st, q_f, q_b, a_mul = tables
    t = u_ssm.shape[0]
    nc = t // SSM_CHUNK
    ug = u_ssm.reshape(nc, SSM_CHUNK, SSM_G, SSM_P).transpose(2, 0, 1, 3).reshape(SSM_G, nc, SSM_ROW)
    tc = _pick(nc, 512)
    s_st = _ssm_states(ug, p_st, tc)
    xf, zb = _ssm_scan(s_st, a_mul, bwd_tile, first_tile, tr)
    yg = _ssm_outputs(ug, xf, zb, w_loc, q_f, q_b, tc)
    return yg.reshape(SSM_G, nc, SSM_CHUNK, SSM_P).transpose(1, 2, 0, 3).reshape(t, SSM_DIM)


def _pool_mixer(ext_sc, u, pos, seq_len, tm):
    halves = []
    for half in range(2):
        sums = []
        for w in POOL_WINDOWS[2 * half:2 * half + 2]:
            lo = w // 2
            hi = w - lo - 1
            acc = ext_sc[pl.ds(POOL_HALO - lo, tm), half * LANES:(half + 1) * LANES]
            for d in range(-lo + 1, hi + 1):
                acc = acc + ext_sc[pl.ds(POOL_HALO + d, tm), half * LANES:(half + 1) * LANES]
            cnt = (jnp.minimum(pos + hi, seq_len - 1) - jnp.maximum(pos - lo, 0) + 1).astype(F32)
            sums.append(acc / cnt)
        lane = lax.broadcasted_iota(I32, (tm, LANES), 1)
        halves.append(jnp.where(lane < POOL_GROUP, sums[0], sums[1]))
    return jnp.concatenate(halves, axis=1) - u


def _route(h2, wr_ref, br_ref):
    logits = jnp.dot(h2, wr_ref[...], preferred_element_type=F32, precision=HIGHEST)
    scores = jax.nn.sigmoid(logits)
    choice = scores + br_ref[...]
    lane = lax.broadcasted_iota(I32, choice.shape, 1)
    neg = jnp.float32(-jnp.inf)

    def first_argmax(c):
        m = jnp.max(c, axis=1, keepdims=True)
        return jnp.min(jnp.where(c == m, lane, N_EXPERTS), axis=1, keepdims=True), m

    in_group, gscore = [], []
    for g in range(N_EXPERT_GROUPS):
        ing = (lane >= g * GROUP_SIZE) & (lane < (g + 1) * GROUP_SIZE)
        cg = jnp.where(ing, choice, neg)
        i1, m1 = first_argmax(cg)
        m2 = jnp.max(jnp.where(lane == i1, neg, cg), axis=1, keepdims=True)
        in_group.append(ing)
        gscore.append(m1 + m2)
    emask = jnp.zeros(choice.shape, jnp.bool_)
    for g in range(N_EXPERT_GROUPS):
        ahead = jnp.zeros(gscore[g].shape, I32)
        for o in range(N_EXPERT_GROUPS):
            if o == g:
                continue
            beats = (gscore[o] > gscore[g]) | ((gscore[o] == gscore[g]) & (o < g))
            ahead = ahead + beats.astype(I32)
        emask = emask | (in_group[g] & (ahead < TOPK_GROUPS))
    c = jnp.where(emask, choice, neg)
    picks = []
    for _ in range(TOP_K):
        i, _m = first_argmax(c)
        oh = lane == i
        w = jnp.sum(jnp.where(oh, scores, 0.0), axis=1, keepdims=True)
        c = jnp.where(oh, neg, c)
        picks.append((i, w, oh))
    wsum = picks[0][1]
    for _, w, _ in picks[1:]:
        wsum = wsum + w
    return [(i, w / wsum * ROUTED_SCALE, oh) for i, w, oh in picks]


def _post_kernel(ts_ref, tf_ref, tl_ref, tp_ref, tn_ref,
                 x_ref, at_ref, upp_ref, up_ref, upn_ref, ys_ref, gt_ref, mod_ref,
                 wao_ref, pbd_ref, psc_ref, wpo_ref, wgl_ref, wo_ref, g2_ref, wr_ref, br_ref, tril_ref,
                 x1_ref, h2_ref, idx_ref, gate_ref, rank_ref, cnt_ref, ext_sc, carry_sc, *, tm):
    del ts_ref
    i = pl.program_id(0)
    mod = mod_ref[0]

    attn = jnp.dot(at_ref[...], wao_ref[...], preferred_element_type=F32)

    u = up_ref[...]
    ext_sc[0:POOL_HALO] = jnp.where(tf_ref[i] == 1, 0.0, upp_ref[...])
    ext_sc[POOL_HALO:POOL_HALO + tm] = u
    ext_sc[POOL_HALO + tm:2 * POOL_HALO + tm] = jnp.where(tl_ref[i] == 1, 0.0, upn_ref[...])
    pos = tp_ref[i] + lax.broadcasted_iota(I32, (tm, LANES), 0)
    pm = _pool_mixer(ext_sc, u, pos, tn_ref[i], tm)
    pool = jnp.dot(pm.astype(BF16), pbd_ref[...], preferred_element_type=F32) * psc_ref[...]
    pool = jnp.dot(pool.astype(BF16), wpo_ref[...], preferred_element_type=F32)

    z = jnp.dot(jax.nn.gelu(ys_ref[...]).astype(BF16), wgl_ref[...], preferred_element_type=F32)
    ssm = z[:, :D_MODEL] * jax.nn.sigmoid(z[:, D_MODEL:])

    merged = (gt_ref[:, 0:D_MODEL].astype(F32) * attn
              + gt_ref[:, D_MODEL:2 * D_MODEL].astype(F32) * pool
              + gt_ref[:, 2 * D_MODEL:3 * D_MODEL].astype(F32) * ssm)
    mix = jnp.dot(merged.astype(BF16), wo_ref[...], preferred_element_type=F32)
    x1 = x_ref[...] + mod[2:3] * mix
    x1_ref[...] = x1
    h2 = _rmsnorm_mod(x1, g2_ref[...], mod[4:5], mod[3:4])
    h2_ref[...] = h2

    picks = _route(h2, wr_ref, br_ref)

    @pl.when(i == 0)
    def _():
        carry_sc[...] = jnp.zeros(carry_sc.shape, F32)

    sel = picks[0][2]
    for _, _, oh in picks[1:]:
        sel = sel | oh
    sel_f = sel.astype(F32)
    ranks = jnp.dot(tril_ref[...], sel_f.astype(BF16), preferred_element_type=F32) + carry_sc[...]
    carry = carry_sc[...] + jnp.sum(sel_f, axis=0, keepdims=True)
    carry_sc[...] = carry
    cnt_ref[...] = carry

    lane = lax.broadcasted_iota(I32, (tm, LANES), 1)
    idx_o = jnp.zeros((tm, LANES), I32)
    gate_o = jnp.zeros((tm, LANES), F32)
    rank_o = jnp.zeros((tm, LANES), I32)
    for k, (ik, gk, oh) in enumerate(picks):
        rk = jnp.sum(jnp.where(oh, ranks, 0.0), axis=1, keepdims=True).astype(I32)
        idx_o = jnp.where(lane == k, ik, idx_o)
        gate_o = jnp.where(lane == k, gk, gate_o)
        rank_o = jnp.where(lane == k, rk, rank_o)
    idx_ref[...] = idx_o
    gate_ref[...] = gate_o
    rank_ref[...] = rank_o


def _post(x, attn, u_pool, y_ssm, gates, mod_l, wts, meta, tm):
    t = x.shape[0]
    nh = t // POOL_HALO
    hb = tm // POOL_HALO
    npf = 5
    const = lambda i, *_: (0, 0)
    row = lambda i, *_: (i, 0)
    grid_spec = pltpu.PrefetchScalarGridSpec(
        num_scalar_prefetch=npf,
        grid=(t // tm,),
        in_specs=[
            pl.BlockSpec((tm, D_MODEL), row),
            pl.BlockSpec((tm, ATTN_DIM), row),
            pl.BlockSpec((POOL_HALO, POOL_DIM), lambda i, *_: (jnp.maximum(i * hb - 1, 0), 0)),
            pl.BlockSpec((tm, POOL_DIM), row),
            pl.BlockSpec((POOL_HALO, POOL_DIM), lambda i, *_: (jnp.minimum((i + 1) * hb, nh - 1), 0)),
            pl.BlockSpec((tm, SSM_DIM), row),
            pl.BlockSpec((tm, 3 * D_MODEL), row),
            pl.BlockSpec((1, N_ADA, D_MODEL), lambda i, ts, *_: (ts[i], 0, 0)),
            pl.BlockSpec((ATTN_DIM, D_MODEL), const),
            pl.BlockSpec((POOL_DIM, POOL_DIM), const),
            pl.BlockSpec((1, POOL_DIM), const),
            pl.BlockSpec((POOL_DIM, D_MODEL), const),
            pl.BlockSpec((SSM_DIM, 2 * D_MODEL), const),
            pl.BlockSpec((D_MODEL, D_MODEL), const),
            pl.BlockSpec((1, D_MODEL), const),
            pl.BlockSpec((D_MODEL, N_EXPERTS), const),
            pl.BlockSpec((1, N_EXPERTS), const),
            pl.BlockSpec((tm, tm), const),
        ],
        out_specs=[
            pl.BlockSpec((tm, D_MODEL), row),
            pl.BlockSpec((tm, D_MODEL), row),
            pl.BlockSpec((tm, LANES), row),
            pl.BlockSpec((tm, LANES), row),
            pl.BlockSpec((tm, LANES), row),
            pl.BlockSpec((1, N_EXPERTS), const),
        ],
        scratch_shapes=[pltpu.VMEM((tm + 2 * POOL_HALO, POOL_DIM), F32), pltpu.VMEM((1, N_EXPERTS), F32)],
    )
    return pl.pallas_call(
        functools.partial(_post_kernel, tm=tm),
        grid_spec=grid_spec,
        out_shape=[
            jax.ShapeDtypeStruct((t, D_MODEL), F32),
            jax.ShapeDtypeStruct((t, D_MODEL), F32),
            jax.ShapeDtypeStruct((t, LANES), I32),
            jax.ShapeDtypeStruct((t, LANES), F32),
            jax.ShapeDtypeStruct((t, LANES), I32),
            jax.ShapeDtypeStruct((1, N_EXPERTS), F32),
        ],
        compiler_params=_cparams(("arbitrary",)),
        name="mixer_merge_route",
    )(*meta, x, attn, u_pool, u_pool, u_pool, y_ssm, gates, mod_l, *wts)


def _expert_kernel(be_ref, nu_ref, tok_ref, tokn_ref, h2_hbm, wg_ref, wu_ref, wd_ref, y_ref,
                   xg_sc, wg_sc, wu_sc, wd_sc, sem, *, bm):
    b = pl.program_id(0)
    n_used = nu_ref[0]
    slot = b % 2

    def row_copy(tok, s, r):
        return pltpu.make_async_copy(h2_hbm.at[pl.ds(tok, 1), :], xg_sc.at[s, pl.ds(r, 1), :], sem.at[s])

    def issue(tok_smem, s):
        def body(r, c):
            row_copy(tok_smem[0, 0, r], s, r).start()
            return c
        lax.fori_loop(0, bm, body, 0)

    @pl.when(b == 0)
    def _():
        issue(tok_ref, 0)

    @pl.when(b + 1 < n_used)
    def _():
        issue(tokn_ref, 1 - slot)

    @pl.when(b < n_used)
    def _():
        def wbody(r, c):
            row_copy(0, slot, r).wait()
            return c
        lax.fori_loop(0, bm, wbody, 0)

        @pl.when((b == 0) | (be_ref[b] != be_ref[jnp.maximum(b - 1, 0)]))
        def _():
            wg_sc[...] = wg_ref[0].astype(BF16)
            wu_sc[...] = wu_ref[0].astype(BF16)
            wd_sc[...] = wd_ref[0].astype(BF16)

        x = xg_sc[slot].astype(BF16)
        g = jnp.dot(x, wg_sc[...], preferred_element_type=F32)
        u = jnp.dot(x, wu_sc[...], preferred_element_type=F32)
        hmid = (g * jax.nn.sigmoid(g) * u).astype(BF16)
        y_ref[...] = jnp.dot(hmid, wd_sc[...], preferred_element_type=F32)

    @pl.when(b >= n_used)
    def _():
        y_ref[...] = jnp.zeros(y_ref.shape, F32)


def _experts(h2, slot_token3, block_expert, n_used, w_gate, w_up, w_down, bm):
    nb = slot_token3.shape[0]
    grid_spec = pltpu.PrefetchScalarGridSpec(
        num_scalar_prefetch=2,
        grid=(nb,),
        in_specs=[
            pl.BlockSpec((1, 1, bm), lambda b, be, nu: (b, 0, 0), memory_space=pltpu.SMEM),
            pl.BlockSpec((1, 1, bm), lambda b, be, nu: (jnp.minimum(b + 1, nb - 1), 0, 0),
                         memory_space=pltpu.SMEM),
            pl.BlockSpec(memory_space=pl.ANY),
            pl.BlockSpec((1, D_MODEL, D_EXPERT), lambda b, be, nu: (be[b], 0, 0)),
            pl.BlockSpec((1, D_MODEL, D_EXPERT), lambda b, be, nu: (be[b], 0, 0)),
            pl.BlockSpec((1, D_EXPERT, D_MODEL), lambda b, be, nu: (be[b], 0, 0)),
        ],
        out_specs=pl.BlockSpec((bm, D_MODEL), lambda b, be, nu: (b, 0)),
        scratch_shapes=[
            pltpu.VMEM((2, bm, D_MODEL), F32),
            pltpu.VMEM((D_MODEL, D_EXPERT), BF16),
            pltpu.VMEM((D_MODEL, D_EXPERT), BF16),
            pltpu.VMEM((D_EXPERT, D_MODEL), BF16),
            pltpu.SemaphoreType.DMA((2,)),
        ],
    )
    return pl.pallas_call(
        functools.partial(_expert_kernel, bm=bm),
        grid_spec=grid_spec,
        out_shape=jax.ShapeDtypeStruct((nb * bm, D_MODEL), F32),
        compiler_params=_cparams(("arbitrary",)),
        name="routed_experts",
    )(block_expert, n_used, slot_token3, slot_token3, h2, w_gate, w_up, w_down)


def _combine_kernel(ts_ref, d_ref, dn_ref, y_hbm, x1_ref, h2_ref, gate_ref, mod_ref, wsg_ref, wsu_ref, wsd_ref,
                    o_ref, yg_sc, sem, *, tc):
    del ts_ref
    i = pl.program_id(0)
    n = pl.num_programs(0)
    slot = i % 2
    n_copies = tc * TOP_K

    def row_copy(src, s, row):
        return pltpu.make_async_copy(y_hbm.at[pl.ds(src, 1), :], yg_sc.at[s, pl.ds(row, 1), :], sem.at[s])

    def issue(dest_smem, s):
        def body(a, c):
            row_copy(dest_smem[0, 0, a], s, (a & (TOP_K - 1)) * tc + (a >> 3)).start()
            return c
        lax.fori_loop(0, n_copies, body, 0)

    @pl.when(i == 0)
    def _():
        issue(d_ref, 0)

    @pl.when(i + 1 < n)
    def _():
        issue(dn_ref, 1 - slot)

    def wbody(a, c):
        row_copy(0, slot, a).wait()
        return c
    lax.fori_loop(0, n_copies, wbody, 0)

    gate = gate_ref[...]
    routed = gate[:, 0:1] * yg_sc[slot, 0:tc, :]
    for k in range(1, TOP_K):
        routed = routed + gate[:, k:k + 1] * yg_sc[slot, k * tc:(k + 1) * tc, :]
    hb = h2_ref[...].astype(BF16)
    g = jnp.dot(hb, wsg_ref[...], preferred_element_type=F32)
    u = jnp.dot(hb, wsu_ref[...], preferred_element_type=F32)
    shared = jnp.dot((g * jax.nn.sigmoid(g) * u).astype(BF16), wsd_ref[...], preferred_element_type=F32)
    o_ref[...] = x1_ref[...] + mod_ref[0][5:6] * (routed + shared)


def _combine(y_buf, dest3, x1, h2, gate, mod_l, ws_gate, ws_up, ws_down, tile_seq, tc):
    t = x1.shape[0]
    n = t // tc
    d_sh = ws_gate.shape[1]
    const = lambda i, ts: (0, 0)
    row = lambda i, ts: (i, 0)
    grid_spec = pltpu.PrefetchScalarGridSpec(
        num_scalar_prefetch=1,
        grid=(n,),
        in_specs=[
            pl.BlockSpec((1, 1, tc * TOP_K), lambda i, ts: (i, 0, 0), memory_space=pltpu.SMEM),
            pl.BlockSpec((1, 1, tc * TOP_K), lambda i, ts: (jnp.minimum(i + 1, n - 1), 0, 0),
                         memory_space=pltpu.SMEM),
            pl.BlockSpec(memory_space=pl.ANY),
            pl.BlockSpec((tc, D_MODEL), row),
            pl.BlockSpec((tc, D_MODEL), row),
            pl.BlockSpec((tc, LANES), row),
            pl.BlockSpec((1, N_ADA, D_MODEL), lambda i, ts: (ts[i], 0, 0)),
            pl.BlockSpec((D_MODEL, d_sh), const),
            pl.BlockSpec((D_MODEL, d_sh), const),
            pl.BlockSpec((d_sh, D_MODEL), const),
        ],
        out_specs=pl.BlockSpec((tc, D_MODEL), row),
        scratch_shapes=[pltpu.VMEM((2, tc * TOP_K, D_MODEL), F32), pltpu.SemaphoreType.DMA((2,))],
    )
    return pl.pallas_call(
        functools.partial(_combine_kernel, tc=tc),
        grid_spec=grid_spec,
        out_shape=jax.ShapeDtypeStruct((t, D_MODEL), F32),
        compiler_params=_cparams(("arbitrary",)),
        name="moe_combine",
    )(tile_seq, dest3, dest3, y_buf, x1, h2, gate, mod_l, ws_gate, ws_up, ws_down)


def _tile_meta(seq_lens, tile):
    seq, first, last, pos0, slen = [], [], [], [], []
    for s, n in enumerate(seq_lens):
        nt = n // tile
        for j in range(nt):
            seq.append(s)
            first.append(int(j == 0))
            last.append(int(j == nt - 1))
            pos0.append(j * tile)
            slen.append(n)
    return tuple(jnp.asarray(np.asarray(a, np.int32)) for a in (seq, first, last, pos0, slen))


def _scan_meta(seq_lens, tr):
    bwd, first = [], []
    base = 0
    for n in seq_lens:
        nt = n // SSM_CHUNK // tr
        for j in range(nt):
            bwd.append(base + nt - 1 - j)
            first.append(int(j == 0))
        base += nt
    return jnp.asarray(np.asarray(bwd, np.int32)), jnp.asarray(np.asarray(first, np.int32))


def _rope_tables(seq_lens):
    quarter = HEAD_DIM // 4
    freqs = ROPE_THETA ** (-jnp.arange(quarter, dtype=F32) / quarter)
    sign = jnp.tile(jnp.concatenate([-jnp.ones((quarter,), F32), jnp.ones((quarter,), F32)]), 2)
    cos_l, sin_l = [], []
    cache = {}
    for n in seq_lens:
        if n not in cache:
            pos = jnp.arange(n)
            ar = (pos // GRID_W).astype(F32)[:, None] * freqs
            ac = (pos % GRID_W).astype(F32)[:, None] * freqs
            ang = jnp.concatenate([ar, ar, ac, ac], axis=-1)
            cache[n] = (jnp.tile(jnp.cos(ang), (1, 2)), jnp.tile(jnp.sin(ang) * sign, (1, 2)))
        cos_l.append(cache[n][0])
        sin_l.append(cache[n][1])
    return jnp.concatenate(cos_l, axis=0), jnp.concatenate(sin_l, axis=0)


def _block_diag_ones(n, blk):
    r = np.arange(n) // blk
    return jnp.asarray((r[:, None] == r[None, :]).astype(np.float32)).astype(BF16)


def _moe_plan(idx, rank, counts, t, bm, nb):
    cnt = counts.reshape(N_EXPERTS).astype(I32)
    padded = (cnt + bm - 1) // bm * bm
    pend = jnp.cumsum(padded)
    pstart = pend - padded
    n_used = (pend[-1] // bm).astype(I32).reshape(1)
    block_expert = jnp.minimum(
        jnp.searchsorted(pend, jnp.arange(nb, dtype=I32) * bm, side='right'), N_EXPERTS - 1).astype(I32)
    dest = pstart[idx] + rank
    slot_token = jnp.zeros((nb * bm,), I32).at[dest.reshape(-1)].set(
        jnp.arange(t * TOP_K, dtype=I32) // TOP_K)
    return dest, slot_token, block_expert, n_used


def kernel(x_prompt, x_sample, c_prompt, c_sample, w_ada, b_ada, norm1_g, w_in, q_norm_g, k_norm_g, w_attn_o, pool_w, pool_scale, w_pool_o, ssm_a_re, ssm_a_im, ssm_log_dt, ssm_b_re, ssm_b_im, ssm_c_re, ssm_c_im, ssm_d, w_glu, w_out, norm2_g, w_router, b_router, w_exp_gate, w_exp_up, w_exp_down, w_sh_gate, w_sh_up, w_sh_down):
    b1, l1, d = x_prompt.shape
    b2, l2, _ = x_sample.shape
    depth = w_in.shape[0]
    assert d == D_MODEL and b1 + b2 <= MOD_ROWS
    seq_lens = [l1] * b1 + [l2] * b2
    t = b1 * l1 + b2 * l2
    lmin = min(l1, l2)

    tm_in = _pick(lmin, 512)
    tm_post = _pick(lmin, 256)
    tc_comb = _pick(lmin, 128)
    tr_scan = _pick(lmin // SSM_CHUNK, 256)
    bm = 256
    nb = -(-(t * TOP_K + N_EXPERTS * (bm - 1)) // bm)

    x = jnp.concatenate([x_prompt.reshape(b1 * l1, d), x_sample.reshape(b2 * l2, d)], axis=0)
    c_all = jnp.concatenate([c_prompt, c_sample, jnp.zeros((MOD_ROWS - b1 - b2, d), F32)], axis=0)
    mod = _modulation(c_all, w_ada, b_ada).reshape(depth, MOD_ROWS, N_ADA, d)

    cos_t, sin_t = _rope_tables(seq_lens)
    ones_bd = _block_diag_ones(ATTN_DIM, HEAD_DIM)
    tril = jnp.asarray(np.tril(np.ones((tm_post, tm_post), np.float32), -1)).astype(BF16)
    meta_in = _tile_meta(seq_lens, tm_in)
    meta_post = _tile_meta(seq_lens, tm_post)
    meta_comb = _tile_meta(seq_lens, tc_comb)
    bwd_tile, first_tile = _scan_meta(seq_lens, tr_scan)

    for l in range(depth):
        q8, k2, v2, u_pool, u_ssm, gates = _inproj(
            x, mod[l], norm1_g[l].reshape(1, d), w_in[l].astype(BF16),
            jnp.tile(q_norm_g[l], N_Q_HEADS).reshape(1, ATTN_DIM),
            jnp.tile(k_norm_g[l], N_KV_HEADS).reshape(1, KV_DIM),
            cos_t, sin_t, ones_bd, meta_in[0], tm_in)
        attn = jnp.concatenate(
            [_flash(q8, k2, v2, 0, b1, l1), _flash(q8, k2, v2, b1 * l1, b2, l2)], axis=0)
        tables = _ssm_tables(ssm_a_re[l], ssm_a_im[l], ssm_log_dt[l], ssm_b_re[l], ssm_b_im[l],
                             ssm_c_re[l], ssm_c_im[l], ssm_d[l])
        y_ssm = _bidir_ssm(u_ssm, tables, bwd_tile, first_tile, tr_scan)

        pool_bd = jax.scipy.linalg.block_diag(*[pool_w[l, g] for g in range(len(POOL_WINDOWS))])
        wts = (w_attn_o[l].astype(BF16), pool_bd.astype(BF16), pool_scale[l].reshape(1, POOL_DIM),
               w_pool_o[l].astype(BF16), w_glu[l].astype(BF16), w_out[l].astype(BF16),
               norm2_g[l].reshape(1, d), w_router[l], b_router[l].reshape(1, N_EXPERTS), tril)
        x1, h2, idx, gate, rank, counts = _post(x, attn, u_pool, y_ssm, gates, mod[l], wts, meta_post, tm_post)

        dest, slot_token, block_expert, n_used = _moe_plan(idx[:, :TOP_K], rank[:, :TOP_K], counts, t, bm, nb)
        y_buf = _experts(h2, slot_token.reshape(nb, 1, bm), block_expert, n_used,
                         w_exp_gate[l], w_exp_up[l], w_exp_down[l], bm)
        x = _combine(y_buf, dest.reshape(t // tc_comb, 1, tc_comb * TOP_K), x1, h2, gate, mod[l],
                     w_sh_gate[l].astype(BF16), w_sh_up[l].astype(BF16), w_sh_down[l].astype(BF16),
                     meta_comb[0], tc_comb)

    return (x[:b1 * l1].reshape(b1, l1, d), x[b1 * l1:].reshape(b2, l2, d))
```

```python
import functools
import math

import jax
import jax.numpy as jnp
import numpy as np
from jax import lax
from jax.experimental import pallas as pl
from jax.experimental.pallas import tpu as pltpu

F32 = jnp.float32
BF16 = jnp.bfloat16
I32 = jnp.int32
HIGHEST = lax.Precision.HIGHEST

D_MODEL = 1024
GRID_W = 64
HEAD_DIM = 64
N_Q_HEADS = 8
N_KV_HEADS = 2
Q_PER_KV = N_Q_HEADS // N_KV_HEADS
ATTN_DIM = N_Q_HEADS * HEAD_DIM
KV_DIM = N_KV_HEADS * HEAD_DIM
ROPE_THETA = 10000.0
POOL_WINDOWS = (2, 4, 8, 16)
POOL_DIM = 256
POOL_GROUP = 64
POOL_HALO = 8
SSM_DIM = 256
SSM_P = 16
SSM_G = 16
SSM_N = 64
SSM_CHUNK = 16
SSM_ROW = SSM_CHUNK * SSM_P
SSM_STATE = 2 * SSM_N
N_EXPERTS = 256
TOP_K = 8
N_EXPERT_GROUPS = 8
GROUP_SIZE = N_EXPERTS // N_EXPERT_GROUPS
TOPK_GROUPS = 4
D_EXPERT = 256
ROUTED_SCALE = 2.5
N_ADA = 6
EPS = 1e-6
IN_DIM = ATTN_DIM + 2 * KV_DIM + POOL_DIM + SSM_DIM + 3 * D_MODEL
OFF_K = ATTN_DIM
OFF_V = OFF_K + KV_DIM
OFF_POOL = OFF_V + KV_DIM
OFF_SSM = OFF_POOL + POOL_DIM
OFF_GATES = OFF_SSM + SSM_DIM
MOD_ROWS = 8

V7X_VMEM_BYTES = 64 * 1024 * 1024
VMEM_LIMIT = V7X_VMEM_BYTES - 8 * 1024 * 1024
LANES = 128


def _cparams(sem):
    return pltpu.CompilerParams(dimension_semantics=sem, vmem_limit_bytes=VMEM_LIMIT)


def _pick(n, pref):
    t = min(n, pref)
    while n % t:
        t //= 2
    return t


ROW_TILE = D_MODEL // LANES


def _store_row_tiles(ref, x):
    n = x.shape[0]
    for s in range(ROW_TILE):
        ref[pl.ds(s, n, stride=ROW_TILE), :] = x[:, s * LANES:(s + 1) * LANES]


def _load_row_tiles(ref, first_row, n):
    return jnp.concatenate(
        [ref[pl.ds(first_row * ROW_TILE + s, n, stride=ROW_TILE), :] for s in range(ROW_TILE)], axis=1)


def _mod_kernel(c_ref, w_ref, b_ref, o_ref):
    c = c_ref[...]
    a = c * jax.nn.sigmoid(c)
    o_ref[0] = jnp.dot(a, w_ref[0], preferred_element_type=F32, precision=HIGHEST) + b_ref[0]


def _modulation(c_all, w_ada, b_ada):
    depth, d, n = w_ada.shape
    bn = _pick(n, 1536)
    return pl.pallas_call(
        _mod_kernel,
        grid=(depth, n // bn),
        in_specs=[
            pl.BlockSpec((MOD_ROWS, d), lambda l, j: (0, 0)),
            pl.BlockSpec((1, d, bn), lambda l, j: (l, 0, j)),
            pl.BlockSpec((1, 1, bn), lambda l, j: (l, 0, j)),
        ],
        out_specs=pl.BlockSpec((1, MOD_ROWS, bn), lambda l, j: (l, 0, j)),
        out_shape=jax.ShapeDtypeStruct((depth, MOD_ROWS, n), F32),
        compiler_params=_cparams(("arbitrary", "arbitrary")),
        name="adaln_mod",
    )(c_all, w_ada, b_ada.reshape(depth, 1, n))


def _rmsnorm_mod(x, g, scale, shift):
    y = x * lax.rsqrt(jnp.mean(x * x, axis=-1, keepdims=True) + EPS)
    return (y * g) * (1.0 + scale) + shift


def _head_norm_rope(z, gain, cos, sin_signed, ones_bd):
    z2 = z * z
    hi = z2.astype(BF16)
    lo = (z2 - hi.astype(F32)).astype(BF16)
    ss = (jnp.dot(hi, ones_bd, preferred_element_type=F32)
          + jnp.dot(lo, ones_bd, preferred_element_type=F32))
    y = (z * lax.rsqrt(ss * (1.0 / HEAD_DIM) + EPS)) * gain
    w = z.shape[1]
    quarter = HEAD_DIM // 4
    from_right = pltpu.roll(y, w - quarter, 1)
    from_left = pltpu.roll(y, quarter, 1)
    lane = lax.broadcasted_iota(I32, y.shape, 1)
    rot = jnp.where((lane & quarter) == 0, from_right, from_left)
    return y * cos + rot * sin_signed


def _inproj_kernel(ts_ref, x_ref, mod_ref, g_ref, w_ref, gq_ref, gk_ref, cos_ref, sin_ref, ones_ref,
                   q_ref, k_ref, v_ref, up_ref, us_ref, gt_ref):
    del ts_ref
    x = x_ref[...]
    mod = mod_ref[0]
    h = _rmsnorm_mod(x, g_ref[...], mod[1:2], mod[0:1]).astype(BF16)

    cos = cos_ref[...]
    sin = sin_ref[...]
    zq = jnp.dot(h, w_ref[:, 0:ATTN_DIM], preferred_element_type=F32)
    reps = ATTN_DIM // LANES
    yq = _head_norm_rope(zq, gq_ref[...], jnp.concatenate([cos] * reps, axis=1),
                         jnp.concatenate([sin] * reps, axis=1), ones_ref[...]) * (HEAD_DIM ** -0.5)
    for hh in range(N_Q_HEADS):
        q_ref[hh] = yq[:, hh * HEAD_DIM:(hh + 1) * HEAD_DIM].astype(BF16)

    zkv = jnp.dot(h, w_ref[:, OFF_K:OFF_POOL], preferred_element_type=F32)
    yk = _head_norm_rope(zkv[:, :KV_DIM], gk_ref[...], cos, sin, ones_ref[0:KV_DIM, 0:KV_DIM])
    zv = zkv[:, KV_DIM:]
    for j in range(N_KV_HEADS):
        k_ref[j] = yk[:, j * HEAD_DIM:(j + 1) * HEAD_DIM].astype(BF16)
        v_ref[j] = zv[:, j * HEAD_DIM:(j + 1) * HEAD_DIM].astype(BF16)

    zps = jnp.dot(h, w_ref[:, OFF_POOL:OFF_GATES], preferred_element_type=F32)
    up_ref[...] = zps[:, :POOL_DIM]
    us_ref[...] = zps[:, POOL_DIM:]
    for c in range(3):
        zg = jnp.dot(h, w_ref[:, OFF_GATES + c * D_MODEL:OFF_GATES + (c + 1) * D_MODEL],
                     preferred_element_type=F32)
        gt_ref[:, c * D_MODEL:(c + 1) * D_MODEL] = jax.nn.sigmoid(zg).astype(BF16)


def _inproj(x, mod_l, g1, w_in_bf, gq_t, gk_t, cos_t, sin_t, ones_bd, tile_seq, tm):
    t = x.shape[0]
    const = lambda i, ts: (0, 0)
    row = lambda i, ts: (i, 0)
    grid_spec = pltpu.PrefetchScalarGridSpec(
        num_scalar_prefetch=1,
        grid=(t // tm,),
        in_specs=[
            pl.BlockSpec((tm, D_MODEL), row),
            pl.BlockSpec((1, N_ADA, D_MODEL), lambda i, ts: (ts[i], 0, 0)),
            pl.BlockSpec((1, D_MODEL), const),
            pl.BlockSpec((D_MODEL, IN_DIM), const),
            pl.BlockSpec((1, ATTN_DIM), const),
            pl.BlockSpec((1, KV_DIM), const),
            pl.BlockSpec((tm, LANES), row),
            pl.BlockSpec((tm, LANES), row),
            pl.BlockSpec((ATTN_DIM, ATTN_DIM), const),
        ],
        out_specs=[
            pl.BlockSpec((N_Q_HEADS, tm, HEAD_DIM), lambda i, ts: (0, i, 0)),
            pl.BlockSpec((N_KV_HEADS, tm, HEAD_DIM), lambda i, ts: (0, i, 0)),
            pl.BlockSpec((N_KV_HEADS, tm, HEAD_DIM), lambda i, ts: (0, i, 0)),
            pl.BlockSpec((tm, POOL_DIM), row),
            pl.BlockSpec((tm, SSM_DIM), row),
            pl.BlockSpec((tm, 3 * D_MODEL), row),
        ],
    )
    return pl.pallas_call(
        _inproj_kernel,
        grid_spec=grid_spec,
        out_shape=[
            jax.ShapeDtypeStruct((N_Q_HEADS, t, HEAD_DIM), BF16),
            jax.ShapeDtypeStruct((N_KV_HEADS, t, HEAD_DIM), BF16),
            jax.ShapeDtypeStruct((N_KV_HEADS, t, HEAD_DIM), BF16),
            jax.ShapeDtypeStruct((t, POOL_DIM), F32),
            jax.ShapeDtypeStruct((t, SSM_DIM), F32),
            jax.ShapeDtypeStruct((t, 3 * D_MODEL), BF16),
        ],
        compiler_params=_cparams(("arbitrary",)),
        name="norm1_inproj",
    )(tile_seq, x, mod_l, g1, w_in_bf, gq_t, gk_t, cos_t, sin_t, ones_bd)


def _flash_kernel(q_ref, k_ref, v_ref, o_ref, m_sc, l_sc, acc_sc, *, tq, tk):
    kb = pl.program_id(3)

    @pl.when(kb == 0)
    def _():
        m_sc[...] = jnp.full(m_sc.shape, -jnp.inf, F32)
        l_sc[...] = jnp.zeros(l_sc.shape, F32)
        acc_sc[...] = jnp.zeros(acc_sc.shape, F32)

    q = q_ref[...].reshape(Q_PER_KV * tq, HEAD_DIM)
    s = lax.dot_general(q, k_ref[0], (((1,), (1,)), ((), ())), preferred_element_type=F32)
    m_prev = m_sc[...]
    m_new = jnp.maximum(m_prev, jnp.max(s, axis=1, keepdims=True))
    alpha = jnp.exp(m_prev - m_new)
    p = jnp.exp(s - jnp.concatenate([m_new] * (tk // LANES), axis=1))
    l_sc[...] = alpha * l_sc[...] + jnp.sum(p, axis=1, keepdims=True)
    m_sc[...] = m_new
    acc_sc[...] = acc_sc[...] * alpha[:, :HEAD_DIM] + jnp.dot(
        p.astype(BF16), v_ref[0], preferred_element_type=F32)

    @pl.when(kb == pl.num_programs(3) - 1)
    def _():
        o = acc_sc[...] / l_sc[:, :HEAD_DIM]
        for hh in range(Q_PER_KV):
            o_ref[:, hh * HEAD_DIM:(hh + 1) * HEAD_DIM] = o[hh * tq:(hh + 1) * tq].astype(BF16)


def _flash(q8, k2, v2, off, bsz, seq):
    tq = _pick(seq, 256)
    tk = _pick(seq, 512)
    nq, nk = seq // tq, seq // tk
    oq, ok = off // tq, off // tk
    rows = Q_PER_KV * tq
    return pl.pallas_call(
        functools.partial(_flash_kernel, tq=tq, tk=tk),
        grid=(N_KV_HEADS, bsz, nq, nk),
        in_specs=[
            pl.BlockSpec((Q_PER_KV, tq, HEAD_DIM), lambda j, b, i, kb: (j, oq + b * nq + i, 0)),
            pl.BlockSpec((1, tk, HEAD_DIM), lambda j, b, i, kb: (j, ok + b * nk + kb, 0)),
            pl.BlockSpec((1, tk, HEAD_DIM), lambda j, b, i, kb: (j, ok + b * nk + kb, 0)),
        ],
        out_specs=pl.BlockSpec((tq, Q_PER_KV * HEAD_DIM), lambda j, b, i, kb: (b * nq + i, j)),
        out_shape=jax.ShapeDtypeStruct((bsz * seq, ATTN_DIM), BF16),
        scratch_shapes=[
            pltpu.VMEM((rows, LANES), F32),
            pltpu.VMEM((rows, LANES), F32),
            pltpu.VMEM((rows, HEAD_DIM), F32),
        ],
        compiler_params=_cparams(("parallel", "parallel", "parallel", "arbitrary")),
        name="flash_gqa",
    )(q8, k2, v2)


def _ssm_tables(a_re, a_im, log_dt, b_re, b_im, c_re, c_im, d_skip):
    tc = SSM_CHUNK
    a = lax.complex(a_re.astype(F32), a_im.astype(F32))
    dt = jnp.exp(log_dt.astype(F32))[..., None]
    adt = a * dt
    a_bar = jnp.exp(adt)
    b_bar = ((a_bar - 1.0) / a)[..., None] * lax.complex(b_re.astype(F32), b_im.astype(F32))
    c_mat = lax.complex(c_re.astype(F32), c_im.astype(F32))
    taus = jnp.arange(tc + 1, dtype=F32)
    pw = jnp.exp(adt[None] * taus[:, None, None, None])

    kern = jnp.real(jnp.einsum('dgpn,tdgn,dgnq->dgtpq', c_mat, pw[:tc], b_bar))
    s_idx = jnp.arange(tc)[:, None]
    t_idx = jnp.arange(tc)[None, :]
    lag_f = jnp.clip(t_idx - s_idx, 0, tc - 1)
    lag_b = jnp.clip(s_idx - t_idx, 0, tc - 1)
    kf = jnp.where((t_idx >= s_idx)[None, :, :, None, None], kern[0][:, lag_f], 0.0)
    kb = jnp.where((s_idx >= t_idx)[None, :, :, None, None], kern[1][:, lag_b], 0.0)
    w_loc = (kf + kb).transpose(0, 1, 4, 2, 3).reshape(SSM_G, SSM_ROW, SSM_ROW)
    w_loc = w_loc + jnp.eye(SSM_ROW, dtype=F32)[None] * jnp.tile(
        d_skip.astype(F32).reshape(SSM_G, 1, SSM_P), (1, tc, 1)).reshape(SSM_G, 1, SSM_ROW)

    def ri(z, axis):
        return jnp.concatenate([jnp.real(z), jnp.imag(z)], axis=axis)

    pf = jnp.einsum('sgn,gnq->gsqn', pw[:tc][::-1, 0], b_bar[0])
    pb = jnp.einsum('sgn,gnq->gsqn', pw[:tc, 1], b_bar[1])
    p_st = jnp.concatenate([ri(pf, -1), ri(pb, -1)], axis=-1).reshape(SSM_G, SSM_ROW, 2 * SSM_STATE)
    qf = jnp.einsum('gpn,tgn->gntp', c_mat[0], pw[1:tc + 1, 0])
    qb = jnp.einsum('gpn,tgn->gntp', c_mat[1], pw[1:tc + 1, 1][::-1])
    q_f = jnp.concatenate([jnp.real(qf), -jnp.imag(qf)], axis=1).reshape(SSM_G, SSM_STATE, SSM_ROW)
    q_b = jnp.concatenate([jnp.real(qb), -jnp.imag(qb)], axis=1).reshape(SSM_G, SSM_STATE, SSM_ROW)
    a16 = pw[tc]
    a1 = jnp.concatenate([jnp.real(a16), jnp.real(a16)], axis=-1).reshape(2, SSM_G * SSM_STATE)
    a2 = jnp.concatenate([-jnp.imag(a16), jnp.imag(a16)], axis=-1).reshape(2, SSM_G * SSM_STATE)
    a_mul = jnp.stack([a1[0], a2[0], a1[1], a2[1]], axis=0)
    return w_loc.astype(BF16), p_st.astype(BF16), q_f.astype(BF16), q_b.astype(BF16), a_mul


def _ssm_state_kernel(u_ref, p_ref, s_ref):
    s = jnp.dot(u_ref[0].astype(BF16), p_ref[0], preferred_element_type=F32)
    s_ref[0] = s[:, :SSM_STATE]
    s_ref[1] = s[:, SSM_STATE:]


def _ssm_states(ug, p_st, tc):
    g, nc, _ = ug.shape
    return pl.pallas_call(
        _ssm_state_kernel,
        grid=(g, nc // tc),
        in_specs=[
            pl.BlockSpec((1, tc, SSM_ROW), lambda gi, i: (gi, i, 0)),
            pl.BlockSpec((1, SSM_ROW, 2 * SSM_STATE), lambda gi, i: (gi, 0, 0)),
        ],
        out_specs=pl.BlockSpec((2, tc, SSM_STATE), lambda gi, i: (0, i, gi)),
        out_shape=jax.ShapeDtypeStruct((2, nc, g * SSM_STATE), F32),
        compiler_params=_cparams(("arbitrary", "arbitrary")),
        name="ssm_chunk_states",
    )(ug, p_st)


def _swap_halves(x):
    parts = [pltpu.roll(x[:, g * LANES:(g + 1) * LANES], LANES // 2, 1) for g in range(x.shape[1] // LANES)]
    return jnp.concatenate(parts, axis=1)


def _ssm_scan_kernel(bt_ref, first_ref, sf_ref, sb_ref, am_ref, xf_ref, zb_ref, st_sc, sfw_sc, sbw_sc, *, tr):
    n = pl.program_id(0)
    del bt_ref

    @pl.when(first_ref[n] == 1)
    def _():
        st_sc[...] = jnp.zeros(st_sc.shape, F32)

    sfw_sc[...] = _swap_halves(sf_ref[0])
    sbw_sc[...] = _swap_halves(sb_ref[0])
    a1f, a2f, a1b, a2b = am_ref[0:1], am_ref[1:2], am_ref[2:3], am_ref[3:4]

    def body(r, carry):
        xf, xfw, zb, zbw = carry
        xf_ref[pl.ds(r, 1), :] = xf
        sf = sf_ref[0, pl.ds(r, 1), :]
        sfw = sfw_sc[pl.ds(r, 1), :]
        nxf = a1f * xf + a2f * xfw + sf
        nxfw = a1f * xfw - a2f * xf + sfw
        rb = tr - 1 - r
        zb_ref[pl.ds(rb, 1), :] = zb
        sb = sb_ref[0, pl.ds(rb, 1), :]
        sbw = sbw_sc[pl.ds(rb, 1), :]
        nzb = a1b * zb + a2b * zbw + sb
        nzbw = a1b * zbw - a2b * zb + sbw
        return nxf, nxfw, nzb, nzbw

    init = (st_sc[0:1], st_sc[1:2], st_sc[2:3], st_sc[3:4])
    xf, xfw, zb, zbw = lax.fori_loop(0, tr, body, init)
    st_sc[0:1] = xf
    st_sc[1:2] = xfw
    st_sc[2:3] = zb
    st_sc[3:4] = zbw


def _ssm_scan(s_st, a_mul, bwd_tile, first_tile, tr):
    _, nc, w = s_st.shape
    grid_spec = pltpu.PrefetchScalarGridSpec(
        num_scalar_prefetch=2,
        grid=(nc // tr,),
        in_specs=[
            pl.BlockSpec((1, tr, w), lambda n, bt, ft: (0, n, 0)),
            pl.BlockSpec((1, tr, w), lambda n, bt, ft: (1, bt[n], 0)),
            pl.BlockSpec((4, w), lambda n, bt, ft: (0, 0)),
        ],
        out_specs=[
            pl.BlockSpec((tr, w), lambda n, bt, ft: (n, 0)),
            pl.BlockSpec((tr, w), lambda n, bt, ft: (bt[n], 0)),
        ],
        scratch_shapes=[pltpu.VMEM((4, w), F32), pltpu.VMEM((tr, w), F32), pltpu.VMEM((tr, w), F32)],
    )
    return pl.pallas_call(
        functools.partial(_ssm_scan_kernel, tr=tr),
        grid_spec=grid_spec,
        out_shape=[jax.ShapeDtypeStruct((nc, w), F32), jax.ShapeDtypeStruct((nc, w), F32)],
        compiler_params=_cparams(("arbitrary",)),
        name="ssm_chunk_scan",
    )(bwd_tile, first_tile, s_st, s_st, a_mul)


def _ssm_out_kernel(u_ref, xf_ref, zb_ref, w_ref, qf_ref, qb_ref, y_ref):
    y = jnp.dot(u_ref[0].astype(BF16), w_ref[0], preferred_element_type=F32)
    y += jnp.dot(xf_ref[...].astype(BF16), qf_ref[0], preferred_element_type=F32)
    y += jnp.dot(zb_ref[...].astype(BF16), qb_ref[0], preferred_element_type=F32)
    y_ref[0] = y


def _ssm_outputs(ug, xf, zb, w_loc, q_f, q_b, tc):
    g, nc, _ = ug.shape
    return pl.pallas_call(
        _ssm_out_kernel,
        grid=(g, nc // tc),
        in_specs=[
            pl.BlockSpec((1, tc, SSM_ROW), lambda gi, i: (gi, i, 0)),
            pl.BlockSpec((tc, SSM_STATE), lambda gi, i: (i, gi)),
            pl.BlockSpec((tc, SSM_STATE), lambda gi, i: (i, gi)),
            pl.BlockSpec((1, SSM_ROW, SSM_ROW), lambda gi, i: (gi, 0, 0)),
            pl.BlockSpec((1, SSM_STATE, SSM_ROW), lambda gi, i: (gi, 0, 0)),
            pl.BlockSpec((1, SSM_STATE, SSM_ROW), lambda gi, i: (gi, 0, 0)),
        ],
        out_specs=pl.BlockSpec((1, tc, SSM_ROW), lambda gi, i: (gi, i, 0)),
        out_shape=jax.ShapeDtypeStruct((g, nc, SSM_ROW), F32),
        compiler_params=_cparams(("arbitrary", "arbitrary")),
        name="ssm_chunk_outputs",
    )(ug, xf, zb, w_loc, q_f, q_b)


def _bidir_ssm(u_ssm, tables, bwd_tile, first_tile, tr):
    w_loc, p_st, q_f, q_b, a_mul = tables
    t = u_ssm.shape[0]
    nc = t // SSM_CHUNK
    ug = u_ssm.reshape(nc, SSM_CHUNK, SSM_G, SSM_P).transpose(2, 0, 1, 3).reshape(SSM_G, nc, SSM_ROW)
    tc = _pick(nc, 512)
    s_st = _ssm_states(ug, p_st, tc)
    xf, zb = _ssm_scan(s_st, a_mul, bwd_tile, first_tile, tr)
    yg = _ssm_outputs(ug, xf, zb, w_loc, q_f, q_b, tc)
    return yg.reshape(SSM_G, nc, SSM_CHUNK, SSM_P).transpose(1, 2, 0, 3).reshape(t, SSM_DIM)


def _pool_mixer(ext_sc, u, pos, seq_len, tm):
    halves = []
    for half in range(2):
        sums = []
        for w in POOL_WINDOWS[2 * half:2 * half + 2]:
            lo = w // 2
            hi = w - lo - 1
            acc = ext_sc[pl.ds(POOL_HALO - lo, tm), half * LANES:(half + 1) * LANES]
            for d in range(-lo + 1, hi + 1):
                acc = acc + ext_sc[pl.ds(POOL_HALO + d, tm), half * LANES:(half + 1) * LANES]
            cnt = (jnp.minimum(pos + hi, seq_len - 1) - jnp.maximum(pos - lo, 0) + 1).astype(F32)
            sums.append(acc / cnt)
        lane = lax.broadcasted_iota(I32, (tm, LANES), 1)
        halves.append(jnp.where(lane < POOL_GROUP, sums[0], sums[1]))
    return jnp.concatenate(halves, axis=1) - u


def _route(h2, wr_ref, br_ref):
    logits = jnp.dot(h2, wr_ref[...], preferred_element_type=F32, precision=HIGHEST)
    scores = jax.nn.sigmoid(logits)
    choice = scores + br_ref[...]
    lane = lax.broadcasted_iota(I32, choice.shape, 1)
    neg = jnp.float32(-jnp.inf)

    def first_argmax(c):
        m = jnp.max(c, axis=1, keepdims=True)
        return jnp.min(jnp.where(c == m, lane, N_EXPERTS), axis=1, keepdims=True), m

    in_group, gscore = [], []
    for g in range(N_EXPERT_GROUPS):
        ing = (lane >= g * GROUP_SIZE) & (lane < (g + 1) * GROUP_SIZE)
        cg = jnp.where(ing, choice, neg)
        i1, m1 = first_argmax(cg)
        m2 = jnp.max(jnp.where(lane == i1, neg, cg), axis=1, keepdims=True)
        in_group.append(ing)
        gscore.append(m1 + m2)
    emask = jnp.zeros(choice.shape, jnp.bool_)
    for g in range(N_EXPERT_GROUPS):
        ahead = jnp.zeros(gscore[g].shape, I32)
        for o in range(N_EXPERT_GROUPS):
            if o == g:
                continue
            beats = (gscore[o] > gscore[g]) | ((gscore[o] == gscore[g]) & (o < g))
            ahead = ahead + beats.astype(I32)
        emask = emask | (in_group[g] & (ahead < TOPK_GROUPS))
    c = jnp.where(emask, choice, neg)
    picks = []
    for _ in range(TOP_K):
        i, _m = first_argmax(c)
        oh = lane == i
        w = jnp.sum(jnp.where(oh, scores, 0.0), axis=1, keepdims=True)
        c = jnp.where(oh, neg, c)
        picks.append((i, w, oh))
    wsum = picks[0][1]
    for _, w, _ in picks[1:]:
        wsum = wsum + w
    return [(i, w / wsum * ROUTED_SCALE, oh) for i, w, oh in picks]


def _post_kernel(ts_ref, tf_ref, tl_ref, tp_ref, tn_ref,
                 x_ref, at_ref, upp_ref, up_ref, upn_ref, ys_ref, gt_ref, mod_ref,
                 wao_ref, pbd_ref, psc_ref, wpo_ref, wgl_ref, wo_ref, g2_ref, wr_ref, br_ref, tril_ref,
                 x1_ref, h2_ref, idx_ref, gate_ref, rank_ref, cnt_ref, ext_sc, carry_sc, *, tm):
    del ts_ref
    i = pl.program_id(0)
    mod = mod_ref[0]

    attn = jnp.dot(at_ref[...], wao_ref[...], preferred_element_type=F32)

    u = up_ref[...]
    ext_sc[0:POOL_HALO] = jnp.where(tf_ref[i] == 1, 0.0, upp_ref[...])
    ext_sc[POOL_HALO:POOL_HALO + tm] = u
    ext_sc[POOL_HALO + tm:2 * POOL_HALO + tm] = jnp.where(tl_ref[i] == 1, 0.0, upn_ref[...])
    pos = tp_ref[i] + lax.broadcasted_iota(I32, (tm, LANES), 0)
    pm = _pool_mixer(ext_sc, u, pos, tn_ref[i], tm)
    pool = jnp.dot(pm.astype(BF16), pbd_ref[...], preferred_element_type=F32) * psc_ref[...]
    pool = jnp.dot(pool.astype(BF16), wpo_ref[...], preferred_element_type=F32)

    z = jnp.dot(jax.nn.gelu(ys_ref[...]).astype(BF16), wgl_ref[...], preferred_element_type=F32)
    ssm = z[:, :D_MODEL] * jax.nn.sigmoid(z[:, D_MODEL:])

    merged = (gt_ref[:, 0:D_MODEL].astype(F32) * attn
              + gt_ref[:, D_MODEL:2 * D_MODEL].astype(F32) * pool
              + gt_ref[:, 2 * D_MODEL:3 * D_MODEL].astype(F32) * ssm)
    mix = jnp.dot(merged.astype(BF16), wo_ref[...], preferred_element_type=F32)
    x1 = x_ref[...] + mod[2:3] * mix
    x1_ref[...] = x1
    h2 = _rmsnorm_mod(x1, g2_ref[...], mod[4:5], mod[3:4])
    _store_row_tiles(h2_ref, h2)

    picks = _route(h2, wr_ref, br_ref)

    @pl.when(i == 0)
    def _():
        carry_sc[...] = jnp.zeros(carry_sc.shape, F32)

    sel = picks[0][2]
    for _, _, oh in picks[1:]:
        sel = sel | oh
    sel_f = sel.astype(F32)
    ranks = jnp.dot(tril_ref[...], sel_f.astype(BF16), preferred_element_type=F32) + carry_sc[...]
    carry = carry_sc[...] + jnp.sum(sel_f, axis=0, keepdims=True)
    carry_sc[...] = carry
    cnt_ref[...] = carry

    lane = lax.broadcasted_iota(I32, (tm, LANES), 1)
    idx_o = jnp.zeros((tm, LANES), I32)
    gate_o = jnp.zeros((tm, LANES), F32)
    rank_o = jnp.zeros((tm, LANES), I32)
    for k, (ik, gk, oh) in enumerate(picks):
        rk = jnp.sum(jnp.where(oh, ranks, 0.0), axis=1, keepdims=True).astype(I32)
        idx_o = jnp.where(lane == k, ik, idx_o)
        gate_o = jnp.where(lane == k, gk, gate_o)
        rank_o = jnp.where(lane == k, rk, rank_o)
    idx_ref[...] = idx_o
    gate_ref[...] = gate_o
    rank_ref[...] = rank_o


def _post(x, attn, u_pool, y_ssm, gates, mod_l, wts, meta, tm):
    t = x.shape[0]
    nh = t // POOL_HALO
    hb = tm // POOL_HALO
    npf = 5
    const = lambda i, *_: (0, 0)
    row = lambda i, *_: (i, 0)
    grid_spec = pltpu.PrefetchScalarGridSpec(
        num_scalar_prefetch=npf,
        grid=(t // tm,),
        in_specs=[
            pl.BlockSpec((tm, D_MODEL), row),
            pl.BlockSpec((tm, ATTN_DIM), row),
            pl.BlockSpec((POOL_HALO, POOL_DIM), lambda i, *_: (jnp.maximum(i * hb - 1, 0), 0)),
            pl.BlockSpec((tm, POOL_DIM), row),
            pl.BlockSpec((POOL_HALO, POOL_DIM), lambda i, *_: (jnp.minimum((i + 1) * hb, nh - 1), 0)),
            pl.BlockSpec((tm, SSM_DIM), row),
            pl.BlockSpec((tm, 3 * D_MODEL), row),
            pl.BlockSpec((1, N_ADA, D_MODEL), lambda i, ts, *_: (ts[i], 0, 0)),
            pl.BlockSpec((ATTN_DIM, D_MODEL), const),
            pl.BlockSpec((POOL_DIM, POOL_DIM), const),
            pl.BlockSpec((1, POOL_DIM), const),
            pl.BlockSpec((POOL_DIM, D_MODEL), const),
            pl.BlockSpec((SSM_DIM, 2 * D_MODEL), const),
            pl.BlockSpec((D_MODEL, D_MODEL), const),
            pl.BlockSpec((1, D_MODEL), const),
            pl.BlockSpec((D_MODEL, N_EXPERTS), const),
            pl.BlockSpec((1, N_EXPERTS), const),
            pl.BlockSpec((tm, tm), const),
        ],
        out_specs=[
            pl.BlockSpec((tm, D_MODEL), row),
            pl.BlockSpec((tm * ROW_TILE, LANES), row),
            pl.BlockSpec((tm, LANES), row),
            pl.BlockSpec((tm, LANES), row),
            pl.BlockSpec((tm, LANES), row),
            pl.BlockSpec((1, N_EXPERTS), const),
        ],
        scratch_shapes=[pltpu.VMEM((tm + 2 * POOL_HALO, POOL_DIM), F32), pltpu.VMEM((1, N_EXPERTS), F32)],
    )
    return pl.pallas_call(
        functools.partial(_post_kernel, tm=tm),
        grid_spec=grid_spec,
        out_shape=[
            jax.ShapeDtypeStruct((t, D_MODEL), F32),
            jax.ShapeDtypeStruct((t * ROW_TILE, LANES), F32),
            jax.ShapeDtypeStruct((t, LANES), I32),
            jax.ShapeDtypeStruct((t, LANES), F32),
            jax.ShapeDtypeStruct((t, LANES), I32),
            jax.ShapeDtypeStruct((1, N_EXPERTS), F32),
        ],
        compiler_params=_cparams(("arbitrary",)),
        name="mixer_merge_route",
    )(*meta, x, attn, u_pool, u_pool, u_pool, y_ssm, gates, mod_l, *wts)


def _expert_kernel(be_ref, nu_ref, tok_ref, tokn_ref, h2_hbm, wg_ref, wu_ref, wd_ref, y_ref,
                   xg_sc, wg_sc, wu_sc, wd_sc, sem, *, bm):
    b = pl.program_id(0)
    n_used = nu_ref[0]
    slot = b % 2

    def row_copy(tok, s, r):
        return pltpu.make_async_copy(h2_hbm.at[pl.ds(tok * ROW_TILE, ROW_TILE), :],
                                     xg_sc.at[s, pl.ds(r * ROW_TILE, ROW_TILE), :], sem.at[s])

    def issue(tok_smem, s):
        def body(r, c):
            row_copy(tok_smem[0, 0, r], s, r).start()
            return c
        lax.fori_loop(0, bm, body, 0, unroll=8)

    @pl.when(b == 0)
    def _():
        issue(tok_ref, 0)

    @pl.when(b + 1 < n_used)
    def _():
        issue(tokn_ref, 1 - slot)

    @pl.when(b < n_used)
    def _():
        pltpu.make_async_copy(h2_hbm.at[pl.ds(0, bm * ROW_TILE), :], xg_sc.at[slot], sem.at[slot]).wait()

        @pl.when((b == 0) | (be_ref[b] != be_ref[jnp.maximum(b - 1, 0)]))
        def _():
            wg_sc[...] = wg_ref[0].astype(BF16)
            wu_sc[...] = wu_ref[0].astype(BF16)
            wd_sc[...] = wd_ref[0].astype(BF16)

        x = _load_row_tiles(xg_sc.at[slot], 0, bm).astype(BF16)
        g = jnp.dot(x, wg_sc[...], preferred_element_type=F32)
        u = jnp.dot(x, wu_sc[...], preferred_element_type=F32)
        hmid = (g * jax.nn.sigmoid(g) * u).astype(BF16)
        _store_row_tiles(y_ref, jnp.dot(hmid, wd_sc[...], preferred_element_type=F32))

    @pl.when(b >= n_used)
    def _():
        y_ref[...] = jnp.zeros(y_ref.shape, F32)


def _experts(h2, slot_token3, block_expert, n_used, w_gate, w_up, w_down, bm):
    nb = slot_token3.shape[0]
    grid_spec = pltpu.PrefetchScalarGridSpec(
        num_scalar_prefetch=2,
        grid=(nb,),
        in_specs=[
            pl.BlockSpec((1, 1, bm), lambda b, be, nu: (b, 0, 0), memory_space=pltpu.SMEM),
            pl.BlockSpec((1, 1, bm), lambda b, be, nu: (jnp.minimum(b + 1, nb - 1), 0, 0),
                         memory_space=pltpu.SMEM),
            pl.BlockSpec(memory_space=pl.ANY),
            pl.BlockSpec((1, D_MODEL, D_EXPERT), lambda b, be, nu: (be[b], 0, 0)),
            pl.BlockSpec((1, D_MODEL, D_EXPERT), lambda b, be, nu: (be[b], 0, 0)),
            pl.BlockSpec((1, D_EXPERT, D_MODEL), lambda b, be, nu: (be[b], 0, 0)),
        ],
        out_specs=pl.BlockSpec((bm * ROW_TILE, LANES), lambda b, be, nu: (b, 0)),
        scratch_shapes=[
            pltpu.VMEM((2, bm * ROW_TILE, LANES), F32),
            pltpu.VMEM((D_MODEL, D_EXPERT), BF16),
            pltpu.VMEM((D_MODEL, D_EXPERT), BF16),
            pltpu.VMEM((D_EXPERT, D_MODEL), BF16),
            pltpu.SemaphoreType.DMA((2,)),
        ],
    )
    return pl.pallas_call(
        functools.partial(_expert_kernel, bm=bm),
        grid_spec=grid_spec,
        out_shape=jax.ShapeDtypeStruct((nb * bm * ROW_TILE, LANES), F32),
        compiler_params=_cparams(("arbitrary",)),
        name="routed_experts",
    )(block_expert, n_used, slot_token3, slot_token3, h2, w_gate, w_up, w_down)


def _combine_kernel(ts_ref, d_ref, dn_ref, y_hbm, x1_ref, h2_ref, gate_ref, mod_ref, wsg_ref, wsu_ref, wsd_ref,
                    o_ref, yg_sc, sem, *, tc):
    del ts_ref
    i = pl.program_id(0)
    n = pl.num_programs(0)
    slot = i % 2
    n_copies = tc * TOP_K

    def row_copy(src, s, row):
        return pltpu.make_async_copy(y_hbm.at[pl.ds(src * ROW_TILE, ROW_TILE), :],
                                     yg_sc.at[s, pl.ds(row * ROW_TILE, ROW_TILE), :], sem.at[s])

    def issue(dest_smem, s):
        def body(a, c):
            row_copy(dest_smem[0, 0, a], s, (a & (TOP_K - 1)) * tc + (a >> 3)).start()
            return c
        lax.fori_loop(0, n_copies, body, 0, unroll=8)

    @pl.when(i == 0)
    def _():
        issue(d_ref, 0)

    @pl.when(i + 1 < n)
    def _():
        issue(dn_ref, 1 - slot)

    pltpu.make_async_copy(y_hbm.at[pl.ds(0, n_copies * ROW_TILE), :], yg_sc.at[slot], sem.at[slot]).wait()

    gate = gate_ref[...]
    yg = yg_sc.at[slot]
    routed = gate[:, 0:1] * _load_row_tiles(yg, 0, tc)
    for k in range(1, TOP_K):
        routed = routed + gate[:, k:k + 1] * _load_row_tiles(yg, k * tc, tc)
    hb = _load_row_tiles(h2_ref, 0, tc).astype(BF16)
    g = jnp.dot(hb, wsg_ref[...], preferred_element_type=F32)
    u = jnp.dot(hb, wsu_ref[...], preferred_element_type=F32)
    shared = jnp.dot((g * jax.nn.sigmoid(g) * u).astype(BF16), wsd_ref[...], preferred_element_type=F32)
    o_ref[...] = x1_ref[...] + mod_ref[0][5:6] * (routed + shared)


def _combine(y_buf, dest3, x1, h2, gate, mod_l, ws_gate, ws_up, ws_down, tile_seq, tc):
    t = x1.shape[0]
    n = t // tc
    d_sh = ws_gate.shape[1]
    const = lambda i, ts: (0, 0)
    row = lambda i, ts: (i, 0)
    grid_spec = pltpu.PrefetchScalarGridSpec(
        num_scalar_prefetch=1,
        grid=(n,),
        in_specs=[
            pl.BlockSpec((1, 1, tc * TOP_K), lambda i, ts: (i, 0, 0), memory_space=pltpu.SMEM),
            pl.BlockSpec((1, 1, tc * TOP_K), lambda i, ts: (jnp.minimum(i + 1, n - 1), 0, 0),
                         memory_space=pltpu.SMEM),
            pl.BlockSpec(memory_space=pl.ANY),
            pl.BlockSpec((tc, D_MODEL), row),
            pl.BlockSpec((tc * ROW_TILE, LANES), row),
            pl.BlockSpec((tc, LANES), row),
            pl.BlockSpec((1, N_ADA, D_MODEL), lambda i, ts: (ts[i], 0, 0)),
            pl.BlockSpec((D_MODEL, d_sh), const),
            pl.BlockSpec((D_MODEL, d_sh), const),
            pl.BlockSpec((d_sh, D_MODEL), const),
        ],
        out_specs=pl.BlockSpec((tc, D_MODEL), row),
        scratch_shapes=[pltpu.VMEM((2, tc * TOP_K * ROW_TILE, LANES), F32), pltpu.SemaphoreType.DMA((2,))],
    )
    return pl.pallas_call(
        functools.partial(_combine_kernel, tc=tc),
        grid_spec=grid_spec,
        out_shape=jax.ShapeDtypeStruct((t, D_MODEL), F32),
        compiler_params=_cparams(("arbitrary",)),
        name="moe_combine",
    )(tile_seq, dest3, dest3, y_buf, x1, h2, gate, mod_l, ws_gate, ws_up, ws_down)


def _tile_meta(seq_lens, tile):
    seq, first, last, pos0, slen = [], [], [], [], []
    for s, n in enumerate(seq_lens):
        nt = n // tile
        for j in range(nt):
            seq.append(s)
            first.append(int(j == 0))
            last.append(int(j == nt - 1))
            pos0.append(j * tile)
            slen.append(n)
    return tuple(jnp.asarray(np.asarray(a, np.int32)) for a in (seq, first, last, pos0, slen))


def _scan_meta(seq_lens, tr):
    bwd, first = [], []
    base = 0
    for n in seq_lens:
        nt = n // SSM_CHUNK // tr
        for j in range(nt):
            bwd.append(base + nt - 1 - j)
            first.append(int(j == 0))
        base += nt
    return jnp.asarray(np.asarray(bwd, np.int32)), jnp.asarray(np.asarray(first, np.int32))


def _rope_tables(seq_lens):
    quarter = HEAD_DIM // 4
    freqs = ROPE_THETA ** (-jnp.arange(quarter, dtype=F32) / quarter)
    sign = jnp.tile(jnp.concatenate([-jnp.ones((quarter,), F32), jnp.ones((quarter,), F32)]), 2)
    cos_l, sin_l = [], []
    cache = {}
    for n in seq_lens:
        if n not in cache:
            pos = jnp.arange(n)
            ar = (pos // GRID_W).astype(F32)[:, None] * freqs
            ac = (pos % GRID_W).astype(F32)[:, None] * freqs
            ang = jnp.concatenate([ar, ar, ac, ac], axis=-1)
            cache[n] = (jnp.tile(jnp.cos(ang), (1, 2)), jnp.tile(jnp.sin(ang) * sign, (1, 2)))
        cos_l.append(cache[n][0])
        sin_l.append(cache[n][1])
    return jnp.concatenate(cos_l, axis=0), jnp.concatenate(sin_l, axis=0)


def _block_diag_ones(n, blk):
    r = np.arange(n) // blk
    return jnp.asarray((r[:, None] == r[None, :]).astype(np.float32)).astype(BF16)


def _dest_kernel(idx_ref, rank_ref, ps_ref, d_ref):
    idx = idx_ref[...]
    out = rank_ref[...]
    lane_e = lax.broadcasted_iota(I32, (idx.shape[0], N_EXPERTS), 1)
    lane = lax.broadcasted_iota(I32, idx.shape, 1)
    for k in range(TOP_K):
        start = jnp.sum(jnp.where(lane_e == idx[:, k:k + 1], ps_ref[...], 0.0), axis=1, keepdims=True)
        out = jnp.where(lane == k, out + start.astype(I32), out)
    d_ref[...] = out


def _dest_slots(idx, rank, pstart):
    t = idx.shape[0]
    tm = _pick(t, 512)
    row = lambda i: (i, 0)
    return pl.pallas_call(
        _dest_kernel,
        grid=(t // tm,),
        in_specs=[pl.BlockSpec((tm, LANES), row), pl.BlockSpec((tm, LANES), row),
                  pl.BlockSpec((1, N_EXPERTS), lambda i: (0, 0))],
        out_specs=pl.BlockSpec((tm, LANES), row),
        out_shape=jax.ShapeDtypeStruct((t, LANES), I32),
        compiler_params=_cparams(("arbitrary",)),
        name="moe_dest_slots",
    )(idx, rank, pstart)


def _moe_plan(idx, rank, counts, t, bm, nb):
    cnt = counts.reshape(N_EXPERTS).astype(I32)
    padded = (cnt + bm - 1) // bm * bm
    pend = jnp.cumsum(padded)
    pstart = pend - padded
    n_used = (pend[-1] // bm).astype(I32).reshape(1)
    block_expert = jnp.minimum(
        jnp.searchsorted(pend, jnp.arange(nb, dtype=I32) * bm, side='right'), N_EXPERTS - 1).astype(I32)
    dest = _dest_slots(idx, rank, pstart.astype(F32).reshape(1, N_EXPERTS))[:, :TOP_K]
    slot_token = jnp.zeros((nb * bm,), I32).at[dest.reshape(-1)].set(
        jnp.arange(t * TOP_K, dtype=I32) // TOP_K, unique_indices=True, mode='promise_in_bounds')
    return dest, slot_token, block_expert, n_used


def kernel(x_prompt, x_sample, c_prompt, c_sample, w_ada, b_ada, norm1_g, w_in, q_norm_g, k_norm_g, w_attn_o, pool_w, pool_scale, w_pool_o, ssm_a_re, ssm_a_im, ssm_log_dt, ssm_b_re, ssm_b_im, ssm_c_re, ssm_c_im, ssm_d, w_glu, w_out, norm2_g, w_router, b_router, w_exp_gate, w_exp_up, w_exp_down, w_sh_gate, w_sh_up, w_sh_down):
    b1, l1, d = x_prompt.shape
    b2, l2, _ = x_sample.shape
    depth = w_in.shape[0]
    assert d == D_MODEL and b1 + b2 <= MOD_ROWS
    seq_lens = [l1] * b1 + [l2] * b2
    t = b1 * l1 + b2 * l2
    lmin = min(l1, l2)

    tm_in = _pick(lmin, 512)
    tm_post = _pick(lmin, 256)
    tc_comb = _pick(lmin, 128)
    tr_scan = _pick(lmin // SSM_CHUNK, 256)
    bm = 256
    nb = -(-(t * TOP_K + N_EXPERTS * (bm - 1)) // bm)

    x = jnp.concatenate([x_prompt.reshape(b1 * l1, d), x_sample.reshape(b2 * l2, d)], axis=0)
    c_all = jnp.concatenate([c_prompt, c_sample, jnp.zeros((MOD_ROWS - b1 - b2, d), F32)], axis=0)
    mod = _modulation(c_all, w_ada, b_ada).reshape(depth, MOD_ROWS, N_ADA, d)

    cos_t, sin_t = _rope_tables(seq_lens)
    ones_bd = _block_diag_ones(ATTN_DIM, HEAD_DIM)
    tril = jnp.asarray(np.tril(np.ones((tm_post, tm_post), np.float32), -1)).astype(BF16)
    meta_in = _tile_meta(seq_lens, tm_in)
    meta_post = _tile_meta(seq_lens, tm_post)
    meta_comb = _tile_meta(seq_lens, tc_comb)
    bwd_tile, first_tile = _scan_meta(seq_lens, tr_scan)

    for l in range(depth):
        q8, k2, v2, u_pool, u_ssm, gates = _inproj(
            x, mod[l], norm1_g[l].reshape(1, d), w_in[l].astype(BF16),
            jnp.tile(q_norm_g[l], N_Q_HEADS).reshape(1, ATTN_DIM),
            jnp.tile(k_norm_g[l], N_KV_HEADS).reshape(1, KV_DIM),
            cos_t, sin_t, ones_bd, meta_in[0], tm_in)
        attn = jnp.concatenate(
            [_flash(q8, k2, v2, 0, b1, l1), _flash(q8, k2, v2, b1 * l1, b2, l2)], axis=0)
        tables = _ssm_tables(ssm_a_re[l], ssm_a_im[l], ssm_log_dt[l], ssm_b_re[l], ssm_b_im[l],
                             ssm_c_re[l], ssm_c_im[l], ssm_d[l])
        y_ssm = _bidir_ssm(u_ssm, tables, bwd_tile, first_tile, tr_scan)

        pool_bd = jax.scipy.linalg.block_diag(*[pool_w[l, g] for g in range(len(POOL_WINDOWS))])
        wts = (w_attn_o[l].astype(BF16), pool_bd.astype(BF16), pool_scale[l].reshape(1, POOL_DIM),
               w_pool_o[l].astype(BF16), w_glu[l].astype(BF16), w_out[l].astype(BF16),
               norm2_g[l].reshape(1, d), w_router[l], b_router[l].reshape(1, N_EXPERTS), tril)
        x1, h2, idx, gate, rank, counts = _post(x, attn, u_pool, y_ssm, gates, mod[l], wts, meta_post, tm_post)

        dest, slot_token, block_expert, n_used = _moe_plan(idx, rank, counts, t, bm, nb)
        y_buf = _experts(h2, slot_token.reshape(nb, 1, bm), block_expert, n_used,
                         w_exp_gate[l], w_exp_up[l], w_exp_down[l], bm)
        x = _combine(y_buf, dest.reshape(t // tc_comb, 1, tc_comb * TOP_K), x1, h2, gate, mod[l],
                     w_sh_gate[l].astype(BF16), w_sh_up[l].astype(BF16), w_sh_down[l].astype(BF16),
                     meta_comb[0], tc_comb)

    return (x[:b1 * l1].reshape(b1, l1, d), x[b1 * l1:].reshape(b2, l2, d))
```

```python
import functools
import math

import jax
import jax.numpy as jnp
import numpy as np
from jax import lax
from jax.experimental import pallas as pl
from jax.experimental.pallas import tpu as pltpu

F32 = jnp.float32
BF16 = jnp.bfloat16
I32 = jnp.int32
HIGHEST = lax.Precision.HIGHEST

D_MODEL = 1024
GRID_W = 64
HEAD_DIM = 64
N_Q_HEADS = 8
N_KV_HEADS = 2
Q_PER_KV = N_Q_HEADS // N_KV_HEADS
ATTN_DIM = N_Q_HEADS * HEAD_DIM
KV_DIM = N_KV_HEADS * HEAD_DIM
ROPE_THETA = 10000.0
POOL_WINDOWS = (2, 4, 8, 16)
POOL_DIM = 256
POOL_GROUP = 64
POOL_HALO = 8
SSM_DIM = 256
SSM_P = 16
SSM_G = 16
SSM_N = 64
SSM_CHUNK = 16
SSM_ROW = SSM_CHUNK * SSM_P
SSM_STATE = 2 * SSM_N
N_EXPERTS = 256
TOP_K = 8
N_EXPERT_GROUPS = 8
GROUP_SIZE = N_EXPERTS // N_EXPERT_GROUPS
TOPK_GROUPS = 4
D_EXPERT = 256
ROUTED_SCALE = 2.5
N_ADA = 6
EPS = 1e-6
IN_DIM = ATTN_DIM + 2 * KV_DIM + POOL_DIM + SSM_DIM + 3 * D_MODEL
OFF_K = ATTN_DIM
OFF_V = OFF_K + KV_DIM
OFF_POOL = OFF_V + KV_DIM
OFF_SSM = OFF_POOL + POOL_DIM
OFF_GATES = OFF_SSM + SSM_DIM
MOD_ROWS = 8
LOG2_E = math.log2(math.e)
VT_ROWS = HEAD_DIM + 16

V7X_VMEM_BYTES = 64 * 1024 * 1024
VMEM_LIMIT = V7X_VMEM_BYTES - 8 * 1024 * 1024
LANES = 128


def _cparams(sem):
    return pltpu.CompilerParams(dimension_semantics=sem, vmem_limit_bytes=VMEM_LIMIT)


def _pick(n, pref):
    t = min(n, pref)
    while n % t:
        t //= 2
    return t


ROW_TILE = D_MODEL // LANES


def _store_row_tiles(ref, x):
    n = x.shape[0]
    for s in range(ROW_TILE):
        ref[pl.ds(s, n, stride=ROW_TILE), :] = x[:, s * LANES:(s + 1) * LANES]


def _load_row_tiles(ref, first_row, n):
    return jnp.concatenate(
        [ref[pl.ds(first_row * ROW_TILE + s, n, stride=ROW_TILE), :] for s in range(ROW_TILE)], axis=1)


def _mod_kernel(c_ref, w_ref, b_ref, o_ref):
    c = c_ref[...]
    a = c * jax.nn.sigmoid(c)
    o_ref[0] = jnp.dot(a, w_ref[0], preferred_element_type=F32, precision=HIGHEST) + b_ref[0]


def _modulation(c_all, w_ada, b_ada):
    depth, d, n = w_ada.shape
    bn = _pick(n, 1536)
    return pl.pallas_call(
        _mod_kernel,
        grid=(depth, n // bn),
        in_specs=[
            pl.BlockSpec((MOD_ROWS, d), lambda l, j: (0, 0)),
            pl.BlockSpec((1, d, bn), lambda l, j: (l, 0, j)),
            pl.BlockSpec((1, 1, bn), lambda l, j: (l, 0, j)),
        ],
        out_specs=pl.BlockSpec((1, MOD_ROWS, bn), lambda l, j: (l, 0, j)),
        out_shape=jax.ShapeDtypeStruct((depth, MOD_ROWS, n), F32),
        compiler_params=_cparams(("arbitrary", "arbitrary")),
        name="adaln_mod",
    )(c_all, w_ada, b_ada.reshape(depth, 1, n))


def _rmsnorm_mod(x, g, scale, shift):
    y = x * lax.rsqrt(jnp.mean(x * x, axis=-1, keepdims=True) + EPS)
    return (y * g) * (1.0 + scale) + shift


def _head_norm_rope(z, gain, cos, sin_signed, ones_bd):
    z2 = z * z
    hi = z2.astype(BF16)
    lo = (z2 - hi.astype(F32)).astype(BF16)
    ss = (jnp.dot(hi, ones_bd, preferred_element_type=F32)
          + jnp.dot(lo, ones_bd, preferred_element_type=F32))
    y = (z * lax.rsqrt(ss * (1.0 / HEAD_DIM) + EPS)) * gain
    w = z.shape[1]
    quarter = HEAD_DIM // 4
    from_right = pltpu.roll(y, w - quarter, 1)
    from_left = pltpu.roll(y, quarter, 1)
    lane = lax.broadcasted_iota(I32, y.shape, 1)
    rot = jnp.where((lane & quarter) == 0, from_right, from_left)
    return y * cos + rot * sin_signed


def _inproj_kernel(ts_ref, x_ref, mod_ref, g_ref, w_ref, gq_ref, gk_ref, cos_ref, sin_ref, ones_ref,
                   q_ref, k_ref, v_ref, up_ref, us_ref, gt_ref):
    del ts_ref
    x = x_ref[...]
    mod = mod_ref[0]
    h = _rmsnorm_mod(x, g_ref[...], mod[1:2], mod[0:1]).astype(BF16)

    cos = cos_ref[...]
    sin = sin_ref[...]
    zq = jnp.dot(h, w_ref[:, 0:ATTN_DIM], preferred_element_type=F32)
    reps = ATTN_DIM // LANES
    yq = _head_norm_rope(zq, gq_ref[...], jnp.concatenate([cos] * reps, axis=1),
                         jnp.concatenate([sin] * reps, axis=1), ones_ref[...]) * (HEAD_DIM ** -0.5 * LOG2_E)
    for hh in range(N_Q_HEADS):
        q_ref[hh] = yq[:, hh * HEAD_DIM:(hh + 1) * HEAD_DIM].astype(BF16)

    zkv = jnp.dot(h, w_ref[:, OFF_K:OFF_POOL], preferred_element_type=F32)
    yk = _head_norm_rope(zkv[:, :KV_DIM], gk_ref[...], cos, sin, ones_ref[0:KV_DIM, 0:KV_DIM])
    zv = zkv[:, KV_DIM:]
    for j in range(N_KV_HEADS):
        k_ref[j] = yk[:, j * HEAD_DIM:(j + 1) * HEAD_DIM].astype(BF16)
        v_ref[j, 0:HEAD_DIM, :] = zv[:, j * HEAD_DIM:(j + 1) * HEAD_DIM].T.astype(BF16)
        v_ref[j, HEAD_DIM:VT_ROWS, :] = jnp.ones((VT_ROWS - HEAD_DIM, zv.shape[0]), BF16)

    zps = jnp.dot(h, w_ref[:, OFF_POOL:OFF_GATES], preferred_element_type=F32)
    up_ref[...] = zps[:, :POOL_DIM]
    us_ref[...] = zps[:, POOL_DIM:]
    for c in range(3):
        zg = jnp.dot(h, w_ref[:, OFF_GATES + c * D_MODEL:OFF_GATES + (c + 1) * D_MODEL],
                     preferred_element_type=F32)
        gt_ref[:, c * D_MODEL:(c + 1) * D_MODEL] = jax.nn.sigmoid(zg).astype(BF16)


def _inproj(x, mod_l, g1, w_in_bf, gq_t, gk_t, cos_t, sin_t, ones_bd, tile_seq, tm):
    t = x.shape[0]
    const = lambda i, ts: (0, 0)
    row = lambda i, ts: (i, 0)
    grid_spec = pltpu.PrefetchScalarGridSpec(
        num_scalar_prefetch=1,
        grid=(t // tm,),
        in_specs=[
            pl.BlockSpec((tm, D_MODEL), row),
            pl.BlockSpec((1, N_ADA, D_MODEL), lambda i, ts: (ts[i], 0, 0)),
            pl.BlockSpec((1, D_MODEL), const),
            pl.BlockSpec((D_MODEL, IN_DIM), const),
            pl.BlockSpec((1, ATTN_DIM), const),
            pl.BlockSpec((1, KV_DIM), const),
            pl.BlockSpec((tm, LANES), row),
            pl.BlockSpec((tm, LANES), row),
            pl.BlockSpec((ATTN_DIM, ATTN_DIM), const),
        ],
        out_specs=[
            pl.BlockSpec((N_Q_HEADS, tm, HEAD_DIM), lambda i, ts: (0, i, 0)),
            pl.BlockSpec((N_KV_HEADS, tm, HEAD_DIM), lambda i, ts: (0, i, 0)),
            pl.BlockSpec((N_KV_HEADS, VT_ROWS, tm), lambda i, ts: (0, 0, i)),
            pl.BlockSpec((tm, POOL_DIM), row),
            pl.BlockSpec((tm, SSM_DIM), row),
            pl.BlockSpec((tm, 3 * D_MODEL), row),
        ],
    )
    return pl.pallas_call(
        _inproj_kernel,
        grid_spec=grid_spec,
        out_shape=[
            jax.ShapeDtypeStruct((N_Q_HEADS, t, HEAD_DIM), BF16),
            jax.ShapeDtypeStruct((N_KV_HEADS, t, HEAD_DIM), BF16),
            jax.ShapeDtypeStruct((N_KV_HEADS, VT_ROWS, t), BF16),
            jax.ShapeDtypeStruct((t, POOL_DIM), F32),
            jax.ShapeDtypeStruct((t, SSM_DIM), F32),
            jax.ShapeDtypeStruct((t, 3 * D_MODEL), BF16),
        ],
        compiler_params=_cparams(("arbitrary",)),
        name="norm1_inproj",
    )(tile_seq, x, mod_l, g1, w_in_bf, gq_t, gk_t, cos_t, sin_t, ones_bd)


def _flash_kernel(q_ref, k_ref, vt_ref, o_ref, m_sc, acc_sc):
    kb = pl.program_id(3)

    @pl.when(kb == 0)
    def _():
        m_sc[...] = jnp.full(m_sc.shape, -jnp.inf, F32)
        acc_sc[...] = jnp.zeros(acc_sc.shape, F32)

    k = k_ref[0]
    vt = vt_ref[0]
    heads = range(Q_PER_KV)
    qs = [q_ref[hh] for hh in heads]
    m_prev = [m_sc[hh] for hh in heads]
    acc_prev = [acc_sc[hh] for hh in heads]

    def scores(hh):
        return lax.dot_general(k, qs[hh], (((1,), (1,)), ((), ())), preferred_element_type=F32)

    ahead = 2
    s = {hh: scores(hh) for hh in range(ahead)}
    new = []
    for hh in heads:
        if hh + ahead < Q_PER_KV:
            s[hh + ahead] = scores(hh + ahead)
        m_new = jnp.maximum(m_prev[hh], jnp.max(s[hh], axis=0, keepdims=True))
        alpha = jnp.exp2(m_prev[hh] - m_new)
        p = jnp.exp2(s.pop(hh) - m_new).astype(BF16)
        acc_new = acc_prev[hh] * alpha + jnp.dot(vt, p, preferred_element_type=F32)
        new.append((m_new, acc_new))
    for hh, (m_new, acc_new) in enumerate(new):
        m_sc[hh] = m_new
        acc_sc[hh] = acc_new

    @pl.when(kb == pl.num_programs(3) - 1)
    def _():
        for hh in range(Q_PER_KV):
            acc = acc_sc[hh]
            o = acc[:HEAD_DIM] / acc[HEAD_DIM:HEAD_DIM + 1]
            o_ref[:, hh * HEAD_DIM:(hh + 1) * HEAD_DIM] = o.T.astype(BF16)


def _flash(q8, k2, vt2, off, bsz, seq):
    tq = _pick(seq, 512)
    tk = _pick(seq, 2048)
    nq, nk = seq // tq, seq // tk
    assert off % tq == 0 and off % tk == 0
    oq, ok = off // tq, off // tk
    return pl.pallas_call(
        _flash_kernel,
        grid=(N_KV_HEADS, bsz, nq, nk),
        in_specs=[
            pl.BlockSpec((Q_PER_KV, tq, HEAD_DIM), lambda j, b, i, kb: (j, oq + b * nq + i, 0)),
            pl.BlockSpec((1, tk, HEAD_DIM), lambda j, b, i, kb: (j, ok + b * nk + kb, 0)),
            pl.BlockSpec((1, VT_ROWS, tk), lambda j, b, i, kb: (j, 0, ok + b * nk + kb)),
        ],
        out_specs=pl.BlockSpec((tq, Q_PER_KV * HEAD_DIM), lambda j, b, i, kb: (b * nq + i, j)),
        out_shape=jax.ShapeDtypeStruct((bsz * seq, ATTN_DIM), BF16),
        scratch_shapes=[
            pltpu.VMEM((Q_PER_KV, 1, tq), F32),
            pltpu.VMEM((Q_PER_KV, VT_ROWS, tq), F32),
        ],
        compiler_params=_cparams(("arbitrary", "arbitrary", "arbitrary", "arbitrary")),
        name="flash_gqa",
    )(q8, k2, vt2)


def _ssm_tables(a_re, a_im, log_dt, b_re, b_im, c_re, c_im, d_skip):
    tc = SSM_CHUNK
    a = lax.complex(a_re.astype(F32), a_im.astype(F32))
    dt = jnp.exp(log_dt.astype(F32))[..., None]
    adt = a * dt
    a_bar = jnp.exp(adt)
    b_bar = ((a_bar - 1.0) / a)[..., None] * lax.complex(b_re.astype(F32), b_im.astype(F32))
    c_mat = lax.complex(c_re.astype(F32), c_im.astype(F32))
    taus = jnp.arange(tc + 1, dtype=F32)
    pw = jnp.exp(adt[None] * taus[:, None, None, None])

    kern = jnp.real(jnp.einsum('dgpn,tdgn,dgnq->dgtpq', c_mat, pw[:tc], b_bar))
    s_idx = jnp.arange(tc)[:, None]
    t_idx = jnp.arange(tc)[None, :]
    lag_f = jnp.clip(t_idx - s_idx, 0, tc - 1)
    lag_b = jnp.clip(s_idx - t_idx, 0, tc - 1)
    kf = jnp.where((t_idx >= s_idx)[None, :, :, None, None], kern[0][:, lag_f], 0.0)
    kb = jnp.where((s_idx >= t_idx)[None, :, :, None, None], kern[1][:, lag_b], 0.0)
    w_loc = (kf + kb).transpose(0, 1, 4, 2, 3).reshape(SSM_G, SSM_ROW, SSM_ROW)
    w_loc = w_loc + jnp.eye(SSM_ROW, dtype=F32)[None] * jnp.tile(
        d_skip.astype(F32).reshape(SSM_G, 1, SSM_P), (1, tc, 1)).reshape(SSM_G, 1, SSM_ROW)

    def ri(z, axis):
        return jnp.concatenate([jnp.real(z), jnp.imag(z)], axis=axis)

    pf = jnp.einsum('sgn,gnq->gsqn', pw[:tc][::-1, 0], b_bar[0])
    pb = jnp.einsum('sgn,gnq->gsqn', pw[:tc, 1], b_bar[1])
    p_st = jnp.concatenate([ri(pf, -1), ri(pb, -1)], axis=-1).reshape(SSM_G, SSM_ROW, 2 * SSM_STATE)
    qf = jnp.einsum('gpn,tgn->gntp', c_mat[0], pw[1:tc + 1, 0])
    qb = jnp.einsum('gpn,tgn->gntp', c_mat[1], pw[1:tc + 1, 1][::-1])
    q_f = jnp.concatenate([jnp.real(qf), -jnp.imag(qf)], axis=1).reshape(SSM_G, SSM_STATE, SSM_ROW)
    q_b = jnp.concatenate([jnp.real(qb), -jnp.imag(qb)], axis=1).reshape(SSM_G, SSM_STATE, SSM_ROW)
    a16 = pw[tc]
    a1 = jnp.concatenate([jnp.real(a16), jnp.real(a16)], axis=-1).reshape(2, SSM_G * SSM_STATE)
    a2 = jnp.concatenate([-jnp.imag(a16), jnp.imag(a16)], axis=-1).reshape(2, SSM_G * SSM_STATE)
    a_mul = jnp.stack([a1[0], a2[0], a1[1], a2[1]], axis=0)
    return w_loc.astype(BF16), p_st.astype(BF16), q_f.astype(BF16), q_b.astype(BF16), a_mul


def _ssm_state_kernel(u_ref, p_ref, s_ref):
    s = jnp.dot(u_ref[0].astype(BF16), p_ref[0], preferred_element_type=F32)
    s_ref[0] = s[:, :SSM_STATE]
    s_ref[1] = s[:, SSM_STATE:]


def _ssm_states(ug, p_st, tc):
    g, nc, _ = ug.shape
    return pl.pallas_call(
        _ssm_state_kernel,
        grid=(g, nc // tc),
        in_specs=[
            pl.BlockSpec((1, tc, SSM_ROW), lambda gi, i: (gi, i, 0)),
            pl.BlockSpec((1, SSM_ROW, 2 * SSM_STATE), lambda gi, i: (gi, 0, 0)),
        ],
        out_specs=pl.BlockSpec((2, tc, SSM_STATE), lambda gi, i: (0, i, gi)),
        out_shape=jax.ShapeDtypeStruct((2, nc, g * SSM_STATE), F32),
        compiler_params=_cparams(("arbitrary", "arbitrary")),
        name="ssm_chunk_states",
    )(ug, p_st)


def _swap_halves(x):
    parts = [pltpu.roll(x[:, g * LANES:(g + 1) * LANES], LANES // 2, 1) for g in range(x.shape[1] // LANES)]
    return jnp.concatenate(parts, axis=1)


def _ssm_scan_kernel(bt_ref, first_ref, sf_ref, sb_ref, am_ref, xf_ref, zb_ref, st_sc, sfw_sc, sbw_sc, *, tr):
    n = pl.program_id(0)
    del bt_ref

    @pl.when(first_ref[n] == 1)
    def _():
        st_sc[...] = jnp.zeros(st_sc.shape, F32)

    sfw_sc[...] = _swap_halves(sf_ref[0])
    sbw_sc[...] = _swap_halves(sb_ref[0])
    a1f, a2f, a1b, a2b = am_ref[0:1], am_ref[1:2], am_ref[2:3], am_ref[3:4]

    def body(r, carry):
        xf, xfw, zb, zbw = carry
        xf_ref[pl.ds(r, 1), :] = xf
        sf = sf_ref[0, pl.ds(r, 1), :]
        sfw = sfw_sc[pl.ds(r, 1), :]
        nxf = a1f * xf + a2f * xfw + sf
        nxfw = a1f * xfw - a2f * xf + sfw
        rb = tr - 1 - r
        zb_ref[pl.ds(rb, 1), :] = zb
        sb = sb_ref[0, pl.ds(rb, 1), :]
        sbw = sbw_sc[pl.ds(rb, 1), :]
        nzb = a1b * zb + a2b * zbw + sb
        nzbw = a1b * zbw - a2b * zb + sbw
        return nxf, nxfw, nzb, nzbw

    init = (st_sc[0:1], st_sc[1:2], st_sc[2:3], st_sc[3:4])
    xf, xfw, zb, zbw = lax.fori_loop(0, tr, body, init)
    st_sc[0:1] = xf
    st_sc[1:2] = xfw
    st_sc[2:3] = zb
    st_sc[3:4] = zbw


def _ssm_scan(s_st, a_mul, bwd_tile, first_tile, tr):
    _, nc, w = s_st.shape
    grid_spec = pltpu.PrefetchScalarGridSpec(
        num_scalar_prefetch=2,
        grid=(nc // tr,),
        in_specs=[
            pl.BlockSpec((1, tr, w), lambda n, bt, ft: (0, n, 0)),
            pl.BlockSpec((1, tr, w), lambda n, bt, ft: (1, bt[n], 0)),
            pl.BlockSpec((4, w), lambda n, bt, ft: (0, 0)),
        ],
        out_specs=[
            pl.BlockSpec((tr, w), lambda n, bt, ft: (n, 0)),
            pl.BlockSpec((tr, w), lambda n, bt, ft: (bt[n], 0)),
        ],
        scratch_shapes=[pltpu.VMEM((4, w), F32), pltpu.VMEM((tr, w), F32), pltpu.VMEM((tr, w), F32)],
    )
    return pl.pallas_call(
        functools.partial(_ssm_scan_kernel, tr=tr),
        grid_spec=grid_spec,
        out_shape=[jax.ShapeDtypeStruct((nc, w), F32), jax.ShapeDtypeStruct((nc, w), F32)],
        compiler_params=_cparams(("arbitrary",)),
        name="ssm_chunk_scan",
    )(bwd_tile, first_tile, s_st, s_st, a_mul)


def _ssm_out_kernel(u_ref, xf_ref, zb_ref, w_ref, qf_ref, qb_ref, y_ref):
    y = jnp.dot(u_ref[0].astype(BF16), w_ref[0], preferred_element_type=F32)
    y += jnp.dot(xf_ref[...].astype(BF16), qf_ref[0], preferred_element_type=F32)
    y += jnp.dot(zb_ref[...].astype(BF16), qb_ref[0], preferred_element_type=F32)
    y_ref[0] = y


def _ssm_outputs(ug, xf, zb, w_loc, q_f, q_b, tc):
    g, nc, _ = ug.shape
    return pl.pallas_call(
        _ssm_out_kernel,
        grid=(g, nc // tc),
        in_specs=[
            pl.BlockSpec((1, tc, SSM_ROW), lambda gi, i: (gi, i, 0)),
            pl.BlockSpec((tc, SSM_STATE), lambda gi, i: (i, gi)),
            pl.BlockSpec((tc, SSM_STATE), lambda gi, i: (i, gi)),
            pl.BlockSpec((1, SSM_ROW, SSM_ROW), lambda gi, i: (gi, 0, 0)),
            pl.BlockSpec((1, SSM_STATE, SSM_ROW), lambda gi, i: (gi, 0, 0)),
            pl.BlockSpec((1, SSM_STATE, SSM_ROW), lambda gi, i: (gi, 0, 0)),
        ],
        out_specs=pl.BlockSpec((1, tc, SSM_ROW), lambda gi, i: (gi, i, 0)),
        out_shape=jax.ShapeDtypeStruct((g, nc, SSM_ROW), F32),
        compiler_params=_cparams(("arbitrary", "arbitrary")),
        name="ssm_chunk_outputs",
    )(ug, xf, zb, w_loc, q_f, q_b)


def _bidir_ssm(u_ssm, tables, bwd_tile, first_tile, tr):
    w_loc, p_st, q_f, q_b, a_mul = tables
    t = u_ssm.shape[0]
    nc = t // SSM_CHUNK
    ug = u_ssm.reshape(nc, SSM_CHUNK, SSM_G, SSM_P).transpose(2, 0, 1, 3).reshape(SSM_G, nc, SSM_ROW)
    tc = _pick(nc, 512)
    s_st = _ssm_states(ug, p_st, tc)
    xf, zb = _ssm_scan(s_st, a_mul, bwd_tile, first_tile, tr)
    yg = _ssm_outputs(ug, xf, zb, w_loc, q_f, q_b, tc)
    return yg.reshape(SSM_G, nc, SSM_CHUNK, SSM_P).transpose(1, 2, 0, 3).reshape(t, SSM_DIM)


def _pool_mixer(ext_sc, u, pos, seq_len, tm):
    halves = []
    for half in range(2):
        sums = []
        for w in POOL_WINDOWS[2 * half:2 * half + 2]:
            lo = w // 2
            hi = w - lo - 1
            acc = ext_sc[pl.ds(POOL_HALO - lo, tm), half * LANES:(half + 1) * LANES]
            for d in range(-lo + 1, hi + 1):
                acc = acc + ext_sc[pl.ds(POOL_HALO + d, tm), half * LANES:(half + 1) * LANES]
            cnt = (jnp.minimum(pos + hi, seq_len - 1) - jnp.maximum(pos - lo, 0) + 1).astype(F32)
            sums.append(acc / cnt)
        lane = lax.broadcasted_iota(I32, (tm, LANES), 1)
        halves.append(jnp.where(lane < POOL_GROUP, sums[0], sums[1]))
    return jnp.concatenate(halves, axis=1) - u


def _route(h2, wr_ref, br_ref):
    logits = jnp.dot(h2, wr_ref[...], preferred_element_type=F32, precision=HIGHEST)
    scores = jax.nn.sigmoid(logits)
    choice = scores + br_ref[...]
    lane = lax.broadcasted_iota(I32, choice.shape, 1)
    neg = jnp.float32(-jnp.inf)

    def first_argmax(c):
        m = jnp.max(c, axis=1, keepdims=True)
        return jnp.min(jnp.where(c == m, lane, N_EXPERTS), axis=1, keepdims=True), m

    in_group, gscore = [], []
    for g in range(N_EXPERT_GROUPS):
        ing = (lane >= g * GROUP_SIZE) & (lane < (g + 1) * GROUP_SIZE)
        cg = jnp.where(ing, choice, neg)
        i1, m1 = first_argmax(cg)
        m2 = jnp.max(jnp.where(lane == i1, neg, cg), axis=1, keepdims=True)
        in_group.append(ing)
        gscore.append(m1 + m2)
    emask = jnp.zeros(choice.shape, jnp.bool_)
    for g in range(N_EXPERT_GROUPS):
        ahead = jnp.zeros(gscore[g].shape, I32)
        for o in range(N_EXPERT_GROUPS):
            if o == g:
                continue
            beats = (gscore[o] > gscore[g]) | ((gscore[o] == gscore[g]) & (o < g))
            ahead = ahead + beats.astype(I32)
        emask = emask | (in_group[g] & (ahead < TOPK_GROUPS))
    c = jnp.where(emask, choice, neg)
    picks = []
    for _ in range(TOP_K):
        i, _m = first_argmax(c)
        oh = lane == i
        w = jnp.sum(jnp.where(oh, scores, 0.0), axis=1, keepdims=True)
        c = jnp.where(oh, neg, c)
        picks.append((i, w, oh))
    wsum = picks[0][1]
    for _, w, _ in picks[1:]:
        wsum = wsum + w
    return [(i, w / wsum * ROUTED_SCALE, oh) for i, w, oh in picks]


def _post_kernel(ts_ref, tf_ref, tl_ref, tp_ref, tn_ref,
                 x_ref, at_ref, upp_ref, up_ref, upn_ref, ys_ref, gt_ref, mod_ref,
                 wao_ref, pbd_ref, psc_ref, wpo_ref, wgl_ref, wo_ref, g2_ref, wr_ref, br_ref, tril_ref,
                 x1_ref, h2_ref, idx_ref, gate_ref, rank_ref, cnt_ref, ext_sc, carry_sc, *, tm):
    del ts_ref
    i = pl.program_id(0)
    mod = mod_ref[0]

    attn = jnp.dot(at_ref[...], wao_ref[...], preferred_element_type=F32)

    u = up_ref[...]
    ext_sc[0:POOL_HALO] = jnp.where(tf_ref[i] == 1, 0.0, upp_ref[...])
    ext_sc[POOL_HALO:POOL_HALO + tm] = u
    ext_sc[POOL_HALO + tm:2 * POOL_HALO + tm] = jnp.where(tl_ref[i] == 1, 0.0, upn_ref[...])
    pos = tp_ref[i] + lax.broadcasted_iota(I32, (tm, LANES), 0)
    pm = _pool_mixer(ext_sc, u, pos, tn_ref[i], tm)
    pool = jnp.dot(pm.astype(BF16), pbd_ref[...], preferred_element_type=F32) * psc_ref[...]
    pool = jnp.dot(pool.astype(BF16), wpo_ref[...], preferred_element_type=F32)

    z = jnp.dot(jax.nn.gelu(ys_ref[...]).astype(BF16), wgl_ref[...], preferred_element_type=F32)
    ssm = z[:, :D_MODEL] * jax.nn.sigmoid(z[:, D_MODEL:])

    merged = (gt_ref[:, 0:D_MODEL].astype(F32) * attn
              + gt_ref[:, D_MODEL:2 * D_MODEL].astype(F32) * pool
              + gt_ref[:, 2 * D_MODEL:3 * D_MODEL].astype(F32) * ssm)
    mix = jnp.dot(merged.astype(BF16), wo_ref[...], preferred_element_type=F32)
    x1 = x_ref[...] + mod[2:3] * mix
    x1_ref[...] = x1
    h2 = _rmsnorm_mod(x1, g2_ref[...], mod[4:5], mod[3:4])
    _store_row_tiles(h2_ref, h2)

    picks = _route(h2, wr_ref, br_ref)

    @pl.when(i == 0)
    def _():
        carry_sc[...] = jnp.zeros(carry_sc.shape, F32)

    sel = picks[0][2]
    for _, _, oh in picks[1:]:
        sel = sel | oh
    sel_f = sel.astype(F32)
    ranks = jnp.dot(tril_ref[...], sel_f.astype(BF16), preferred_element_type=F32) + carry_sc[...]
    carry = carry_sc[...] + jnp.sum(sel_f, axis=0, keepdims=True)
    carry_sc[...] = carry
    cnt_ref[...] = carry

    lane = lax.broadcasted_iota(I32, (tm, LANES), 1)
    idx_o = jnp.zeros((tm, LANES), I32)
    gate_o = jnp.zeros((tm, LANES), F32)
    rank_o = jnp.zeros((tm, LANES), I32)
    for k, (ik, gk, oh) in enumerate(picks):
        rk = jnp.sum(jnp.where(oh, ranks, 0.0), axis=1, keepdims=True).astype(I32)
        idx_o = jnp.where(lane == k, ik, idx_o)
        gate_o = jnp.where(lane == k, gk, gate_o)
        rank_o = jnp.where(lane == k, rk, rank_o)
    idx_ref[...] = idx_o
    gate_ref[...] = gate_o
    rank_ref[...] = rank_o


def _post(x, attn, u_pool, y_ssm, gates, mod_l, wts, meta, tm):
    t = x.shape[0]
    nh = t // POOL_HALO
    hb = tm // POOL_HALO
    npf = 5
    const = lambda i, *_: (0, 0)
    row = lambda i, *_: (i, 0)
    grid_spec = pltpu.PrefetchScalarGridSpec(
        num_scalar_prefetch=npf,
        grid=(t // tm,),
        in_specs=[
            pl.BlockSpec((tm, D_MODEL), row),
            pl.BlockSpec((tm, ATTN_DIM), row),
            pl.BlockSpec((POOL_HALO, POOL_DIM), lambda i, *_: (jnp.maximum(i * hb - 1, 0), 0)),
            pl.BlockSpec((tm, POOL_DIM), row),
            pl.BlockSpec((POOL_HALO, POOL_DIM), lambda i, *_: (jnp.minimum((i + 1) * hb, nh - 1), 0)),
            pl.BlockSpec((tm, SSM_DIM), row),
            pl.BlockSpec((tm, 3 * D_MODEL), row),
            pl.BlockSpec((1, N_ADA, D_MODEL), lambda i, ts, *_: (ts[i], 0, 0)),
            pl.BlockSpec((ATTN_DIM, D_MODEL), const),
            pl.BlockSpec((POOL_DIM, POOL_DIM), const),
            pl.BlockSpec((1, POOL_DIM), const),
            pl.BlockSpec((POOL_DIM, D_MODEL), const),
            pl.BlockSpec((SSM_DIM, 2 * D_MODEL), const),
            pl.BlockSpec((D_MODEL, D_MODEL), const),
            pl.BlockSpec((1, D_MODEL), const),
            pl.BlockSpec((D_MODEL, N_EXPERTS), const),
            pl.BlockSpec((1, N_EXPERTS), const),
            pl.BlockSpec((tm, tm), const),
        ],
        out_specs=[
            pl.BlockSpec((tm, D_MODEL), row),
            pl.BlockSpec((tm * ROW_TILE, LANES), row),
            pl.BlockSpec((tm, LANES), row),
            pl.BlockSpec((tm, LANES), row),
            pl.BlockSpec((tm, LANES), row),
            pl.BlockSpec((1, N_EXPERTS), const),
        ],
        scratch_shapes=[pltpu.VMEM((tm + 2 * POOL_HALO, POOL_DIM), F32), pltpu.VMEM((1, N_EXPERTS), F32)],
    )
    return pl.pallas_call(
        functools.partial(_post_kernel, tm=tm),
        grid_spec=grid_spec,
        out_shape=[
            jax.ShapeDtypeStruct((t, D_MODEL), F32),
            jax.ShapeDtypeStruct((t * ROW_TILE, LANES), F32),
            jax.ShapeDtypeStruct((t, LANES), I32),
            jax.ShapeDtypeStruct((t, LANES), F32),
            jax.ShapeDtypeStruct((t, LANES), I32),
            jax.ShapeDtypeStruct((1, N_EXPERTS), F32),
        ],
        compiler_params=_cparams(("arbitrary",)),
        name="mixer_merge_route",
    )(*meta, x, attn, u_pool, u_pool, u_pool, y_ssm, gates, mod_l, *wts)


def _expert_kernel(be_ref, nu_ref, tok_ref, tokn_ref, h2_hbm, wg_ref, wu_ref, wd_ref, y_ref,
                   xg_sc, wg_sc, wu_sc, wd_sc, sem, *, bm):
    b = pl.program_id(0)
    n_used = nu_ref[0]
    slot = b % 2

    def row_copy(tok, s, r):
        return pltpu.make_async_copy(h2_hbm.at[pl.ds(tok * ROW_TILE, ROW_TILE), :],
                                     xg_sc.at[s, pl.ds(r * ROW_TILE, ROW_TILE), :], sem.at[s])

    def issue(tok_smem, s):
        def body(r, c):
            row_copy(tok_smem[0, 0, r], s, r).start()
            return c
        lax.fori_loop(0, bm, body, 0, unroll=8)

    @pl.when(b == 0)
    def _():
        issue(tok_ref, 0)

    @pl.when(b + 1 < n_used)
    def _():
        issue(tokn_ref, 1 - slot)

    @pl.when(b < n_used)
    def _():
        pltpu.make_async_copy(h2_hbm.at[pl.ds(0, bm * ROW_TILE), :], xg_sc.at[slot], sem.at[slot]).wait()

        @pl.when((b == 0) | (be_ref[b] != be_ref[jnp.maximum(b - 1, 0)]))
        def _():
            wg_sc[...] = wg_ref[0, 0].astype(BF16)
            wu_sc[...] = wu_ref[0, 0].astype(BF16)
            wd_sc[...] = wd_ref[0, 0].astype(BF16)

        x = _load_row_tiles(xg_sc.at[slot], 0, bm).astype(BF16)
        g = jnp.dot(x, wg_sc[...], preferred_element_type=F32)
        u = jnp.dot(x, wu_sc[...], preferred_element_type=F32)
        hmid = (g * jax.nn.sigmoid(g) * u).astype(BF16)
        _store_row_tiles(y_ref, jnp.dot(hmid, wd_sc[...], preferred_element_type=F32))

    @pl.when(b >= n_used)
    def _():
        y_ref[...] = jnp.zeros(y_ref.shape, F32)


def _experts(h2, slot_token3, block_expert, n_used, w_gate, w_up, w_down, layer, bm):
    nb = slot_token3.shape[0]
    grid_spec = pltpu.PrefetchScalarGridSpec(
        num_scalar_prefetch=2,
        grid=(nb,),
        in_specs=[
            pl.BlockSpec((1, 1, bm), lambda b, be, nu: (b, 0, 0), memory_space=pltpu.SMEM),
            pl.BlockSpec((1, 1, bm), lambda b, be, nu: (jnp.minimum(b + 1, nb - 1), 0, 0),
                         memory_space=pltpu.SMEM),
            pl.BlockSpec(memory_space=pl.ANY),
            pl.BlockSpec((1, 1, D_MODEL, D_EXPERT), lambda b, be, nu: (layer, be[b], 0, 0)),
            pl.BlockSpec((1, 1, D_MODEL, D_EXPERT), lambda b, be, nu: (layer, be[b], 0, 0)),
            pl.BlockSpec((1, 1, D_EXPERT, D_MODEL), lambda b, be, nu: (layer, be[b], 0, 0)),
        ],
        out_specs=pl.BlockSpec((bm * ROW_TILE, LANES), lambda b, be, nu: (b, 0)),
        scratch_shapes=[
            pltpu.VMEM((2, bm * ROW_TILE, LANES), F32),
            pltpu.VMEM((D_MODEL, D_EXPERT), BF16),
            pltpu.VMEM((D_MODEL, D_EXPERT), BF16),
            pltpu.VMEM((D_EXPERT, D_MODEL), BF16),
            pltpu.SemaphoreType.DMA((2,)),
        ],
    )
    return pl.pallas_call(
        functools.partial(_expert_kernel, bm=bm),
        grid_spec=grid_spec,
        out_shape=jax.ShapeDtypeStruct((nb * bm * ROW_TILE, LANES), F32),
        compiler_params=_cparams(("arbitrary",)),
        name="routed_experts",
    )(block_expert, n_used, slot_token3, slot_token3, h2, w_gate, w_up, w_down)


def _combine_kernel(ts_ref, d_ref, dn_ref, y_hbm, x1_ref, h2_ref, gate_ref, mod_ref, wsg_ref, wsu_ref, wsd_ref,
                    o_ref, yg_sc, sem, *, tc):
    del ts_ref
    i = pl.program_id(0)
    n = pl.num_programs(0)
    slot = i % 2
    n_copies = tc * TOP_K

    def row_copy(src, s, row):
        return pltpu.make_async_copy(y_hbm.at[pl.ds(src * ROW_TILE, ROW_TILE), :],
                                     yg_sc.at[s, pl.ds(row * ROW_TILE, ROW_TILE), :], sem.at[s])

    def issue(dest_smem, s):
        def body(a, c):
            row_copy(dest_smem[0, 0, a], s, (a & (TOP_K - 1)) * tc + (a >> 3)).start()
            return c
        lax.fori_loop(0, n_copies, body, 0, unroll=8)

    @pl.when(i == 0)
    def _():
        issue(d_ref, 0)

    @pl.when(i + 1 < n)
    def _():
        issue(dn_ref, 1 - slot)

    pltpu.make_async_copy(y_hbm.at[pl.ds(0, n_copies * ROW_TILE), :], yg_sc.at[slot], sem.at[slot]).wait()

    gate = gate_ref[...]
    yg = yg_sc.at[slot]
    routed = gate[:, 0:1] * _load_row_tiles(yg, 0, tc)
    for k in range(1, TOP_K):
        routed = routed + gate[:, k:k + 1] * _load_row_tiles(yg, k * tc, tc)
    hb = _load_row_tiles(h2_ref, 0, tc).astype(BF16)
    g = jnp.dot(hb, wsg_ref[...], preferred_element_type=F32)
    u = jnp.dot(hb, wsu_ref[...], preferred_element_type=F32)
    shared = jnp.dot((g * jax.nn.sigmoid(g) * u).astype(BF16), wsd_ref[...], preferred_element_type=F32)
    o_ref[...] = x1_ref[...] + mod_ref[0][5:6] * (routed + shared)


def _combine(y_buf, dest3, x1, h2, gate, mod_l, ws_gate, ws_up, ws_down, tile_seq, tc):
    t = x1.shape[0]
    n = t // tc
    d_sh = ws_gate.shape[1]
    const = lambda i, ts: (0, 0)
    row = lambda i, ts: (i, 0)
    grid_spec = pltpu.PrefetchScalarGridSpec(
        num_scalar_prefetch=1,
        grid=(n,),
        in_specs=[
            pl.BlockSpec((1, 1, tc * TOP_K), lambda i, ts: (i, 0, 0), memory_space=pltpu.SMEM),
            pl.BlockSpec((1, 1, tc * TOP_K), lambda i, ts: (jnp.minimum(i + 1, n - 1), 0, 0),
                         memory_space=pltpu.SMEM),
            pl.BlockSpec(memory_space=pl.ANY),
            pl.BlockSpec((tc, D_MODEL), row),
            pl.BlockSpec((tc * ROW_TILE, LANES), row),
            pl.BlockSpec((tc, LANES), row),
            pl.BlockSpec((1, N_ADA, D_MODEL), lambda i, ts: (ts[i], 0, 0)),
            pl.BlockSpec((D_MODEL, d_sh), const),
            pl.BlockSpec((D_MODEL, d_sh), const),
            pl.BlockSpec((d_sh, D_MODEL), const),
        ],
        out_specs=pl.BlockSpec((tc, D_MODEL), row),
        scratch_shapes=[pltpu.VMEM((2, tc * TOP_K * ROW_TILE, LANES), F32), pltpu.SemaphoreType.DMA((2,))],
    )
    return pl.pallas_call(
        functools.partial(_combine_kernel, tc=tc),
        grid_spec=grid_spec,
        out_shape=jax.ShapeDtypeStruct((t, D_MODEL), F32),
        compiler_params=_cparams(("arbitrary",)),
        name="moe_combine",
    )(tile_seq, dest3, dest3, y_buf, x1, h2, gate, mod_l, ws_gate, ws_up, ws_down)


def _tile_meta(seq_lens, tile):
    seq, first, last, pos0, slen = [], [], [], [], []
    for s, n in enumerate(seq_lens):
        nt = n // tile
        for j in range(nt):
            seq.append(s)
            first.append(int(j == 0))
            last.append(int(j == nt - 1))
            pos0.append(j * tile)
            slen.append(n)
    return tuple(jnp.asarray(np.asarray(a, np.int32)) for a in (seq, first, last, pos0, slen))


def _scan_meta(seq_lens, tr):
    bwd, first = [], []
    base = 0
    for n in seq_lens:
        nt = n // SSM_CHUNK // tr
        for j in range(nt):
            bwd.append(base + nt - 1 - j)
            first.append(int(j == 0))
        base += nt
    return jnp.asarray(np.asarray(bwd, np.int32)), jnp.asarray(np.asarray(first, np.int32))


def _rope_tables(seq_lens):
    quarter = HEAD_DIM // 4
    freqs = ROPE_THETA ** (-jnp.arange(quarter, dtype=F32) / quarter)
    sign = jnp.tile(jnp.concatenate([-jnp.ones((quarter,), F32), jnp.ones((quarter,), F32)]), 2)
    cos_l, sin_l = [], []
    cache = {}
    for n in seq_lens:
        if n not in cache:
            pos = jnp.arange(n)
            ar = (pos // GRID_W).astype(F32)[:, None] * freqs
            ac = (pos % GRID_W).astype(F32)[:, None] * freqs
            ang = jnp.concatenate([ar, ar, ac, ac], axis=-1)
            cache[n] = (jnp.tile(jnp.cos(ang), (1, 2)), jnp.tile(jnp.sin(ang) * sign, (1, 2)))
        cos_l.append(cache[n][0])
        sin_l.append(cache[n][1])
    return jnp.concatenate(cos_l, axis=0), jnp.concatenate(sin_l, axis=0)


def _block_diag_ones(n, blk):
    r = np.arange(n) // blk
    return jnp.asarray((r[:, None] == r[None, :]).astype(np.float32)).astype(BF16)


def _dest_kernel(idx_ref, rank_ref, ps_ref, d_ref):
    idx = idx_ref[...]
    out = rank_ref[...]
    lane_e = lax.broadcasted_iota(I32, (idx.shape[0], N_EXPERTS), 1)
    lane = lax.broadcasted_iota(I32, idx.shape, 1)
    for k in range(TOP_K):
        start = jnp.sum(jnp.where(lane_e == idx[:, k:k + 1], ps_ref[...], 0.0), axis=1, keepdims=True)
        out = jnp.where(lane == k, out + start.astype(I32), out)
    d_ref[...] = out


def _dest_slots(idx, rank, pstart):
    t = idx.shape[0]
    tm = _pick(t, 512)
    row = lambda i: (i, 0)
    return pl.pallas_call(
        _dest_kernel,
        grid=(t // tm,),
        in_specs=[pl.BlockSpec((tm, LANES), row), pl.BlockSpec((tm, LANES), row),
                  pl.BlockSpec((1, N_EXPERTS), lambda i: (0, 0))],
        out_specs=pl.BlockSpec((tm, LANES), row),
        out_shape=jax.ShapeDtypeStruct((t, LANES), I32),
        compiler_params=_cparams(("arbitrary",)),
        name="moe_dest_slots",
    )(idx, rank, pstart)


def _moe_plan(idx, rank, counts, t, bm, nb):
    cnt = counts.reshape(N_EXPERTS).astype(I32)
    padded = (cnt + bm - 1) // bm * bm
    pend = jnp.cumsum(padded)
    pstart = pend - padded
    n_used = (pend[-1] // bm).astype(I32).reshape(1)
    block_expert = jnp.minimum(
        jnp.searchsorted(pend, jnp.arange(nb, dtype=I32) * bm, side='right'), N_EXPERTS - 1).astype(I32)
    dest = _dest_slots(idx, rank, pstart.astype(F32).reshape(1, N_EXPERTS))[:, :TOP_K]
    slot_token = jnp.zeros((nb * bm,), I32).at[dest.reshape(-1)].set(
        jnp.arange(t * TOP_K, dtype=I32) // TOP_K, unique_indices=True, mode='promise_in_bounds')
    return dest, slot_token, block_expert, n_used


def kernel(x_prompt, x_sample, c_prompt, c_sample, w_ada, b_ada, norm1_g, w_in, q_norm_g, k_norm_g, w_attn_o, pool_w, pool_scale, w_pool_o, ssm_a_re, ssm_a_im, ssm_log_dt, ssm_b_re, ssm_b_im, ssm_c_re, ssm_c_im, ssm_d, w_glu, w_out, norm2_g, w_router, b_router, w_exp_gate, w_exp_up, w_exp_down, w_sh_gate, w_sh_up, w_sh_down):
    b1, l1, d = x_prompt.shape
    b2, l2, _ = x_sample.shape
    depth = w_in.shape[0]
    assert d == D_MODEL and b1 + b2 <= MOD_ROWS
    seq_lens = [l1] * b1 + [l2] * b2
    t = b1 * l1 + b2 * l2
    lmin = min(l1, l2)

    tm_in = _pick(lmin, 512)
    tm_post = _pick(lmin, 256)
    tc_comb = _pick(lmin, 128)
    tr_scan = _pick(lmin // SSM_CHUNK, 256)
    bm = 256
    nb = -(-(t * TOP_K + N_EXPERTS * (bm - 1)) // bm)

    x = jnp.concatenate([x_prompt.reshape(b1 * l1, d), x_sample.reshape(b2 * l2, d)], axis=0)
    c_all = jnp.concatenate([c_prompt, c_sample, jnp.zeros((MOD_ROWS - b1 - b2, d), F32)], axis=0)
    mod = _modulation(c_all, w_ada, b_ada).reshape(depth, MOD_ROWS, N_ADA, d)

    cos_t, sin_t = _rope_tables(seq_lens)
    ones_bd = _block_diag_ones(ATTN_DIM, HEAD_DIM)
    tril = jnp.asarray(np.tril(np.ones((tm_post, tm_post), np.float32), -1)).astype(BF16)
    meta_in = _tile_meta(seq_lens, tm_in)
    meta_post = _tile_meta(seq_lens, tm_post)
    meta_comb = _tile_meta(seq_lens, tc_comb)
    bwd_tile, first_tile = _scan_meta(seq_lens, tr_scan)

    for l in range(depth):
        q8, k2, v2, u_pool, u_ssm, gates = _inproj(
            x, mod[l], norm1_g[l].reshape(1, d), w_in[l].astype(BF16),
            jnp.tile(q_norm_g[l], N_Q_HEADS).reshape(1, ATTN_DIM),
            jnp.tile(k_norm_g[l], N_KV_HEADS).reshape(1, KV_DIM),
            cos_t, sin_t, ones_bd, meta_in[0], tm_in)
        attn = jnp.concatenate(
            [_flash(q8, k2, v2, 0, b1, l1), _flash(q8, k2, v2, b1 * l1, b2, l2)], axis=0)
        tables = _ssm_tables(ssm_a_re[l], ssm_a_im[l], ssm_log_dt[l], ssm_b_re[l], ssm_b_im[l],
                             ssm_c_re[l], ssm_c_im[l], ssm_d[l])
        y_ssm = _bidir_ssm(u_ssm, tables, bwd_tile, first_tile, tr_scan)

        pool_bd = jax.scipy.linalg.block_diag(*[pool_w[l, g] for g in range(len(POOL_WINDOWS))])
        wts = (w_attn_o[l].astype(BF16), pool_bd.astype(BF16), pool_scale[l].reshape(1, POOL_DIM),
               w_pool_o[l].astype(BF16), w_glu[l].astype(BF16), w_out[l].astype(BF16),
               norm2_g[l].reshape(1, d), w_router[l], b_router[l].reshape(1, N_EXPERTS), tril)
        x1, h2, idx, gate, rank, counts = _post(x, attn, u_pool, y_ssm, gates, mod[l], wts, meta_post, tm_post)

        dest, slot_token, block_expert, n_used = _moe_plan(idx, rank, counts, t, bm, nb)
        y_buf = _experts(h2, slot_token.reshape(nb, 1, bm), block_expert, n_used,
                         w_exp_gate, w_exp_up, w_exp_down, l, bm)
        x = _combine(y_buf, dest.reshape(t // tc_comb, 1, tc_comb * TOP_K), x1, h2, gate, mod[l],
                     w_sh_gate[l].astype(BF16), w_sh_up[l].astype(BF16), w_sh_down[l].astype(BF16),
                     meta_comb[0], tc_comb)

    return (x[:b1 * l1].reshape(b1, l1, d), x[b1 * l1:].reshape(b2, l2, d))
```

```python
import functools
import math

import jax
import jax.numpy as jnp
import numpy as np
from jax import lax
from jax.experimental import pallas as pl
from jax.experimental.pallas import tpu as pltpu

F32 = jnp.float32
BF16 = jnp.bfloat16
I32 = jnp.int32
HIGHEST = lax.Precision.HIGHEST

D_MODEL = 1024
GRID_W = 64
HEAD_DIM = 64
N_Q_HEADS = 8
N_KV_HEADS = 2
Q_PER_KV = N_Q_HEADS // N_KV_HEADS
ATTN_DIM = N_Q_HEADS * HEAD_DIM
KV_DIM = N_KV_HEADS * HEAD_DIM
ROPE_THETA = 10000.0
POOL_WINDOWS = (2, 4, 8, 16)
POOL_DIM = 256
POOL_GROUP = 64
POOL_HALO = 8
SSM_DIM = 256
SSM_P = 16
SSM_G = 16
SSM_N = 64
SSM_CHUNK = 16
SSM_ROW = SSM_CHUNK * SSM_P
SSM_STATE = 2 * SSM_N
N_EXPERTS = 256
TOP_K = 8
N_EXPERT_GROUPS = 8
GROUP_SIZE = N_EXPERTS // N_EXPERT_GROUPS
TOPK_GROUPS = 4
D_EXPERT = 256
ROUTED_SCALE = 2.5
N_ADA = 6
EPS = 1e-6
IN_DIM = ATTN_DIM + 2 * KV_DIM + POOL_DIM + SSM_DIM + 3 * D_MODEL
OFF_K = ATTN_DIM
OFF_V = OFF_K + KV_DIM
OFF_POOL = OFF_V + KV_DIM
OFF_SSM = OFF_POOL + POOL_DIM
OFF_GATES = OFF_SSM + SSM_DIM
MOD_ROWS = 8
LOG2_E = math.log2(math.e)
VT_ROWS = HEAD_DIM + 16

V7X_VMEM_BYTES = 64 * 1024 * 1024
VMEM_LIMIT = V7X_VMEM_BYTES - 8 * 1024 * 1024
LANES = 128


def _cparams(sem):
    return pltpu.CompilerParams(dimension_semantics=sem, vmem_limit_bytes=VMEM_LIMIT)


def _pick(n, pref):
    t = min(n, pref)
    while n % t:
        t //= 2
    return t


ROW_TILE = D_MODEL // LANES


def _store_row_tiles(ref, x):
    n = x.shape[0]
    for s in range(ROW_TILE):
        ref[pl.ds(s, n, stride=ROW_TILE), :] = x[:, s * LANES:(s + 1) * LANES]


def _load_row_tiles(ref, first_row, n):
    return jnp.concatenate(
        [ref[pl.ds(first_row * ROW_TILE + s, n, stride=ROW_TILE), :] for s in range(ROW_TILE)], axis=1)


def _mod_kernel(c_ref, w_ref, b_ref, o_ref):
    c = c_ref[...]
    a = c * jax.nn.sigmoid(c)
    o_ref[0] = jnp.dot(a, w_ref[0], preferred_element_type=F32, precision=HIGHEST) + b_ref[0]


def _modulation(c_all, w_ada, b_ada):
    depth, d, n = w_ada.shape
    bn = _pick(n, 1536)
    return pl.pallas_call(
        _mod_kernel,
        grid=(depth, n // bn),
        in_specs=[
            pl.BlockSpec((MOD_ROWS, d), lambda l, j: (0, 0)),
            pl.BlockSpec((1, d, bn), lambda l, j: (l, 0, j)),
            pl.BlockSpec((1, 1, bn), lambda l, j: (l, 0, j)),
        ],
        out_specs=pl.BlockSpec((1, MOD_ROWS, bn), lambda l, j: (l, 0, j)),
        out_shape=jax.ShapeDtypeStruct((depth, MOD_ROWS, n), F32),
        compiler_params=_cparams(("arbitrary", "arbitrary")),
        name="adaln_mod",
    )(c_all, w_ada, b_ada.reshape(depth, 1, n))


def _rmsnorm_mod(x, g, scale, shift):
    y = x * lax.rsqrt(jnp.mean(x * x, axis=-1, keepdims=True) + EPS)
    return (y * g) * (1.0 + scale) + shift


def _head_norm_rope(z, gain, cos, sin_signed, ones_bd):
    z2 = z * z
    hi = z2.astype(BF16)
    lo = (z2 - hi.astype(F32)).astype(BF16)
    ss = (jnp.dot(hi, ones_bd, preferred_element_type=F32)
          + jnp.dot(lo, ones_bd, preferred_element_type=F32))
    y = (z * lax.rsqrt(ss * (1.0 / HEAD_DIM) + EPS)) * gain
    w = z.shape[1]
    quarter = HEAD_DIM // 4
    from_right = pltpu.roll(y, w - quarter, 1)
    from_left = pltpu.roll(y, quarter, 1)
    lane = lax.broadcasted_iota(I32, y.shape, 1)
    rot = jnp.where((lane & quarter) == 0, from_right, from_left)
    return y * cos + rot * sin_signed


def _inproj_kernel(ts_ref, x_ref, mod_ref, g_ref, w_ref, gq_ref, gk_ref, cos_ref, sin_ref, ones_ref,
                   q_ref, k_ref, v_ref, up_ref, us_ref, gt_ref):
    del ts_ref
    x = x_ref[...]
    mod = mod_ref[0]
    h = _rmsnorm_mod(x, g_ref[...], mod[1:2], mod[0:1]).astype(BF16)

    cos = cos_ref[...]
    sin = sin_ref[...]
    zq = jnp.dot(h, w_ref[:, 0:ATTN_DIM], preferred_element_type=F32)
    reps = ATTN_DIM // LANES
    yq = _head_norm_rope(zq, gq_ref[...], jnp.concatenate([cos] * reps, axis=1),
                         jnp.concatenate([sin] * reps, axis=1), ones_ref[...]) * (HEAD_DIM ** -0.5 * LOG2_E)
    for hh in range(N_Q_HEADS):
        q_ref[hh] = yq[:, hh * HEAD_DIM:(hh + 1) * HEAD_DIM].astype(BF16)

    zkv = jnp.dot(h, w_ref[:, OFF_K:OFF_POOL], preferred_element_type=F32)
    yk = _head_norm_rope(zkv[:, :KV_DIM], gk_ref[...], cos, sin, ones_ref[0:KV_DIM, 0:KV_DIM])
    zv = zkv[:, KV_DIM:]
    for j in range(N_KV_HEADS):
        k_ref[j] = yk[:, j * HEAD_DIM:(j + 1) * HEAD_DIM].astype(BF16)
        v_ref[j, 0:HEAD_DIM, :] = zv[:, j * HEAD_DIM:(j + 1) * HEAD_DIM].T.astype(BF16)
        v_ref[j, HEAD_DIM:VT_ROWS, :] = jnp.ones((VT_ROWS - HEAD_DIM, zv.shape[0]), BF16)

    zps = jnp.dot(h, w_ref[:, OFF_POOL:OFF_GATES], preferred_element_type=F32)
    up_ref[...] = zps[:, :POOL_DIM]
    us_ref[...] = zps[:, POOL_DIM:]
    for c in range(3):
        zg = jnp.dot(h, w_ref[:, OFF_GATES + c * D_MODEL:OFF_GATES + (c + 1) * D_MODEL],
                     preferred_element_type=F32)
        gt_ref[:, c * D_MODEL:(c + 1) * D_MODEL] = jax.nn.sigmoid(zg).astype(BF16)


def _inproj(x, mod_l, g1, w_in_bf, gq_t, gk_t, cos_t, sin_t, ones_bd, tile_seq, tm):
    t = x.shape[0]
    const = lambda i, ts: (0, 0)
    row = lambda i, ts: (i, 0)
    grid_spec = pltpu.PrefetchScalarGridSpec(
        num_scalar_prefetch=1,
        grid=(t // tm,),
        in_specs=[
            pl.BlockSpec((tm, D_MODEL), row),
            pl.BlockSpec((1, N_ADA, D_MODEL), lambda i, ts: (ts[i], 0, 0)),
            pl.BlockSpec((1, D_MODEL), const),
            pl.BlockSpec((D_MODEL, IN_DIM), const),
            pl.BlockSpec((1, ATTN_DIM), const),
            pl.BlockSpec((1, KV_DIM), const),
            pl.BlockSpec((tm, LANES), row),
            pl.BlockSpec((tm, LANES), row),
            pl.BlockSpec((ATTN_DIM, ATTN_DIM), const),
        ],
        out_specs=[
            pl.BlockSpec((N_Q_HEADS, tm, HEAD_DIM), lambda i, ts: (0, i, 0)),
            pl.BlockSpec((N_KV_HEADS, tm, HEAD_DIM), lambda i, ts: (0, i, 0)),
            pl.BlockSpec((N_KV_HEADS, VT_ROWS, tm), lambda i, ts: (0, 0, i)),
            pl.BlockSpec((tm, POOL_DIM), row),
            pl.BlockSpec((tm, SSM_DIM), row),
            pl.BlockSpec((tm, 3 * D_MODEL), row),
        ],
    )
    return pl.pallas_call(
        _inproj_kernel,
        grid_spec=grid_spec,
        out_shape=[
            jax.ShapeDtypeStruct((N_Q_HEADS, t, HEAD_DIM), BF16),
            jax.ShapeDtypeStruct((N_KV_HEADS, t, HEAD_DIM), BF16),
            jax.ShapeDtypeStruct((N_KV_HEADS, VT_ROWS, t), BF16),
            jax.ShapeDtypeStruct((t, POOL_DIM), F32),
            jax.ShapeDtypeStruct((t, SSM_DIM), F32),
            jax.ShapeDtypeStruct((t, 3 * D_MODEL), BF16),
        ],
        compiler_params=_cparams(("arbitrary",)),
        name="norm1_inproj",
    )(tile_seq, x, mod_l, g1, w_in_bf, gq_t, gk_t, cos_t, sin_t, ones_bd)


def _flash_kernel(q_ref, k_ref, vt_ref, o_ref, m_sc, acc_sc):
    kb = pl.program_id(3)

    @pl.when(kb == 0)
    def _():
        m_sc[...] = jnp.full(m_sc.shape, -jnp.inf, F32)
        acc_sc[...] = jnp.zeros(acc_sc.shape, F32)

    k = k_ref[0]
    vt = vt_ref[0]
    heads = range(Q_PER_KV)
    qs = [q_ref[hh] for hh in heads]
    m_prev = [m_sc[hh] for hh in heads]
    acc_prev = [acc_sc[hh] for hh in heads]

    def scores(hh):
        return lax.dot_general(k, qs[hh], (((1,), (1,)), ((), ())), preferred_element_type=F32)

    ahead = 2
    s = {hh: scores(hh) for hh in range(ahead)}
    new = []
    for hh in heads:
        if hh + ahead < Q_PER_KV:
            s[hh + ahead] = scores(hh + ahead)
        m_new = jnp.maximum(m_prev[hh], jnp.max(s[hh], axis=0, keepdims=True))
        alpha = jnp.exp2(m_prev[hh] - m_new)
        p = jnp.exp2(s.pop(hh) - m_new).astype(BF16)
        acc_new = acc_prev[hh] * alpha + jnp.dot(vt, p, preferred_element_type=F32)
        new.append((m_new, acc_new))
    for hh, (m_new, acc_new) in enumerate(new):
        m_sc[hh] = m_new
        acc_sc[hh] = acc_new

    @pl.when(kb == pl.num_programs(3) - 1)
    def _():
        for hh in range(Q_PER_KV):
            acc = acc_sc[hh]
            o = acc[:HEAD_DIM] / acc[HEAD_DIM:HEAD_DIM + 1]
            o_ref[:, hh * HEAD_DIM:(hh + 1) * HEAD_DIM] = o.T.astype(BF16)


def _flash(q8, k2, vt2, off, bsz, seq):
    tq = _pick(seq, 512)
    tk = _pick(seq, 2048)
    nq, nk = seq // tq, seq // tk
    assert off % tq == 0 and off % tk == 0
    oq, ok = off // tq, off // tk
    return pl.pallas_call(
        _flash_kernel,
        grid=(N_KV_HEADS, bsz, nq, nk),
        in_specs=[
            pl.BlockSpec((Q_PER_KV, tq, HEAD_DIM), lambda j, b, i, kb: (j, oq + b * nq + i, 0)),
            pl.BlockSpec((1, tk, HEAD_DIM), lambda j, b, i, kb: (j, ok + b * nk + kb, 0)),
            pl.BlockSpec((1, VT_ROWS, tk), lambda j, b, i, kb: (j, 0, ok + b * nk + kb)),
        ],
        out_specs=pl.BlockSpec((tq, Q_PER_KV * HEAD_DIM), lambda j, b, i, kb: (b * nq + i, j)),
        out_shape=jax.ShapeDtypeStruct((bsz * seq, ATTN_DIM), BF16),
        scratch_shapes=[
            pltpu.VMEM((Q_PER_KV, 1, tq), F32),
            pltpu.VMEM((Q_PER_KV, VT_ROWS, tq), F32),
        ],
        compiler_params=_cparams(("arbitrary", "arbitrary", "arbitrary", "arbitrary")),
        name="flash_gqa",
    )(q8, k2, vt2)


def _ssm_tables(a_re, a_im, log_dt, b_re, b_im, c_re, c_im, d_skip):
    tc = SSM_CHUNK
    a = lax.complex(a_re.astype(F32), a_im.astype(F32))
    dt = jnp.exp(log_dt.astype(F32))[..., None]
    adt = a * dt
    a_bar = jnp.exp(adt)
    b_bar = ((a_bar - 1.0) / a)[..., None] * lax.complex(b_re.astype(F32), b_im.astype(F32))
    c_mat = lax.complex(c_re.astype(F32), c_im.astype(F32))
    taus = jnp.arange(tc + 1, dtype=F32)
    pw = jnp.exp(adt[None] * taus[:, None, None, None])

    kern = jnp.real(jnp.einsum('dgpn,tdgn,dgnq->dgtpq', c_mat, pw[:tc], b_bar))
    s_idx = jnp.arange(tc)[:, None]
    t_idx = jnp.arange(tc)[None, :]
    lag_f = jnp.clip(t_idx - s_idx, 0, tc - 1)
    lag_b = jnp.clip(s_idx - t_idx, 0, tc - 1)
    kf = jnp.where((t_idx >= s_idx)[None, :, :, None, None], kern[0][:, lag_f], 0.0)
    kb = jnp.where((s_idx >= t_idx)[None, :, :, None, None], kern[1][:, lag_b], 0.0)
    w_loc = (kf + kb).transpose(0, 1, 4, 2, 3).reshape(SSM_G, SSM_ROW, SSM_ROW)
    w_loc = w_loc + jnp.eye(SSM_ROW, dtype=F32)[None] * jnp.tile(
        d_skip.astype(F32).reshape(SSM_G, 1, SSM_P), (1, tc, 1)).reshape(SSM_G, 1, SSM_ROW)

    def ri(z, axis):
        return jnp.concatenate([jnp.real(z), jnp.imag(z)], axis=axis)

    pf = jnp.einsum('sgn,gnq->gsqn', pw[:tc][::-1, 0], b_bar[0])
    pb = jnp.einsum('sgn,gnq->gsqn', pw[:tc, 1], b_bar[1])
    p_st = jnp.concatenate([ri(pf, -1), ri(pb, -1)], axis=-1).reshape(SSM_G, SSM_ROW, 2 * SSM_STATE)
    qf = jnp.einsum('gpn,tgn->gntp', c_mat[0], pw[1:tc + 1, 0])
    qb = jnp.einsum('gpn,tgn->gntp', c_mat[1], pw[1:tc + 1, 1][::-1])
    q_f = jnp.concatenate([jnp.real(qf), -jnp.imag(qf)], axis=1).reshape(SSM_G, SSM_STATE, SSM_ROW)
    q_b = jnp.concatenate([jnp.real(qb), -jnp.imag(qb)], axis=1).reshape(SSM_G, SSM_STATE, SSM_ROW)
    a16 = pw[tc]
    a1 = jnp.concatenate([jnp.real(a16), jnp.real(a16)], axis=-1).reshape(2, SSM_G * SSM_STATE)
    a2 = jnp.concatenate([-jnp.imag(a16), jnp.imag(a16)], axis=-1).reshape(2, SSM_G * SSM_STATE)
    a_mul = jnp.stack([a1[0], a2[0], a1[1], a2[1]], axis=0)
    return w_loc.astype(BF16), p_st.astype(BF16), q_f.astype(BF16), q_b.astype(BF16), a_mul


def _ssm_state_kernel(u_ref, p_ref, s_ref):
    s = jnp.dot(u_ref[0].astype(BF16), p_ref[0], preferred_element_type=F32)
    s_ref[0] = s[:, :SSM_STATE]
    s_ref[1] = s[:, SSM_STATE:]


def _ssm_states(ug, p_st, tc):
    g, nc, _ = ug.shape
    return pl.pallas_call(
        _ssm_state_kernel,
        grid=(g, nc // tc),
        in_specs=[
            pl.BlockSpec((1, tc, SSM_ROW), lambda gi, i: (gi, i, 0)),
            pl.BlockSpec((1, SSM_ROW, 2 * SSM_STATE), lambda gi, i: (gi, 0, 0)),
        ],
        out_specs=pl.BlockSpec((2, tc, SSM_STATE), lambda gi, i: (0, i, gi)),
        out_shape=jax.ShapeDtypeStruct((2, nc, g * SSM_STATE), F32),
        compiler_params=_cparams(("arbitrary", "arbitrary")),
        name="ssm_chunk_states",
    )(ug, p_st)


def _swap_halves(x):
    parts = [pltpu.roll(x[:, g * LANES:(g + 1) * LANES], LANES // 2, 1) for g in range(x.shape[1] // LANES)]
    return jnp.concatenate(parts, axis=1)


def _ssm_scan_kernel(bt_ref, first_ref, sf_ref, sb_ref, am_ref, xf_ref, zb_ref, st_sc, sfw_sc, sbw_sc, *, tr):
    n = pl.program_id(0)
    del bt_ref

    @pl.when(first_ref[n] == 1)
    def _():
        st_sc[...] = jnp.zeros(st_sc.shape, F32)

    sfw_sc[...] = _swap_halves(sf_ref[0])
    sbw_sc[...] = _swap_halves(sb_ref[0])
    a1f, a2f, a1b, a2b = am_ref[0:1], am_ref[1:2], am_ref[2:3], am_ref[3:4]

    def body(r, carry):
        xf, xfw, zb, zbw = carry
        xf_ref[pl.ds(r, 1), :] = xf
        sf = sf_ref[0, pl.ds(r, 1), :]
        sfw = sfw_sc[pl.ds(r, 1), :]
        nxf = a1f * xf + a2f * xfw + sf
        nxfw = a1f * xfw - a2f * xf + sfw
        rb = tr - 1 - r
        zb_ref[pl.ds(rb, 1), :] = zb
        sb = sb_ref[0, pl.ds(rb, 1), :]
        sbw = sbw_sc[pl.ds(rb, 1), :]
        nzb = a1b * zb + a2b * zbw + sb
        nzbw = a1b * zbw - a2b * zb + sbw
        return nxf, nxfw, nzb, nzbw

    init = (st_sc[0:1], st_sc[1:2], st_sc[2:3], st_sc[3:4])
    xf, xfw, zb, zbw = lax.fori_loop(0, tr, body, init)
    st_sc[0:1] = xf
    st_sc[1:2] = xfw
    st_sc[2:3] = zb
    st_sc[3:4] = zbw


def _ssm_scan(s_st, a_mul, bwd_tile, first_tile, tr):
    _, nc, w = s_st.shape
    grid_spec = pltpu.PrefetchScalarGridSpec(
        num_scalar_prefetch=2,
        grid=(nc // tr,),
        in_specs=[
            pl.BlockSpec((1, tr, w), lambda n, bt, ft: (0, n, 0)),
            pl.BlockSpec((1, tr, w), lambda n, bt, ft: (1, bt[n], 0)),
            pl.BlockSpec((4, w), lambda n, bt, ft: (0, 0)),
        ],
        out_specs=[
            pl.BlockSpec((tr, w), lambda n, bt, ft: (n, 0)),
            pl.BlockSpec((tr, w), lambda n, bt, ft: (bt[n], 0)),
        ],
        scratch_shapes=[pltpu.VMEM((4, w), F32), pltpu.VMEM((tr, w), F32), pltpu.VMEM((tr, w), F32)],
    )
    return pl.pallas_call(
        functools.partial(_ssm_scan_kernel, tr=tr),
        grid_spec=grid_spec,
        out_shape=[jax.ShapeDtypeStruct((nc, w), F32), jax.ShapeDtypeStruct((nc, w), F32)],
        compiler_params=_cparams(("arbitrary",)),
        name="ssm_chunk_scan",
    )(bwd_tile, first_tile, s_st, s_st, a_mul)


def _ssm_out_kernel(u_ref, xf_ref, zb_ref, w_ref, qf_ref, qb_ref, y_ref):
    y = jnp.dot(u_ref[0].astype(BF16), w_ref[0], preferred_element_type=F32)
    y += jnp.dot(xf_ref[...].astype(BF16), qf_ref[0], preferred_element_type=F32)
    y += jnp.dot(zb_ref[...].astype(BF16), qb_ref[0], preferred_element_type=F32)
    y_ref[0] = y


def _ssm_outputs(ug, xf, zb, w_loc, q_f, q_b, tc):
    g, nc, _ = ug.shape
    return pl.pallas_call(
        _ssm_out_kernel,
        grid=(g, nc // tc),
        in_specs=[
            pl.BlockSpec((1, tc, SSM_ROW), lambda gi, i: (gi, i, 0)),
            pl.BlockSpec((tc, SSM_STATE), lambda gi, i: (i, gi)),
            pl.BlockSpec((tc, SSM_STATE), lambda gi, i: (i, gi)),
            pl.BlockSpec((1, SSM_ROW, SSM_ROW), lambda gi, i: (gi, 0, 0)),
            pl.BlockSpec((1, SSM_STATE, SSM_ROW), lambda gi, i: (gi, 0, 0)),
            pl.BlockSpec((1, SSM_STATE, SSM_ROW), lambda gi, i: (gi, 0, 0)),
        ],
        out_specs=pl.BlockSpec((1, tc, SSM_ROW), lambda gi, i: (gi, i, 0)),
        out_shape=jax.ShapeDtypeStruct((g, nc, SSM_ROW), F32),
        compiler_params=_cparams(("arbitrary", "arbitrary")),
        name="ssm_chunk_outputs",
    )(ug, xf, zb, w_loc, q_f, q_b)


def _bidir_ssm(u_ssm, tables, bwd_tile, first_tile, tr):
    w_loc, p_st, q_f, q_b, a_mul = tables
    t = u_ssm.shape[0]
    nc = t // SSM_CHUNK
    ug = u_ssm.reshape(nc, SSM_CHUNK, SSM_G, SSM_P).transpose(2, 0, 1, 3).reshape(SSM_G, nc, SSM_ROW)
    tc = _pick(nc, 512)
    s_st = _ssm_states(ug, p_st, tc)
    xf, zb = _ssm_scan(s_st, a_mul, bwd_tile, first_tile, tr)
    yg = _ssm_outputs(ug, xf, zb, w_loc, q_f, q_b, tc)
    return yg.reshape(SSM_G, nc, SSM_CHUNK, SSM_P).transpose(1, 2, 0, 3).reshape(t, SSM_DIM)


def _pool_mixer(ext_sc, u, pos, seq_len, tm):
    halves = []
    for half in range(2):
        sums = []
        for w in POOL_WINDOWS[2 * half:2 * half + 2]:
            lo = w // 2
            hi = w - lo - 1
            acc = ext_sc[pl.ds(POOL_HALO - lo, tm), half * LANES:(half + 1) * LANES]
            for d in range(-lo + 1, hi + 1):
                acc = acc + ext_sc[pl.ds(POOL_HALO + d, tm), half * LANES:(half + 1) * LANES]
            cnt = (jnp.minimum(pos + hi, seq_len - 1) - jnp.maximum(pos - lo, 0) + 1).astype(F32)
            sums.append(acc / cnt)
        lane = lax.broadcasted_iota(I32, (tm, LANES), 1)
        halves.append(jnp.where(lane < POOL_GROUP, sums[0], sums[1]))
    return jnp.concatenate(halves, axis=1) - u


def _route(h2, wr_ref, br_ref):
    logits = jnp.dot(h2, wr_ref[...], preferred_element_type=F32, precision=HIGHEST)
    scores = jax.nn.sigmoid(logits)
    choice = scores + br_ref[...]
    lane = lax.broadcasted_iota(I32, choice.shape, 1)
    neg = jnp.float32(-jnp.inf)

    def first_argmax(c):
        m = jnp.max(c, axis=1, keepdims=True)
        return jnp.min(jnp.where(c == m, lane, N_EXPERTS), axis=1, keepdims=True), m

    in_group, gscore = [], []
    for g in range(N_EXPERT_GROUPS):
        ing = (lane >= g * GROUP_SIZE) & (lane < (g + 1) * GROUP_SIZE)
        cg = jnp.where(ing, choice, neg)
        i1, m1 = first_argmax(cg)
        m2 = jnp.max(jnp.where(lane == i1, neg, cg), axis=1, keepdims=True)
        in_group.append(ing)
        gscore.append(m1 + m2)
    emask = jnp.zeros(choice.shape, jnp.bool_)
    for g in range(N_EXPERT_GROUPS):
        ahead = jnp.zeros(gscore[g].shape, I32)
        for o in range(N_EXPERT_GROUPS):
            if o == g:
                continue
            beats = (gscore[o] > gscore[g]) | ((gscore[o] == gscore[g]) & (o < g))
            ahead = ahead + beats.astype(I32)
        emask = emask | (in_group[g] & (ahead < TOPK_GROUPS))
    c = jnp.where(emask, choice, neg)
    picks = []
    for _ in range(TOP_K):
        i, _m = first_argmax(c)
        oh = lane == i
        w = jnp.sum(jnp.where(oh, scores, 0.0), axis=1, keepdims=True)
        c = jnp.where(oh, neg, c)
        picks.append((i, w, oh))
    wsum = picks[0][1]
    for _, w, _ in picks[1:]:
        wsum = wsum + w
    return [(i, w / wsum * ROUTED_SCALE, oh) for i, w, oh in picks]


def _post_kernel(ts_ref, tf_ref, tl_ref, tp_ref, tn_ref,
                 x_ref, at_ref, upp_ref, up_ref, upn_ref, ys_ref, gt_ref, mod_ref,
                 wao_ref, pbd_ref, psc_ref, wpo_ref, wgl_ref, wo_ref, g2_ref, wr_ref, br_ref, tril_ref,
                 x1_ref, h2_ref, idx_ref, gate_ref, rank_ref, cnt_ref, ext_sc, carry_sc, *, tm):
    del ts_ref
    i = pl.program_id(0)
    mod = mod_ref[0]

    attn = jnp.dot(at_ref[...], wao_ref[...], preferred_element_type=F32)

    u = up_ref[...]
    ext_sc[0:POOL_HALO] = jnp.where(tf_ref[i] == 1, 0.0, upp_ref[...])
    ext_sc[POOL_HALO:POOL_HALO + tm] = u
    ext_sc[POOL_HALO + tm:2 * POOL_HALO + tm] = jnp.where(tl_ref[i] == 1, 0.0, upn_ref[...])
    pos = tp_ref[i] + lax.broadcasted_iota(I32, (tm, LANES), 0)
    pm = _pool_mixer(ext_sc, u, pos, tn_ref[i], tm)
    pool = jnp.dot(pm.astype(BF16), pbd_ref[...], preferred_element_type=F32) * psc_ref[...]
    pool = jnp.dot(pool.astype(BF16), wpo_ref[...], preferred_element_type=F32)

    z = jnp.dot(jax.nn.gelu(ys_ref[...]).astype(BF16), wgl_ref[...], preferred_element_type=F32)
    ssm = z[:, :D_MODEL] * jax.nn.sigmoid(z[:, D_MODEL:])

    merged = (gt_ref[:, 0:D_MODEL].astype(F32) * attn
              + gt_ref[:, D_MODEL:2 * D_MODEL].astype(F32) * pool
              + gt_ref[:, 2 * D_MODEL:3 * D_MODEL].astype(F32) * ssm)
    mix = jnp.dot(merged.astype(BF16), wo_ref[...], preferred_element_type=F32)
    x1 = x_ref[...] + mod[2:3] * mix
    x1_ref[...] = x1
    h2 = _rmsnorm_mod(x1, g2_ref[...], mod[4:5], mod[3:4])
    _store_row_tiles(h2_ref, h2)

    picks = _route(h2, wr_ref, br_ref)

    @pl.when(i == 0)
    def _():
        carry_sc[...] = jnp.zeros(carry_sc.shape, F32)

    sel = picks[0][2]
    for _, _, oh in picks[1:]:
        sel = sel | oh
    sel_f = sel.astype(F32)
    ranks = jnp.dot(tril_ref[...], sel_f.astype(BF16), preferred_element_type=F32) + carry_sc[...]
    carry = carry_sc[...] + jnp.sum(sel_f, axis=0, keepdims=True)
    carry_sc[...] = carry
    cnt_ref[...] = carry

    lane = lax.broadcasted_iota(I32, (tm, LANES), 1)
    idx_o = jnp.zeros((tm, LANES), I32)
    gate_o = jnp.zeros((tm, LANES), F32)
    rank_o = jnp.zeros((tm, LANES), I32)
    for k, (ik, gk, oh) in enumerate(picks):
        rk = jnp.sum(jnp.where(oh, ranks, 0.0), axis=1, keepdims=True).astype(I32)
        idx_o = jnp.where(lane == k, ik, idx_o)
        gate_o = jnp.where(lane == k, gk, gate_o)
        rank_o = jnp.where(lane == k, rk, rank_o)
    idx_ref[...] = idx_o
    gate_ref[...] = gate_o
    rank_ref[...] = rank_o


def _post(x, attn, u_pool, y_ssm, gates, mod_l, wts, meta, tm):
    t = x.shape[0]
    nh = t // POOL_HALO
    hb = tm // POOL_HALO
    npf = 5
    const = lambda i, *_: (0, 0)
    row = lambda i, *_: (i, 0)
    grid_spec = pltpu.PrefetchScalarGridSpec(
        num_scalar_prefetch=npf,
        grid=(t // tm,),
        in_specs=[
            pl.BlockSpec((tm, D_MODEL), row),
            pl.BlockSpec((tm, ATTN_DIM), row),
            pl.BlockSpec((POOL_HALO, POOL_DIM), lambda i, *_: (jnp.maximum(i * hb - 1, 0), 0)),
            pl.BlockSpec((tm, POOL_DIM), row),
            pl.BlockSpec((POOL_HALO, POOL_DIM), lambda i, *_: (jnp.minimum((i + 1) * hb, nh - 1), 0)),
            pl.BlockSpec((tm, SSM_DIM), row),
            pl.BlockSpec((tm, 3 * D_MODEL), row),
            pl.BlockSpec((1, N_ADA, D_MODEL), lambda i, ts, *_: (ts[i], 0, 0)),
            pl.BlockSpec((ATTN_DIM, D_MODEL), const),
            pl.BlockSpec((POOL_DIM, POOL_DIM), const),
            pl.BlockSpec((1, POOL_DIM), const),
            pl.BlockSpec((POOL_DIM, D_MODEL), const),
            pl.BlockSpec((SSM_DIM, 2 * D_MODEL), const),
            pl.BlockSpec((D_MODEL, D_MODEL), const),
            pl.BlockSpec((1, D_MODEL), const),
            pl.BlockSpec((D_MODEL, N_EXPERTS), const),
            pl.BlockSpec((1, N_EXPERTS), const),
            pl.BlockSpec((tm, tm), const),
        ],
        out_specs=[
            pl.BlockSpec((tm, D_MODEL), row),
            pl.BlockSpec((tm * ROW_TILE, LANES), row),
            pl.BlockSpec((tm, LANES), row),
            pl.BlockSpec((tm, LANES), row),
            pl.BlockSpec((tm, LANES), row),
            pl.BlockSpec((1, N_EXPERTS), const),
        ],
        scratch_shapes=[pltpu.VMEM((tm + 2 * POOL_HALO, POOL_DIM), F32), pltpu.VMEM((1, N_EXPERTS), F32)],
    )
    return pl.pallas_call(
        functools.partial(_post_kernel, tm=tm),
        grid_spec=grid_spec,
        out_shape=[
            jax.ShapeDtypeStruct((t, D_MODEL), F32),
            jax.ShapeDtypeStruct((t * ROW_TILE, LANES), F32),
            jax.ShapeDtypeStruct((t, LANES), I32),
            jax.ShapeDtypeStruct((t, LANES), F32),
            jax.ShapeDtypeStruct((t, LANES), I32),
            jax.ShapeDtypeStruct((1, N_EXPERTS), F32),
        ],
        compiler_params=_cparams(("arbitrary",)),
        name="mixer_merge_route",
    )(*meta, x, attn, u_pool, u_pool, u_pool, y_ssm, gates, mod_l, *wts)


def _dispatch_kernel(lb_ref, has_ref, d_ref, h_hbm, x_hbm, zero_sc, sem, zsem, *, tn, bm):
    i = pl.program_id(0)
    n = pl.num_programs(0)
    n_copies = tn * TOP_K

    def slot_rows(ref, slot):
        return ref.at[pl.ds(slot * ROW_TILE, ROW_TILE), :]

    def pad_loop(act):
        def expert(e, c):
            @pl.when(has_ref[e] == 1)
            def _():
                act(pltpu.make_async_copy(zero_sc, x_hbm.at[pl.ds(lb_ref[e] * ROW_TILE, bm * ROW_TILE), :], zsem))
            return c
        lax.fori_loop(0, N_EXPERTS, expert, 0)

    @pl.when(i == 0)
    def _():
        zero_sc[...] = jnp.zeros(zero_sc.shape, F32)
        pad_loop(lambda cp: cp.start())
        pad_loop(lambda cp: cp.wait())

    def body(a, c):
        pltpu.make_async_copy(slot_rows(h_hbm, i * tn + (a >> 3)), slot_rows(x_hbm, d_ref[0, 0, a]),
                              sem.at[i % 2]).start()
        return c
    lax.fori_loop(0, n_copies, body, 0, unroll=8)

    def wait_tile(s):
        whole = x_hbm.at[pl.ds(0, n_copies * ROW_TILE), :]
        pltpu.make_async_copy(whole, whole, sem.at[s]).wait()

    @pl.when(i > 0)
    def _():
        wait_tile((i - 1) % 2)

    @pl.when(i == n - 1)
    def _():
        wait_tile(i % 2)


def _dispatch(h2t, dest3, last_block, has_rows, n_rows, tn, bm):
    n = dest3.shape[0]
    grid_spec = pltpu.PrefetchScalarGridSpec(
        num_scalar_prefetch=2,
        grid=(n,),
        in_specs=[
            pl.BlockSpec((1, 1, tn * TOP_K), lambda i, lo, hi: (i, 0, 0), memory_space=pltpu.SMEM),
            pl.BlockSpec(memory_space=pl.ANY),
        ],
        out_specs=pl.BlockSpec(memory_space=pl.ANY),
        scratch_shapes=[pltpu.VMEM((bm * ROW_TILE, LANES), F32), pltpu.SemaphoreType.DMA((2,)),
                        pltpu.SemaphoreType.DMA(())],
    )
    return pl.pallas_call(
        functools.partial(_dispatch_kernel, tn=tn, bm=bm),
        grid_spec=grid_spec,
        out_shape=jax.ShapeDtypeStruct((n_rows * ROW_TILE, LANES), F32),
        compiler_params=_cparams(("arbitrary",)),
        name="moe_dispatch",
    )(last_block, has_rows, dest3, h2t)


def _expert_kernel(be_ref, nu_ref, x_ref, wg_ref, wu_ref, wd_ref, y_ref, wg_sc, wu_sc, wd_sc, *, bm):
    b = pl.program_id(0)
    n_used = nu_ref[0]

    @pl.when(b < n_used)
    def _():
        @pl.when((b == 0) | (be_ref[b] != be_ref[jnp.maximum(b - 1, 0)]))
        def _():
            wg_sc[...] = wg_ref[0, 0].astype(BF16)
            wu_sc[...] = wu_ref[0, 0].astype(BF16)
            wd_sc[...] = wd_ref[0, 0].astype(BF16)

        x = _load_row_tiles(x_ref, 0, bm).astype(BF16)
        g = jnp.dot(x, wg_sc[...], preferred_element_type=F32)
        u = jnp.dot(x, wu_sc[...], preferred_element_type=F32)
        hmid = (g * jax.nn.sigmoid(g) * u).astype(BF16)
        _store_row_tiles(y_ref, jnp.dot(hmid, wd_sc[...], preferred_element_type=F32))

    @pl.when(b >= n_used)
    def _():
        y_ref[...] = jnp.zeros(y_ref.shape, F32)


def _experts(x_buf, block_expert, n_used, w_gate, w_up, w_down, layer, bm):
    nb = x_buf.shape[0] // (bm * ROW_TILE)
    grid_spec = pltpu.PrefetchScalarGridSpec(
        num_scalar_prefetch=2,
        grid=(nb,),
        in_specs=[
            pl.BlockSpec((bm * ROW_TILE, LANES), lambda b, be, nu: (jnp.minimum(b, nu[0] - 1), 0)),
            pl.BlockSpec((1, 1, D_MODEL, D_EXPERT), lambda b, be, nu: (layer, be[b], 0, 0)),
            pl.BlockSpec((1, 1, D_MODEL, D_EXPERT), lambda b, be, nu: (layer, be[b], 0, 0)),
            pl.BlockSpec((1, 1, D_EXPERT, D_MODEL), lambda b, be, nu: (layer, be[b], 0, 0)),
        ],
        out_specs=pl.BlockSpec((bm * ROW_TILE, LANES), lambda b, be, nu: (b, 0)),
        scratch_shapes=[
            pltpu.VMEM((D_MODEL, D_EXPERT), BF16),
            pltpu.VMEM((D_MODEL, D_EXPERT), BF16),
            pltpu.VMEM((D_EXPERT, D_MODEL), BF16),
        ],
    )
    return pl.pallas_call(
        functools.partial(_expert_kernel, bm=bm),
        grid_spec=grid_spec,
        out_shape=jax.ShapeDtypeStruct((nb * bm * ROW_TILE, LANES), F32),
        compiler_params=_cparams(("arbitrary",)),
        name="routed_experts",
    )(block_expert, n_used, x_buf, w_gate, w_up, w_down)


def _combine_kernel(ts_ref, d_ref, dn_ref, y_hbm, x1_ref, h2_ref, gate_ref, mod_ref, wsg_ref, wsu_ref, wsd_ref,
                    o_ref, yg_sc, sem, *, tc):
    del ts_ref
    i = pl.program_id(0)
    n = pl.num_programs(0)
    slot = i % 2
    n_copies = tc * TOP_K

    def row_copy(src, s, row):
        return pltpu.make_async_copy(y_hbm.at[pl.ds(src * ROW_TILE, ROW_TILE), :],
                                     yg_sc.at[s, pl.ds(row * ROW_TILE, ROW_TILE), :], sem.at[s])

    def issue(dest_smem, s):
        def body(a, c):
            row_copy(dest_smem[0, 0, a], s, (a & (TOP_K - 1)) * tc + (a >> 3)).start()
            return c
        lax.fori_loop(0, n_copies, body, 0, unroll=8)

    @pl.when(i == 0)
    def _():
        issue(d_ref, 0)

    @pl.when(i + 1 < n)
    def _():
        issue(dn_ref, 1 - slot)

    pltpu.make_async_copy(y_hbm.at[pl.ds(0, n_copies * ROW_TILE), :], yg_sc.at[slot], sem.at[slot]).wait()

    gate = gate_ref[...]
    yg = yg_sc.at[slot]
    routed = gate[:, 0:1] * _load_row_tiles(yg, 0, tc)
    for k in range(1, TOP_K):
        routed = routed + gate[:, k:k + 1] * _load_row_tiles(yg, k * tc, tc)
    hb = _load_row_tiles(h2_ref, 0, tc).astype(BF16)
    g = jnp.dot(hb, wsg_ref[...], preferred_element_type=F32)
    u = jnp.dot(hb, wsu_ref[...], preferred_element_type=F32)
    shared = jnp.dot((g * jax.nn.sigmoid(g) * u).astype(BF16), wsd_ref[...], preferred_element_type=F32)
    o_ref[...] = x1_ref[...] + mod_ref[0][5:6] * (routed + shared)


def _combine(y_buf, dest3, x1, h2, gate, mod_l, ws_gate, ws_up, ws_down, tile_seq, tc):
    t = x1.shape[0]
    n = t // tc
    d_sh = ws_gate.shape[1]
    const = lambda i, ts: (0, 0)
    row = lambda i, ts: (i, 0)
    grid_spec = pltpu.PrefetchScalarGridSpec(
        num_scalar_prefetch=1,
        grid=(n,),
        in_specs=[
            pl.BlockSpec((1, 1, tc * TOP_K), lambda i, ts: (i, 0, 0), memory_space=pltpu.SMEM),
            pl.BlockSpec((1, 1, tc * TOP_K), lambda i, ts: (jnp.minimum(i + 1, n - 1), 0, 0),
                         memory_space=pltpu.SMEM),
            pl.BlockSpec(memory_space=pl.ANY),
            pl.BlockSpec((tc, D_MODEL), row),
            pl.BlockSpec((tc * ROW_TILE, LANES), row),
            pl.BlockSpec((tc, LANES), row),
            pl.BlockSpec((1, N_ADA, D_MODEL), lambda i, ts: (ts[i], 0, 0)),
            pl.BlockSpec((D_MODEL, d_sh), const),
            pl.BlockSpec((D_MODEL, d_sh), const),
            pl.BlockSpec((d_sh, D_MODEL), const),
        ],
        out_specs=pl.BlockSpec((tc, D_MODEL), row),
        scratch_shapes=[pltpu.VMEM((2, tc * TOP_K * ROW_TILE, LANES), F32), pltpu.SemaphoreType.DMA((2,))],
    )
    return pl.pallas_call(
        functools.partial(_combine_kernel, tc=tc),
        grid_spec=grid_spec,
        out_shape=jax.ShapeDtypeStruct((t, D_MODEL), F32),
        compiler_params=_cparams(("arbitrary",)),
        name="moe_combine",
    )(tile_seq, dest3, dest3, y_buf, x1, h2, gate, mod_l, ws_gate, ws_up, ws_down)


def _tile_meta(seq_lens, tile):
    seq, first, last, pos0, slen = [], [], [], [], []
    for s, n in enumerate(seq_lens):
        nt = n // tile
        for j in range(nt):
            seq.append(s)
            first.append(int(j == 0))
            last.append(int(j == nt - 1))
            pos0.append(j * tile)
            slen.append(n)
    return tuple(jnp.asarray(np.asarray(a, np.int32)) for a in (seq, first, last, pos0, slen))


def _scan_meta(seq_lens, tr):
    bwd, first = [], []
    base = 0
    for n in seq_lens:
        nt = n // SSM_CHUNK // tr
        for j in range(nt):
            bwd.append(base + nt - 1 - j)
            first.append(int(j == 0))
        base += nt
    return jnp.asarray(np.asarray(bwd, np.int32)), jnp.asarray(np.asarray(first, np.int32))


def _rope_tables(seq_lens):
    quarter = HEAD_DIM // 4
    freqs = ROPE_THETA ** (-jnp.arange(quarter, dtype=F32) / quarter)
    sign = jnp.tile(jnp.concatenate([-jnp.ones((quarter,), F32), jnp.ones((quarter,), F32)]), 2)
    cos_l, sin_l = [], []
    cache = {}
    for n in seq_lens:
        if n not in cache:
            pos = jnp.arange(n)
            ar = (pos // GRID_W).astype(F32)[:, None] * freqs
            ac = (pos % GRID_W).astype(F32)[:, None] * freqs
            ang = jnp.concatenate([ar, ar, ac, ac], axis=-1)
            cache[n] = (jnp.tile(jnp.cos(ang), (1, 2)), jnp.tile(jnp.sin(ang) * sign, (1, 2)))
        cos_l.append(cache[n][0])
        sin_l.append(cache[n][1])
    return jnp.concatenate(cos_l, axis=0), jnp.concatenate(sin_l, axis=0)


def _block_diag_ones(n, blk):
    r = np.arange(n) // blk
    return jnp.asarray((r[:, None] == r[None, :]).astype(np.float32)).astype(BF16)


def _dest_kernel(idx_ref, rank_ref, ps_ref, d_ref):
    idx = idx_ref[...]
    out = rank_ref[...]
    lane_e = lax.broadcasted_iota(I32, (idx.shape[0], N_EXPERTS), 1)
    lane = lax.broadcasted_iota(I32, idx.shape, 1)
    for k in range(TOP_K):
        start = jnp.sum(jnp.where(lane_e == idx[:, k:k + 1], ps_ref[...], 0.0), axis=1, keepdims=True)
        out = jnp.where(lane == k, out + start.astype(I32), out)
    d_ref[...] = out


def _dest_slots(idx, rank, pstart):
    t = idx.shape[0]
    tm = _pick(t, 512)
    row = lambda i: (i, 0)
    return pl.pallas_call(
        _dest_kernel,
        grid=(t // tm,),
        in_specs=[pl.BlockSpec((tm, LANES), row), pl.BlockSpec((tm, LANES), row),
                  pl.BlockSpec((1, N_EXPERTS), lambda i: (0, 0))],
        out_specs=pl.BlockSpec((tm, LANES), row),
        out_shape=jax.ShapeDtypeStruct((t, LANES), I32),
        compiler_params=_cparams(("arbitrary",)),
        name="moe_dest_slots",
    )(idx, rank, pstart)


def _moe_plan(idx, rank, counts, t, bm, nb):
    cnt = counts.reshape(N_EXPERTS).astype(I32)
    padded = (cnt + bm - 1) // bm * bm
    pend = jnp.cumsum(padded)
    pstart = pend - padded
    n_used = (pend[-1] // bm).astype(I32).reshape(1)
    block_expert = jnp.minimum(
        jnp.searchsorted(pend, jnp.arange(nb, dtype=I32) * bm, side='right'), N_EXPERTS - 1).astype(I32)
    dest = _dest_slots(idx, rank, pstart.astype(F32).reshape(1, N_EXPERTS))[:, :TOP_K]
    return dest, (pend - bm).astype(I32), (cnt > 0).astype(I32), block_expert, n_used


def kernel(x_prompt, x_sample, c_prompt, c_sample, w_ada, b_ada, norm1_g, w_in, q_norm_g, k_norm_g, w_attn_o, pool_w, pool_scale, w_pool_o, ssm_a_re, ssm_a_im, ssm_log_dt, ssm_b_re, ssm_b_im, ssm_c_re, ssm_c_im, ssm_d, w_glu, w_out, norm2_g, w_router, b_router, w_exp_gate, w_exp_up, w_exp_down, w_sh_gate, w_sh_up, w_sh_down):
    b1, l1, d = x_prompt.shape
    b2, l2, _ = x_sample.shape
    depth = w_in.shape[0]
    assert d == D_MODEL and b1 + b2 <= MOD_ROWS
    seq_lens = [l1] * b1 + [l2] * b2
    t = b1 * l1 + b2 * l2
    lmin = min(l1, l2)

    tm_in = _pick(lmin, 512)
    tm_post = _pick(lmin, 256)
    tc_comb = _pick(lmin, 128)
    tr_scan = _pick(lmin // SSM_CHUNK, 256)
    bm = 256
    nb = -(-(t * TOP_K + N_EXPERTS * (bm - 1)) // bm)

    x = jnp.concatenate([x_prompt.reshape(b1 * l1, d), x_sample.reshape(b2 * l2, d)], axis=0)
    c_all = jnp.concatenate([c_prompt, c_sample, jnp.zeros((MOD_ROWS - b1 - b2, d), F32)], axis=0)
    mod = _modulation(c_all, w_ada, b_ada).reshape(depth, MOD_ROWS, N_ADA, d)

    cos_t, sin_t = _rope_tables(seq_lens)
    ones_bd = _block_diag_ones(ATTN_DIM, HEAD_DIM)
    tril = jnp.asarray(np.tril(np.ones((tm_post, tm_post), np.float32), -1)).astype(BF16)
    meta_in = _tile_meta(seq_lens, tm_in)
    meta_post = _tile_meta(seq_lens, tm_post)
    meta_comb = _tile_meta(seq_lens, tc_comb)
    bwd_tile, first_tile = _scan_meta(seq_lens, tr_scan)

    for l in range(depth):
        q8, k2, v2, u_pool, u_ssm, gates = _inproj(
            x, mod[l], norm1_g[l].reshape(1, d), w_in[l].astype(BF16),
            jnp.tile(q_norm_g[l], N_Q_HEADS).reshape(1, ATTN_DIM),
            jnp.tile(k_norm_g[l], N_KV_HEADS).reshape(1, KV_DIM),
            cos_t, sin_t, ones_bd, meta_in[0], tm_in)
        attn = jnp.concatenate(
            [_flash(q8, k2, v2, 0, b1, l1), _flash(q8, k2, v2, b1 * l1, b2, l2)], axis=0)
        tables = _ssm_tables(ssm_a_re[l], ssm_a_im[l], ssm_log_dt[l], ssm_b_re[l], ssm_b_im[l],
                             ssm_c_re[l], ssm_c_im[l], ssm_d[l])
        y_ssm = _bidir_ssm(u_ssm, tables, bwd_tile, first_tile, tr_scan)

        pool_bd = jax.scipy.linalg.block_diag(*[pool_w[l, g] for g in range(len(POOL_WINDOWS))])
        wts = (w_attn_o[l].astype(BF16), pool_bd.astype(BF16), pool_scale[l].reshape(1, POOL_DIM),
               w_pool_o[l].astype(BF16), w_glu[l].astype(BF16), w_out[l].astype(BF16),
               norm2_g[l].reshape(1, d), w_router[l], b_router[l].reshape(1, N_EXPERTS), tril)
        x1, h2, idx, gate, rank, counts = _post(x, attn, u_pool, y_ssm, gates, mod[l], wts, meta_post, tm_post)

        dest, last_block, has_rows, block_expert, n_used = _moe_plan(idx, rank, counts, t, bm, nb)
        dest3 = dest.reshape(t // tc_comb, 1, tc_comb * TOP_K)
        x_buf = _dispatch(h2, dest3, last_block, has_rows, nb * bm, tc_comb, bm)
        y_buf = _experts(x_buf, block_expert, n_used, w_exp_gate, w_exp_up, w_exp_down, l, bm)
        x = _combine(y_buf, dest3, x1, h2, gate, mod[l],
                     w_sh_gate[l].astype(BF16), w_sh_up[l].astype(BF16), w_sh_down[l].astype(BF16),
                     meta_comb[0], tc_comb)

    return (x[:b1 * l1].reshape(b1, l1, d), x[b1 * l1:].reshape(b2, l2, d))
```

```python
import functools
import math

import jax
import jax.numpy as jnp
import numpy as np
from jax import lax
from jax.experimental import pallas as pl
from jax.experimental.pallas import tpu as pltpu

F32 = jnp.float32
BF16 = jnp.bfloat16
I32 = jnp.int32
HIGHEST = lax.Precision.HIGHEST

D_MODEL = 1024
GRID_W = 64
HEAD_DIM = 64
N_Q_HEADS = 8
N_KV_HEADS = 2
Q_PER_KV = N_Q_HEADS // N_KV_HEADS
ATTN_DIM = N_Q_HEADS * HEAD_DIM
KV_DIM = N_KV_HEADS * HEAD_DIM
ROPE_THETA = 10000.0
POOL_WINDOWS = (2, 4, 8, 16)
POOL_DIM = 256
POOL_GROUP = 64
POOL_HALO = 8
SSM_DIM = 256
SSM_P = 16
SSM_G = 16
SSM_N = 64
SSM_CHUNK = 16
SSM_ROW = SSM_CHUNK * SSM_P
SSM_STATE = 2 * SSM_N
N_EXPERTS = 256
TOP_K = 8
N_EXPERT_GROUPS = 8
GROUP_SIZE = N_EXPERTS // N_EXPERT_GROUPS
TOPK_GROUPS = 4
D_EXPERT = 256
ROUTED_SCALE = 2.5
N_ADA = 6
EPS = 1e-6
IN_DIM = ATTN_DIM + 2 * KV_DIM + POOL_DIM + SSM_DIM + 3 * D_MODEL
OFF_K = ATTN_DIM
OFF_V = OFF_K + KV_DIM
OFF_POOL = OFF_V + KV_DIM
OFF_SSM = OFF_POOL + POOL_DIM
OFF_GATES = OFF_SSM + SSM_DIM
MOD_ROWS = 8
LOG2_E = math.log2(math.e)
VT_ROWS = HEAD_DIM + 16

V7X_VMEM_BYTES = 64 * 1024 * 1024
VMEM_LIMIT = V7X_VMEM_BYTES - 8 * 1024 * 1024
LANES = 128


def _cparams(sem):
    return pltpu.CompilerParams(dimension_semantics=sem, vmem_limit_bytes=VMEM_LIMIT)


def _pick(n, pref):
    t = min(n, pref)
    while n % t:
        t //= 2
    return t


ROW_TILE = D_MODEL // LANES


def _store_row_tiles(ref, x):
    n = x.shape[0]
    for s in range(ROW_TILE):
        ref[pl.ds(s, n, stride=ROW_TILE), :] = x[:, s * LANES:(s + 1) * LANES]


def _load_row_tiles(ref, first_row, n):
    return jnp.concatenate(
        [ref[pl.ds(first_row * ROW_TILE + s, n, stride=ROW_TILE), :] for s in range(ROW_TILE)], axis=1)


def _mod_kernel(c_ref, w_ref, b_ref, o_ref):
    c = c_ref[...]
    a = c * jax.nn.sigmoid(c)
    o_ref[0] = jnp.dot(a, w_ref[0], preferred_element_type=F32, precision=HIGHEST) + b_ref[0]


def _modulation(c_all, w_ada, b_ada):
    depth, d, n = w_ada.shape
    bn = _pick(n, 1536)
    return pl.pallas_call(
        _mod_kernel,
        grid=(depth, n // bn),
        in_specs=[
            pl.BlockSpec((MOD_ROWS, d), lambda l, j: (0, 0)),
            pl.BlockSpec((1, d, bn), lambda l, j: (l, 0, j)),
            pl.BlockSpec((1, 1, bn), lambda l, j: (l, 0, j)),
        ],
        out_specs=pl.BlockSpec((1, MOD_ROWS, bn), lambda l, j: (l, 0, j)),
        out_shape=jax.ShapeDtypeStruct((depth, MOD_ROWS, n), F32),
        compiler_params=_cparams(("arbitrary", "arbitrary")),
        name="adaln_mod",
    )(c_all, w_ada, b_ada.reshape(depth, 1, n))


def _rmsnorm_mod(x, g, scale, shift):
    y = x * lax.rsqrt(jnp.mean(x * x, axis=-1, keepdims=True) + EPS)
    return (y * g) * (1.0 + scale) + shift


def _head_norm_rope(z, gain, cos, sin_signed, ones_bd):
    z2 = z * z
    hi = z2.astype(BF16)
    lo = (z2 - hi.astype(F32)).astype(BF16)
    ss = (jnp.dot(hi, ones_bd, preferred_element_type=F32)
          + jnp.dot(lo, ones_bd, preferred_element_type=F32))
    y = (z * lax.rsqrt(ss * (1.0 / HEAD_DIM) + EPS)) * gain
    w = z.shape[1]
    quarter = HEAD_DIM // 4
    from_right = pltpu.roll(y, w - quarter, 1)
    from_left = pltpu.roll(y, quarter, 1)
    lane = lax.broadcasted_iota(I32, y.shape, 1)
    rot = jnp.where((lane & quarter) == 0, from_right, from_left)
    return y * cos + rot * sin_signed


def _inproj_kernel(ts_ref, x_ref, mod_ref, g_ref, w_ref, gq_ref, gk_ref, cos_ref, sin_ref, ones_ref,
                   q_ref, k_ref, v_ref, up_ref, us_ref, gt_ref):
    del ts_ref
    x = x_ref[...]
    mod = mod_ref[0]
    h = _rmsnorm_mod(x, g_ref[...], mod[1:2], mod[0:1]).astype(BF16)

    cos = cos_ref[...]
    sin = sin_ref[...]
    zq = jnp.dot(h, w_ref[:, 0:ATTN_DIM], preferred_element_type=F32)
    reps = ATTN_DIM // LANES
    yq = _head_norm_rope(zq, gq_ref[...], jnp.concatenate([cos] * reps, axis=1),
                         jnp.concatenate([sin] * reps, axis=1), ones_ref[...]) * (HEAD_DIM ** -0.5 * LOG2_E)
    for hh in range(N_Q_HEADS):
        q_ref[hh] = yq[:, hh * HEAD_DIM:(hh + 1) * HEAD_DIM].astype(BF16)

    zkv = jnp.dot(h, w_ref[:, OFF_K:OFF_POOL], preferred_element_type=F32)
    yk = _head_norm_rope(zkv[:, :KV_DIM], gk_ref[...], cos, sin, ones_ref[0:KV_DIM, 0:KV_DIM])
    zv = zkv[:, KV_DIM:]
    for j in range(N_KV_HEADS):
        k_ref[j] = yk[:, j * HEAD_DIM:(j + 1) * HEAD_DIM].astype(BF16)
        v_ref[j, 0:HEAD_DIM, :] = zv[:, j * HEAD_DIM:(j + 1) * HEAD_DIM].T.astype(BF16)
        v_ref[j, HEAD_DIM:VT_ROWS, :] = jnp.ones((VT_ROWS - HEAD_DIM, zv.shape[0]), BF16)

    zps = jnp.dot(h, w_ref[:, OFF_POOL:OFF_GATES], preferred_element_type=F32)
    up_ref[...] = zps[:, :POOL_DIM]
    us_ref[...] = zps[:, POOL_DIM:]
    for c in range(3):
        zg = jnp.dot(h, w_ref[:, OFF_GATES + c * D_MODEL:OFF_GATES + (c + 1) * D_MODEL],
                     preferred_element_type=F32)
        gt_ref[:, c * D_MODEL:(c + 1) * D_MODEL] = jax.nn.sigmoid(zg).astype(BF16)


def _inproj(x, mod_l, g1, w_in_bf, gq_t, gk_t, cos_t, sin_t, ones_bd, tile_seq, tm):
    t = x.shape[0]
    const = lambda i, ts: (0, 0)
    row = lambda i, ts: (i, 0)
    grid_spec = pltpu.PrefetchScalarGridSpec(
        num_scalar_prefetch=1,
        grid=(t // tm,),
        in_specs=[
            pl.BlockSpec((tm, D_MODEL), row),
            pl.BlockSpec((1, N_ADA, D_MODEL), lambda i, ts: (ts[i], 0, 0)),
            pl.BlockSpec((1, D_MODEL), const),
            pl.BlockSpec((D_MODEL, IN_DIM), const),
            pl.BlockSpec((1, ATTN_DIM), const),
            pl.BlockSpec((1, KV_DIM), const),
            pl.BlockSpec((tm, LANES), row),
            pl.BlockSpec((tm, LANES), row),
            pl.BlockSpec((ATTN_DIM, ATTN_DIM), const),
        ],
        out_specs=[
            pl.BlockSpec((N_Q_HEADS, tm, HEAD_DIM), lambda i, ts: (0, i, 0)),
            pl.BlockSpec((N_KV_HEADS, tm, HEAD_DIM), lambda i, ts: (0, i, 0)),
            pl.BlockSpec((N_KV_HEADS, VT_ROWS, tm), lambda i, ts: (0, 0, i)),
            pl.BlockSpec((tm, POOL_DIM), row),
            pl.BlockSpec((tm, SSM_DIM), row),
            pl.BlockSpec((tm, 3 * D_MODEL), row),
        ],
    )
    return pl.pallas_call(
        _inproj_kernel,
        grid_spec=grid_spec,
        out_shape=[
            jax.ShapeDtypeStruct((N_Q_HEADS, t, HEAD_DIM), BF16),
            jax.ShapeDtypeStruct((N_KV_HEADS, t, HEAD_DIM), BF16),
            jax.ShapeDtypeStruct((N_KV_HEADS, VT_ROWS, t), BF16),
            jax.ShapeDtypeStruct((t, POOL_DIM), F32),
            jax.ShapeDtypeStruct((t, SSM_DIM), F32),
            jax.ShapeDtypeStruct((t, 3 * D_MODEL), BF16),
        ],
        compiler_params=_cparams(("arbitrary",)),
        name="norm1_inproj",
    )(tile_seq, x, mod_l, g1, w_in_bf, gq_t, gk_t, cos_t, sin_t, ones_bd)


def _flash_kernel(q_ref, k_ref, vt_ref, o_ref, m_sc, acc_sc):
    kb = pl.program_id(3)

    @pl.when(kb == 0)
    def _():
        m_sc[...] = jnp.full(m_sc.shape, -jnp.inf, F32)
        acc_sc[...] = jnp.zeros(acc_sc.shape, F32)

    k = k_ref[0]
    vt = vt_ref[0]
    heads = range(Q_PER_KV)
    qs = [q_ref[hh] for hh in heads]
    m_prev = [m_sc[hh] for hh in heads]
    acc_prev = [acc_sc[hh] for hh in heads]

    def scores(hh):
        return lax.dot_general(k, qs[hh], (((1,), (1,)), ((), ())), preferred_element_type=F32)

    ahead = 2
    s = {hh: scores(hh) for hh in range(ahead)}
    new = []
    for hh in heads:
        if hh + ahead < Q_PER_KV:
            s[hh + ahead] = scores(hh + ahead)
        m_new = jnp.maximum(m_prev[hh], jnp.max(s[hh], axis=0, keepdims=True))
        alpha = jnp.exp2(m_prev[hh] - m_new)
        p = jnp.exp2(s.pop(hh) - m_new).astype(BF16)
        acc_new = acc_prev[hh] * alpha + jnp.dot(vt, p, preferred_element_type=F32)
        new.append((m_new, acc_new))
    for hh, (m_new, acc_new) in enumerate(new):
        m_sc[hh] = m_new
        acc_sc[hh] = acc_new

    @pl.when(kb == pl.num_programs(3) - 1)
    def _():
        for hh in range(Q_PER_KV):
            acc = acc_sc[hh]
            o = acc[:HEAD_DIM] / acc[HEAD_DIM:HEAD_DIM + 1]
            o_ref[:, hh * HEAD_DIM:(hh + 1) * HEAD_DIM] = o.T.astype(BF16)


def _flash(q8, k2, vt2, off, bsz, seq):
    tq = _pick(seq, 512)
    tk = _pick(seq, 2048)
    nq, nk = seq // tq, seq // tk
    assert off % tq == 0 and off % tk == 0
    oq, ok = off // tq, off // tk
    return pl.pallas_call(
        _flash_kernel,
        grid=(N_KV_HEADS, bsz, nq, nk),
        in_specs=[
            pl.BlockSpec((Q_PER_KV, tq, HEAD_DIM), lambda j, b, i, kb: (j, oq + b * nq + i, 0)),
            pl.BlockSpec((1, tk, HEAD_DIM), lambda j, b, i, kb: (j, ok + b * nk + kb, 0)),
            pl.BlockSpec((1, VT_ROWS, tk), lambda j, b, i, kb: (j, 0, ok + b * nk + kb)),
        ],
        out_specs=pl.BlockSpec((tq, Q_PER_KV * HEAD_DIM), lambda j, b, i, kb: (b * nq + i, j)),
        out_shape=jax.ShapeDtypeStruct((bsz * seq, ATTN_DIM), BF16),
        scratch_shapes=[
            pltpu.VMEM((Q_PER_KV, 1, tq), F32),
            pltpu.VMEM((Q_PER_KV, VT_ROWS, tq), F32),
        ],
        compiler_params=_cparams(("arbitrary", "arbitrary", "arbitrary", "arbitrary")),
        name="flash_gqa",
    )(q8, k2, vt2)


def _ssm_tables(a_re, a_im, log_dt, b_re, b_im, c_re, c_im, d_skip):
    tc = SSM_CHUNK
    a = lax.complex(a_re.astype(F32), a_im.astype(F32))
    dt = jnp.exp(log_dt.astype(F32))[..., None]
    adt = a * dt
    a_bar = jnp.exp(adt)
    b_bar = ((a_bar - 1.0) / a)[..., None] * lax.complex(b_re.astype(F32), b_im.astype(F32))
    c_mat = lax.complex(c_re.astype(F32), c_im.astype(F32))
    taus = jnp.arange(tc + 1, dtype=F32)
    pw = jnp.exp(adt[None] * taus[:, None, None, None])

    kern = jnp.real(jnp.einsum('dgpn,tdgn,dgnq->dgtpq', c_mat, pw[:tc], b_bar))
    s_idx = jnp.arange(tc)[:, None]
    t_idx = jnp.arange(tc)[None, :]
    lag_f = jnp.clip(t_idx - s_idx, 0, tc - 1)
    lag_b = jnp.clip(s_idx - t_idx, 0, tc - 1)
    kf = jnp.where((t_idx >= s_idx)[None, :, :, None, None], kern[0][:, lag_f], 0.0)
    kb = jnp.where((s_idx >= t_idx)[None, :, :, None, None], kern[1][:, lag_b], 0.0)
    w_loc = (kf + kb).transpose(0, 1, 4, 2, 3).reshape(SSM_G, SSM_ROW, SSM_ROW)
    w_loc = w_loc + jnp.eye(SSM_ROW, dtype=F32)[None] * jnp.tile(
        d_skip.astype(F32).reshape(SSM_G, 1, SSM_P), (1, tc, 1)).reshape(SSM_G, 1, SSM_ROW)

    def ri(z, axis):
        return jnp.concatenate([jnp.real(z), jnp.imag(z)], axis=axis)

    pf = jnp.einsum('sgn,gnq->gsqn', pw[:tc][::-1, 0], b_bar[0])
    pb = jnp.einsum('sgn,gnq->gsqn', pw[:tc, 1], b_bar[1])
    p_st = jnp.concatenate([ri(pf, -1), ri(pb, -1)], axis=-1).reshape(SSM_G, SSM_ROW, 2 * SSM_STATE)
    qf = jnp.einsum('gpn,tgn->gntp', c_mat[0], pw[1:tc + 1, 0])
    qb = jnp.einsum('gpn,tgn->gntp', c_mat[1], pw[1:tc + 1, 1][::-1])
    q_f = jnp.concatenate([jnp.real(qf), -jnp.imag(qf)], axis=1).reshape(SSM_G, SSM_STATE, SSM_ROW)
    q_b = jnp.concatenate([jnp.real(qb), -jnp.imag(qb)], axis=1).reshape(SSM_G, SSM_STATE, SSM_ROW)
    a16 = pw[tc]
    a1 = jnp.concatenate([jnp.real(a16), jnp.real(a16)], axis=-1).reshape(2, SSM_G * SSM_STATE)
    a2 = jnp.concatenate([-jnp.imag(a16), jnp.imag(a16)], axis=-1).reshape(2, SSM_G * SSM_STATE)
    a_mul = jnp.stack([a1[0], a2[0], a1[1], a2[1]], axis=0)
    return w_loc.astype(BF16), p_st.astype(BF16), q_f.astype(BF16), q_b.astype(BF16), a_mul


def _ssm_state_kernel(u_ref, p_ref, s_ref):
    s = jnp.dot(u_ref[0].astype(BF16), p_ref[0], preferred_element_type=F32)
    s_ref[0] = s[:, :SSM_STATE]
    s_ref[1] = s[:, SSM_STATE:]


def _ssm_states(ug, p_st, tc):
    g, nc, _ = ug.shape
    return pl.pallas_call(
        _ssm_state_kernel,
        grid=(g, nc // tc),
        in_specs=[
            pl.BlockSpec((1, tc, SSM_ROW), lambda gi, i: (gi, i, 0)),
            pl.BlockSpec((1, SSM_ROW, 2 * SSM_STATE), lambda gi, i: (gi, 0, 0)),
        ],
        out_specs=pl.BlockSpec((2, tc, SSM_STATE), lambda gi, i: (0, i, gi)),
        out_shape=jax.ShapeDtypeStruct((2, nc, g * SSM_STATE), F32),
        compiler_params=_cparams(("arbitrary", "arbitrary")),
        name="ssm_chunk_states",
    )(ug, p_st)


def _swap_halves(x):
    parts = [pltpu.roll(x[:, g * LANES:(g + 1) * LANES], LANES // 2, 1) for g in range(x.shape[1] // LANES)]
    return jnp.concatenate(parts, axis=1)


def _ssm_scan_kernel(bt_ref, first_ref, sf_ref, sb_ref, am_ref, xf_ref, zb_ref, st_sc, sfw_sc, sbw_sc, *, tr):
    n = pl.program_id(0)
    del bt_ref

    @pl.when(first_ref[n] == 1)
    def _():
        st_sc[...] = jnp.zeros(st_sc.shape, F32)

    sfw_sc[...] = _swap_halves(sf_ref[0])
    sbw_sc[...] = _swap_halves(sb_ref[0])
    a1f, a2f, a1b, a2b = am_ref[0:1], am_ref[1:2], am_ref[2:3], am_ref[3:4]

    def body(r, carry):
        xf, xfw, zb, zbw = carry
        xf_ref[pl.ds(r, 1), :] = xf
        sf = sf_ref[0, pl.ds(r, 1), :]
        sfw = sfw_sc[pl.ds(r, 1), :]
        nxf = a1f * xf + a2f * xfw + sf
        nxfw = a1f * xfw - a2f * xf + sfw
        rb = tr - 1 - r
        zb_ref[pl.ds(rb, 1), :] = zb
        sb = sb_ref[0, pl.ds(rb, 1), :]
        sbw = sbw_sc[pl.ds(rb, 1), :]
        nzb = a1b * zb + a2b * zbw + sb
        nzbw = a1b * zbw - a2b * zb + sbw
        return nxf, nxfw, nzb, nzbw

    init = (st_sc[0:1], st_sc[1:2], st_sc[2:3], st_sc[3:4])
    xf, xfw, zb, zbw = lax.fori_loop(0, tr, body, init)
    st_sc[0:1] = xf
    st_sc[1:2] = xfw
    st_sc[2:3] = zb
    st_sc[3:4] = zbw


def _ssm_scan(s_st, a_mul, bwd_tile, first_tile, tr):
    _, nc, w = s_st.shape
    grid_spec = pltpu.PrefetchScalarGridSpec(
        num_scalar_prefetch=2,
        grid=(nc // tr,),
        in_specs=[
            pl.BlockSpec((1, tr, w), lambda n, bt, ft: (0, n, 0)),
            pl.BlockSpec((1, tr, w), lambda n, bt, ft: (1, bt[n], 0)),
            pl.BlockSpec((4, w), lambda n, bt, ft: (0, 0)),
        ],
        out_specs=[
            pl.BlockSpec((tr, w), lambda n, bt, ft: (n, 0)),
            pl.BlockSpec((tr, w), lambda n, bt, ft: (bt[n], 0)),
        ],
        scratch_shapes=[pltpu.VMEM((4, w), F32), pltpu.VMEM((tr, w), F32), pltpu.VMEM((tr, w), F32)],
    )
    return pl.pallas_call(
        functools.partial(_ssm_scan_kernel, tr=tr),
        grid_spec=grid_spec,
        out_shape=[jax.ShapeDtypeStruct((nc, w), F32), jax.ShapeDtypeStruct((nc, w), F32)],
        compiler_params=_cparams(("arbitrary",)),
        name="ssm_chunk_scan",
    )(bwd_tile, first_tile, s_st, s_st, a_mul)


def _ssm_out_kernel(u_ref, xf_ref, zb_ref, w_ref, qf_ref, qb_ref, y_ref):
    y = jnp.dot(u_ref[0].astype(BF16), w_ref[0], preferred_element_type=F32)
    y += jnp.dot(xf_ref[...].astype(BF16), qf_ref[0], preferred_element_type=F32)
    y += jnp.dot(zb_ref[...].astype(BF16), qb_ref[0], preferred_element_type=F32)
    y_ref[0] = y


def _ssm_outputs(ug, xf, zb, w_loc, q_f, q_b, tc):
    g, nc, _ = ug.shape
    return pl.pallas_call(
        _ssm_out_kernel,
        grid=(g, nc // tc),
        in_specs=[
            pl.BlockSpec((1, tc, SSM_ROW), lambda gi, i: (gi, i, 0)),
            pl.BlockSpec((tc, SSM_STATE), lambda gi, i: (i, gi)),
            pl.BlockSpec((tc, SSM_STATE), lambda gi, i: (i, gi)),
            pl.BlockSpec((1, SSM_ROW, SSM_ROW), lambda gi, i: (gi, 0, 0)),
            pl.BlockSpec((1, SSM_STATE, SSM_ROW), lambda gi, i: (gi, 0, 0)),
            pl.BlockSpec((1, SSM_STATE, SSM_ROW), lambda gi, i: (gi, 0, 0)),
        ],
        out_specs=pl.BlockSpec((1, tc, SSM_ROW), lambda gi, i: (gi, i, 0)),
        out_shape=jax.ShapeDtypeStruct((g, nc, SSM_ROW), F32),
        compiler_params=_cparams(("arbitrary", "arbitrary")),
        name="ssm_chunk_outputs",
    )(ug, xf, zb, w_loc, q_f, q_b)


def _bidir_ssm(u_ssm, tables, bwd_tile, first_tile, tr):
    w_loc, p_st, q_f, q_b, a_mul = tables
    t = u_ssm.shape[0]
    nc = t // SSM_CHUNK
    ug = u_ssm.reshape(nc, SSM_CHUNK, SSM_G, SSM_P).transpose(2, 0, 1, 3).reshape(SSM_G, nc, SSM_ROW)
    tc = _pick(nc, 512)
    s_st = _ssm_states(ug, p_st, tc)
    xf, zb = _ssm_scan(s_st, a_mul, bwd_tile, first_tile, tr)
    yg = _ssm_outputs(ug, xf, zb, w_loc, q_f, q_b, tc)
    return yg.reshape(SSM_G, nc, SSM_CHUNK, SSM_P).transpose(1, 2, 0, 3).reshape(t, SSM_DIM)


def _pool_mixer(ext_sc, u, pos, seq_len, tm):
    halves = []
    for half in range(2):
        sums = []
        for w in POOL_WINDOWS[2 * half:2 * half + 2]:
            lo = w // 2
            hi = w - lo - 1
            acc = ext_sc[pl.ds(POOL_HALO - lo, tm), half * LANES:(half + 1) * LANES]
            for d in range(-lo + 1, hi + 1):
                acc = acc + ext_sc[pl.ds(POOL_HALO + d, tm), half * LANES:(half + 1) * LANES]
            cnt = (jnp.minimum(pos + hi, seq_len - 1) - jnp.maximum(pos - lo, 0) + 1).astype(F32)
            sums.append(acc / cnt)
        lane = lax.broadcasted_iota(I32, (tm, LANES), 1)
        halves.append(jnp.where(lane < POOL_GROUP, sums[0], sums[1]))
    return jnp.concatenate(halves, axis=1) - u


def _route(h2, wr_ref, br_ref):
    logits = jnp.dot(h2, wr_ref[...], preferred_element_type=F32, precision=HIGHEST)
    scores = jax.nn.sigmoid(logits)
    choice = scores + br_ref[...]
    lane = lax.broadcasted_iota(I32, choice.shape, 1)
    neg = jnp.float32(-jnp.inf)

    def first_argmax(c):
        m = jnp.max(c, axis=1, keepdims=True)
        return jnp.min(jnp.where(c == m, lane, N_EXPERTS), axis=1, keepdims=True), m

    in_group, gscore = [], []
    for g in range(N_EXPERT_GROUPS):
        ing = (lane >= g * GROUP_SIZE) & (lane < (g + 1) * GROUP_SIZE)
        cg = jnp.where(ing, choice, neg)
        i1, m1 = first_argmax(cg)
        m2 = jnp.max(jnp.where(lane == i1, neg, cg), axis=1, keepdims=True)
        in_group.append(ing)
        gscore.append(m1 + m2)
    emask = jnp.zeros(choice.shape, jnp.bool_)
    for g in range(N_EXPERT_GROUPS):
        ahead = jnp.zeros(gscore[g].shape, I32)
        for o in range(N_EXPERT_GROUPS):
            if o == g:
                continue
            beats = (gscore[o] > gscore[g]) | ((gscore[o] == gscore[g]) & (o < g))
            ahead = ahead + beats.astype(I32)
        emask = emask | (in_group[g] & (ahead < TOPK_GROUPS))
    c = jnp.where(emask, choice, neg)
    picks = []
    for _ in range(TOP_K):
        i, _m = first_argmax(c)
        oh = lane == i
        w = jnp.sum(jnp.where(oh, scores, 0.0), axis=1, keepdims=True)
        c = jnp.where(oh, neg, c)
        picks.append((i, w, oh))
    wsum = picks[0][1]
    for _, w, _ in picks[1:]:
        wsum = wsum + w
    return [(i, w / wsum * ROUTED_SCALE, oh) for i, w, oh in picks]


def _post_kernel(ts_ref, tf_ref, tl_ref, tp_ref, tn_ref,
                 x_ref, at_ref, upp_ref, up_ref, upn_ref, ys_ref, gt_ref, mod_ref,
                 wao_ref, pbd_ref, psc_ref, wpo_ref, wgl_ref, wo_ref, g2_ref, wr_ref, br_ref, tril_ref,
                 x1_ref, h2_ref, idx_ref, gate_ref, rank_ref, cnt_ref, ext_sc, carry_sc, *, tm):
    del ts_ref
    i = pl.program_id(0)
    mod = mod_ref[0]

    attn = jnp.dot(at_ref[...], wao_ref[...], preferred_element_type=F32)

    u = up_ref[...]
    ext_sc[0:POOL_HALO] = jnp.where(tf_ref[i] == 1, 0.0, upp_ref[...])
    ext_sc[POOL_HALO:POOL_HALO + tm] = u
    ext_sc[POOL_HALO + tm:2 * POOL_HALO + tm] = jnp.where(tl_ref[i] == 1, 0.0, upn_ref[...])
    pos = tp_ref[i] + lax.broadcasted_iota(I32, (tm, LANES), 0)
    pm = _pool_mixer(ext_sc, u, pos, tn_ref[i], tm)
    pool = jnp.dot(pm.astype(BF16), pbd_ref[...], preferred_element_type=F32) * psc_ref[...]
    pool = jnp.dot(pool.astype(BF16), wpo_ref[...], preferred_element_type=F32)

    z = jnp.dot(jax.nn.gelu(ys_ref[...]).astype(BF16), wgl_ref[...], preferred_element_type=F32)
    ssm = z[:, :D_MODEL] * jax.nn.sigmoid(z[:, D_MODEL:])

    merged = (gt_ref[:, 0:D_MODEL].astype(F32) * attn
              + gt_ref[:, D_MODEL:2 * D_MODEL].astype(F32) * pool
              + gt_ref[:, 2 * D_MODEL:3 * D_MODEL].astype(F32) * ssm)
    mix = jnp.dot(merged.astype(BF16), wo_ref[...], preferred_element_type=F32)
    x1 = x_ref[...] + mod[2:3] * mix
    x1_ref[...] = x1
    h2 = _rmsnorm_mod(x1, g2_ref[...], mod[4:5], mod[3:4])
    _store_row_tiles(h2_ref, h2)

    picks = _route(h2, wr_ref, br_ref)

    @pl.when(i == 0)
    def _():
        carry_sc[...] = jnp.zeros(carry_sc.shape, F32)

    sel = picks[0][2]
    for _, _, oh in picks[1:]:
        sel = sel | oh
    sel_f = sel.astype(F32)
    ranks = jnp.dot(tril_ref[...], sel_f.astype(BF16), preferred_element_type=F32) + carry_sc[...]
    carry = carry_sc[...] + jnp.sum(sel_f, axis=0, keepdims=True)
    carry_sc[...] = carry
    cnt_ref[...] = carry

    lane = lax.broadcasted_iota(I32, (tm, LANES), 1)
    idx_o = jnp.zeros((tm, LANES), I32)
    gate_o = jnp.zeros((tm, LANES), F32)
    rank_o = jnp.zeros((tm, LANES), I32)
    for k, (ik, gk, oh) in enumerate(picks):
        rk = jnp.sum(jnp.where(oh, ranks, 0.0), axis=1, keepdims=True).astype(I32)
        idx_o = jnp.where(lane == k, ik, idx_o)
        gate_o = jnp.where(lane == k, gk, gate_o)
        rank_o = jnp.where(lane == k, rk, rank_o)
    idx_ref[...] = idx_o
    gate_ref[...] = gate_o
    rank_ref[...] = rank_o


def _post(x, attn, u_pool, y_ssm, gates, mod_l, wts, meta, tm):
    t = x.shape[0]
    nh = t // POOL_HALO
    hb = tm // POOL_HALO
    npf = 5
    const = lambda i, *_: (0, 0)
    row = lambda i, *_: (i, 0)
    grid_spec = pltpu.PrefetchScalarGridSpec(
        num_scalar_prefetch=npf,
        grid=(t // tm,),
        in_specs=[
            pl.BlockSpec((tm, D_MODEL), row),
            pl.BlockSpec((tm, ATTN_DIM), row),
            pl.BlockSpec((POOL_HALO, POOL_DIM), lambda i, *_: (jnp.maximum(i * hb - 1, 0), 0)),
            pl.BlockSpec((tm, POOL_DIM), row),
            pl.BlockSpec((POOL_HALO, POOL_DIM), lambda i, *_: (jnp.minimum((i + 1) * hb, nh - 1), 0)),
            pl.BlockSpec((tm, SSM_DIM), row),
            pl.BlockSpec((tm, 3 * D_MODEL), row),
            pl.BlockSpec((1, N_ADA, D_MODEL), lambda i, ts, *_: (ts[i], 0, 0)),
            pl.BlockSpec((ATTN_DIM, D_MODEL), const),
            pl.BlockSpec((POOL_DIM, POOL_DIM), const),
            pl.BlockSpec((1, POOL_DIM), const),
            pl.BlockSpec((POOL_DIM, D_MODEL), const),
            pl.BlockSpec((SSM_DIM, 2 * D_MODEL), const),
            pl.BlockSpec((D_MODEL, D_MODEL), const),
            pl.BlockSpec((1, D_MODEL), const),
            pl.BlockSpec((D_MODEL, N_EXPERTS), const),
            pl.BlockSpec((1, N_EXPERTS), const),
            pl.BlockSpec((tm, tm), const),
        ],
        out_specs=[
            pl.BlockSpec((tm, D_MODEL), row),
            pl.BlockSpec((tm * ROW_TILE, LANES), row),
            pl.BlockSpec((tm, LANES), row),
            pl.BlockSpec((tm, LANES), row),
            pl.BlockSpec((tm, LANES), row),
            pl.BlockSpec((1, N_EXPERTS), const),
        ],
        scratch_shapes=[pltpu.VMEM((tm + 2 * POOL_HALO, POOL_DIM), F32), pltpu.VMEM((1, N_EXPERTS), F32)],
    )
    return pl.pallas_call(
        functools.partial(_post_kernel, tm=tm),
        grid_spec=grid_spec,
        out_shape=[
            jax.ShapeDtypeStruct((t, D_MODEL), F32),
            jax.ShapeDtypeStruct((t * ROW_TILE, LANES), F32),
            jax.ShapeDtypeStruct((t, LANES), I32),
            jax.ShapeDtypeStruct((t, LANES), F32),
            jax.ShapeDtypeStruct((t, LANES), I32),
            jax.ShapeDtypeStruct((1, N_EXPERTS), F32),
        ],
        compiler_params=_cparams(("arbitrary",)),
        name="mixer_merge_route",
    )(*meta, x, attn, u_pool, u_pool, u_pool, y_ssm, gates, mod_l, *wts)


def _dispatch_kernel(lb_ref, has_ref, d_ref, h_hbm, x_hbm, zero_sc, h_sc, sem_in, sem, zsem, *, tn, bm):
    i = pl.program_id(0)
    n = pl.num_programs(0)
    n_copies = tn * TOP_K
    tile_rows = tn * ROW_TILE

    def slot_rows(ref, slot):
        return ref.at[pl.ds(slot * ROW_TILE, ROW_TILE), :]

    def tile_load(tile, s):
        return pltpu.make_async_copy(h_hbm.at[pl.ds(tile * tile_rows, tile_rows), :], h_sc.at[s], sem_in.at[s])

    def pad_loop(act):
        def expert(e, c):
            @pl.when(has_ref[e] == 1)
            def _():
                act(pltpu.make_async_copy(zero_sc, x_hbm.at[pl.ds(lb_ref[e] * ROW_TILE, bm * ROW_TILE), :], zsem))
            return c
        lax.fori_loop(0, N_EXPERTS, expert, 0)

    @pl.when(i == 0)
    def _():
        zero_sc[...] = jnp.zeros(zero_sc.shape, F32)
        pad_loop(lambda cp: cp.start())
        pad_loop(lambda cp: cp.wait())
        tile_load(0, 0).start()

    cur = i % 3

    @pl.when(i + 1 < n)
    def _():
        tile_load(i + 1, (i + 1) % 3).start()

    tile_load(i, cur).wait()

    def body(a, c):
        pltpu.make_async_copy(slot_rows(h_sc.at[cur], a >> 3), slot_rows(x_hbm, d_ref[0, 0, a]),
                              sem.at[cur]).start()
        return c
    lax.fori_loop(0, n_copies, body, 0, unroll=8)

    def wait_tile(s):
        whole = x_hbm.at[pl.ds(0, n_copies * ROW_TILE), :]
        pltpu.make_async_copy(whole, whole, sem.at[s]).wait()

    @pl.when(i > 0)
    def _():
        wait_tile((i + 2) % 3)

    @pl.when(i == n - 1)
    def _():
        wait_tile(cur)


def _dispatch(h2t, dest3, last_block, has_rows, n_rows, tn, bm):
    n = dest3.shape[0]
    grid_spec = pltpu.PrefetchScalarGridSpec(
        num_scalar_prefetch=2,
        grid=(n,),
        in_specs=[
            pl.BlockSpec((1, 1, tn * TOP_K), lambda i, lo, hi: (i, 0, 0), memory_space=pltpu.SMEM),
            pl.BlockSpec(memory_space=pl.ANY),
        ],
        out_specs=pl.BlockSpec(memory_space=pl.ANY),
        scratch_shapes=[pltpu.VMEM((bm * ROW_TILE, LANES), F32), pltpu.VMEM((3, tn * ROW_TILE, LANES), F32),
                        pltpu.SemaphoreType.DMA((3,)), pltpu.SemaphoreType.DMA((3,)),
                        pltpu.SemaphoreType.DMA(())],
    )
    return pl.pallas_call(
        functools.partial(_dispatch_kernel, tn=tn, bm=bm),
        grid_spec=grid_spec,
        out_shape=jax.ShapeDtypeStruct((n_rows * ROW_TILE, LANES), F32),
        compiler_params=_cparams(("arbitrary",)),
        name="moe_dispatch",
    )(last_block, has_rows, dest3, h2t)


def _expert_kernel(be_ref, nu_ref, x_ref, wg_ref, wu_ref, wd_ref, y_ref, wg_sc, wu_sc, wd_sc, *, bm):
    b = pl.program_id(0)
    n_used = nu_ref[0]

    @pl.when(b < n_used)
    def _():
        @pl.when((b == 0) | (be_ref[b] != be_ref[jnp.maximum(b - 1, 0)]))
        def _():
            wg_sc[...] = wg_ref[0, 0].astype(BF16)
            wu_sc[...] = wu_ref[0, 0].astype(BF16)
            wd_sc[...] = wd_ref[0, 0].astype(BF16)

        x = _load_row_tiles(x_ref, 0, bm).astype(BF16)
        g = jnp.dot(x, wg_sc[...], preferred_element_type=F32)
        u = jnp.dot(x, wu_sc[...], preferred_element_type=F32)
        hmid = (g * jax.nn.sigmoid(g) * u).astype(BF16)
        _store_row_tiles(y_ref, jnp.dot(hmid, wd_sc[...], preferred_element_type=F32))

    @pl.when(b >= n_used)
    def _():
        y_ref[...] = jnp.zeros(y_ref.shape, F32)


def _experts(x_buf, block_expert, n_used, w_gate, w_up, w_down, layer, bm):
    nb = x_buf.shape[0] // (bm * ROW_TILE)
    grid_spec = pltpu.PrefetchScalarGridSpec(
        num_scalar_prefetch=2,
        grid=(nb,),
        in_specs=[
            pl.BlockSpec((bm * ROW_TILE, LANES), lambda b, be, nu: (jnp.minimum(b, nu[0] - 1), 0)),
            pl.BlockSpec((1, 1, D_MODEL, D_EXPERT), lambda b, be, nu: (layer, be[b], 0, 0)),
            pl.BlockSpec((1, 1, D_MODEL, D_EXPERT), lambda b, be, nu: (layer, be[b], 0, 0)),
            pl.BlockSpec((1, 1, D_EXPERT, D_MODEL), lambda b, be, nu: (layer, be[b], 0, 0)),
        ],
        out_specs=pl.BlockSpec((bm * ROW_TILE, LANES), lambda b, be, nu: (b, 0)),
        scratch_shapes=[
            pltpu.VMEM((D_MODEL, D_EXPERT), BF16),
            pltpu.VMEM((D_MODEL, D_EXPERT), BF16),
            pltpu.VMEM((D_EXPERT, D_MODEL), BF16),
        ],
    )
    return pl.pallas_call(
        functools.partial(_expert_kernel, bm=bm),
        grid_spec=grid_spec,
        out_shape=jax.ShapeDtypeStruct((nb * bm * ROW_TILE, LANES), F32),
        compiler_params=_cparams(("arbitrary",)),
        name="routed_experts",
    )(block_expert, n_used, x_buf, w_gate, w_up, w_down)


def _combine_kernel(ts_ref, d_ref, dn_ref, y_hbm, x1_ref, h2_ref, gate_ref, mod_ref, wsg_ref, wsu_ref, wsd_ref,
                    o_ref, yg_sc, sem, *, tc):
    del ts_ref
    i = pl.program_id(0)
    n = pl.num_programs(0)
    slot = i % 2
    n_copies = tc * TOP_K

    def row_copy(src, s, row):
        return pltpu.make_async_copy(y_hbm.at[pl.ds(src * ROW_TILE, ROW_TILE), :],
                                     yg_sc.at[s, pl.ds(row * ROW_TILE, ROW_TILE), :], sem.at[s])

    def issue(dest_smem, s):
        def body(a, c):
            row_copy(dest_smem[0, 0, a], s, (a & (TOP_K - 1)) * tc + (a >> 3)).start()
            return c
        lax.fori_loop(0, n_copies, body, 0, unroll=8)

    @pl.when(i == 0)
    def _():
        issue(d_ref, 0)

    @pl.when(i + 1 < n)
    def _():
        issue(dn_ref, 1 - slot)

    pltpu.make_async_copy(y_hbm.at[pl.ds(0, n_copies * ROW_TILE), :], yg_sc.at[slot], sem.at[slot]).wait()

    gate = gate_ref[...]
    yg = yg_sc.at[slot]
    routed = gate[:, 0:1] * _load_row_tiles(yg, 0, tc)
    for k in range(1, TOP_K):
        routed = routed + gate[:, k:k + 1] * _load_row_tiles(yg, k * tc, tc)
    hb = _load_row_tiles(h2_ref, 0, tc).astype(BF16)
    g = jnp.dot(hb, wsg_ref[...], preferred_element_type=F32)
    u = jnp.dot(hb, wsu_ref[...], preferred_element_type=F32)
    shared = jnp.dot((g * jax.nn.sigmoid(g) * u).astype(BF16), wsd_ref[...], preferred_element_type=F32)
    o_ref[...] = x1_ref[...] + mod_ref[0][5:6] * (routed + shared)


def _combine(y_buf, dest3, x1, h2, gate, mod_l, ws_gate, ws_up, ws_down, tile_seq, tc):
    t = x1.shape[0]
    n = t // tc
    d_sh = ws_gate.shape[1]
    const = lambda i, ts: (0, 0)
    row = lambda i, ts: (i, 0)
    grid_spec = pltpu.PrefetchScalarGridSpec(
        num_scalar_prefetch=1,
        grid=(n,),
        in_specs=[
            pl.BlockSpec((1, 1, tc * TOP_K), lambda i, ts: (i, 0, 0), memory_space=pltpu.SMEM),
            pl.BlockSpec((1, 1, tc * TOP_K), lambda i, ts: (jnp.minimum(i + 1, n - 1), 0, 0),
                         memory_space=pltpu.SMEM),
            pl.BlockSpec(memory_space=pl.ANY),
            pl.BlockSpec((tc, D_MODEL), row),
            pl.BlockSpec((tc * ROW_TILE, LANES), row),
            pl.BlockSpec((tc, LANES), row),
            pl.BlockSpec((1, N_ADA, D_MODEL), lambda i, ts: (ts[i], 0, 0)),
            pl.BlockSpec((D_MODEL, d_sh), const),
            pl.BlockSpec((D_MODEL, d_sh), const),
            pl.BlockSpec((d_sh, D_MODEL), const),
        ],
        out_specs=pl.BlockSpec((tc, D_MODEL), row),
        scratch_shapes=[pltpu.VMEM((2, tc * TOP_K * ROW_TILE, LANES), F32), pltpu.SemaphoreType.DMA((2,))],
    )
    return pl.pallas_call(
        functools.partial(_combine_kernel, tc=tc),
        grid_spec=grid_spec,
        out_shape=jax.ShapeDtypeStruct((t, D_MODEL), F32),
        compiler_params=_cparams(("arbitrary",)),
        name="moe_combine",
    )(tile_seq, dest3, dest3, y_buf, x1, h2, gate, mod_l, ws_gate, ws_up, ws_down)


def _tile_meta(seq_lens, tile):
    seq, first, last, pos0, slen = [], [], [], [], []
    for s, n in enumerate(seq_lens):
        nt = n // tile
        for j in range(nt):
            seq.append(s)
            first.append(int(j == 0))
            last.append(int(j == nt - 1))
            pos0.append(j * tile)
            slen.append(n)
    return tuple(jnp.asarray(np.asarray(a, np.int32)) for a in (seq, first, last, pos0, slen))


def _scan_meta(seq_lens, tr):
    bwd, first = [], []
    base = 0
    for n in seq_lens:
        nt = n // SSM_CHUNK // tr
        for j in range(nt):
            bwd.append(base + nt - 1 - j)
            first.append(int(j == 0))
        base += nt
    return jnp.asarray(np.asarray(bwd, np.int32)), jnp.asarray(np.asarray(first, np.int32))


def _rope_tables(seq_lens):
    quarter = HEAD_DIM // 4
    freqs = ROPE_THETA ** (-jnp.arange(quarter, dtype=F32) / quarter)
    sign = jnp.tile(jnp.concatenate([-jnp.ones((quarter,), F32), jnp.ones((quarter,), F32)]), 2)
    cos_l, sin_l = [], []
    cache = {}
    for n in seq_lens:
        if n not in cache:
            pos = jnp.arange(n)
            ar = (pos // GRID_W).astype(F32)[:, None] * freqs
            ac = (pos % GRID_W).astype(F32)[:, None] * freqs
            ang = jnp.concatenate([ar, ar, ac, ac], axis=-1)
            cache[n] = (jnp.tile(jnp.cos(ang), (1, 2)), jnp.tile(jnp.sin(ang) * sign, (1, 2)))
        cos_l.append(cache[n][0])
        sin_l.append(cache[n][1])
    return jnp.concatenate(cos_l, axis=0), jnp.concatenate(sin_l, axis=0)


def _block_diag_ones(n, blk):
    r = np.arange(n) // blk
    return jnp.asarray((r[:, None] == r[None, :]).astype(np.float32)).astype(BF16)


def _dest_kernel(idx_ref, rank_ref, ps_ref, d_ref):
    idx = idx_ref[...]
    out = rank_ref[...]
    lane_e = lax.broadcasted_iota(I32, (idx.shape[0], N_EXPERTS), 1)
    lane = lax.broadcasted_iota(I32, idx.shape, 1)
    for k in range(TOP_K):
        start = jnp.sum(jnp.where(lane_e == idx[:, k:k + 1], ps_ref[...], 0.0), axis=1, keepdims=True)
        out = jnp.where(lane == k, out + start.astype(I32), out)
    d_ref[...] = out


def _dest_slots(idx, rank, pstart):
    t = idx.shape[0]
    tm = _pick(t, 512)
    row = lambda i: (i, 0)
    return pl.pallas_call(
        _dest_kernel,
        grid=(t // tm,),
        in_specs=[pl.BlockSpec((tm, LANES), row), pl.BlockSpec((tm, LANES), row),
                  pl.BlockSpec((1, N_EXPERTS), lambda i: (0, 0))],
        out_specs=pl.BlockSpec((tm, LANES), row),
        out_shape=jax.ShapeDtypeStruct((t, LANES), I32),
        compiler_params=_cparams(("arbitrary",)),
        name="moe_dest_slots",
    )(idx, rank, pstart)


def _moe_plan(idx, rank, counts, t, bm, nb):
    cnt = counts.reshape(N_EXPERTS).astype(I32)
    padded = (cnt + bm - 1) // bm * bm
    pend = jnp.cumsum(padded)
    pstart = pend - padded
    n_used = (pend[-1] // bm).astype(I32).reshape(1)
    block_expert = jnp.minimum(
        jnp.searchsorted(pend, jnp.arange(nb, dtype=I32) * bm, side='right'), N_EXPERTS - 1).astype(I32)
    dest = _dest_slots(idx, rank, pstart.astype(F32).reshape(1, N_EXPERTS))[:, :TOP_K]
    return dest, (pend - bm).astype(I32), (cnt > 0).astype(I32), block_expert, n_used


def kernel(x_prompt, x_sample, c_prompt, c_sample, w_ada, b_ada, norm1_g, w_in, q_norm_g, k_norm_g, w_attn_o, pool_w, pool_scale, w_pool_o, ssm_a_re, ssm_a_im, ssm_log_dt, ssm_b_re, ssm_b_im, ssm_c_re, ssm_c_im, ssm_d, w_glu, w_out, norm2_g, w_router, b_router, w_exp_gate, w_exp_up, w_exp_down, w_sh_gate, w_sh_up, w_sh_down):
    b1, l1, d = x_prompt.shape
    b2, l2, _ = x_sample.shape
    depth = w_in.shape[0]
    assert d == D_MODEL and b1 + b2 <= MOD_ROWS
    seq_lens = [l1] * b1 + [l2] * b2
    t = b1 * l1 + b2 * l2
    lmin = min(l1, l2)

    tm_in = _pick(lmin, 512)
    tm_post = _pick(lmin, 256)
    tc_comb = _pick(lmin, 128)
    tr_scan = _pick(lmin // SSM_CHUNK, 256)
    bm = 256
    nb = -(-(t * TOP_K + N_EXPERTS * (bm - 1)) // bm)

    x = jnp.concatenate([x_prompt.reshape(b1 * l1, d), x_sample.reshape(b2 * l2, d)], axis=0)
    c_all = jnp.concatenate([c_prompt, c_sample, jnp.zeros((MOD_ROWS - b1 - b2, d), F32)], axis=0)
    mod = _modulation(c_all, w_ada, b_ada).reshape(depth, MOD_ROWS, N_ADA, d)

    cos_t, sin_t = _rope_tables(seq_lens)
    ones_bd = _block_diag_ones(ATTN_DIM, HEAD_DIM)
    tril = jnp.asarray(np.tril(np.ones((tm_post, tm_post), np.float32), -1)).astype(BF16)
    meta_in = _tile_meta(seq_lens, tm_in)
    meta_post = _tile_meta(seq_lens, tm_post)
    meta_comb = _tile_meta(seq_lens, tc_comb)
    bwd_tile, first_tile = _scan_meta(seq_lens, tr_scan)

    for l in range(depth):
        q8, k2, v2, u_pool, u_ssm, gates = _inproj(
            x, mod[l], norm1_g[l].reshape(1, d), w_in[l].astype(BF16),
            jnp.tile(q_norm_g[l], N_Q_HEADS).reshape(1, ATTN_DIM),
            jnp.tile(k_norm_g[l], N_KV_HEADS).reshape(1, KV_DIM),
            cos_t, sin_t, ones_bd, meta_in[0], tm_in)
        attn = jnp.concatenate(
            [_flash(q8, k2, v2, 0, b1, l1), _flash(q8, k2, v2, b1 * l1, b2, l2)], axis=0)
        tables = _ssm_tables(ssm_a_re[l], ssm_a_im[l], ssm_log_dt[l], ssm_b_re[l], ssm_b_im[l],
                             ssm_c_re[l], ssm_c_im[l], ssm_d[l])
        y_ssm = _bidir_ssm(u_ssm, tables, bwd_tile, first_tile, tr_scan)

        pool_bd = jax.scipy.linalg.block_diag(*[pool_w[l, g] for g in range(len(POOL_WINDOWS))])
        wts = (w_attn_o[l].astype(BF16), pool_bd.astype(BF16), pool_scale[l].reshape(1, POOL_DIM),
               w_pool_o[l].astype(BF16), w_glu[l].astype(BF16), w_out[l].astype(BF16),
               norm2_g[l].reshape(1, d), w_router[l], b_router[l].reshape(1, N_EXPERTS), tril)
        x1, h2, idx, gate, rank, counts = _post(x, attn, u_pool, y_ssm, gates, mod[l], wts, meta_post, tm_post)

        dest, last_block, has_rows, block_expert, n_used = _moe_plan(idx, rank, counts, t, bm, nb)
        dest3 = dest.reshape(t // tc_comb, 1, tc_comb * TOP_K)
        x_buf = _dispatch(h2, dest3, last_block, has_rows, nb * bm, tc_comb, bm)
        y_buf = _experts(x_buf, block_expert, n_used, w_exp_gate, w_exp_up, w_exp_down, l, bm)
        x = _combine(y_buf, dest3, x1, h2, gate, mod[l],
                     w_sh_gate[l].astype(BF16), w_sh_up[l].astype(BF16), w_sh_down[l].astype(BF16),
                     meta_comb[0], tc_comb)

    return (x[:b1 * l1].reshape(b1, l1, d), x[b1 * l1:].reshape(b2, l2, d))
```

```python
import functools
import math

import jax
import jax.numpy as jnp
import numpy as np
from jax import lax
from jax.experimental import pallas as pl
from jax.experimental.pallas import tpu as pltpu

F32 = jnp.float32
BF16 = jnp.bfloat16
I32 = jnp.int32
HIGHEST = lax.Precision.HIGHEST

D_MODEL = 1024
GRID_W = 64
HEAD_DIM = 64
N_Q_HEADS = 8
N_KV_HEADS = 2
Q_PER_KV = N_Q_HEADS // N_KV_HEADS
ATTN_DIM = N_Q_HEADS * HEAD_DIM
KV_DIM = N_KV_HEADS * HEAD_DIM
ROPE_THETA = 10000.0
POOL_WINDOWS = (2, 4, 8, 16)
POOL_DIM = 256
POOL_GROUP = 64
POOL_HALO = 8
SSM_DIM = 256
SSM_P = 16
SSM_G = 16
SSM_N = 64
SSM_CHUNK = 16
SSM_ROW = SSM_CHUNK * SSM_P
SSM_STATE = 2 * SSM_N
N_EXPERTS = 256
TOP_K = 8
N_EXPERT_GROUPS = 8
GROUP_SIZE = N_EXPERTS // N_EXPERT_GROUPS
TOPK_GROUPS = 4
D_EXPERT = 256
ROUTED_SCALE = 2.5
N_ADA = 6
EPS = 1e-6
IN_DIM = ATTN_DIM + 2 * KV_DIM + POOL_DIM + SSM_DIM + 3 * D_MODEL
OFF_K = ATTN_DIM
OFF_V = OFF_K + KV_DIM
OFF_POOL = OFF_V + KV_DIM
OFF_SSM = OFF_POOL + POOL_DIM
OFF_GATES = OFF_SSM + SSM_DIM
MOD_ROWS = 8
LOG2_E = math.log2(math.e)
VT_ROWS = HEAD_DIM + 16

V7X_VMEM_BYTES = 64 * 1024 * 1024
VMEM_LIMIT = V7X_VMEM_BYTES - 8 * 1024 * 1024
LANES = 128


def _cparams(sem):
    return pltpu.CompilerParams(dimension_semantics=sem, vmem_limit_bytes=VMEM_LIMIT)


def _pick(n, pref):
    t = min(n, pref)
    while n % t:
        t //= 2
    return t


ROW_TILE = D_MODEL // LANES
DMA_PRIORITIES = 2


def _store_row_tiles(ref, x):
    n = x.shape[0]
    for s in range(ROW_TILE):
        ref[pl.ds(s, n, stride=ROW_TILE), :] = x[:, s * LANES:(s + 1) * LANES]


def _load_row_tiles(ref, first_row, n):
    return jnp.concatenate(
        [ref[pl.ds(first_row * ROW_TILE + s, n, stride=ROW_TILE), :] for s in range(ROW_TILE)], axis=1)


def _mod_kernel(c_ref, w_ref, b_ref, o_ref):
    c = c_ref[...]
    a = c * jax.nn.sigmoid(c)
    o_ref[0] = jnp.dot(a, w_ref[0], preferred_element_type=F32, precision=HIGHEST) + b_ref[0]


def _modulation(c_all, w_ada, b_ada):
    depth, d, n = w_ada.shape
    bn = _pick(n, 1536)
    return pl.pallas_call(
        _mod_kernel,
        grid=(depth, n // bn),
        in_specs=[
            pl.BlockSpec((MOD_ROWS, d), lambda l, j: (0, 0)),
            pl.BlockSpec((1, d, bn), lambda l, j: (l, 0, j)),
            pl.BlockSpec((1, 1, bn), lambda l, j: (l, 0, j)),
        ],
        out_specs=pl.BlockSpec((1, MOD_ROWS, bn), lambda l, j: (l, 0, j)),
        out_shape=jax.ShapeDtypeStruct((depth, MOD_ROWS, n), F32),
        compiler_params=_cparams(("arbitrary", "arbitrary")),
        name="adaln_mod",
    )(c_all, w_ada, b_ada.reshape(depth, 1, n))


def _rmsnorm_mod(x, g, scale, shift):
    y = x * lax.rsqrt(jnp.mean(x * x, axis=-1, keepdims=True) + EPS)
    return (y * g) * (1.0 + scale) + shift


def _head_norm_rope(z, gain, cos, sin_signed, ones_bd):
    z2 = z * z
    hi = z2.astype(BF16)
    lo = (z2 - hi.astype(F32)).astype(BF16)
    ss = (jnp.dot(hi, ones_bd, preferred_element_type=F32)
          + jnp.dot(lo, ones_bd, preferred_element_type=F32))
    y = (z * lax.rsqrt(ss * (1.0 / HEAD_DIM) + EPS)) * gain
    w = z.shape[1]
    quarter = HEAD_DIM // 4
    from_right = pltpu.roll(y, w - quarter, 1)
    from_left = pltpu.roll(y, quarter, 1)
    lane = lax.broadcasted_iota(I32, y.shape, 1)
    rot = jnp.where((lane & quarter) == 0, from_right, from_left)
    return y * cos + rot * sin_signed


def _inproj_kernel(ts_ref, x_ref, mod_ref, g_ref, w_ref, gq_ref, gk_ref, cos_ref, sin_ref, ones_ref,
                   q_ref, k_ref, v_ref, up_ref, us_ref, gt_ref):
    del ts_ref
    x = x_ref[...]
    mod = mod_ref[0]
    h = _rmsnorm_mod(x, g_ref[...], mod[1:2], mod[0:1]).astype(BF16)

    cos = cos_ref[...]
    sin = sin_ref[...]
    zq = jnp.dot(h, w_ref[:, 0:ATTN_DIM], preferred_element_type=F32)
    reps = ATTN_DIM // LANES
    yq = _head_norm_rope(zq, gq_ref[...], jnp.concatenate([cos] * reps, axis=1),
                         jnp.concatenate([sin] * reps, axis=1), ones_ref[...]) * (HEAD_DIM ** -0.5 * LOG2_E)
    for hh in range(N_Q_HEADS):
        q_ref[hh] = yq[:, hh * HEAD_DIM:(hh + 1) * HEAD_DIM].astype(BF16)

    zkv = jnp.dot(h, w_ref[:, OFF_K:OFF_POOL], preferred_element_type=F32)
    yk = _head_norm_rope(zkv[:, :KV_DIM], gk_ref[...], cos, sin, ones_ref[0:KV_DIM, 0:KV_DIM])
    zv = zkv[:, KV_DIM:]
    for j in range(N_KV_HEADS):
        k_ref[j] = yk[:, j * HEAD_DIM:(j + 1) * HEAD_DIM].astype(BF16)
        v_ref[j, 0:HEAD_DIM, :] = zv[:, j * HEAD_DIM:(j + 1) * HEAD_DIM].T.astype(BF16)
        v_ref[j, HEAD_DIM:VT_ROWS, :] = jnp.ones((VT_ROWS - HEAD_DIM, zv.shape[0]), BF16)

    zps = jnp.dot(h, w_ref[:, OFF_POOL:OFF_GATES], preferred_element_type=F32)
    up_ref[...] = zps[:, :POOL_DIM]
    us_ref[...] = zps[:, POOL_DIM:]
    for c in range(3):
        zg = jnp.dot(h, w_ref[:, OFF_GATES + c * D_MODEL:OFF_GATES + (c + 1) * D_MODEL],
                     preferred_element_type=F32)
        gt_ref[:, c * D_MODEL:(c + 1) * D_MODEL] = jax.nn.sigmoid(zg).astype(BF16)


def _inproj(x, mod_l, g1, w_in_bf, gq_t, gk_t, cos_t, sin_t, ones_bd, tile_seq, tm):
    t = x.shape[0]
    const = lambda i, ts: (0, 0)
    row = lambda i, ts: (i, 0)
    grid_spec = pltpu.PrefetchScalarGridSpec(
        num_scalar_prefetch=1,
        grid=(t // tm,),
        in_specs=[
            pl.BlockSpec((tm, D_MODEL), row),
            pl.BlockSpec((1, N_ADA, D_MODEL), lambda i, ts: (ts[i], 0, 0)),
            pl.BlockSpec((1, D_MODEL), const),
            pl.BlockSpec((D_MODEL, IN_DIM), const),
            pl.BlockSpec((1, ATTN_DIM), const),
            pl.BlockSpec((1, KV_DIM), const),
            pl.BlockSpec((tm, LANES), row),
            pl.BlockSpec((tm, LANES), row),
            pl.BlockSpec((ATTN_DIM, ATTN_DIM), const),
        ],
        out_specs=[
            pl.BlockSpec((N_Q_HEADS, tm, HEAD_DIM), lambda i, ts: (0, i, 0)),
            pl.BlockSpec((N_KV_HEADS, tm, HEAD_DIM), lambda i, ts: (0, i, 0)),
            pl.BlockSpec((N_KV_HEADS, VT_ROWS, tm), lambda i, ts: (0, 0, i)),
            pl.BlockSpec((tm, POOL_DIM), row),
            pl.BlockSpec((tm, SSM_DIM), row),
            pl.BlockSpec((tm, 3 * D_MODEL), row),
        ],
    )
    return pl.pallas_call(
        _inproj_kernel,
        grid_spec=grid_spec,
        out_shape=[
            jax.ShapeDtypeStruct((N_Q_HEADS, t, HEAD_DIM), BF16),
            jax.ShapeDtypeStruct((N_KV_HEADS, t, HEAD_DIM), BF16),
            jax.ShapeDtypeStruct((N_KV_HEADS, VT_ROWS, t), BF16),
            jax.ShapeDtypeStruct((t, POOL_DIM), F32),
            jax.ShapeDtypeStruct((t, SSM_DIM), F32),
            jax.ShapeDtypeStruct((t, 3 * D_MODEL), BF16),
        ],
        compiler_params=_cparams(("arbitrary",)),
        name="norm1_inproj",
    )(tile_seq, x, mod_l, g1, w_in_bf, gq_t, gk_t, cos_t, sin_t, ones_bd)


def _flash_kernel(q_ref, k_ref, vt_ref, o_ref, m_sc, acc_sc):
    kb = pl.program_id(3)

    @pl.when(kb == 0)
    def _():
        m_sc[...] = jnp.full(m_sc.shape, -jnp.inf, F32)
        acc_sc[...] = jnp.zeros(acc_sc.shape, F32)

    k = k_ref[0]
    vt = vt_ref[0]
    heads = range(Q_PER_KV)
    qs = [q_ref[hh] for hh in heads]
    m_prev = [m_sc[hh] for hh in heads]
    acc_prev = [acc_sc[hh] for hh in heads]

    def scores(hh):
        return lax.dot_general(k, qs[hh], (((1,), (1,)), ((), ())), preferred_element_type=F32)

    ahead = 2
    s = {hh: scores(hh) for hh in range(ahead)}
    new = []
    for hh in heads:
        if hh + ahead < Q_PER_KV:
            s[hh + ahead] = scores(hh + ahead)
        m_new = jnp.maximum(m_prev[hh], jnp.max(s[hh], axis=0, keepdims=True))
        alpha = jnp.exp2(m_prev[hh] - m_new)
        p = jnp.exp2(s.pop(hh) - m_new).astype(BF16)
        acc_new = acc_prev[hh] * alpha + jnp.dot(vt, p, preferred_element_type=F32)
        new.append((m_new, acc_new))
    for hh, (m_new, acc_new) in enumerate(new):
        m_sc[hh] = m_new
        acc_sc[hh] = acc_new

    @pl.when(kb == pl.num_programs(3) - 1)
    def _():
        for hh in range(Q_PER_KV):
            acc = acc_sc[hh]
            o = acc[:HEAD_DIM] / acc[HEAD_DIM:HEAD_DIM + 1]
            o_ref[:, hh * HEAD_DIM:(hh + 1) * HEAD_DIM] = o.T.astype(BF16)


def _flash(q8, k2, vt2, off, bsz, seq):
    tq = _pick(seq, 512)
    tk = _pick(seq, 2048)
    nq, nk = seq // tq, seq // tk
    assert off % tq == 0 and off % tk == 0
    oq, ok = off // tq, off // tk
    return pl.pallas_call(
        _flash_kernel,
        grid=(N_KV_HEADS, bsz, nq, nk),
        in_specs=[
            pl.BlockSpec((Q_PER_KV, tq, HEAD_DIM), lambda j, b, i, kb: (j, oq + b * nq + i, 0)),
            pl.BlockSpec((1, tk, HEAD_DIM), lambda j, b, i, kb: (j, ok + b * nk + kb, 0)),
            pl.BlockSpec((1, VT_ROWS, tk), lambda j, b, i, kb: (j, 0, ok + b * nk + kb)),
        ],
        out_specs=pl.BlockSpec((tq, Q_PER_KV * HEAD_DIM), lambda j, b, i, kb: (b * nq + i, j)),
        out_shape=jax.ShapeDtypeStruct((bsz * seq, ATTN_DIM), BF16),
        scratch_shapes=[
            pltpu.VMEM((Q_PER_KV, 1, tq), F32),
            pltpu.VMEM((Q_PER_KV, VT_ROWS, tq), F32),
        ],
        compiler_params=_cparams(("arbitrary", "arbitrary", "arbitrary", "arbitrary")),
        name="flash_gqa",
    )(q8, k2, vt2)


def _ssm_tables(a_re, a_im, log_dt, b_re, b_im, c_re, c_im, d_skip):
    tc = SSM_CHUNK
    a = lax.complex(a_re.astype(F32), a_im.astype(F32))
    dt = jnp.exp(log_dt.astype(F32))[..., None]
    adt = a * dt
    a_bar = jnp.exp(adt)
    b_bar = ((a_bar - 1.0) / a)[..., None] * lax.complex(b_re.astype(F32), b_im.astype(F32))
    c_mat = lax.complex(c_re.astype(F32), c_im.astype(F32))
    taus = jnp.arange(tc + 1, dtype=F32)
    pw = jnp.exp(adt[None] * taus[:, None, None, None])

    kern = jnp.real(jnp.einsum('dgpn,tdgn,dgnq->dgtpq', c_mat, pw[:tc], b_bar))
    s_idx = jnp.arange(tc)[:, None]
    t_idx = jnp.arange(tc)[None, :]
    lag_f = jnp.clip(t_idx - s_idx, 0, tc - 1)
    lag_b = jnp.clip(s_idx - t_idx, 0, tc - 1)
    kf = jnp.where((t_idx >= s_idx)[None, :, :, None, None], kern[0][:, lag_f], 0.0)
    kb = jnp.where((s_idx >= t_idx)[None, :, :, None, None], kern[1][:, lag_b], 0.0)
    w_loc = (kf + kb).transpose(0, 1, 4, 2, 3).reshape(SSM_G, SSM_ROW, SSM_ROW)
    w_loc = w_loc + jnp.eye(SSM_ROW, dtype=F32)[None] * jnp.tile(
        d_skip.astype(F32).reshape(SSM_G, 1, SSM_P), (1, tc, 1)).reshape(SSM_G, 1, SSM_ROW)

    def ri(z, axis):
        return jnp.concatenate([jnp.real(z), jnp.imag(z)], axis=axis)

    pf = jnp.einsum('sgn,gnq->gsqn', pw[:tc][::-1, 0], b_bar[0])
    pb = jnp.einsum('sgn,gnq->gsqn', pw[:tc, 1], b_bar[1])
    p_st = jnp.concatenate([ri(pf, -1), ri(pb, -1)], axis=-1).reshape(SSM_G, SSM_ROW, 2 * SSM_STATE)
    qf = jnp.einsum('gpn,tgn->gntp', c_mat[0], pw[1:tc + 1, 0])
    qb = jnp.einsum('gpn,tgn->gntp', c_mat[1], pw[1:tc + 1, 1][::-1])
    q_f = jnp.concatenate([jnp.real(qf), -jnp.imag(qf)], axis=1).reshape(SSM_G, SSM_STATE, SSM_ROW)
    q_b = jnp.concatenate([jnp.real(qb), -jnp.imag(qb)], axis=1).reshape(SSM_G, SSM_STATE, SSM_ROW)
    a16 = pw[tc]
    a1 = jnp.concatenate([jnp.real(a16), jnp.real(a16)], axis=-1).reshape(2, SSM_G * SSM_STATE)
    a2 = jnp.concatenate([-jnp.imag(a16), jnp.imag(a16)], axis=-1).reshape(2, SSM_G * SSM_STATE)
    a_mul = jnp.stack([a1[0], a2[0], a1[1], a2[1]], axis=0)
    return w_loc.astype(BF16), p_st.astype(BF16), q_f.astype(BF16), q_b.astype(BF16), a_mul


def _ssm_state_kernel(u_ref, p_ref, s_ref):
    s = jnp.dot(u_ref[0].astype(BF16), p_ref[0], preferred_element_type=F32)
    s_ref[0] = s[:, :SSM_STATE]
    s_ref[1] = s[:, SSM_STATE:]


def _ssm_states(ug, p_st, tc):
    g, nc, _ = ug.shape
    return pl.pallas_call(
        _ssm_state_kernel,
        grid=(g, nc // tc),
        in_specs=[
            pl.BlockSpec((1, tc, SSM_ROW), lambda gi, i: (gi, i, 0)),
            pl.BlockSpec((1, SSM_ROW, 2 * SSM_STATE), lambda gi, i: (gi, 0, 0)),
        ],
        out_specs=pl.BlockSpec((2, tc, SSM_STATE), lambda gi, i: (0, i, gi)),
        out_shape=jax.ShapeDtypeStruct((2, nc, g * SSM_STATE), F32),
        compiler_params=_cparams(("arbitrary", "arbitrary")),
        name="ssm_chunk_states",
    )(ug, p_st)


def _swap_halves(x):
    parts = [pltpu.roll(x[:, g * LANES:(g + 1) * LANES], LANES // 2, 1) for g in range(x.shape[1] // LANES)]
    return jnp.concatenate(parts, axis=1)


def _ssm_scan_kernel(bt_ref, first_ref, sf_ref, sb_ref, am_ref, xf_ref, zb_ref, st_sc, sfw_sc, sbw_sc, *, tr):
    n = pl.program_id(0)
    del bt_ref

    @pl.when(first_ref[n] == 1)
    def _():
        st_sc[...] = jnp.zeros(st_sc.shape, F32)

    sfw_sc[...] = _swap_halves(sf_ref[0])
    sbw_sc[...] = _swap_halves(sb_ref[0])
    a1f, a2f, a1b, a2b = am_ref[0:1], am_ref[1:2], am_ref[2:3], am_ref[3:4]

    def body(r, carry):
        xf, xfw, zb, zbw = carry
        xf_ref[pl.ds(r, 1), :] = xf
        sf = sf_ref[0, pl.ds(r, 1), :]
        sfw = sfw_sc[pl.ds(r, 1), :]
        nxf = a1f * xf + a2f * xfw + sf
        nxfw = a1f * xfw - a2f * xf + sfw
        rb = tr - 1 - r
        zb_ref[pl.ds(rb, 1), :] = zb
        sb = sb_ref[0, pl.ds(rb, 1), :]
        sbw = sbw_sc[pl.ds(rb, 1), :]
        nzb = a1b * zb + a2b * zbw + sb
        nzbw = a1b * zbw - a2b * zb + sbw
        return nxf, nxfw, nzb, nzbw

    init = (st_sc[0:1], st_sc[1:2], st_sc[2:3], st_sc[3:4])
    xf, xfw, zb, zbw = lax.fori_loop(0, tr, body, init)
    st_sc[0:1] = xf
    st_sc[1:2] = xfw
    st_sc[2:3] = zb
    st_sc[3:4] = zbw


def _ssm_scan(s_st, a_mul, bwd_tile, first_tile, tr):
    _, nc, w = s_st.shape
    grid_spec = pltpu.PrefetchScalarGridSpec(
        num_scalar_prefetch=2,
        grid=(nc // tr,),
        in_specs=[
            pl.BlockSpec((1, tr, w), lambda n, bt, ft: (0, n, 0)),
            pl.BlockSpec((1, tr, w), lambda n, bt, ft: (1, bt[n], 0)),
            pl.BlockSpec((4, w), lambda n, bt, ft: (0, 0)),
        ],
        out_specs=[
            pl.BlockSpec((tr, w), lambda n, bt, ft: (n, 0)),
            pl.BlockSpec((tr, w), lambda n, bt, ft: (bt[n], 0)),
        ],
        scratch_shapes=[pltpu.VMEM((4, w), F32), pltpu.VMEM((tr, w), F32), pltpu.VMEM((tr, w), F32)],
    )
    return pl.pallas_call(
        functools.partial(_ssm_scan_kernel, tr=tr),
        grid_spec=grid_spec,
        out_shape=[jax.ShapeDtypeStruct((nc, w), F32), jax.ShapeDtypeStruct((nc, w), F32)],
        compiler_params=_cparams(("arbitrary",)),
        name="ssm_chunk_scan",
    )(bwd_tile, first_tile, s_st, s_st, a_mul)


def _ssm_out_kernel(u_ref, xf_ref, zb_ref, w_ref, qf_ref, qb_ref, y_ref):
    y = jnp.dot(u_ref[0].astype(BF16), w_ref[0], preferred_element_type=F32)
    y += jnp.dot(xf_ref[...].astype(BF16), qf_ref[0], preferred_element_type=F32)
    y += jnp.dot(zb_ref[...].astype(BF16), qb_ref[0], preferred_element_type=F32)
    y_ref[0] = y


def _ssm_outputs(ug, xf, zb, w_loc, q_f, q_b, tc):
    g, nc, _ = ug.shape
    return pl.pallas_call(
        _ssm_out_kernel,
        grid=(g, nc // tc),
        in_specs=[
            pl.BlockSpec((1, tc, SSM_ROW), lambda gi, i: (gi, i, 0)),
            pl.BlockSpec((tc, SSM_STATE), lambda gi, i: (i, gi)),
            pl.BlockSpec((tc, SSM_STATE), lambda gi, i: (i, gi)),
            pl.BlockSpec((1, SSM_ROW, SSM_ROW), lambda gi, i: (gi, 0, 0)),
            pl.BlockSpec((1, SSM_STATE, SSM_ROW), lambda gi, i: (gi, 0, 0)),
            pl.BlockSpec((1, SSM_STATE, SSM_ROW), lambda gi, i: (gi, 0, 0)),
        ],
        out_specs=pl.BlockSpec((1, tc, SSM_ROW), lambda gi, i: (gi, i, 0)),
        out_shape=jax.ShapeDtypeStruct((g, nc, SSM_ROW), F32),
        compiler_params=_cparams(("arbitrary", "arbitrary")),
        name="ssm_chunk_outputs",
    )(ug, xf, zb, w_loc, q_f, q_b)


def _bidir_ssm(u_ssm, tables, bwd_tile, first_tile, tr):
    w_loc, p_st, q_f, q_b, a_mul = tables
    t = u_ssm.shape[0]
    nc = t // SSM_CHUNK
    ug = u_ssm.reshape(nc, SSM_CHUNK, SSM_G, SSM_P).transpose(2, 0, 1, 3).reshape(SSM_G, nc, SSM_ROW)
    tc = _pick(nc, 512)
    s_st = _ssm_states(ug, p_st, tc)
    xf, zb = _ssm_scan(s_st, a_mul, bwd_tile, first_tile, tr)
    yg = _ssm_outputs(ug, xf, zb, w_loc, q_f, q_b, tc)
    return yg.reshape(SSM_G, nc, SSM_CHUNK, SSM_P).transpose(1, 2, 0, 3).reshape(t, SSM_DIM)


def _pool_mixer(ext_sc, u, pos, seq_len, tm):
    halves = []
    for half in range(2):
        sums = []
        for w in POOL_WINDOWS[2 * half:2 * half + 2]:
            lo = w // 2
            hi = w - lo - 1
            acc = ext_sc[pl.ds(POOL_HALO - lo, tm), half * LANES:(half + 1) * LANES]
            for d in range(-lo + 1, hi + 1):
                acc = acc + ext_sc[pl.ds(POOL_HALO + d, tm), half * LANES:(half + 1) * LANES]
            cnt = (jnp.minimum(pos + hi, seq_len - 1) - jnp.maximum(pos - lo, 0) + 1).astype(F32)
            sums.append(acc / cnt)
        lane = lax.broadcasted_iota(I32, (tm, LANES), 1)
        halves.append(jnp.where(lane < POOL_GROUP, sums[0], sums[1]))
    return jnp.concatenate(halves, axis=1) - u


def _route(h2, wrt_ref, brt_ref):
    tm = h2.shape[0]
    w_hi, w_lo = wrt_ref[0], wrt_ref[1]
    h_hi = h2.astype(BF16)
    h_lo = (h2 - h_hi.astype(F32)).astype(BF16)
    nt = (((1,), (1,)), ((), ()))
    logits = (lax.dot_general(w_hi, h_hi, nt, preferred_element_type=F32)
              + (lax.dot_general(w_hi, h_lo, nt, preferred_element_type=F32)
                 + lax.dot_general(w_lo, h_hi, nt, preferred_element_type=F32)))
    scores = jax.nn.sigmoid(logits)
    choice = scores + brt_ref[...]
    neg = jnp.float32(-jnp.inf)

    c3 = choice.reshape(N_EXPERT_GROUPS, GROUP_SIZE, tm)
    pos3 = lax.broadcasted_iota(I32, c3.shape, 1)
    m1 = jnp.max(c3, axis=1, keepdims=True)
    i1 = jnp.min(jnp.where(c3 == m1, pos3, GROUP_SIZE), axis=1, keepdims=True)
    m2 = jnp.max(jnp.where(pos3 == i1, neg, c3), axis=1, keepdims=True)
    gs = m1 + m2
    gid = lax.broadcasted_iota(I32, gs.shape, 0)
    ahead = jnp.zeros(gs.shape, I32)
    for o in range(N_EXPERT_GROUPS):
        go = gs[o:o + 1]
        ahead = ahead + ((go > gs) | ((go == gs) & (gid > o))).astype(I32)
    c = jnp.where(ahead < TOPK_GROUPS, c3, neg).reshape(N_EXPERTS, tm)

    row = lax.broadcasted_iota(I32, c.shape, 0)
    picks = []
    for _ in range(TOP_K):
        m = jnp.max(c, axis=0, keepdims=True)
        i = jnp.min(jnp.where(c == m, row, N_EXPERTS), axis=0, keepdims=True)
        oh = row == i
        w = jnp.sum(jnp.where(oh, scores, 0.0), axis=0, keepdims=True)
        c = jnp.where(oh, neg, c)
        picks.append((i, w, oh))
    wsum = picks[0][1]
    for _, w, _ in picks[1:]:
        wsum = wsum + w
    return [(i, w / wsum * ROUTED_SCALE, oh) for i, w, oh in picks]


def _post_kernel(ts_ref, tf_ref, tl_ref, tp_ref, tn_ref,
                 x_ref, at_ref, upp_ref, up_ref, upn_ref, ys_ref, gt_ref, mod_ref,
                 wao_ref, pbd_ref, psc_ref, wpo_ref, wgl_ref, wo_ref, g2_ref, wr_ref, br_ref, triu_ref,
                 x1_ref, h2_ref, idx_ref, gate_ref, rank_ref, cnt_ref, ext_sc, carry_sc, *, tm):
    del ts_ref
    i = pl.program_id(0)
    mod = mod_ref[0]

    attn = jnp.dot(at_ref[...], wao_ref[...], preferred_element_type=F32)

    u = up_ref[...]
    ext_sc[0:POOL_HALO] = jnp.where(tf_ref[i] == 1, 0.0, upp_ref[...])
    ext_sc[POOL_HALO:POOL_HALO + tm] = u
    ext_sc[POOL_HALO + tm:2 * POOL_HALO + tm] = jnp.where(tl_ref[i] == 1, 0.0, upn_ref[...])
    pos = tp_ref[i] + lax.broadcasted_iota(I32, (tm, LANES), 0)
    pm = _pool_mixer(ext_sc, u, pos, tn_ref[i], tm)
    pool = jnp.dot(pm.astype(BF16), pbd_ref[...], preferred_element_type=F32) * psc_ref[...]
    pool = jnp.dot(pool.astype(BF16), wpo_ref[...], preferred_element_type=F32)

    z = jnp.dot(jax.nn.gelu(ys_ref[...]).astype(BF16), wgl_ref[...], preferred_element_type=F32)
    ssm = z[:, :D_MODEL] * jax.nn.sigmoid(z[:, D_MODEL:])

    merged = (gt_ref[:, 0:D_MODEL].astype(F32) * attn
              + gt_ref[:, D_MODEL:2 * D_MODEL].astype(F32) * pool
              + gt_ref[:, 2 * D_MODEL:3 * D_MODEL].astype(F32) * ssm)
    mix = jnp.dot(merged.astype(BF16), wo_ref[...], preferred_element_type=F32)
    x1 = x_ref[...] + mod[2:3] * mix
    x1_ref[...] = x1
    h2 = _rmsnorm_mod(x1, g2_ref[...], mod[4:5], mod[3:4])
    _store_row_tiles(h2_ref, h2)

    picks = _route(h2, wr_ref, br_ref)

    @pl.when(i == 0)
    def _():
        carry_sc[...] = jnp.zeros(carry_sc.shape, F32)

    sel = picks[0][2]
    for _, _, oh in picks[1:]:
        sel = sel | oh
    sel_f = sel.astype(F32)
    ranks = jnp.dot(sel_f.astype(BF16), triu_ref[...], preferred_element_type=F32) + carry_sc[...]
    carry = carry_sc[...] + jnp.sum(sel_f, axis=1, keepdims=True)
    carry_sc[...] = carry
    cnt_ref[...] = carry

    idx_ref[...] = jnp.concatenate([ik for ik, _, _ in picks], axis=0)
    gate_ref[...] = jnp.concatenate([gk for _, gk, _ in picks], axis=0)
    rank_ref[...] = jnp.concatenate(
        [jnp.sum(jnp.where(oh, ranks, 0.0), axis=0, keepdims=True).astype(I32) for _, _, oh in picks], axis=0)


def _post(x, attn, u_pool, y_ssm, gates, mod_l, wts, meta, tm):
    t = x.shape[0]
    nh = t // POOL_HALO
    hb = tm // POOL_HALO
    npf = 5
    const = lambda i, *_: (0, 0)
    row = lambda i, *_: (i, 0)
    col = lambda i, *_: (0, i)
    grid_spec = pltpu.PrefetchScalarGridSpec(
        num_scalar_prefetch=npf,
        grid=(t // tm,),
        in_specs=[
            pl.BlockSpec((tm, D_MODEL), row),
            pl.BlockSpec((tm, ATTN_DIM), row),
            pl.BlockSpec((POOL_HALO, POOL_DIM), lambda i, *_: (jnp.maximum(i * hb - 1, 0), 0)),
            pl.BlockSpec((tm, POOL_DIM), row),
            pl.BlockSpec((POOL_HALO, POOL_DIM), lambda i, *_: (jnp.minimum((i + 1) * hb, nh - 1), 0)),
            pl.BlockSpec((tm, SSM_DIM), row),
            pl.BlockSpec((tm, 3 * D_MODEL), row),
            pl.BlockSpec((1, N_ADA, D_MODEL), lambda i, ts, *_: (ts[i], 0, 0)),
            pl.BlockSpec((ATTN_DIM, D_MODEL), const),
            pl.BlockSpec((POOL_DIM, POOL_DIM), const),
            pl.BlockSpec((1, POOL_DIM), const),
            pl.BlockSpec((POOL_DIM, D_MODEL), const),
            pl.BlockSpec((SSM_DIM, 2 * D_MODEL), const),
            pl.BlockSpec((D_MODEL, D_MODEL), const),
            pl.BlockSpec((1, D_MODEL), const),
            pl.BlockSpec((2, N_EXPERTS, D_MODEL), lambda i, *_: (0, 0, 0)),
            pl.BlockSpec((N_EXPERTS, 1), const),
            pl.BlockSpec((tm, tm), const),
        ],
        out_specs=[
            pl.BlockSpec((tm, D_MODEL), row),
            pl.BlockSpec((tm * ROW_TILE, LANES), row),
            pl.BlockSpec((TOP_K, tm), col),
            pl.BlockSpec((TOP_K, tm), col),
            pl.BlockSpec((TOP_K, tm), col),
            pl.BlockSpec((N_EXPERTS, 1), const),
        ],
        scratch_shapes=[pltpu.VMEM((tm + 2 * POOL_HALO, POOL_DIM), F32), pltpu.VMEM((N_EXPERTS, 1), F32)],
    )
    return pl.pallas_call(
        functools.partial(_post_kernel, tm=tm),
        grid_spec=grid_spec,
        out_shape=[
            jax.ShapeDtypeStruct((t, D_MODEL), F32),
            jax.ShapeDtypeStruct((t * ROW_TILE, LANES), F32),
            jax.ShapeDtypeStruct((TOP_K, t), I32),
            jax.ShapeDtypeStruct((TOP_K, t), F32),
            jax.ShapeDtypeStruct((TOP_K, t), I32),
            jax.ShapeDtypeStruct((N_EXPERTS, 1), F32),
        ],
        compiler_params=_cparams(("arbitrary",)),
        name="mixer_merge_route",
    )(*meta, x, attn, u_pool, u_pool, u_pool, y_ssm, gates, mod_l, *wts)


def _dispatch_kernel(lb_ref, has_ref, d_ref, h_hbm, x_hbm, zero_sc, h_sc, sem_in, sem, zsem, *, tn, bm):
    i = pl.program_id(0)
    n = pl.num_programs(0)
    n_copies = tn * TOP_K
    tile_rows = tn * ROW_TILE

    def slot_rows(ref, slot):
        return ref.at[pl.ds(slot * ROW_TILE, ROW_TILE), :]

    def tile_load(tile, s):
        return pltpu.make_async_copy(h_hbm.at[pl.ds(tile * tile_rows, tile_rows), :], h_sc.at[s], sem_in.at[s])

    def pad_loop(act):
        def expert(e, c):
            @pl.when(has_ref[e] == 1)
            def _():
                act(pltpu.make_async_copy(zero_sc, x_hbm.at[pl.ds(lb_ref[e] * ROW_TILE, bm * ROW_TILE), :], zsem))
            return c
        lax.fori_loop(0, N_EXPERTS, expert, 0)

    @pl.when(i == 0)
    def _():
        zero_sc[...] = jnp.zeros(zero_sc.shape, F32)
        pad_loop(lambda cp: cp.start())
        pad_loop(lambda cp: cp.wait())
        tile_load(0, 0).start()

    cur = i % 3

    @pl.when(i + 1 < n)
    def _():
        tile_load(i + 1, (i + 1) % 3).start()

    tile_load(i, cur).wait()

    def body(j, c):
        for prio in range(DMA_PRIORITIES):
            a = j * DMA_PRIORITIES + prio
            pltpu.make_async_copy(slot_rows(h_sc.at[cur], a >> 3), slot_rows(x_hbm, d_ref[0, 0, a]),
                                  sem.at[cur]).start(priority=prio)
        return c
    lax.fori_loop(0, n_copies // DMA_PRIORITIES, body, 0, unroll=4)

    def wait_tile(s):
        whole = x_hbm.at[pl.ds(0, n_copies * ROW_TILE), :]
        pltpu.make_async_copy(whole, whole, sem.at[s]).wait()

    @pl.when(i > 0)
    def _():
        wait_tile((i + 2) % 3)

    @pl.when(i == n - 1)
    def _():
        wait_tile(cur)


def _dispatch(h2t, dest3, last_block, has_rows, n_rows, tn, bm):
    n = dest3.shape[0]
    grid_spec = pltpu.PrefetchScalarGridSpec(
        num_scalar_prefetch=2,
        grid=(n,),
        in_specs=[
            pl.BlockSpec((1, 1, tn * TOP_K), lambda i, lo, hi: (i, 0, 0), memory_space=pltpu.SMEM),
            pl.BlockSpec(memory_space=pl.ANY),
        ],
        out_specs=pl.BlockSpec(memory_space=pl.ANY),
        scratch_shapes=[pltpu.VMEM((bm * ROW_TILE, LANES), F32), pltpu.VMEM((3, tn * ROW_TILE, LANES), F32),
                        pltpu.SemaphoreType.DMA((3,)), pltpu.SemaphoreType.DMA((3,)),
                        pltpu.SemaphoreType.DMA(())],
    )
    return pl.pallas_call(
        functools.partial(_dispatch_kernel, tn=tn, bm=bm),
        grid_spec=grid_spec,
        out_shape=jax.ShapeDtypeStruct((n_rows * ROW_TILE, LANES), F32),
        compiler_params=_cparams(("arbitrary",)),
        name="moe_dispatch",
    )(last_block, has_rows, dest3, h2t)


def _expert_kernel(be_ref, nu_ref, x_ref, wg_ref, wu_ref, wd_ref, y_ref, wg_sc, wu_sc, wd_sc, *, bm):
    b = pl.program_id(0)
    n_used = nu_ref[0]

    @pl.when(b < n_used)
    def _():
        @pl.when((b == 0) | (be_ref[b] != be_ref[jnp.maximum(b - 1, 0)]))
        def _():
            wg_sc[...] = wg_ref[0, 0].astype(BF16)
            wu_sc[...] = wu_ref[0, 0].astype(BF16)
            wd_sc[...] = wd_ref[0, 0].astype(BF16)

        x = _load_row_tiles(x_ref, 0, bm).astype(BF16)
        g = jnp.dot(x, wg_sc[...], preferred_element_type=F32)
        u = jnp.dot(x, wu_sc[...], preferred_element_type=F32)
        hmid = (g * jax.nn.sigmoid(g) * u).astype(BF16)
        _store_row_tiles(y_ref, jnp.dot(hmid, wd_sc[...], preferred_element_type=F32))

    @pl.when(b >= n_used)
    def _():
        y_ref[...] = jnp.zeros(y_ref.shape, F32)


def _experts(x_buf, block_expert, n_used, w_gate, w_up, w_down, layer, bm):
    nb = x_buf.shape[0] // (bm * ROW_TILE)
    grid_spec = pltpu.PrefetchScalarGridSpec(
        num_scalar_prefetch=2,
        grid=(nb,),
        in_specs=[
            pl.BlockSpec((bm * ROW_TILE, LANES), lambda b, be, nu: (jnp.minimum(b, nu[0] - 1), 0)),
            pl.BlockSpec((1, 1, D_MODEL, D_EXPERT), lambda b, be, nu: (layer, be[b], 0, 0)),
            pl.BlockSpec((1, 1, D_MODEL, D_EXPERT), lambda b, be, nu: (layer, be[b], 0, 0)),
            pl.BlockSpec((1, 1, D_EXPERT, D_MODEL), lambda b, be, nu: (layer, be[b], 0, 0)),
        ],
        out_specs=pl.BlockSpec((bm * ROW_TILE, LANES), lambda b, be, nu: (b, 0)),
        scratch_shapes=[
            pltpu.VMEM((D_MODEL, D_EXPERT), BF16),
            pltpu.VMEM((D_MODEL, D_EXPERT), BF16),
            pltpu.VMEM((D_EXPERT, D_MODEL), BF16),
        ],
    )
    return pl.pallas_call(
        functools.partial(_expert_kernel, bm=bm),
        grid_spec=grid_spec,
        out_shape=jax.ShapeDtypeStruct((nb * bm * ROW_TILE, LANES), F32),
        compiler_params=_cparams(("arbitrary",)),
        name="routed_experts",
    )(block_expert, n_used, x_buf, w_gate, w_up, w_down)


def _combine_kernel(ts_ref, d_ref, dn_ref, y_hbm, x1_ref, h2_ref, gate_ref, mod_ref, wsg_ref, wsu_ref, wsd_ref,
                    o_ref, yg_sc, sem, *, tc):
    del ts_ref
    i = pl.program_id(0)
    n = pl.num_programs(0)
    slot = i % 2
    n_copies = tc * TOP_K

    def row_copy(src, s, row):
        return pltpu.make_async_copy(y_hbm.at[pl.ds(src * ROW_TILE, ROW_TILE), :],
                                     yg_sc.at[s, pl.ds(row * ROW_TILE, ROW_TILE), :], sem.at[s])

    def issue(dest_smem, s):
        def body(j, c):
            for prio in range(DMA_PRIORITIES):
                a = j * DMA_PRIORITIES + prio
                row_copy(dest_smem[0, 0, a], s, (a & (TOP_K - 1)) * tc + (a >> 3)).start(priority=prio)
            return c
        lax.fori_loop(0, n_copies // DMA_PRIORITIES, body, 0, unroll=4)

    @pl.when(i == 0)
    def _():
        issue(d_ref, 0)

    @pl.when(i + 1 < n)
    def _():
        issue(dn_ref, 1 - slot)

    pltpu.make_async_copy(y_hbm.at[pl.ds(0, n_copies * ROW_TILE), :], yg_sc.at[slot], sem.at[slot]).wait()

    gate = gate_ref[...]
    yg = yg_sc.at[slot]
    routed = gate[:, 0:1] * _load_row_tiles(yg, 0, tc)
    for k in range(1, TOP_K):
        routed = routed + gate[:, k:k + 1] * _load_row_tiles(yg, k * tc, tc)
    hb = _load_row_tiles(h2_ref, 0, tc).astype(BF16)
    g = jnp.dot(hb, wsg_ref[...], preferred_element_type=F32)
    u = jnp.dot(hb, wsu_ref[...], preferred_element_type=F32)
    shared = jnp.dot((g * jax.nn.sigmoid(g) * u).astype(BF16), wsd_ref[...], preferred_element_type=F32)
    o_ref[...] = x1_ref[...] + mod_ref[0][5:6] * (routed + shared)


def _combine(y_buf, dest3, x1, h2, gate, mod_l, ws_gate, ws_up, ws_down, tile_seq, tc):
    t = x1.shape[0]
    n = t // tc
    d_sh = ws_gate.shape[1]
    const = lambda i, ts: (0, 0)
    row = lambda i, ts: (i, 0)
    grid_spec = pltpu.PrefetchScalarGridSpec(
        num_scalar_prefetch=1,
        grid=(n,),
        in_specs=[
            pl.BlockSpec((1, 1, tc * TOP_K), lambda i, ts: (i, 0, 0), memory_space=pltpu.SMEM),
            pl.BlockSpec((1, 1, tc * TOP_K), lambda i, ts: (jnp.minimum(i + 1, n - 1), 0, 0),
                         memory_space=pltpu.SMEM),
            pl.BlockSpec(memory_space=pl.ANY),
            pl.BlockSpec((tc, D_MODEL), row),
            pl.BlockSpec((tc * ROW_TILE, LANES), row),
            pl.BlockSpec((tc, LANES), row),
            pl.BlockSpec((1, N_ADA, D_MODEL), lambda i, ts: (ts[i], 0, 0)),
            pl.BlockSpec((D_MODEL, d_sh), const),
            pl.BlockSpec((D_MODEL, d_sh), const),
            pl.BlockSpec((d_sh, D_MODEL), const),
        ],
        out_specs=pl.BlockSpec((tc, D_MODEL), row),
        scratch_shapes=[pltpu.VMEM((2, tc * TOP_K * ROW_TILE, LANES), F32), pltpu.SemaphoreType.DMA((2,))],
    )
    return pl.pallas_call(
        functools.partial(_combine_kernel, tc=tc),
        grid_spec=grid_spec,
        out_shape=jax.ShapeDtypeStruct((t, D_MODEL), F32),
        compiler_params=_cparams(("arbitrary",)),
        name="moe_combine",
    )(tile_seq, dest3, dest3, y_buf, x1, h2, gate, mod_l, ws_gate, ws_up, ws_down)


def _tile_meta(seq_lens, tile):
    seq, first, last, pos0, slen = [], [], [], [], []
    for s, n in enumerate(seq_lens):
        nt = n // tile
        for j in range(nt):
            seq.append(s)
            first.append(int(j == 0))
            last.append(int(j == nt - 1))
            pos0.append(j * tile)
            slen.append(n)
    return tuple(jnp.asarray(np.asarray(a, np.int32)) for a in (seq, first, last, pos0, slen))


def _scan_meta(seq_lens, tr):
    bwd, first = [], []
    base = 0
    for n in seq_lens:
        nt = n // SSM_CHUNK // tr
        for j in range(nt):
            bwd.append(base + nt - 1 - j)
            first.append(int(j == 0))
        base += nt
    return jnp.asarray(np.asarray(bwd, np.int32)), jnp.asarray(np.asarray(first, np.int32))


def _rope_tables(seq_lens):
    quarter = HEAD_DIM // 4
    freqs = ROPE_THETA ** (-jnp.arange(quarter, dtype=F32) / quarter)
    sign = jnp.tile(jnp.concatenate([-jnp.ones((quarter,), F32), jnp.ones((quarter,), F32)]), 2)
    cos_l, sin_l = [], []
    cache = {}
    for n in seq_lens:
        if n not in cache:
            pos = jnp.arange(n)
            ar = (pos // GRID_W).astype(F32)[:, None] * freqs
            ac = (pos % GRID_W).astype(F32)[:, None] * freqs
            ang = jnp.concatenate([ar, ar, ac, ac], axis=-1)
            cache[n] = (jnp.tile(jnp.cos(ang), (1, 2)), jnp.tile(jnp.sin(ang) * sign, (1, 2)))
        cos_l.append(cache[n][0])
        sin_l.append(cache[n][1])
    return jnp.concatenate(cos_l, axis=0), jnp.concatenate(sin_l, axis=0)


def _split_bf16(w):
    hi = w.astype(BF16)
    return jnp.stack([hi, (w - hi.astype(F32)).astype(BF16)], axis=0)


def _block_diag_ones(n, blk):
    r = np.arange(n) // blk
    return jnp.asarray((r[:, None] == r[None, :]).astype(np.float32)).astype(BF16)


def _dest_kernel(idx_ref, rank_ref, ps_ref, d_ref):
    idx = idx_ref[...]
    row_e = lax.broadcasted_iota(I32, (N_EXPERTS, idx.shape[1]), 0)
    starts = [jnp.sum(jnp.where(row_e == idx[k:k + 1], ps_ref[...], 0.0), axis=0, keepdims=True)
              for k in range(TOP_K)]
    d_ref[...] = rank_ref[...] + jnp.concatenate(starts, axis=0).astype(I32)


def _dest_slots(idx, rank, pstart):
    t = idx.shape[1]
    tm = _pick(t, 1024)
    col = lambda i: (0, i)
    return pl.pallas_call(
        _dest_kernel,
        grid=(t // tm,),
        in_specs=[pl.BlockSpec((TOP_K, tm), col), pl.BlockSpec((TOP_K, tm), col),
                  pl.BlockSpec((N_EXPERTS, 1), lambda i: (0, 0))],
        out_specs=pl.BlockSpec((TOP_K, tm), col),
        out_shape=jax.ShapeDtypeStruct((TOP_K, t), I32),
        compiler_params=_cparams(("arbitrary",)),
        name="moe_dest_slots",
    )(idx, rank, pstart)


def _moe_plan(idx, rank, counts, t, bm, nb):
    cnt = counts.reshape(N_EXPERTS).astype(I32)
    padded = (cnt + bm - 1) // bm * bm
    pend = jnp.cumsum(padded)
    pstart = pend - padded
    n_used = (pend[-1] // bm).astype(I32).reshape(1)
    block_expert = jnp.minimum(
        jnp.searchsorted(pend, jnp.arange(nb, dtype=I32) * bm, side='right'), N_EXPERTS - 1).astype(I32)
    dest = _dest_slots(idx, rank, pstart.astype(F32).reshape(N_EXPERTS, 1)).T
    return dest, (pend - bm).astype(I32), (cnt > 0).astype(I32), block_expert, n_used


def kernel(x_prompt, x_sample, c_prompt, c_sample, w_ada, b_ada, norm1_g, w_in, q_norm_g, k_norm_g, w_attn_o, pool_w, pool_scale, w_pool_o, ssm_a_re, ssm_a_im, ssm_log_dt, ssm_b_re, ssm_b_im, ssm_c_re, ssm_c_im, ssm_d, w_glu, w_out, norm2_g, w_router, b_router, w_exp_gate, w_exp_up, w_exp_down, w_sh_gate, w_sh_up, w_sh_down):
    b1, l1, d = x_prompt.shape
    b2, l2, _ = x_sample.shape
    depth = w_in.shape[0]
    assert d == D_MODEL and b1 + b2 <= MOD_ROWS
    seq_lens = [l1] * b1 + [l2] * b2
    t = b1 * l1 + b2 * l2
    lmin = min(l1, l2)

    tm_in = _pick(lmin, 512)
    tm_post = _pick(lmin, 256)
    tc_comb = _pick(lmin, 128)
    tr_scan = _pick(lmin // SSM_CHUNK, 256)
    bm = 256
    nb = -(-(t * TOP_K + N_EXPERTS * (bm - 1)) // bm)

    x = jnp.concatenate([x_prompt.reshape(b1 * l1, d), x_sample.reshape(b2 * l2, d)], axis=0)
    c_all = jnp.concatenate([c_prompt, c_sample, jnp.zeros((MOD_ROWS - b1 - b2, d), F32)], axis=0)
    mod = _modulation(c_all, w_ada, b_ada).reshape(depth, MOD_ROWS, N_ADA, d)

    cos_t, sin_t = _rope_tables(seq_lens)
    ones_bd = _block_diag_ones(ATTN_DIM, HEAD_DIM)
    triu = jnp.asarray(np.triu(np.ones((tm_post, tm_post), np.float32), 1)).astype(BF16)
    meta_in = _tile_meta(seq_lens, tm_in)
    meta_post = _tile_meta(seq_lens, tm_post)
    meta_comb = _tile_meta(seq_lens, tc_comb)
    bwd_tile, first_tile = _scan_meta(seq_lens, tr_scan)

    for l in range(depth):
        q8, k2, v2, u_pool, u_ssm, gates = _inproj(
            x, mod[l], norm1_g[l].reshape(1, d), w_in[l].astype(BF16),
            jnp.tile(q_norm_g[l], N_Q_HEADS).reshape(1, ATTN_DIM),
            jnp.tile(k_norm_g[l], N_KV_HEADS).reshape(1, KV_DIM),
            cos_t, sin_t, ones_bd, meta_in[0], tm_in)
        attn = jnp.concatenate(
            [_flash(q8, k2, v2, 0, b1, l1), _flash(q8, k2, v2, b1 * l1, b2, l2)], axis=0)
        tables = _ssm_tables(ssm_a_re[l], ssm_a_im[l], ssm_log_dt[l], ssm_b_re[l], ssm_b_im[l],
                             ssm_c_re[l], ssm_c_im[l], ssm_d[l])
        y_ssm = _bidir_ssm(u_ssm, tables, bwd_tile, first_tile, tr_scan)

        pool_bd = jax.scipy.linalg.block_diag(*[pool_w[l, g] for g in range(len(POOL_WINDOWS))])
        wts = (w_attn_o[l].astype(BF16), pool_bd.astype(BF16), pool_scale[l].reshape(1, POOL_DIM),
               w_pool_o[l].astype(BF16), w_glu[l].astype(BF16), w_out[l].astype(BF16),
               norm2_g[l].reshape(1, d), _split_bf16(w_router[l].T), b_router[l].reshape(N_EXPERTS, 1), triu)
        x1, h2, idx, gate, rank, counts = _post(x, attn, u_pool, y_ssm, gates, mod[l], wts, meta_post, tm_post)

        dest, last_block, has_rows, block_expert, n_used = _moe_plan(idx, rank, counts, t, bm, nb)
        dest3 = dest.reshape(t // tc_comb, 1, tc_comb * TOP_K)
        x_buf = _dispatch(h2, dest3, last_block, has_rows, nb * bm, tc_comb, bm)
        y_buf = _experts(x_buf, block_expert, n_used, w_exp_gate, w_exp_up, w_exp_down, l, bm)
        gate = jnp.pad(gate.T, ((0, 0), (0, LANES - TOP_K)))
        x = _combine(y_buf, dest3, x1, h2, gate, mod[l],
                     w_sh_gate[l].astype(BF16), w_sh_up[l].astype(BF16), w_sh_down[l].astype(BF16),
                     meta_comb[0], tc_comb)

    return (x[:b1 * l1].reshape(b1, l1, d), x[b1 * l1:].reshape(b2, l2, d))
```

```python
import functools
import math

import jax
import jax.numpy as jnp
import numpy as np
from jax import lax
from jax.experimental import pallas as pl
from jax.experimental.pallas import tpu as pltpu

F32 = jnp.float32
BF16 = jnp.bfloat16
I32 = jnp.int32
HIGHEST = lax.Precision.HIGHEST

D_MODEL = 1024
GRID_W = 64
HEAD_DIM = 64
N_Q_HEADS = 8
N_KV_HEADS = 2
Q_PER_KV = N_Q_HEADS // N_KV_HEADS
ATTN_DIM = N_Q_HEADS * HEAD_DIM
KV_DIM = N_KV_HEADS * HEAD_DIM
ROPE_THETA = 10000.0
POOL_WINDOWS = (2, 4, 8, 16)
POOL_DIM = 256
POOL_GROUP = 64
POOL_HALO = 8
SSM_DIM = 256
SSM_P = 16
SSM_G = 16
SSM_N = 64
SSM_CHUNK = 16
SSM_ROW = SSM_CHUNK * SSM_P
SSM_STATE = 2 * SSM_N
N_EXPERTS = 256
TOP_K = 8
N_EXPERT_GROUPS = 8
GROUP_SIZE = N_EXPERTS // N_EXPERT_GROUPS
TOPK_GROUPS = 4
D_EXPERT = 256
ROUTED_SCALE = 2.5
N_ADA = 6
EPS = 1e-6
IN_DIM = ATTN_DIM + 2 * KV_DIM + POOL_DIM + SSM_DIM + 3 * D_MODEL
OFF_K = ATTN_DIM
OFF_V = OFF_K + KV_DIM
OFF_POOL = OFF_V + KV_DIM
OFF_SSM = OFF_POOL + POOL_DIM
OFF_GATES = OFF_SSM + SSM_DIM
MOD_ROWS = 8
LOG2_E = math.log2(math.e)
VT_ROWS = HEAD_DIM + 16

V7X_VMEM_BYTES = 64 * 1024 * 1024
VMEM_LIMIT = V7X_VMEM_BYTES - 8 * 1024 * 1024
LANES = 128


def _cparams(sem):
    return pltpu.CompilerParams(dimension_semantics=sem, vmem_limit_bytes=VMEM_LIMIT)


def _pick(n, pref):
    t = min(n, pref)
    while n % t:
        t //= 2
    return t


ROW_TILE = D_MODEL // LANES
DMA_PRIORITIES = 2


def _store_row_tiles(ref, x):
    n = x.shape[0]
    for s in range(ROW_TILE):
        ref[pl.ds(s, n, stride=ROW_TILE), :] = x[:, s * LANES:(s + 1) * LANES]


def _load_row_tiles(ref, first_row, n):
    return jnp.concatenate(
        [ref[pl.ds(first_row * ROW_TILE + s, n, stride=ROW_TILE), :] for s in range(ROW_TILE)], axis=1)


def _mod_kernel(c_ref, w_ref, b_ref, o_ref):
    c = c_ref[...]
    a = c * jax.nn.sigmoid(c)
    o_ref[0] = jnp.dot(a, w_ref[0], preferred_element_type=F32, precision=HIGHEST) + b_ref[0]


def _modulation(c_all, w_ada, b_ada):
    depth, d, n = w_ada.shape
    bn = _pick(n, 1536)
    return pl.pallas_call(
        _mod_kernel,
        grid=(depth, n // bn),
        in_specs=[
            pl.BlockSpec((MOD_ROWS, d), lambda l, j: (0, 0)),
            pl.BlockSpec((1, d, bn), lambda l, j: (l, 0, j)),
            pl.BlockSpec((1, 1, bn), lambda l, j: (l, 0, j)),
        ],
        out_specs=pl.BlockSpec((1, MOD_ROWS, bn), lambda l, j: (l, 0, j)),
        out_shape=jax.ShapeDtypeStruct((depth, MOD_ROWS, n), F32),
        compiler_params=_cparams(("arbitrary", "arbitrary")),
        name="adaln_mod",
    )(c_all, w_ada, b_ada.reshape(depth, 1, n))


def _rmsnorm_mod(x, g, scale, shift):
    y = x * lax.rsqrt(jnp.mean(x * x, axis=-1, keepdims=True) + EPS)
    return (y * g) * (1.0 + scale) + shift


def _head_norm_rope(z, gain, cos, sin_signed, ones_bd):
    z2 = z * z
    hi = z2.astype(BF16)
    lo = (z2 - hi.astype(F32)).astype(BF16)
    ss = (jnp.dot(hi, ones_bd, preferred_element_type=F32)
          + jnp.dot(lo, ones_bd, preferred_element_type=F32))
    y = (z * lax.rsqrt(ss * (1.0 / HEAD_DIM) + EPS)) * gain
    w = z.shape[1]
    quarter = HEAD_DIM // 4
    from_right = pltpu.roll(y, w - quarter, 1)
    from_left = pltpu.roll(y, quarter, 1)
    lane = lax.broadcasted_iota(I32, y.shape, 1)
    rot = jnp.where((lane & quarter) == 0, from_right, from_left)
    return y * cos + rot * sin_signed


def _inproj_kernel(ts_ref, x_ref, mod_ref, g_ref, w_ref, gq_ref, gk_ref, cos_ref, sin_ref, ones_ref,
                   q_ref, k_ref, v_ref, up_ref, us_ref, gt_ref):
    del ts_ref
    x = x_ref[...]
    mod = mod_ref[0]
    h = _rmsnorm_mod(x, g_ref[...], mod[1:2], mod[0:1]).astype(BF16)

    cos = cos_ref[...]
    sin = sin_ref[...]
    zq = jnp.dot(h, w_ref[:, 0:ATTN_DIM], preferred_element_type=F32)
    reps = ATTN_DIM // LANES
    yq = _head_norm_rope(zq, gq_ref[...], jnp.concatenate([cos] * reps, axis=1),
                         jnp.concatenate([sin] * reps, axis=1), ones_ref[...]) * (HEAD_DIM ** -0.5 * LOG2_E)
    for hh in range(N_Q_HEADS):
        q_ref[hh] = yq[:, hh * HEAD_DIM:(hh + 1) * HEAD_DIM].astype(BF16)

    zkv = jnp.dot(h, w_ref[:, OFF_K:OFF_POOL], preferred_element_type=F32)
    yk = _head_norm_rope(zkv[:, :KV_DIM], gk_ref[...], cos, sin, ones_ref[0:KV_DIM, 0:KV_DIM])
    zv = zkv[:, KV_DIM:]
    for j in range(N_KV_HEADS):
        k_ref[j] = yk[:, j * HEAD_DIM:(j + 1) * HEAD_DIM].astype(BF16)
        v_ref[j, 0:HEAD_DIM, :] = zv[:, j * HEAD_DIM:(j + 1) * HEAD_DIM].T.astype(BF16)
        v_ref[j, HEAD_DIM:VT_ROWS, :] = jnp.ones((VT_ROWS - HEAD_DIM, zv.shape[0]), BF16)

    zps = jnp.dot(h, w_ref[:, OFF_POOL:OFF_GATES], preferred_element_type=F32)
    up_ref[...] = zps[:, :POOL_DIM]
    us_ref[...] = zps[:, POOL_DIM:]
    for c in range(3):
        zg = jnp.dot(h, w_ref[:, OFF_GATES + c * D_MODEL:OFF_GATES + (c + 1) * D_MODEL],
                     preferred_element_type=F32)
        gt_ref[:, c * D_MODEL:(c + 1) * D_MODEL] = jax.nn.sigmoid(zg).astype(BF16)


def _inproj(x, mod_l, g1, w_in_bf, gq_t, gk_t, cos_t, sin_t, ones_bd, tile_seq, tm):
    t = x.shape[0]
    const = lambda i, ts: (0, 0)
    row = lambda i, ts: (i, 0)
    grid_spec = pltpu.PrefetchScalarGridSpec(
        num_scalar_prefetch=1,
        grid=(t // tm,),
        in_specs=[
            pl.BlockSpec((tm, D_MODEL), row),
            pl.BlockSpec((1, N_ADA, D_MODEL), lambda i, ts: (ts[i], 0, 0)),
            pl.BlockSpec((1, D_MODEL), const),
            pl.BlockSpec((D_MODEL, IN_DIM), const),
            pl.BlockSpec((1, ATTN_DIM), const),
            pl.BlockSpec((1, KV_DIM), const),
            pl.BlockSpec((tm, LANES), row),
            pl.BlockSpec((tm, LANES), row),
            pl.BlockSpec((ATTN_DIM, ATTN_DIM), const),
        ],
        out_specs=[
            pl.BlockSpec((N_Q_HEADS, tm, HEAD_DIM), lambda i, ts: (0, i, 0)),
            pl.BlockSpec((N_KV_HEADS, tm, HEAD_DIM), lambda i, ts: (0, i, 0)),
            pl.BlockSpec((N_KV_HEADS, VT_ROWS, tm), lambda i, ts: (0, 0, i)),
            pl.BlockSpec((tm, POOL_DIM), row),
            pl.BlockSpec((tm, SSM_DIM), row),
            pl.BlockSpec((tm, 3 * D_MODEL), row),
        ],
    )
    return pl.pallas_call(
        _inproj_kernel,
        grid_spec=grid_spec,
        out_shape=[
            jax.ShapeDtypeStruct((N_Q_HEADS, t, HEAD_DIM), BF16),
            jax.ShapeDtypeStruct((N_KV_HEADS, t, HEAD_DIM), BF16),
            jax.ShapeDtypeStruct((N_KV_HEADS, VT_ROWS, t), BF16),
            jax.ShapeDtypeStruct((t, POOL_DIM), F32),
            jax.ShapeDtypeStruct((t, SSM_DIM), F32),
            jax.ShapeDtypeStruct((t, 3 * D_MODEL), BF16),
        ],
        compiler_params=_cparams(("arbitrary",)),
        name="norm1_inproj",
    )(tile_seq, x, mod_l, g1, w_in_bf, gq_t, gk_t, cos_t, sin_t, ones_bd)


def _flash_kernel(q_ref, k_ref, vt_ref, o_ref, m_sc, acc_sc):
    kb = pl.program_id(3)

    @pl.when(kb == 0)
    def _():
        m_sc[...] = jnp.full(m_sc.shape, -jnp.inf, F32)
        acc_sc[...] = jnp.zeros(acc_sc.shape, F32)

    k = k_ref[0]
    vt = vt_ref[0]
    heads = range(Q_PER_KV)
    qs = [q_ref[hh] for hh in heads]
    m_prev = [m_sc[hh] for hh in heads]
    acc_prev = [acc_sc[hh] for hh in heads]

    def scores(hh):
        return lax.dot_general(k, qs[hh], (((1,), (1,)), ((), ())), preferred_element_type=F32)

    ahead = 2
    s = {hh: scores(hh) for hh in range(ahead)}
    new = []
    for hh in heads:
        if hh + ahead < Q_PER_KV:
            s[hh + ahead] = scores(hh + ahead)
        m_new = jnp.maximum(m_prev[hh], jnp.max(s[hh], axis=0, keepdims=True))
        alpha = jnp.exp2(m_prev[hh] - m_new)
        p = jnp.exp2(s.pop(hh) - m_new).astype(BF16)
        acc_new = acc_prev[hh] * alpha + jnp.dot(vt, p, preferred_element_type=F32)
        new.append((m_new, acc_new))
    for hh, (m_new, acc_new) in enumerate(new):
        m_sc[hh] = m_new
        acc_sc[hh] = acc_new

    @pl.when(kb == pl.num_programs(3) - 1)
    def _():
        for hh in range(Q_PER_KV):
            acc = acc_sc[hh]
            o = acc[:HEAD_DIM] / acc[HEAD_DIM:HEAD_DIM + 1]
            o_ref[:, hh * HEAD_DIM:(hh + 1) * HEAD_DIM] = o.T.astype(BF16)


def _flash(q8, k2, vt2, off, bsz, seq):
    tq = _pick(seq, 512)
    tk = _pick(seq, 2048)
    nq, nk = seq // tq, seq // tk
    assert off % tq == 0 and off % tk == 0
    oq, ok = off // tq, off // tk
    return pl.pallas_call(
        _flash_kernel,
        grid=(N_KV_HEADS, bsz, nq, nk),
        in_specs=[
            pl.BlockSpec((Q_PER_KV, tq, HEAD_DIM), lambda j, b, i, kb: (j, oq + b * nq + i, 0)),
            pl.BlockSpec((1, tk, HEAD_DIM), lambda j, b, i, kb: (j, ok + b * nk + kb, 0)),
            pl.BlockSpec((1, VT_ROWS, tk), lambda j, b, i, kb: (j, 0, ok + b * nk + kb)),
        ],
        out_specs=pl.BlockSpec((tq, Q_PER_KV * HEAD_DIM), lambda j, b, i, kb: (b * nq + i, j)),
        out_shape=jax.ShapeDtypeStruct((bsz * seq, ATTN_DIM), BF16),
        scratch_shapes=[
            pltpu.VMEM((Q_PER_KV, 1, tq), F32),
            pltpu.VMEM((Q_PER_KV, VT_ROWS, tq), F32),
        ],
        compiler_params=_cparams(("arbitrary", "arbitrary", "arbitrary", "arbitrary")),
        name="flash_gqa",
    )(q8, k2, vt2)


def _ssm_tables(a_re, a_im, log_dt, b_re, b_im, c_re, c_im, d_skip):
    tc = SSM_CHUNK
    a = lax.complex(a_re.astype(F32), a_im.astype(F32))
    dt = jnp.exp(log_dt.astype(F32))[..., None]
    adt = a * dt
    a_bar = jnp.exp(adt)
    b_bar = ((a_bar - 1.0) / a)[..., None] * lax.complex(b_re.astype(F32), b_im.astype(F32))
    c_mat = lax.complex(c_re.astype(F32), c_im.astype(F32))
    taus = jnp.arange(tc + 1, dtype=F32)
    pw = jnp.exp(adt[None] * taus[:, None, None, None])

    kern = jnp.real(jnp.einsum('dgpn,tdgn,dgnq->dgtpq', c_mat, pw[:tc], b_bar))
    s_idx = jnp.arange(tc)[:, None]
    t_idx = jnp.arange(tc)[None, :]
    lag_f = jnp.clip(t_idx - s_idx, 0, tc - 1)
    lag_b = jnp.clip(s_idx - t_idx, 0, tc - 1)
    kf = jnp.where((t_idx >= s_idx)[None, :, :, None, None], kern[0][:, lag_f], 0.0)
    kb = jnp.where((s_idx >= t_idx)[None, :, :, None, None], kern[1][:, lag_b], 0.0)
    w_loc = (kf + kb).transpose(0, 1, 4, 2, 3).reshape(SSM_G, SSM_ROW, SSM_ROW)
    w_loc = w_loc + jnp.eye(SSM_ROW, dtype=F32)[None] * jnp.tile(
        d_skip.astype(F32).reshape(SSM_G, 1, SSM_P), (1, tc, 1)).reshape(SSM_G, 1, SSM_ROW)

    def ri(z, axis):
        return jnp.concatenate([jnp.real(z), jnp.imag(z)], axis=axis)

    pf = jnp.einsum('sgn,gnq->gsqn', pw[:tc][::-1, 0], b_bar[0])
    pb = jnp.einsum('sgn,gnq->gsqn', pw[:tc, 1], b_bar[1])
    p_st = jnp.concatenate([ri(pf, -1), ri(pb, -1)], axis=-1).reshape(SSM_G, SSM_ROW, 2 * SSM_STATE)
    qf = jnp.einsum('gpn,tgn->gntp', c_mat[0], pw[1:tc + 1, 0])
    qb = jnp.einsum('gpn,tgn->gntp', c_mat[1], pw[1:tc + 1, 1][::-1])
    q_f = jnp.concatenate([jnp.real(qf), -jnp.imag(qf)], axis=1).reshape(SSM_G, SSM_STATE, SSM_ROW)
    q_b = jnp.concatenate([jnp.real(qb), -jnp.imag(qb)], axis=1).reshape(SSM_G, SSM_STATE, SSM_ROW)
    a16 = pw[tc]
    a1 = jnp.concatenate([jnp.real(a16), jnp.real(a16)], axis=-1).reshape(2, SSM_G * SSM_STATE)
    a2 = jnp.concatenate([-jnp.imag(a16), jnp.imag(a16)], axis=-1).reshape(2, SSM_G * SSM_STATE)
    a_mul = jnp.stack([a1[0], a2[0], a1[1], a2[1]], axis=0)
    return w_loc.astype(BF16), p_st.astype(BF16), q_f.astype(BF16), q_b.astype(BF16), a_mul


def _ssm_state_kernel(u_ref, p_ref, s_ref):
    s = jnp.dot(u_ref[0].astype(BF16), p_ref[0], preferred_element_type=F32)
    s_ref[0] = s[:, :SSM_STATE]
    s_ref[1] = s[:, SSM_STATE:]


def _ssm_states(ug, p_st, tc):
    g, nc, _ = ug.shape
    return pl.pallas_call(
        _ssm_state_kernel,
        grid=(g, nc // tc),
        in_specs=[
            pl.BlockSpec((1, tc, SSM_ROW), lambda gi, i: (gi, i, 0)),
            pl.BlockSpec((1, SSM_ROW, 2 * SSM_STATE), lambda gi, i: (gi, 0, 0)),
        ],
        out_specs=pl.BlockSpec((2, tc, SSM_STATE), lambda gi, i: (0, i, gi)),
        out_shape=jax.ShapeDtypeStruct((2, nc, g * SSM_STATE), F32),
        compiler_params=_cparams(("arbitrary", "arbitrary")),
        name="ssm_chunk_states",
    )(ug, p_st)


def _swap_halves(x):
    parts = [pltpu.roll(x[:, g * LANES:(g + 1) * LANES], LANES // 2, 1) for g in range(x.shape[1] // LANES)]
    return jnp.concatenate(parts, axis=1)


def _ssm_scan_kernel(bt_ref, first_ref, sf_ref, sb_ref, am_ref, xf_ref, zb_ref, st_sc, sfw_sc, sbw_sc, *, tr):
    n = pl.program_id(0)
    del bt_ref

    @pl.when(first_ref[n] == 1)
    def _():
        st_sc[...] = jnp.zeros(st_sc.shape, F32)

    sfw_sc[...] = _swap_halves(sf_ref[0])
    sbw_sc[...] = _swap_halves(sb_ref[0])
    a1f, a2f, a1b, a2b = am_ref[0:1], am_ref[1:2], am_ref[2:3], am_ref[3:4]

    def body(r, carry):
        xf, xfw, zb, zbw = carry
        xf_ref[pl.ds(r, 1), :] = xf
        sf = sf_ref[0, pl.ds(r, 1), :]
        sfw = sfw_sc[pl.ds(r, 1), :]
        nxf = a1f * xf + a2f * xfw + sf
        nxfw = a1f * xfw - a2f * xf + sfw
        rb = tr - 1 - r
        zb_ref[pl.ds(rb, 1), :] = zb
        sb = sb_ref[0, pl.ds(rb, 1), :]
        sbw = sbw_sc[pl.ds(rb, 1), :]
        nzb = a1b * zb + a2b * zbw + sb
        nzbw = a1b * zbw - a2b * zb + sbw
        return nxf, nxfw, nzb, nzbw

    init = (st_sc[0:1], st_sc[1:2], st_sc[2:3], st_sc[3:4])
    xf, xfw, zb, zbw = lax.fori_loop(0, tr, body, init)
    st_sc[0:1] = xf
    st_sc[1:2] = xfw
    st_sc[2:3] = zb
    st_sc[3:4] = zbw


def _ssm_scan(s_st, a_mul, bwd_tile, first_tile, tr):
    _, nc, w = s_st.shape
    grid_spec = pltpu.PrefetchScalarGridSpec(
        num_scalar_prefetch=2,
        grid=(nc // tr,),
        in_specs=[
            pl.BlockSpec((1, tr, w), lambda n, bt, ft: (0, n, 0)),
            pl.BlockSpec((1, tr, w), lambda n, bt, ft: (1, bt[n], 0)),
            pl.BlockSpec((4, w), lambda n, bt, ft: (0, 0)),
        ],
        out_specs=[
            pl.BlockSpec((tr, w), lambda n, bt, ft: (n, 0)),
            pl.BlockSpec((tr, w), lambda n, bt, ft: (bt[n], 0)),
        ],
        scratch_shapes=[pltpu.VMEM((4, w), F32), pltpu.VMEM((tr, w), F32), pltpu.VMEM((tr, w), F32)],
    )
    return pl.pallas_call(
        functools.partial(_ssm_scan_kernel, tr=tr),
        grid_spec=grid_spec,
        out_shape=[jax.ShapeDtypeStruct((nc, w), F32), jax.ShapeDtypeStruct((nc, w), F32)],
        compiler_params=_cparams(("arbitrary",)),
        name="ssm_chunk_scan",
    )(bwd_tile, first_tile, s_st, s_st, a_mul)


def _ssm_out_kernel(u_ref, xf_ref, zb_ref, w_ref, qf_ref, qb_ref, y_ref):
    y = jnp.dot(u_ref[0].astype(BF16), w_ref[0], preferred_element_type=F32)
    y += jnp.dot(xf_ref[...].astype(BF16), qf_ref[0], preferred_element_type=F32)
    y += jnp.dot(zb_ref[...].astype(BF16), qb_ref[0], preferred_element_type=F32)
    y_ref[0] = y


def _ssm_outputs(ug, xf, zb, w_loc, q_f, q_b, tc):
    g, nc, _ = ug.shape
    return pl.pallas_call(
        _ssm_out_kernel,
        grid=(g, nc // tc),
        in_specs=[
            pl.BlockSpec((1, tc, SSM_ROW), lambda gi, i: (gi, i, 0)),
            pl.BlockSpec((tc, SSM_STATE), lambda gi, i: (i, gi)),
            pl.BlockSpec((tc, SSM_STATE), lambda gi, i: (i, gi)),
            pl.BlockSpec((1, SSM_ROW, SSM_ROW), lambda gi, i: (gi, 0, 0)),
            pl.BlockSpec((1, SSM_STATE, SSM_ROW), lambda gi, i: (gi, 0, 0)),
            pl.BlockSpec((1, SSM_STATE, SSM_ROW), lambda gi, i: (gi, 0, 0)),
        ],
        out_specs=pl.BlockSpec((1, tc, SSM_ROW), lambda gi, i: (gi, i, 0)),
        out_shape=jax.ShapeDtypeStruct((g, nc, SSM_ROW), F32),
        compiler_params=_cparams(("arbitrary", "arbitrary")),
        name="ssm_chunk_outputs",
    )(ug, xf, zb, w_loc, q_f, q_b)


def _bidir_ssm(u_ssm, tables, bwd_tile, first_tile, tr):
    w_loc, p_st, q_f, q_b, a_mul = tables
    t = u_ssm.shape[0]
    nc = t // SSM_CHUNK
    ug = u_ssm.reshape(nc, SSM_CHUNK, SSM_G, SSM_P).transpose(2, 0, 1, 3).reshape(SSM_G, nc, SSM_ROW)
    tc = _pick(nc, 512)
    s_st = _ssm_states(ug, p_st, tc)
    xf, zb = _ssm_scan(s_st, a_mul, bwd_tile, first_tile, tr)
    yg = _ssm_outputs(ug, xf, zb, w_loc, q_f, q_b, tc)
    return yg.reshape(SSM_G, nc, SSM_CHUNK, SSM_P).transpose(1, 2, 0, 3).reshape(t, SSM_DIM)


def _pool_mixer(ext_sc, u, pos, seq_len, tm):
    halves = []
    for half in range(2):
        sums = []
        for w in POOL_WINDOWS[2 * half:2 * half + 2]:
            lo = w // 2
            hi = w - lo - 1
            acc = ext_sc[pl.ds(POOL_HALO - lo, tm), half * LANES:(half + 1) * LANES]
            for d in range(-lo + 1, hi + 1):
                acc = acc + ext_sc[pl.ds(POOL_HALO + d, tm), half * LANES:(half + 1) * LANES]
            cnt = (jnp.minimum(pos + hi, seq_len - 1) - jnp.maximum(pos - lo, 0) + 1).astype(F32)
            sums.append(acc / cnt)
        lane = lax.broadcasted_iota(I32, (tm, LANES), 1)
        halves.append(jnp.where(lane < POOL_GROUP, sums[0], sums[1]))
    return jnp.concatenate(halves, axis=1) - u


def _route(h2, wrt_ref, brt_ref):
    tm = h2.shape[0]
    w_hi, w_lo = wrt_ref[0], wrt_ref[1]
    h_hi = h2.astype(BF16)
    h_lo = (h2 - h_hi.astype(F32)).astype(BF16)
    nt = (((1,), (1,)), ((), ()))
    logits = (lax.dot_general(w_hi, h_hi, nt, preferred_element_type=F32)
              + (lax.dot_general(w_hi, h_lo, nt, preferred_element_type=F32)
                 + lax.dot_general(w_lo, h_hi, nt, preferred_element_type=F32)))
    scores = jax.nn.sigmoid(logits)
    choice = scores + brt_ref[...]
    neg = jnp.float32(-jnp.inf)

    c3 = choice.reshape(N_EXPERT_GROUPS, GROUP_SIZE, tm)
    pos3 = lax.broadcasted_iota(I32, c3.shape, 1)
    m1 = jnp.max(c3, axis=1, keepdims=True)
    i1 = jnp.min(jnp.where(c3 == m1, pos3, GROUP_SIZE), axis=1, keepdims=True)
    m2 = jnp.max(jnp.where(pos3 == i1, neg, c3), axis=1, keepdims=True)
    gs = m1 + m2
    gid = lax.broadcasted_iota(I32, gs.shape, 0)
    ahead = jnp.zeros(gs.shape, I32)
    for o in range(N_EXPERT_GROUPS):
        go = gs[o:o + 1]
        ahead = ahead + ((go > gs) | ((go == gs) & (gid > o))).astype(I32)
    c = jnp.where(ahead < TOPK_GROUPS, c3, neg).reshape(N_EXPERTS, tm)

    row = lax.broadcasted_iota(I32, c.shape, 0)
    picks = []
    for _ in range(TOP_K):
        m = jnp.max(c, axis=0, keepdims=True)
        i = jnp.min(jnp.where(c == m, row, N_EXPERTS), axis=0, keepdims=True)
        oh = row == i
        w = jnp.sum(jnp.where(oh, scores, 0.0), axis=0, keepdims=True)
        c = jnp.where(oh, neg, c)
        picks.append((i, w, oh))
    wsum = picks[0][1]
    for _, w, _ in picks[1:]:
        wsum = wsum + w
    return [(i, w / wsum * ROUTED_SCALE, oh) for i, w, oh in picks]


def _post_kernel(ts_ref, tf_ref, tl_ref, tp_ref, tn_ref,
                 x_ref, at_ref, upp_ref, up_ref, upn_ref, ys_ref, gt_ref, mod_ref,
                 wao_ref, pbd_ref, psc_ref, wpo_ref, wgl_ref, wo_ref, g2_ref, wr_ref, br_ref, triu_ref,
                 x1_ref, h2_ref, idx_ref, gate_ref, rank_ref, cnt_ref, ext_sc, carry_sc, *, tm):
    del ts_ref
    i = pl.program_id(0)
    mod = mod_ref[0]

    attn = jnp.dot(at_ref[...], wao_ref[...], preferred_element_type=F32)

    u = up_ref[...]
    ext_sc[0:POOL_HALO] = jnp.where(tf_ref[i] == 1, 0.0, upp_ref[...])
    ext_sc[POOL_HALO:POOL_HALO + tm] = u
    ext_sc[POOL_HALO + tm:2 * POOL_HALO + tm] = jnp.where(tl_ref[i] == 1, 0.0, upn_ref[...])
    pos = tp_ref[i] + lax.broadcasted_iota(I32, (tm, LANES), 0)
    pm = _pool_mixer(ext_sc, u, pos, tn_ref[i], tm)
    pool = jnp.dot(pm.astype(BF16), pbd_ref[...], preferred_element_type=F32) * psc_ref[...]
    pool = jnp.dot(pool.astype(BF16), wpo_ref[...], preferred_element_type=F32)

    z = jnp.dot(jax.nn.gelu(ys_ref[...]).astype(BF16), wgl_ref[...], preferred_element_type=F32)
    ssm = z[:, :D_MODEL] * jax.nn.sigmoid(z[:, D_MODEL:])

    merged = (gt_ref[:, 0:D_MODEL].astype(F32) * attn
              + gt_ref[:, D_MODEL:2 * D_MODEL].astype(F32) * pool
              + gt_ref[:, 2 * D_MODEL:3 * D_MODEL].astype(F32) * ssm)
    mix = jnp.dot(merged.astype(BF16), wo_ref[...], preferred_element_type=F32)
    x1 = x_ref[...] + mod[2:3] * mix
    x1_ref[...] = x1
    h2 = _rmsnorm_mod(x1, g2_ref[...], mod[4:5], mod[3:4])
    _store_row_tiles(h2_ref, h2)

    picks = _route(h2, wr_ref, br_ref)

    @pl.when(i == 0)
    def _():
        carry_sc[...] = jnp.zeros(carry_sc.shape, F32)

    sel = picks[0][2]
    for _, _, oh in picks[1:]:
        sel = sel | oh
    sel_f = sel.astype(F32)
    ranks = jnp.dot(sel_f.astype(BF16), triu_ref[...], preferred_element_type=F32) + carry_sc[...]
    carry = carry_sc[...] + jnp.sum(sel_f, axis=1, keepdims=True)
    carry_sc[...] = carry
    cnt_ref[...] = carry

    idx_ref[...] = jnp.concatenate([ik for ik, _, _ in picks], axis=0)
    gate_ref[...] = jnp.concatenate([gk for _, gk, _ in picks], axis=0)
    rank_ref[...] = jnp.concatenate(
        [jnp.sum(jnp.where(oh, ranks, 0.0), axis=0, keepdims=True).astype(I32) for _, _, oh in picks], axis=0)


def _post(x, attn, u_pool, y_ssm, gates, mod_l, wts, meta, tm):
    t = x.shape[0]
    nh = t // POOL_HALO
    hb = tm // POOL_HALO
    npf = 5
    const = lambda i, *_: (0, 0)
    row = lambda i, *_: (i, 0)
    col = lambda i, *_: (0, i)
    grid_spec = pltpu.PrefetchScalarGridSpec(
        num_scalar_prefetch=npf,
        grid=(t // tm,),
        in_specs=[
            pl.BlockSpec((tm, D_MODEL), row),
            pl.BlockSpec((tm, ATTN_DIM), row),
            pl.BlockSpec((POOL_HALO, POOL_DIM), lambda i, *_: (jnp.maximum(i * hb - 1, 0), 0)),
            pl.BlockSpec((tm, POOL_DIM), row),
            pl.BlockSpec((POOL_HALO, POOL_DIM), lambda i, *_: (jnp.minimum((i + 1) * hb, nh - 1), 0)),
            pl.BlockSpec((tm, SSM_DIM), row),
            pl.BlockSpec((tm, 3 * D_MODEL), row),
            pl.BlockSpec((1, N_ADA, D_MODEL), lambda i, ts, *_: (ts[i], 0, 0)),
            pl.BlockSpec((ATTN_DIM, D_MODEL), const),
            pl.BlockSpec((POOL_DIM, POOL_DIM), const),
            pl.BlockSpec((1, POOL_DIM), const),
            pl.BlockSpec((POOL_DIM, D_MODEL), const),
            pl.BlockSpec((SSM_DIM, 2 * D_MODEL), const),
            pl.BlockSpec((D_MODEL, D_MODEL), const),
            pl.BlockSpec((1, D_MODEL), const),
            pl.BlockSpec((2, N_EXPERTS, D_MODEL), lambda i, *_: (0, 0, 0)),
            pl.BlockSpec((N_EXPERTS, 1), const),
            pl.BlockSpec((tm, tm), const),
        ],
        out_specs=[
            pl.BlockSpec((tm, D_MODEL), row),
            pl.BlockSpec((tm * ROW_TILE, LANES), row),
            pl.BlockSpec((TOP_K, tm), col),
            pl.BlockSpec((TOP_K, tm), col),
            pl.BlockSpec((TOP_K, tm), col),
            pl.BlockSpec((N_EXPERTS, 1), const),
        ],
        scratch_shapes=[pltpu.VMEM((tm + 2 * POOL_HALO, POOL_DIM), F32), pltpu.VMEM((N_EXPERTS, 1), F32)],
    )
    return pl.pallas_call(
        functools.partial(_post_kernel, tm=tm),
        grid_spec=grid_spec,
        out_shape=[
            jax.ShapeDtypeStruct((t, D_MODEL), F32),
            jax.ShapeDtypeStruct((t * ROW_TILE, LANES), F32),
            jax.ShapeDtypeStruct((TOP_K, t), I32),
            jax.ShapeDtypeStruct((TOP_K, t), F32),
            jax.ShapeDtypeStruct((TOP_K, t), I32),
            jax.ShapeDtypeStruct((N_EXPERTS, 1), F32),
        ],
        compiler_params=_cparams(("arbitrary",)),
        name="mixer_merge_route",
    )(*meta, x, attn, u_pool, u_pool, u_pool, y_ssm, gates, mod_l, *wts)


def _dispatch_kernel(lb_ref, has_ref, d_ref, h_hbm, x_hbm, zero_sc, h_sc, sem_in, sem, zsem, *, tn, bm):
    i = pl.program_id(0)
    n = pl.num_programs(0)
    n_copies = tn * TOP_K
    tile_rows = tn * ROW_TILE

    def slot_rows(ref, slot):
        return ref.at[pl.ds(slot * ROW_TILE, ROW_TILE), :]

    def tile_load(tile, s):
        return pltpu.make_async_copy(h_hbm.at[pl.ds(tile * tile_rows, tile_rows), :], h_sc.at[s], sem_in.at[s])

    def pad_loop(act):
        def expert(e, c):
            @pl.when(has_ref[e] == 1)
            def _():
                act(pltpu.make_async_copy(zero_sc, x_hbm.at[pl.ds(lb_ref[e] * ROW_TILE, bm * ROW_TILE), :], zsem))
            return c
        lax.fori_loop(0, N_EXPERTS, expert, 0)

    @pl.when(i == 0)
    def _():
        zero_sc[...] = jnp.zeros(zero_sc.shape, F32)
        pad_loop(lambda cp: cp.start())
        pad_loop(lambda cp: cp.wait())
        tile_load(0, 0).start()

    cur = i % 3

    @pl.when(i + 1 < n)
    def _():
        tile_load(i + 1, (i + 1) % 3).start()

    tile_load(i, cur).wait()

    def body(tok, c):
        src = slot_rows(h_sc.at[cur], tok)
        for k in range(TOP_K):
            pltpu.make_async_copy(src, slot_rows(x_hbm, d_ref[0, 0, tok * TOP_K + k]),
                                  sem.at[cur]).start(priority=k % DMA_PRIORITIES)
        return c
    lax.fori_loop(0, tn, body, 0, unroll=2)

    def wait_tile(s):
        whole = x_hbm.at[pl.ds(0, n_copies * ROW_TILE), :]
        pltpu.make_async_copy(whole, whole, sem.at[s]).wait()

    @pl.when(i > 0)
    def _():
        wait_tile((i + 2) % 3)

    @pl.when(i == n - 1)
    def _():
        wait_tile(cur)


def _dispatch(h2t, dest3, last_block, has_rows, n_rows, tn, bm):
    n = dest3.shape[0]
    grid_spec = pltpu.PrefetchScalarGridSpec(
        num_scalar_prefetch=2,
        grid=(n,),
        in_specs=[
            pl.BlockSpec((1, 1, tn * TOP_K), lambda i, lo, hi: (i, 0, 0), memory_space=pltpu.SMEM),
            pl.BlockSpec(memory_space=pl.ANY),
        ],
        out_specs=pl.BlockSpec(memory_space=pl.ANY),
        scratch_shapes=[pltpu.VMEM((bm * ROW_TILE, LANES), F32), pltpu.VMEM((3, tn * ROW_TILE, LANES), F32),
                        pltpu.SemaphoreType.DMA((3,)), pltpu.SemaphoreType.DMA((3,)),
                        pltpu.SemaphoreType.DMA(())],
    )
    return pl.pallas_call(
        functools.partial(_dispatch_kernel, tn=tn, bm=bm),
        grid_spec=grid_spec,
        out_shape=jax.ShapeDtypeStruct((n_rows * ROW_TILE, LANES), F32),
        compiler_params=_cparams(("arbitrary",)),
        name="moe_dispatch",
    )(last_block, has_rows, dest3, h2t)


def _expert_kernel(be_ref, nu_ref, x_ref, wg_ref, wu_ref, wd_ref, y_ref, wg_sc, wu_sc, wd_sc, *, bm):
    b = pl.program_id(0)
    n_used = nu_ref[0]

    @pl.when(b < n_used)
    def _():
        @pl.when((b == 0) | (be_ref[b] != be_ref[jnp.maximum(b - 1, 0)]))
        def _():
            wg_sc[...] = wg_ref[0, 0].astype(BF16)
            wu_sc[...] = wu_ref[0, 0].astype(BF16)
            wd_sc[...] = wd_ref[0, 0].astype(BF16)

        x = _load_row_tiles(x_ref, 0, bm).astype(BF16)
        g = jnp.dot(x, wg_sc[...], preferred_element_type=F32)
        u = jnp.dot(x, wu_sc[...], preferred_element_type=F32)
        hmid = (g * jax.nn.sigmoid(g) * u).astype(BF16)
        _store_row_tiles(y_ref, jnp.dot(hmid, wd_sc[...], preferred_element_type=F32))

    @pl.when(b >= n_used)
    def _():
        y_ref[...] = jnp.zeros(y_ref.shape, F32)


def _experts(x_buf, block_expert, n_used, w_gate, w_up, w_down, layer, bm):
    nb = x_buf.shape[0] // (bm * ROW_TILE)
    grid_spec = pltpu.PrefetchScalarGridSpec(
        num_scalar_prefetch=2,
        grid=(nb,),
        in_specs=[
            pl.BlockSpec((bm * ROW_TILE, LANES), lambda b, be, nu: (jnp.minimum(b, nu[0] - 1), 0)),
            pl.BlockSpec((1, 1, D_MODEL, D_EXPERT), lambda b, be, nu: (layer, be[b], 0, 0)),
            pl.BlockSpec((1, 1, D_MODEL, D_EXPERT), lambda b, be, nu: (layer, be[b], 0, 0)),
            pl.BlockSpec((1, 1, D_EXPERT, D_MODEL), lambda b, be, nu: (layer, be[b], 0, 0)),
        ],
        out_specs=pl.BlockSpec((bm * ROW_TILE, LANES), lambda b, be, nu: (b, 0)),
        scratch_shapes=[
            pltpu.VMEM((D_MODEL, D_EXPERT), BF16),
            pltpu.VMEM((D_MODEL, D_EXPERT), BF16),
            pltpu.VMEM((D_EXPERT, D_MODEL), BF16),
        ],
    )
    return pl.pallas_call(
        functools.partial(_expert_kernel, bm=bm),
        grid_spec=grid_spec,
        out_shape=jax.ShapeDtypeStruct((nb * bm * ROW_TILE, LANES), F32),
        compiler_params=_cparams(("arbitrary",)),
        name="routed_experts",
    )(block_expert, n_used, x_buf, w_gate, w_up, w_down)


def _combine_kernel(ts_ref, d_ref, dn_ref, y_hbm, x1_ref, h2_ref, gate_ref, mod_ref, wsg_ref, wsu_ref, wsd_ref,
                    o_ref, yg_sc, sem, *, tc):
    del ts_ref
    i = pl.program_id(0)
    n = pl.num_programs(0)
    slot = i % 2
    n_copies = tc * TOP_K

    def row_copy(src, s, row):
        return pltpu.make_async_copy(y_hbm.at[pl.ds(src * ROW_TILE, ROW_TILE), :],
                                     yg_sc.at[s, pl.ds(row * ROW_TILE, ROW_TILE), :], sem.at[s])

    def issue(dest_smem, s):
        def body(tok, c):
            for k in range(TOP_K):
                row_copy(dest_smem[0, 0, tok * TOP_K + k], s, k * tc + tok).start(priority=k % DMA_PRIORITIES)
            return c
        lax.fori_loop(0, tc, body, 0, unroll=2)

    @pl.when(i == 0)
    def _():
        issue(d_ref, 0)

    @pl.when(i + 1 < n)
    def _():
        issue(dn_ref, 1 - slot)

    pltpu.make_async_copy(y_hbm.at[pl.ds(0, n_copies * ROW_TILE), :], yg_sc.at[slot], sem.at[slot]).wait()

    gate = gate_ref[...]
    yg = yg_sc.at[slot]
    routed = gate[:, 0:1] * _load_row_tiles(yg, 0, tc)
    for k in range(1, TOP_K):
        routed = routed + gate[:, k:k + 1] * _load_row_tiles(yg, k * tc, tc)
    hb = _load_row_tiles(h2_ref, 0, tc).astype(BF16)
    g = jnp.dot(hb, wsg_ref[...], preferred_element_type=F32)
    u = jnp.dot(hb, wsu_ref[...], preferred_element_type=F32)
    shared = jnp.dot((g * jax.nn.sigmoid(g) * u).astype(BF16), wsd_ref[...], preferred_element_type=F32)
    o_ref[...] = x1_ref[...] + mod_ref[0][5:6] * (routed + shared)


def _combine(y_buf, dest3, x1, h2, gate, mod_l, ws_gate, ws_up, ws_down, tile_seq, tc):
    t = x1.shape[0]
    n = t // tc
    d_sh = ws_gate.shape[1]
    const = lambda i, ts: (0, 0)
    row = lambda i, ts: (i, 0)
    grid_spec = pltpu.PrefetchScalarGridSpec(
        num_scalar_prefetch=1,
        grid=(n,),
        in_specs=[
            pl.BlockSpec((1, 1, tc * TOP_K), lambda i, ts: (i, 0, 0), memory_space=pltpu.SMEM),
            pl.BlockSpec((1, 1, tc * TOP_K), lambda i, ts: (jnp.minimum(i + 1, n - 1), 0, 0),
                         memory_space=pltpu.SMEM),
            pl.BlockSpec(memory_space=pl.ANY),
            pl.BlockSpec((tc, D_MODEL), row),
            pl.BlockSpec((tc * ROW_TILE, LANES), row),
            pl.BlockSpec((tc, LANES), row),
            pl.BlockSpec((1, N_ADA, D_MODEL), lambda i, ts: (ts[i], 0, 0)),
            pl.BlockSpec((D_MODEL, d_sh), const),
            pl.BlockSpec((D_MODEL, d_sh), const),
            pl.BlockSpec((d_sh, D_MODEL), const),
        ],
        out_specs=pl.BlockSpec((tc, D_MODEL), row),
        scratch_shapes=[pltpu.VMEM((2, tc * TOP_K * ROW_TILE, LANES), F32), pltpu.SemaphoreType.DMA((2,))],
    )
    return pl.pallas_call(
        functools.partial(_combine_kernel, tc=tc),
        grid_spec=grid_spec,
        out_shape=jax.ShapeDtypeStruct((t, D_MODEL), F32),
        compiler_params=_cparams(("arbitrary",)),
        name="moe_combine",
    )(tile_seq, dest3, dest3, y_buf, x1, h2, gate, mod_l, ws_gate, ws_up, ws_down)


def _tile_meta(seq_lens, tile):
    seq, first, last, pos0, slen = [], [], [], [], []
    for s, n in enumerate(seq_lens):
        nt = n // tile
        for j in range(nt):
            seq.append(s)
            first.append(int(j == 0))
            last.append(int(j == nt - 1))
            pos0.append(j * tile)
            slen.append(n)
    return tuple(jnp.asarray(np.asarray(a, np.int32)) for a in (seq, first, last, pos0, slen))


def _scan_meta(seq_lens, tr):
    bwd, first = [], []
    base = 0
    for n in seq_lens:
        nt = n // SSM_CHUNK // tr
        for j in range(nt):
            bwd.append(base + nt - 1 - j)
            first.append(int(j == 0))
        base += nt
    return jnp.asarray(np.asarray(bwd, np.int32)), jnp.asarray(np.asarray(first, np.int32))


def _rope_tables(seq_lens):
    quarter = HEAD_DIM // 4
    freqs = ROPE_THETA ** (-jnp.arange(quarter, dtype=F32) / quarter)
    sign = jnp.tile(jnp.concatenate([-jnp.ones((quarter,), F32), jnp.ones((quarter,), F32)]), 2)
    cos_l, sin_l = [], []
    cache = {}
    for n in seq_lens:
        if n not in cache:
            pos = jnp.arange(n)
            ar = (pos // GRID_W).astype(F32)[:, None] * freqs
            ac = (pos % GRID_W).astype(F32)[:, None] * freqs
            ang = jnp.concatenate([ar, ar, ac, ac], axis=-1)
            cache[n] = (jnp.tile(jnp.cos(ang), (1, 2)), jnp.tile(jnp.sin(ang) * sign, (1, 2)))
        cos_l.append(cache[n][0])
        sin_l.append(cache[n][1])
    return jnp.concatenate(cos_l, axis=0), jnp.concatenate(sin_l, axis=0)


def _split_bf16(w):
    hi = w.astype(BF16)
    return jnp.stack([hi, (w - hi.astype(F32)).astype(BF16)], axis=0)


def _block_diag_ones(n, blk):
    r = np.arange(n) // blk
    return jnp.asarray((r[:, None] == r[None, :]).astype(np.float32)).astype(BF16)


def _dest_kernel(idx_ref, rank_ref, ps_ref, d_ref):
    idx = idx_ref[...]
    row_e = lax.broadcasted_iota(I32, (N_EXPERTS, idx.shape[1]), 0)
    starts = [jnp.sum(jnp.where(row_e == idx[k:k + 1], ps_ref[...], 0.0), axis=0, keepdims=True)
              for k in range(TOP_K)]
    d_ref[...] = rank_ref[...] + jnp.concatenate(starts, axis=0).astype(I32)


def _dest_slots(idx, rank, pstart):
    t = idx.shape[1]
    tm = _pick(t, 1024)
    col = lambda i: (0, i)
    return pl.pallas_call(
        _dest_kernel,
        grid=(t // tm,),
        in_specs=[pl.BlockSpec((TOP_K, tm), col), pl.BlockSpec((TOP_K, tm), col),
                  pl.BlockSpec((N_EXPERTS, 1), lambda i: (0, 0))],
        out_specs=pl.BlockSpec((TOP_K, tm), col),
        out_shape=jax.ShapeDtypeStruct((TOP_K, t), I32),
        compiler_params=_cparams(("arbitrary",)),
        name="moe_dest_slots",
    )(idx, rank, pstart)


def _moe_plan(idx, rank, counts, t, bm, nb):
    cnt = counts.reshape(N_EXPERTS).astype(I32)
    padded = (cnt + bm - 1) // bm * bm
    pend = jnp.cumsum(padded)
    pstart = pend - padded
    n_used = (pend[-1] // bm).astype(I32).reshape(1)
    block_expert = jnp.minimum(
        jnp.searchsorted(pend, jnp.arange(nb, dtype=I32) * bm, side='right'), N_EXPERTS - 1).astype(I32)
    dest = _dest_slots(idx, rank, pstart.astype(F32).reshape(N_EXPERTS, 1)).T
    return dest, (pend - bm).astype(I32), (cnt > 0).astype(I32), block_expert, n_used


def kernel(x_prompt, x_sample, c_prompt, c_sample, w_ada, b_ada, norm1_g, w_in, q_norm_g, k_norm_g, w_attn_o, pool_w, pool_scale, w_pool_o, ssm_a_re, ssm_a_im, ssm_log_dt, ssm_b_re, ssm_b_im, ssm_c_re, ssm_c_im, ssm_d, w_glu, w_out, norm2_g, w_router, b_router, w_exp_gate, w_exp_up, w_exp_down, w_sh_gate, w_sh_up, w_sh_down):
    b1, l1, d = x_prompt.shape
    b2, l2, _ = x_sample.shape
    depth = w_in.shape[0]
    assert d == D_MODEL and b1 + b2 <= MOD_ROWS
    seq_lens = [l1] * b1 + [l2] * b2
    t = b1 * l1 + b2 * l2
    lmin = min(l1, l2)

    tm_in = _pick(lmin, 512)
    tm_post = _pick(lmin, 256)
    tc_comb = _pick(lmin, 128)
    tr_scan = _pick(lmin // SSM_CHUNK, 256)
    bm = 256
    nb = -(-(t * TOP_K + N_EXPERTS * (bm - 1)) // bm)

    x = jnp.concatenate([x_prompt.reshape(b1 * l1, d), x_sample.reshape(b2 * l2, d)], axis=0)
    c_all = jnp.concatenate([c_prompt, c_sample, jnp.zeros((MOD_ROWS - b1 - b2, d), F32)], axis=0)
    mod = _modulation(c_all, w_ada, b_ada).reshape(depth, MOD_ROWS, N_ADA, d)

    cos_t, sin_t = _rope_tables(seq_lens)
    ones_bd = _block_diag_ones(ATTN_DIM, HEAD_DIM)
    triu = jnp.asarray(np.triu(np.ones((tm_post, tm_post), np.float32), 1)).astype(BF16)
    meta_in = _tile_meta(seq_lens, tm_in)
    meta_post = _tile_meta(seq_lens, tm_post)
    meta_comb = _tile_meta(seq_lens, tc_comb)
    bwd_tile, first_tile = _scan_meta(seq_lens, tr_scan)

    for l in range(depth):
        q8, k2, v2, u_pool, u_ssm, gates = _inproj(
            x, mod[l], norm1_g[l].reshape(1, d), w_in[l].astype(BF16),
            jnp.tile(q_norm_g[l], N_Q_HEADS).reshape(1, ATTN_DIM),
            jnp.tile(k_norm_g[l], N_KV_HEADS).reshape(1, KV_DIM),
            cos_t, sin_t, ones_bd, meta_in[0], tm_in)
        attn = jnp.concatenate(
            [_flash(q8, k2, v2, 0, b1, l1), _flash(q8, k2, v2, b1 * l1, b2, l2)], axis=0)
        tables = _ssm_tables(ssm_a_re[l], ssm_a_im[l], ssm_log_dt[l], ssm_b_re[l], ssm_b_im[l],
                             ssm_c_re[l], ssm_c_im[l], ssm_d[l])
        y_ssm = _bidir_ssm(u_ssm, tables, bwd_tile, first_tile, tr_scan)

        pool_bd = jax.scipy.linalg.block_diag(*[pool_w[l, g] for g in range(len(POOL_WINDOWS))])
        wts = (w_attn_o[l].astype(BF16), pool_bd.astype(BF16), pool_scale[l].reshape(1, POOL_DIM),
               w_pool_o[l].astype(BF16), w_glu[l].astype(BF16), w_out[l].astype(BF16),
               norm2_g[l].reshape(1, d), _split_bf16(w_router[l].T), b_router[l].reshape(N_EXPERTS, 1), triu)
        x1, h2, idx, gate, rank, counts = _post(x, attn, u_pool, y_ssm, gates, mod[l], wts, meta_post, tm_post)

        dest, last_block, has_rows, block_expert, n_used = _moe_plan(idx, rank, counts, t, bm, nb)
        dest3 = dest.reshape(t // tc_comb, 1, tc_comb * TOP_K)
        x_buf = _dispatch(h2, dest3, last_block, has_rows, nb * bm, tc_comb, bm)
        y_buf = _experts(x_buf, block_expert, n_used, w_exp_gate, w_exp_up, w_exp_down, l, bm)
        gate = jnp.pad(gate.T, ((0, 0), (0, LANES - TOP_K)))
        x = _combine(y_buf, dest3, x1, h2, gate, mod[l],
                     w_sh_gate[l].astype(BF16), w_sh_up[l].astype(BF16), w_sh_down[l].astype(BF16),
                     meta_comb[0], tc_comb)

    return (x[:b1 * l1].reshape(b1, l1, d), x[b1 * l1:].reshape(b2, l2, d))
```

```python
import functools
import math

import jax
import jax.numpy as jnp
import numpy as np
from jax import lax
from jax.experimental import pallas as pl
from jax.experimental.pallas import tpu as pltpu

F32 = jnp.float32
BF16 = jnp.bfloat16
I32 = jnp.int32
HIGHEST = lax.Precision.HIGHEST

D_MODEL = 1024
GRID_W = 64
HEAD_DIM = 64
N_Q_HEADS = 8
N_KV_HEADS = 2
Q_PER_KV = N_Q_HEADS // N_KV_HEADS
ATTN_DIM = N_Q_HEADS * HEAD_DIM
KV_DIM = N_KV_HEADS * HEAD_DIM
ROPE_THETA = 10000.0
POOL_WINDOWS = (2, 4, 8, 16)
POOL_DIM = 256
POOL_GROUP = 64
POOL_HALO = 8
SSM_DIM = 256
SSM_P = 16
SSM_G = 16
SSM_N = 64
SSM_CHUNK = 16
SSM_ROW = SSM_CHUNK * SSM_P
SSM_STATE = 2 * SSM_N
N_EXPERTS = 256
TOP_K = 8
N_EXPERT_GROUPS = 8
GROUP_SIZE = N_EXPERTS // N_EXPERT_GROUPS
TOPK_GROUPS = 4
D_EXPERT = 256
ROUTED_SCALE = 2.5
N_ADA = 6
EPS = 1e-6
IN_DIM = ATTN_DIM + 2 * KV_DIM + POOL_DIM + SSM_DIM + 3 * D_MODEL
OFF_K = ATTN_DIM
OFF_V = OFF_K + KV_DIM
OFF_POOL = OFF_V + KV_DIM
OFF_SSM = OFF_POOL + POOL_DIM
OFF_GATES = OFF_SSM + SSM_DIM
MOD_ROWS = 8
LOG2_E = math.log2(math.e)
VT_ROWS = HEAD_DIM + 16

V7X_VMEM_BYTES = 64 * 1024 * 1024
VMEM_LIMIT = V7X_VMEM_BYTES - 8 * 1024 * 1024
LANES = 128


def _cparams(sem):
    return pltpu.CompilerParams(dimension_semantics=sem, vmem_limit_bytes=VMEM_LIMIT)


def _pick(n, pref):
    t = min(n, pref)
    while n % t:
        t //= 2
    return t


U32 = jnp.uint32
ROW_TILE = D_MODEL // 2 // LANES
DMA_PRIORITIES = 2


def _store_row_tiles(ref, x):
    n = x.shape[0]
    half = D_MODEL // 2
    lo = pltpu.bitcast(x[:, :half].astype(BF16).astype(F32), U32)
    hi = pltpu.bitcast(x[:, half:].astype(BF16).astype(F32), U32)
    w = (lo >> 16) | (hi & jnp.uint32(0xFFFF0000))
    for s in range(ROW_TILE):
        ref[pl.ds(s, n, stride=ROW_TILE), :] = w[:, s * LANES:(s + 1) * LANES]


def _load_row_tiles(ref, first_row, n):
    w = jnp.concatenate(
        [ref[pl.ds(first_row * ROW_TILE + s, n, stride=ROW_TILE), :] for s in range(ROW_TILE)], axis=1)
    lo = pltpu.bitcast(w << 16, F32)
    hi = pltpu.bitcast(w & jnp.uint32(0xFFFF0000), F32)
    return jnp.concatenate([lo, hi], axis=1)


def _mod_kernel(c_ref, w_ref, b_ref, o_ref):
    c = c_ref[...]
    a = c * jax.nn.sigmoid(c)
    o_ref[0] = jnp.dot(a, w_ref[0], preferred_element_type=F32, precision=HIGHEST) + b_ref[0]


def _modulation(c_all, w_ada, b_ada):
    depth, d, n = w_ada.shape
    bn = _pick(n, 1536)
    return pl.pallas_call(
        _mod_kernel,
        grid=(depth, n // bn),
        in_specs=[
            pl.BlockSpec((MOD_ROWS, d), lambda l, j: (0, 0)),
            pl.BlockSpec((1, d, bn), lambda l, j: (l, 0, j)),
            pl.BlockSpec((1, 1, bn), lambda l, j: (l, 0, j)),
        ],
        out_specs=pl.BlockSpec((1, MOD_ROWS, bn), lambda l, j: (l, 0, j)),
        out_shape=jax.ShapeDtypeStruct((depth, MOD_ROWS, n), F32),
        compiler_params=_cparams(("arbitrary", "arbitrary")),
        name="adaln_mod",
    )(c_all, w_ada, b_ada.reshape(depth, 1, n))


def _rmsnorm_mod(x, g, scale, shift):
    y = x * lax.rsqrt(jnp.mean(x * x, axis=-1, keepdims=True) + EPS)
    return (y * g) * (1.0 + scale) + shift


def _head_norm_rope(z, gain, cos, sin_signed, ones_bd):
    z2 = z * z
    hi = z2.astype(BF16)
    lo = (z2 - hi.astype(F32)).astype(BF16)
    ss = (jnp.dot(hi, ones_bd, preferred_element_type=F32)
          + jnp.dot(lo, ones_bd, preferred_element_type=F32))
    y = (z * lax.rsqrt(ss * (1.0 / HEAD_DIM) + EPS)) * gain
    w = z.shape[1]
    quarter = HEAD_DIM // 4
    from_right = pltpu.roll(y, w - quarter, 1)
    from_left = pltpu.roll(y, quarter, 1)
    lane = lax.broadcasted_iota(I32, y.shape, 1)
    rot = jnp.where((lane & quarter) == 0, from_right, from_left)
    return y * cos + rot * sin_signed


def _inproj_kernel(ts_ref, x_ref, mod_ref, g_ref, w_ref, gq_ref, gk_ref, cos_ref, sin_ref, ones_ref,
                   q_ref, k_ref, v_ref, up_ref, us_ref, gt_ref):
    del ts_ref
    x = x_ref[...]
    mod = mod_ref[0]
    h = _rmsnorm_mod(x, g_ref[...], mod[1:2], mod[0:1]).astype(BF16)

    cos = cos_ref[...]
    sin = sin_ref[...]
    zq = jnp.dot(h, w_ref[:, 0:ATTN_DIM], preferred_element_type=F32)
    reps = ATTN_DIM // LANES
    yq = _head_norm_rope(zq, gq_ref[...], jnp.concatenate([cos] * reps, axis=1),
                         jnp.concatenate([sin] * reps, axis=1), ones_ref[...]) * (HEAD_DIM ** -0.5 * LOG2_E)
    for hh in range(N_Q_HEADS):
        q_ref[hh] = yq[:, hh * HEAD_DIM:(hh + 1) * HEAD_DIM].astype(BF16)

    zkv = jnp.dot(h, w_ref[:, OFF_K:OFF_POOL], preferred_element_type=F32)
    yk = _head_norm_rope(zkv[:, :KV_DIM], gk_ref[...], cos, sin, ones_ref[0:KV_DIM, 0:KV_DIM])
    zv = zkv[:, KV_DIM:]
    for j in range(N_KV_HEADS):
        k_ref[j] = yk[:, j * HEAD_DIM:(j + 1) * HEAD_DIM].astype(BF16)
        v_ref[j, 0:HEAD_DIM, :] = zv[:, j * HEAD_DIM:(j + 1) * HEAD_DIM].T.astype(BF16)
        v_ref[j, HEAD_DIM:VT_ROWS, :] = jnp.ones((VT_ROWS - HEAD_DIM, zv.shape[0]), BF16)

    zps = jnp.dot(h, w_ref[:, OFF_POOL:OFF_GATES], preferred_element_type=F32)
    up_ref[...] = zps[:, :POOL_DIM]
    us_ref[...] = zps[:, POOL_DIM:]
    for c in range(3):
        zg = jnp.dot(h, w_ref[:, OFF_GATES + c * D_MODEL:OFF_GATES + (c + 1) * D_MODEL],
                     preferred_element_type=F32)
        gt_ref[:, c * D_MODEL:(c + 1) * D_MODEL] = jax.nn.sigmoid(zg).astype(BF16)


def _inproj(x, mod_l, g1, w_in_bf, gq_t, gk_t, cos_t, sin_t, ones_bd, tile_seq, tm):
    t = x.shape[0]
    const = lambda i, ts: (0, 0)
    row = lambda i, ts: (i, 0)
    grid_spec = pltpu.PrefetchScalarGridSpec(
        num_scalar_prefetch=1,
        grid=(t // tm,),
        in_specs=[
            pl.BlockSpec((tm, D_MODEL), row),
            pl.BlockSpec((1, N_ADA, D_MODEL), lambda i, ts: (ts[i], 0, 0)),
            pl.BlockSpec((1, D_MODEL), const),
            pl.BlockSpec((D_MODEL, IN_DIM), const),
            pl.BlockSpec((1, ATTN_DIM), const),
            pl.BlockSpec((1, KV_DIM), const),
            pl.BlockSpec((tm, LANES), row),
            pl.BlockSpec((tm, LANES), row),
            pl.BlockSpec((ATTN_DIM, ATTN_DIM), const),
        ],
        out_specs=[
            pl.BlockSpec((N_Q_HEADS, tm, HEAD_DIM), lambda i, ts: (0, i, 0)),
            pl.BlockSpec((N_KV_HEADS, tm, HEAD_DIM), lambda i, ts: (0, i, 0)),
            pl.BlockSpec((N_KV_HEADS, VT_ROWS, tm), lambda i, ts: (0, 0, i)),
            pl.BlockSpec((tm, POOL_DIM), row),
            pl.BlockSpec((tm, SSM_DIM), row),
            pl.BlockSpec((tm, 3 * D_MODEL), row),
        ],
    )
    return pl.pallas_call(
        _inproj_kernel,
        grid_spec=grid_spec,
        out_shape=[
            jax.ShapeDtypeStruct((N_Q_HEADS, t, HEAD_DIM), BF16),
            jax.ShapeDtypeStruct((N_KV_HEADS, t, HEAD_DIM), BF16),
            jax.ShapeDtypeStruct((N_KV_HEADS, VT_ROWS, t), BF16),
            jax.ShapeDtypeStruct((t, POOL_DIM), F32),
            jax.ShapeDtypeStruct((t, SSM_DIM), F32),
            jax.ShapeDtypeStruct((t, 3 * D_MODEL), BF16),
        ],
        compiler_params=_cparams(("arbitrary",)),
        name="norm1_inproj",
    )(tile_seq, x, mod_l, g1, w_in_bf, gq_t, gk_t, cos_t, sin_t, ones_bd)


def _flash_kernel(q_ref, k_ref, vt_ref, o_ref, m_sc, acc_sc):
    kb = pl.program_id(3)

    @pl.when(kb == 0)
    def _():
        m_sc[...] = jnp.full(m_sc.shape, -jnp.inf, F32)
        acc_sc[...] = jnp.zeros(acc_sc.shape, F32)

    k = k_ref[0]
    vt = vt_ref[0]
    heads = range(Q_PER_KV)
    qs = [q_ref[hh] for hh in heads]
    m_prev = [m_sc[hh] for hh in heads]
    acc_prev = [acc_sc[hh] for hh in heads]

    def scores(hh):
        return lax.dot_general(k, qs[hh], (((1,), (1,)), ((), ())), preferred_element_type=F32)

    ahead = 2
    s = {hh: scores(hh) for hh in range(ahead)}
    new = []
    for hh in heads:
        if hh + ahead < Q_PER_KV:
            s[hh + ahead] = scores(hh + ahead)
        m_new = jnp.maximum(m_prev[hh], jnp.max(s[hh], axis=0, keepdims=True))
        alpha = jnp.exp2(m_prev[hh] - m_new)
        p = jnp.exp2(s.pop(hh) - m_new).astype(BF16)
        acc_new = acc_prev[hh] * alpha + jnp.dot(vt, p, preferred_element_type=F32)
        new.append((m_new, acc_new))
    for hh, (m_new, acc_new) in enumerate(new):
        m_sc[hh] = m_new
        acc_sc[hh] = acc_new

    @pl.when(kb == pl.num_programs(3) - 1)
    def _():
        for hh in range(Q_PER_KV):
            acc = acc_sc[hh]
            o = acc[:HEAD_DIM] / acc[HEAD_DIM:HEAD_DIM + 1]
            o_ref[:, hh * HEAD_DIM:(hh + 1) * HEAD_DIM] = o.T.astype(BF16)


def _flash(q8, k2, vt2, off, bsz, seq):
    tq = _pick(seq, 512)
    tk = _pick(seq, 2048)
    nq, nk = seq // tq, seq // tk
    assert off % tq == 0 and off % tk == 0
    oq, ok = off // tq, off // tk
    return pl.pallas_call(
        _flash_kernel,
        grid=(N_KV_HEADS, bsz, nq, nk),
        in_specs=[
            pl.BlockSpec((Q_PER_KV, tq, HEAD_DIM), lambda j, b, i, kb: (j, oq + b * nq + i, 0)),
            pl.BlockSpec((1, tk, HEAD_DIM), lambda j, b, i, kb: (j, ok + b * nk + kb, 0)),
            pl.BlockSpec((1, VT_ROWS, tk), lambda j, b, i, kb: (j, 0, ok + b * nk + kb)),
        ],
        out_specs=pl.BlockSpec((tq, Q_PER_KV * HEAD_DIM), lambda j, b, i, kb: (b * nq + i, j)),
        out_shape=jax.ShapeDtypeStruct((bsz * seq, ATTN_DIM), BF16),
        scratch_shapes=[
            pltpu.VMEM((Q_PER_KV, 1, tq), F32),
            pltpu.VMEM((Q_PER_KV, VT_ROWS, tq), F32),
        ],
        compiler_params=_cparams(("arbitrary", "arbitrary", "arbitrary", "arbitrary")),
        name="flash_gqa",
    )(q8, k2, vt2)


def _ssm_tables(a_re, a_im, log_dt, b_re, b_im, c_re, c_im, d_skip):
    tc = SSM_CHUNK
    a = lax.complex(a_re.astype(F32), a_im.astype(F32))
    dt = jnp.exp(log_dt.astype(F32))[..., None]
    adt = a * dt
    a_bar = jnp.exp(adt)
    b_bar = ((a_bar - 1.0) / a)[..., None] * lax.complex(b_re.astype(F32), b_im.astype(F32))
    c_mat = lax.complex(c_re.astype(F32), c_im.astype(F32))
    taus = jnp.arange(tc + 1, dtype=F32)
    pw = jnp.exp(adt[None] * taus[:, None, None, None])

    kern = jnp.real(jnp.einsum('dgpn,tdgn,dgnq->dgtpq', c_mat, pw[:tc], b_bar))
    s_idx = jnp.arange(tc)[:, None]
    t_idx = jnp.arange(tc)[None, :]
    lag_f = jnp.clip(t_idx - s_idx, 0, tc - 1)
    lag_b = jnp.clip(s_idx - t_idx, 0, tc - 1)
    kf = jnp.where((t_idx >= s_idx)[None, :, :, None, None], kern[0][:, lag_f], 0.0)
    kb = jnp.where((s_idx >= t_idx)[None, :, :, None, None], kern[1][:, lag_b], 0.0)
    w_loc = (kf + kb).transpose(0, 1, 4, 2, 3).reshape(SSM_G, SSM_ROW, SSM_ROW)
    w_loc = w_loc + jnp.eye(SSM_ROW, dtype=F32)[None] * jnp.tile(
        d_skip.astype(F32).reshape(SSM_G, 1, SSM_P), (1, tc, 1)).reshape(SSM_G, 1, SSM_ROW)

    def ri(z, axis):
        return jnp.concatenate([jnp.real(z), jnp.imag(z)], axis=axis)

    pf = jnp.einsum('sgn,gnq->gsqn', pw[:tc][::-1, 0], b_bar[0])
    pb = jnp.einsum('sgn,gnq->gsqn', pw[:tc, 1], b_bar[1])
    p_st = jnp.concatenate([ri(pf, -1), ri(pb, -1)], axis=-1).reshape(SSM_G, SSM_ROW, 2 * SSM_STATE)
    qf = jnp.einsum('gpn,tgn->gntp', c_mat[0], pw[1:tc + 1, 0])
    qb = jnp.einsum('gpn,tgn->gntp', c_mat[1], pw[1:tc + 1, 1][::-1])
    q_f = jnp.concatenate([jnp.real(qf), -jnp.imag(qf)], axis=1).reshape(SSM_G, SSM_STATE, SSM_ROW)
    q_b = jnp.concatenate([jnp.real(qb), -jnp.imag(qb)], axis=1).reshape(SSM_G, SSM_STATE, SSM_ROW)
    a16 = pw[tc]
    a1 = jnp.concatenate([jnp.real(a16), jnp.real(a16)], axis=-1).reshape(2, SSM_G * SSM_STATE)
    a2 = jnp.concatenate([-jnp.imag(a16), jnp.imag(a16)], axis=-1).reshape(2, SSM_G * SSM_STATE)
    a_mul = jnp.stack([a1[0], a2[0], a1[1], a2[1]], axis=0)
    return w_loc.astype(BF16), p_st.astype(BF16), q_f.astype(BF16), q_b.astype(BF16), a_mul


def _ssm_state_kernel(u_ref, p_ref, s_ref):
    s = jnp.dot(u_ref[0].astype(BF16), p_ref[0], preferred_element_type=F32)
    s_ref[0] = s[:, :SSM_STATE]
    s_ref[1] = s[:, SSM_STATE:]


def _ssm_states(ug, p_st, tc):
    g, nc, _ = ug.shape
    return pl.pallas_call(
        _ssm_state_kernel,
        grid=(g, nc // tc),
        in_specs=[
            pl.BlockSpec((1, tc, SSM_ROW), lambda gi, i: (gi, i, 0)),
            pl.BlockSpec((1, SSM_ROW, 2 * SSM_STATE), lambda gi, i: (gi, 0, 0)),
        ],
        out_specs=pl.BlockSpec((2, tc, SSM_STATE), lambda gi, i: (0, i, gi)),
        out_shape=jax.ShapeDtypeStruct((2, nc, g * SSM_STATE), F32),
        compiler_params=_cparams(("arbitrary", "arbitrary")),
        name="ssm_chunk_states",
    )(ug, p_st)


def _swap_halves(x):
    parts = [pltpu.roll(x[:, g * LANES:(g + 1) * LANES], LANES // 2, 1) for g in range(x.shape[1] // LANES)]
    return jnp.concatenate(parts, axis=1)


def _ssm_scan_kernel(bt_ref, first_ref, sf_ref, sb_ref, am_ref, xf_ref, zb_ref, st_sc, sfw_sc, sbw_sc, *, tr):
    n = pl.program_id(0)
    del bt_ref

    @pl.when(first_ref[n] == 1)
    def _():
        st_sc[...] = jnp.zeros(st_sc.shape, F32)

    sfw_sc[...] = _swap_halves(sf_ref[0])
    sbw_sc[...] = _swap_halves(sb_ref[0])
    a1f, a2f, a1b, a2b = am_ref[0:1], am_ref[1:2], am_ref[2:3], am_ref[3:4]

    def body(r, carry):
        xf, xfw, zb, zbw = carry
        xf_ref[pl.ds(r, 1), :] = xf
        sf = sf_ref[0, pl.ds(r, 1), :]
        sfw = sfw_sc[pl.ds(r, 1), :]
        nxf = a1f * xf + a2f * xfw + sf
        nxfw = a1f * xfw - a2f * xf + sfw
        rb = tr - 1 - r
        zb_ref[pl.ds(rb, 1), :] = zb
        sb = sb_ref[0, pl.ds(rb, 1), :]
        sbw = sbw_sc[pl.ds(rb, 1), :]
        nzb = a1b * zb + a2b * zbw + sb
        nzbw = a1b * zbw - a2b * zb + sbw
        return nxf, nxfw, nzb, nzbw

    init = (st_sc[0:1], st_sc[1:2], st_sc[2:3], st_sc[3:4])
    xf, xfw, zb, zbw = lax.fori_loop(0, tr, body, init)
    st_sc[0:1] = xf
    st_sc[1:2] = xfw
    st_sc[2:3] = zb
    st_sc[3:4] = zbw


def _ssm_scan(s_st, a_mul, bwd_tile, first_tile, tr):
    _, nc, w = s_st.shape
    grid_spec = pltpu.PrefetchScalarGridSpec(
        num_scalar_prefetch=2,
        grid=(nc // tr,),
        in_specs=[
            pl.BlockSpec((1, tr, w), lambda n, bt, ft: (0, n, 0)),
            pl.BlockSpec((1, tr, w), lambda n, bt, ft: (1, bt[n], 0)),
            pl.BlockSpec((4, w), lambda n, bt, ft: (0, 0)),
        ],
        out_specs=[
            pl.BlockSpec((tr, w), lambda n, bt, ft: (n, 0)),
            pl.BlockSpec((tr, w), lambda n, bt, ft: (bt[n], 0)),
        ],
        scratch_shapes=[pltpu.VMEM((4, w), F32), pltpu.VMEM((tr, w), F32), pltpu.VMEM((tr, w), F32)],
    )
    return pl.pallas_call(
        functools.partial(_ssm_scan_kernel, tr=tr),
        grid_spec=grid_spec,
        out_shape=[jax.ShapeDtypeStruct((nc, w), F32), jax.ShapeDtypeStruct((nc, w), F32)],
        compiler_params=_cparams(("arbitrary",)),
        name="ssm_chunk_scan",
    )(bwd_tile, first_tile, s_st, s_st, a_mul)


def _ssm_out_kernel(u_ref, xf_ref, zb_ref, w_ref, qf_ref, qb_ref, y_ref):
    y = jnp.dot(u_ref[0].astype(BF16), w_ref[0], preferred_element_type=F32)
    y += jnp.dot(xf_ref[...].astype(BF16), qf_ref[0], preferred_element_type=F32)
    y += jnp.dot(zb_ref[...].astype(BF16), qb_ref[0], preferred_element_type=F32)
    y_ref[0] = y


def _ssm_outputs(ug, xf, zb, w_loc, q_f, q_b, tc):
    g, nc, _ = ug.shape
    return pl.pallas_call(
        _ssm_out_kernel,
        grid=(g, nc // tc),
        in_specs=[
            pl.BlockSpec((1, tc, SSM_ROW), lambda gi, i: (gi, i, 0)),
            pl.BlockSpec((tc, SSM_STATE), lambda gi, i: (i, gi)),
            pl.BlockSpec((tc, SSM_STATE), lambda gi, i: (i, gi)),
            pl.BlockSpec((1, SSM_ROW, SSM_ROW), lambda gi, i: (gi, 0, 0)),
            pl.BlockSpec((1, SSM_STATE, SSM_ROW), lambda gi, i: (gi, 0, 0)),
            pl.BlockSpec((1, SSM_STATE, SSM_ROW), lambda gi, i: (gi, 0, 0)),
        ],
        out_specs=pl.BlockSpec((1, tc, SSM_ROW), lambda gi, i: (gi, i, 0)),
        out_shape=jax.ShapeDtypeStruct((g, nc, SSM_ROW), F32),
        compiler_params=_cparams(("arbitrary", "arbitrary")),
        name="ssm_chunk_outputs",
    )(ug, xf, zb, w_loc, q_f, q_b)


def _bidir_ssm(u_ssm, tables, bwd_tile, first_tile, tr):
    w_loc, p_st, q_f, q_b, a_mul = tables
    t = u_ssm.shape[0]
    nc = t // SSM_CHUNK
    ug = u_ssm.reshape(nc, SSM_CHUNK, SSM_G, SSM_P).transpose(2, 0, 1, 3).reshape(SSM_G, nc, SSM_ROW)
    tc = _pick(nc, 512)
    s_st = _ssm_states(ug, p_st, tc)
    xf, zb = _ssm_scan(s_st, a_mul, bwd_tile, first_tile, tr)
    yg = _ssm_outputs(ug, xf, zb, w_loc, q_f, q_b, tc)
    return yg.reshape(SSM_G, nc, SSM_CHUNK, SSM_P).transpose(1, 2, 0, 3).reshape(t, SSM_DIM)


def _pool_mixer(ext_sc, u, pos, seq_len, tm):
    halves = []
    for half in range(2):
        sums = []
        for w in POOL_WINDOWS[2 * half:2 * half + 2]:
            lo = w // 2
            hi = w - lo - 1
            acc = ext_sc[pl.ds(POOL_HALO - lo, tm), half * LANES:(half + 1) * LANES]
            for d in range(-lo + 1, hi + 1):
                acc = acc + ext_sc[pl.ds(POOL_HALO + d, tm), half * LANES:(half + 1) * LANES]
            cnt = (jnp.minimum(pos + hi, seq_len - 1) - jnp.maximum(pos - lo, 0) + 1).astype(F32)
            sums.append(acc / cnt)
        lane = lax.broadcasted_iota(I32, (tm, LANES), 1)
        halves.append(jnp.where(lane < POOL_GROUP, sums[0], sums[1]))
    return jnp.concatenate(halves, axis=1) - u


def _route(h2, wrt_ref, brt_ref):
    tm = h2.shape[0]
    w_hi, w_lo = wrt_ref[0], wrt_ref[1]
    h_hi = h2.astype(BF16)
    h_lo = (h2 - h_hi.astype(F32)).astype(BF16)
    nt = (((1,), (1,)), ((), ()))
    logits = (lax.dot_general(w_hi, h_hi, nt, preferred_element_type=F32)
              + (lax.dot_general(w_hi, h_lo, nt, preferred_element_type=F32)
                 + lax.dot_general(w_lo, h_hi, nt, preferred_element_type=F32)))
    scores = jax.nn.sigmoid(logits)
    choice = scores + brt_ref[...]
    neg = jnp.float32(-jnp.inf)

    c3 = choice.reshape(N_EXPERT_GROUPS, GROUP_SIZE, tm)
    pos3 = lax.broadcasted_iota(I32, c3.shape, 1)
    m1 = jnp.max(c3, axis=1, keepdims=True)
    i1 = jnp.min(jnp.where(c3 == m1, pos3, GROUP_SIZE), axis=1, keepdims=True)
    m2 = jnp.max(jnp.where(pos3 == i1, neg, c3), axis=1, keepdims=True)
    gs = m1 + m2
    gid = lax.broadcasted_iota(I32, gs.shape, 0)
    ahead = jnp.zeros(gs.shape, I32)
    for o in range(N_EXPERT_GROUPS):
        go = gs[o:o + 1]
        ahead = ahead + ((go > gs) | ((go == gs) & (gid > o))).astype(I32)
    c = jnp.where(ahead < TOPK_GROUPS, c3, neg).reshape(N_EXPERTS, tm)

    row = lax.broadcasted_iota(I32, c.shape, 0)
    picks = []
    for _ in range(TOP_K):
        m = jnp.max(c, axis=0, keepdims=True)
        i = jnp.min(jnp.where(c == m, row, N_EXPERTS), axis=0, keepdims=True)
        oh = row == i
        w = jnp.sum(jnp.where(oh, scores, 0.0), axis=0, keepdims=True)
        c = jnp.where(oh, neg, c)
        picks.append((i, w, oh))
    wsum = picks[0][1]
    for _, w, _ in picks[1:]:
        wsum = wsum + w
    return [(i, w / wsum * ROUTED_SCALE, oh) for i, w, oh in picks]


def _post_kernel(ts_ref, tf_ref, tl_ref, tp_ref, tn_ref,
                 x_ref, at_ref, upp_ref, up_ref, upn_ref, ys_ref, gt_ref, mod_ref,
                 wao_ref, pbd_ref, psc_ref, wpo_ref, wgl_ref, wo_ref, g2_ref, wr_ref, br_ref, triu_ref,
                 x1_ref, h2_ref, idx_ref, gate_ref, rank_ref, cnt_ref, ext_sc, carry_sc, *, tm):
    del ts_ref
    i = pl.program_id(0)
    mod = mod_ref[0]

    attn = jnp.dot(at_ref[...], wao_ref[...], preferred_element_type=F32)

    u = up_ref[...]
    ext_sc[0:POOL_HALO] = jnp.where(tf_ref[i] == 1, 0.0, upp_ref[...])
    ext_sc[POOL_HALO:POOL_HALO + tm] = u
    ext_sc[POOL_HALO + tm:2 * POOL_HALO + tm] = jnp.where(tl_ref[i] == 1, 0.0, upn_ref[...])
    pos = tp_ref[i] + lax.broadcasted_iota(I32, (tm, LANES), 0)
    pm = _pool_mixer(ext_sc, u, pos, tn_ref[i], tm)
    pool = jnp.dot(pm.astype(BF16), pbd_ref[...], preferred_element_type=F32) * psc_ref[...]
    pool = jnp.dot(pool.astype(BF16), wpo_ref[...], preferred_element_type=F32)

    z = jnp.dot(jax.nn.gelu(ys_ref[...]).astype(BF16), wgl_ref[...], preferred_element_type=F32)
    ssm = z[:, :D_MODEL] * jax.nn.sigmoid(z[:, D_MODEL:])

    merged = (gt_ref[:, 0:D_MODEL].astype(F32) * attn
              + gt_ref[:, D_MODEL:2 * D_MODEL].astype(F32) * pool
              + gt_ref[:, 2 * D_MODEL:3 * D_MODEL].astype(F32) * ssm)
    mix = jnp.dot(merged.astype(BF16), wo_ref[...], preferred_element_type=F32)
    x1 = x_ref[...] + mod[2:3] * mix
    x1_ref[...] = x1
    h2 = _rmsnorm_mod(x1, g2_ref[...], mod[4:5], mod[3:4])
    _store_row_tiles(h2_ref, h2)

    picks = _route(h2, wr_ref, br_ref)

    @pl.when(i == 0)
    def _():
        carry_sc[...] = jnp.zeros(carry_sc.shape, F32)

    sel = picks[0][2]
    for _, _, oh in picks[1:]:
        sel = sel | oh
    sel_f = sel.astype(F32)
    ranks = jnp.dot(sel_f.astype(BF16), triu_ref[...], preferred_element_type=F32) + carry_sc[...]
    carry = carry_sc[...] + jnp.sum(sel_f, axis=1, keepdims=True)
    carry_sc[...] = carry
    cnt_ref[...] = carry

    idx_ref[...] = jnp.concatenate([ik for ik, _, _ in picks], axis=0)
    gate_ref[...] = jnp.concatenate([gk for _, gk, _ in picks], axis=0)
    rank_ref[...] = jnp.concatenate(
        [jnp.sum(jnp.where(oh, ranks, 0.0), axis=0, keepdims=True).astype(I32) for _, _, oh in picks], axis=0)


def _post(x, attn, u_pool, y_ssm, gates, mod_l, wts, meta, tm):
    t = x.shape[0]
    nh = t // POOL_HALO
    hb = tm // POOL_HALO
    npf = 5
    const = lambda i, *_: (0, 0)
    row = lambda i, *_: (i, 0)
    col = lambda i, *_: (0, i)
    grid_spec = pltpu.PrefetchScalarGridSpec(
        num_scalar_prefetch=npf,
        grid=(t // tm,),
        in_specs=[
            pl.BlockSpec((tm, D_MODEL), row),
            pl.BlockSpec((tm, ATTN_DIM), row),
            pl.BlockSpec((POOL_HALO, POOL_DIM), lambda i, *_: (jnp.maximum(i * hb - 1, 0), 0)),
            pl.BlockSpec((tm, POOL_DIM), row),
            pl.BlockSpec((POOL_HALO, POOL_DIM), lambda i, *_: (jnp.minimum((i + 1) * hb, nh - 1), 0)),
            pl.BlockSpec((tm, SSM_DIM), row),
            pl.BlockSpec((tm, 3 * D_MODEL), row),
            pl.BlockSpec((1, N_ADA, D_MODEL), lambda i, ts, *_: (ts[i], 0, 0)),
            pl.BlockSpec((ATTN_DIM, D_MODEL), const),
            pl.BlockSpec((POOL_DIM, POOL_DIM), const),
            pl.BlockSpec((1, POOL_DIM), const),
            pl.BlockSpec((POOL_DIM, D_MODEL), const),
            pl.BlockSpec((SSM_DIM, 2 * D_MODEL), const),
            pl.BlockSpec((D_MODEL, D_MODEL), const),
            pl.BlockSpec((1, D_MODEL), const),
            pl.BlockSpec((2, N_EXPERTS, D_MODEL), lambda i, *_: (0, 0, 0)),
            pl.BlockSpec((N_EXPERTS, 1), const),
            pl.BlockSpec((tm, tm), const),
        ],
        out_specs=[
            pl.BlockSpec((tm, D_MODEL), row),
            pl.BlockSpec((tm * ROW_TILE, LANES), row),
            pl.BlockSpec((TOP_K, tm), col),
            pl.BlockSpec((TOP_K, tm), col),
            pl.BlockSpec((TOP_K, tm), col),
            pl.BlockSpec((N_EXPERTS, 1), const),
        ],
        scratch_shapes=[pltpu.VMEM((tm + 2 * POOL_HALO, POOL_DIM), F32), pltpu.VMEM((N_EXPERTS, 1), F32)],
    )
    return pl.pallas_call(
        functools.partial(_post_kernel, tm=tm),
        grid_spec=grid_spec,
        out_shape=[
            jax.ShapeDtypeStruct((t, D_MODEL), F32),
            jax.ShapeDtypeStruct((t * ROW_TILE, LANES), U32),
            jax.ShapeDtypeStruct((TOP_K, t), I32),
            jax.ShapeDtypeStruct((TOP_K, t), F32),
            jax.ShapeDtypeStruct((TOP_K, t), I32),
            jax.ShapeDtypeStruct((N_EXPERTS, 1), F32),
        ],
        compiler_params=_cparams(("arbitrary",)),
        name="mixer_merge_route",
    )(*meta, x, attn, u_pool, u_pool, u_pool, y_ssm, gates, mod_l, *wts)


def _dispatch_kernel(lb_ref, has_ref, d_ref, h_hbm, x_hbm, zero_sc, h_sc, sem_in, sem, zsem, *, tn, bm):
    i = pl.program_id(0)
    n = pl.num_programs(0)
    n_copies = tn * TOP_K
    tile_rows = tn * ROW_TILE

    def slot_rows(ref, slot):
        return ref.at[pl.ds(slot * ROW_TILE, ROW_TILE), :]

    def tile_load(tile, s):
        return pltpu.make_async_copy(h_hbm.at[pl.ds(tile * tile_rows, tile_rows), :], h_sc.at[s], sem_in.at[s])

    def pad_loop(act):
        def expert(e, c):
            @pl.when(has_ref[e] == 1)
            def _():
                act(pltpu.make_async_copy(zero_sc, x_hbm.at[pl.ds(lb_ref[e] * ROW_TILE, bm * ROW_TILE), :], zsem))
            return c
        lax.fori_loop(0, N_EXPERTS, expert, 0)

    @pl.when(i == 0)
    def _():
        zero_sc[...] = jnp.zeros(zero_sc.shape, U32)
        pad_loop(lambda cp: cp.start())
        pad_loop(lambda cp: cp.wait())
        tile_load(0, 0).start()

    cur = i % 3

    @pl.when(i + 1 < n)
    def _():
        tile_load(i + 1, (i + 1) % 3).start()

    tile_load(i, cur).wait()

    def body(tok, c):
        src = slot_rows(h_sc.at[cur], tok)
        for k in range(TOP_K):
            pltpu.make_async_copy(src, slot_rows(x_hbm, d_ref[0, 0, tok * TOP_K + k]),
                                  sem.at[cur]).start(priority=k % DMA_PRIORITIES)
        return c
    lax.fori_loop(0, tn, body, 0, unroll=2)

    def wait_tile(s):
        whole = x_hbm.at[pl.ds(0, n_copies * ROW_TILE), :]
        pltpu.make_async_copy(whole, whole, sem.at[s]).wait()

    @pl.when(i > 0)
    def _():
        wait_tile((i + 2) % 3)

    @pl.when(i == n - 1)
    def _():
        wait_tile(cur)


def _dispatch(h2t, dest3, last_block, has_rows, n_rows, tn, bm):
    n = dest3.shape[0]
    grid_spec = pltpu.PrefetchScalarGridSpec(
        num_scalar_prefetch=2,
        grid=(n,),
        in_specs=[
            pl.BlockSpec((1, 1, tn * TOP_K), lambda i, lo, hi: (i, 0, 0), memory_space=pltpu.SMEM),
            pl.BlockSpec(memory_space=pl.ANY),
        ],
        out_specs=pl.BlockSpec(memory_space=pl.ANY),
        scratch_shapes=[pltpu.VMEM((bm * ROW_TILE, LANES), U32), pltpu.VMEM((3, tn * ROW_TILE, LANES), U32),
                        pltpu.SemaphoreType.DMA((3,)), pltpu.SemaphoreType.DMA((3,)),
                        pltpu.SemaphoreType.DMA(())],
    )
    return pl.pallas_call(
        functools.partial(_dispatch_kernel, tn=tn, bm=bm),
        grid_spec=grid_spec,
        out_shape=jax.ShapeDtypeStruct((n_rows * ROW_TILE, LANES), U32),
        compiler_params=_cparams(("arbitrary",)),
        name="moe_dispatch",
    )(last_block, has_rows, dest3, h2t)


def _expert_kernel(be_ref, nu_ref, x_ref, wg_ref, wu_ref, wd_ref, y_ref, wg_sc, wu_sc, wd_sc, *, bm):
    b = pl.program_id(0)
    n_used = nu_ref[0]

    @pl.when(b < n_used)
    def _():
        @pl.when((b == 0) | (be_ref[b] != be_ref[jnp.maximum(b - 1, 0)]))
        def _():
            wg_sc[...] = wg_ref[0, 0].astype(BF16)
            wu_sc[...] = wu_ref[0, 0].astype(BF16)
            wd_sc[...] = wd_ref[0, 0].astype(BF16)

        x = _load_row_tiles(x_ref, 0, bm).astype(BF16)
        g = jnp.dot(x, wg_sc[...], preferred_element_type=F32)
        u = jnp.dot(x, wu_sc[...], preferred_element_type=F32)
        hmid = (g * jax.nn.sigmoid(g) * u).astype(BF16)
        _store_row_tiles(y_ref, jnp.dot(hmid, wd_sc[...], preferred_element_type=F32))

    @pl.when(b >= n_used)
    def _():
        y_ref[...] = jnp.zeros(y_ref.shape, U32)


def _experts(x_buf, block_expert, n_used, w_gate, w_up, w_down, layer, bm):
    nb = x_buf.shape[0] // (bm * ROW_TILE)
    grid_spec = pltpu.PrefetchScalarGridSpec(
        num_scalar_prefetch=2,
        grid=(nb,),
        in_specs=[
            pl.BlockSpec((bm * ROW_TILE, LANES), lambda b, be, nu: (jnp.minimum(b, nu[0] - 1), 0)),
            pl.BlockSpec((1, 1, D_MODEL, D_EXPERT), lambda b, be, nu: (layer, be[b], 0, 0)),
            pl.BlockSpec((1, 1, D_MODEL, D_EXPERT), lambda b, be, nu: (layer, be[b], 0, 0)),
            pl.BlockSpec((1, 1, D_EXPERT, D_MODEL), lambda b, be, nu: (layer, be[b], 0, 0)),
        ],
        out_specs=pl.BlockSpec((bm * ROW_TILE, LANES), lambda b, be, nu: (b, 0)),
        scratch_shapes=[
            pltpu.VMEM((D_MODEL, D_EXPERT), BF16),
            pltpu.VMEM((D_MODEL, D_EXPERT), BF16),
            pltpu.VMEM((D_EXPERT, D_MODEL), BF16),
        ],
    )
    return pl.pallas_call(
        functools.partial(_expert_kernel, bm=bm),
        grid_spec=grid_spec,
        out_shape=jax.ShapeDtypeStruct((nb * bm * ROW_TILE, LANES), U32),
        compiler_params=_cparams(("arbitrary",)),
        name="routed_experts",
    )(block_expert, n_used, x_buf, w_gate, w_up, w_down)


def _combine_kernel(ts_ref, d_ref, dn_ref, y_hbm, x1_ref, h2_ref, gate_ref, mod_ref, wsg_ref, wsu_ref, wsd_ref,
                    o_ref, yg_sc, sem, *, tc):
    del ts_ref
    i = pl.program_id(0)
    n = pl.num_programs(0)
    slot = i % 2
    n_copies = tc * TOP_K

    def row_copy(src, s, row):
        return pltpu.make_async_copy(y_hbm.at[pl.ds(src * ROW_TILE, ROW_TILE), :],
                                     yg_sc.at[s, pl.ds(row * ROW_TILE, ROW_TILE), :], sem.at[s])

    def issue(dest_smem, s):
        def body(tok, c):
            for k in range(TOP_K):
                row_copy(dest_smem[0, 0, tok * TOP_K + k], s, k * tc + tok).start(priority=k % DMA_PRIORITIES)
            return c
        lax.fori_loop(0, tc, body, 0, unroll=2)

    @pl.when(i == 0)
    def _():
        issue(d_ref, 0)

    @pl.when(i + 1 < n)
    def _():
        issue(dn_ref, 1 - slot)

    pltpu.make_async_copy(y_hbm.at[pl.ds(0, n_copies * ROW_TILE), :], yg_sc.at[slot], sem.at[slot]).wait()

    gate = gate_ref[...]
    yg = yg_sc.at[slot]
    routed = gate[:, 0:1] * _load_row_tiles(yg, 0, tc)
    for k in range(1, TOP_K):
        routed = routed + gate[:, k:k + 1] * _load_row_tiles(yg, k * tc, tc)
    hb = _load_row_tiles(h2_ref, 0, tc).astype(BF16)
    g = jnp.dot(hb, wsg_ref[...], preferred_element_type=F32)
    u = jnp.dot(hb, wsu_ref[...], preferred_element_type=F32)
    shared = jnp.dot((g * jax.nn.sigmoid(g) * u).astype(BF16), wsd_ref[...], preferred_element_type=F32)
    o_ref[...] = x1_ref[...] + mod_ref[0][5:6] * (routed + shared)


def _combine(y_buf, dest3, x1, h2, gate, mod_l, ws_gate, ws_up, ws_down, tile_seq, tc):
    t = x1.shape[0]
    n = t // tc
    d_sh = ws_gate.shape[1]
    const = lambda i, ts: (0, 0)
    row = lambda i, ts: (i, 0)
    grid_spec = pltpu.PrefetchScalarGridSpec(
        num_scalar_prefetch=1,
        grid=(n,),
        in_specs=[
            pl.BlockSpec((1, 1, tc * TOP_K), lambda i, ts: (i, 0, 0), memory_space=pltpu.SMEM),
            pl.BlockSpec((1, 1, tc * TOP_K), lambda i, ts: (jnp.minimum(i + 1, n - 1), 0, 0),
                         memory_space=pltpu.SMEM),
            pl.BlockSpec(memory_space=pl.ANY),
            pl.BlockSpec((tc, D_MODEL), row),
            pl.BlockSpec((tc * ROW_TILE, LANES), row),
            pl.BlockSpec((tc, LANES), row),
            pl.BlockSpec((1, N_ADA, D_MODEL), lambda i, ts: (ts[i], 0, 0)),
            pl.BlockSpec((D_MODEL, d_sh), const),
            pl.BlockSpec((D_MODEL, d_sh), const),
            pl.BlockSpec((d_sh, D_MODEL), const),
        ],
        out_specs=pl.BlockSpec((tc, D_MODEL), row),
        scratch_shapes=[pltpu.VMEM((2, tc * TOP_K * ROW_TILE, LANES), U32), pltpu.SemaphoreType.DMA((2,))],
    )
    return pl.pallas_call(
        functools.partial(_combine_kernel, tc=tc),
        grid_spec=grid_spec,
        out_shape=jax.ShapeDtypeStruct((t, D_MODEL), F32),
        compiler_params=_cparams(("arbitrary",)),
        name="moe_combine",
    )(tile_seq, dest3, dest3, y_buf, x1, h2, gate, mod_l, ws_gate, ws_up, ws_down)


def _tile_meta(seq_lens, tile):
    seq, first, last, pos0, slen = [], [], [], [], []
    for s, n in enumerate(seq_lens):
        nt = n // tile
        for j in range(nt):
            seq.append(s)
            first.append(int(j == 0))
            last.append(int(j == nt - 1))
            pos0.append(j * tile)
            slen.append(n)
    return tuple(jnp.asarray(np.asarray(a, np.int32)) for a in (seq, first, last, pos0, slen))


def _scan_meta(seq_lens, tr):
    bwd, first = [], []
    base = 0
    for n in seq_lens:
        nt = n // SSM_CHUNK // tr
        for j in range(nt):
            bwd.append(base + nt - 1 - j)
            first.append(int(j == 0))
        base += nt
    return jnp.asarray(np.asarray(bwd, np.int32)), jnp.asarray(np.asarray(first, np.int32))


def _rope_tables(seq_lens):
    quarter = HEAD_DIM // 4
    freqs = ROPE_THETA ** (-jnp.arange(quarter, dtype=F32) / quarter)
    sign = jnp.tile(jnp.concatenate([-jnp.ones((quarter,), F32), jnp.ones((quarter,), F32)]), 2)
    cos_l, sin_l = [], []
    cache = {}
    for n in seq_lens:
        if n not in cache:
            pos = jnp.arange(n)
            ar = (pos // GRID_W).astype(F32)[:, None] * freqs
            ac = (pos % GRID_W).astype(F32)[:, None] * freqs
            ang = jnp.concatenate([ar, ar, ac, ac], axis=-1)
            cache[n] = (jnp.tile(jnp.cos(ang), (1, 2)), jnp.tile(jnp.sin(ang) * sign, (1, 2)))
        cos_l.append(cache[n][0])
        sin_l.append(cache[n][1])
    return jnp.concatenate(cos_l, axis=0), jnp.concatenate(sin_l, axis=0)


def _split_bf16(w):
    hi = w.astype(BF16)
    return jnp.stack([hi, (w - hi.astype(F32)).astype(BF16)], axis=0)


def _block_diag_ones(n, blk):
    r = np.arange(n) // blk
    return jnp.asarray((r[:, None] == r[None, :]).astype(np.float32)).astype(BF16)


def _dest_kernel(idx_ref, rank_ref, ps_ref, d_ref):
    idx = idx_ref[...]
    row_e = lax.broadcasted_iota(I32, (N_EXPERTS, idx.shape[1]), 0)
    starts = [jnp.sum(jnp.where(row_e == idx[k:k + 1], ps_ref[...], 0.0), axis=0, keepdims=True)
              for k in range(TOP_K)]
    d_ref[...] = rank_ref[...] + jnp.concatenate(starts, axis=0).astype(I32)


def _dest_slots(idx, rank, pstart):
    t = idx.shape[1]
    tm = _pick(t, 1024)
    col = lambda i: (0, i)
    return pl.pallas_call(
        _dest_kernel,
        grid=(t // tm,),
        in_specs=[pl.BlockSpec((TOP_K, tm), col), pl.BlockSpec((TOP_K, tm), col),
                  pl.BlockSpec((N_EXPERTS, 1), lambda i: (0, 0))],
        out_specs=pl.BlockSpec((TOP_K, tm), col),
        out_shape=jax.ShapeDtypeStruct((TOP_K, t), I32),
        compiler_params=_cparams(("arbitrary",)),
        name="moe_dest_slots",
    )(idx, rank, pstart)


def _moe_plan(idx, rank, counts, t, bm, nb):
    cnt = counts.reshape(N_EXPERTS).astype(I32)
    padded = (cnt + bm - 1) // bm * bm
    pend = jnp.cumsum(padded)
    pstart = pend - padded
    n_used = (pend[-1] // bm).astype(I32).reshape(1)
    block_expert = jnp.minimum(
        jnp.searchsorted(pend, jnp.arange(nb, dtype=I32) * bm, side='right'), N_EXPERTS - 1).astype(I32)
    dest = _dest_slots(idx, rank, pstart.astype(F32).reshape(N_EXPERTS, 1)).T
    return dest, (pend - bm).astype(I32), (cnt > 0).astype(I32), block_expert, n_used


def kernel(x_prompt, x_sample, c_prompt, c_sample, w_ada, b_ada, norm1_g, w_in, q_norm_g, k_norm_g, w_attn_o, pool_w, pool_scale, w_pool_o, ssm_a_re, ssm_a_im, ssm_log_dt, ssm_b_re, ssm_b_im, ssm_c_re, ssm_c_im, ssm_d, w_glu, w_out, norm2_g, w_router, b_router, w_exp_gate, w_exp_up, w_exp_down, w_sh_gate, w_sh_up, w_sh_down):
    b1, l1, d = x_prompt.shape
    b2, l2, _ = x_sample.shape
    depth = w_in.shape[0]
    assert d == D_MODEL and b1 + b2 <= MOD_ROWS
    seq_lens = [l1] * b1 + [l2] * b2
    t = b1 * l1 + b2 * l2
    lmin = min(l1, l2)

    tm_in = _pick(lmin, 512)
    tm_post = _pick(lmin, 256)
    tc_comb = _pick(lmin, 128)
    tr_scan = _pick(lmin // SSM_CHUNK, 256)
    bm = 256
    nb = -(-(t * TOP_K + N_EXPERTS * (bm - 1)) // bm)

    x = jnp.concatenate([x_prompt.reshape(b1 * l1, d), x_sample.reshape(b2 * l2, d)], axis=0)
    c_all = jnp.concatenate([c_prompt, c_sample, jnp.zeros((MOD_ROWS - b1 - b2, d), F32)], axis=0)
    mod = _modulation(c_all, w_ada, b_ada).reshape(depth, MOD_ROWS, N_ADA, d)

    cos_t, sin_t = _rope_tables(seq_lens)
    ones_bd = _block_diag_ones(ATTN_DIM, HEAD_DIM)
    triu = jnp.asarray(np.triu(np.ones((tm_post, tm_post), np.float32), 1)).astype(BF16)
    meta_in = _tile_meta(seq_lens, tm_in)
    meta_post = _tile_meta(seq_lens, tm_post)
    meta_comb = _tile_meta(seq_lens, tc_comb)
    bwd_tile, first_tile = _scan_meta(seq_lens, tr_scan)

    for l in range(depth):
        q8, k2, v2, u_pool, u_ssm, gates = _inproj(
            x, mod[l], norm1_g[l].reshape(1, d), w_in[l].astype(BF16),
            jnp.tile(q_norm_g[l], N_Q_HEADS).reshape(1, ATTN_DIM),
            jnp.tile(k_norm_g[l], N_KV_HEADS).reshape(1, KV_DIM),
            cos_t, sin_t, ones_bd, meta_in[0], tm_in)
        attn = jnp.concatenate(
            [_flash(q8, k2, v2, 0, b1, l1), _flash(q8, k2, v2, b1 * l1, b2, l2)], axis=0)
        tables = _ssm_tables(ssm_a_re[l], ssm_a_im[l], ssm_log_dt[l], ssm_b_re[l], ssm_b_im[l],
                             ssm_c_re[l], ssm_c_im[l], ssm_d[l])
        y_ssm = _bidir_ssm(u_ssm, tables, bwd_tile, first_tile, tr_scan)

        pool_bd = jax.scipy.linalg.block_diag(*[pool_w[l, g] for g in range(len(POOL_WINDOWS))])
        wts = (w_attn_o[l].astype(BF16), pool_bd.astype(BF16), pool_scale[l].reshape(1, POOL_DIM),
               w_pool_o[l].astype(BF16), w_glu[l].astype(BF16), w_out[l].astype(BF16),
               norm2_g[l].reshape(1, d), _split_bf16(w_router[l].T), b_router[l].reshape(N_EXPERTS, 1), triu)
        x1, h2, idx, gate, rank, counts = _post(x, attn, u_pool, y_ssm, gates, mod[l], wts, meta_post, tm_post)

        dest, last_block, has_rows, block_expert, n_used = _moe_plan(idx, rank, counts, t, bm, nb)
        dest3 = dest.reshape(t // tc_comb, 1, tc_comb * TOP_K)
        x_buf = _dispatch(h2, dest3, last_block, has_rows, nb * bm, tc_comb, bm)
        y_buf = _experts(x_buf, block_expert, n_used, w_exp_gate, w_exp_up, w_exp_down, l, bm)
        gate = jnp.pad(gate.T, ((0, 0), (0, LANES - TOP_K)))
        x = _combine(y_buf, dest3, x1, h2, gate, mod[l],
                     w_sh_gate[l].astype(BF16), w_sh_up[l].astype(BF16), w_sh_down[l].astype(BF16),
                     meta_comb[0], tc_comb)

    return (x[:b1 * l1].reshape(b1, l1, d), x[b1 * l1:].reshape(b2, l2, d))
```

```python
import functools
import math

import jax
import jax.numpy as jnp
import numpy as np
from jax import lax
from jax.experimental import pallas as pl
from jax.experimental.pallas import tpu as pltpu

F32 = jnp.float32
BF16 = jnp.bfloat16
I32 = jnp.int32
HIGHEST = lax.Precision.HIGHEST

D_MODEL = 1024
GRID_W = 64
HEAD_DIM = 64
N_Q_HEADS = 8
N_KV_HEADS = 2
Q_PER_KV = N_Q_HEADS // N_KV_HEADS
ATTN_DIM = N_Q_HEADS * HEAD_DIM
KV_DIM = N_KV_HEADS * HEAD_DIM
ROPE_THETA = 10000.0
POOL_WINDOWS = (2, 4, 8, 16)
POOL_DIM = 256
POOL_GROUP = 64
POOL_HALO = 8
SSM_DIM = 256
SSM_P = 16
SSM_G = 16
SSM_N = 64
SSM_CHUNK = 16
SSM_ROW = SSM_CHUNK * SSM_P
SSM_STATE = 2 * SSM_N
N_EXPERTS = 256
TOP_K = 8
N_EXPERT_GROUPS = 8
GROUP_SIZE = N_EXPERTS // N_EXPERT_GROUPS
TOPK_GROUPS = 4
D_EXPERT = 256
ROUTED_SCALE = 2.5
N_ADA = 6
EPS = 1e-6
IN_DIM = ATTN_DIM + 2 * KV_DIM + POOL_DIM + SSM_DIM + 3 * D_MODEL
OFF_K = ATTN_DIM
OFF_V = OFF_K + KV_DIM
OFF_POOL = OFF_V + KV_DIM
OFF_SSM = OFF_POOL + POOL_DIM
OFF_GATES = OFF_SSM + SSM_DIM
MOD_ROWS = 8
LOG2_E = math.log2(math.e)
VT_ROWS = HEAD_DIM + 16

V7X_VMEM_BYTES = 64 * 1024 * 1024
VMEM_LIMIT = V7X_VMEM_BYTES - 8 * 1024 * 1024
LANES = 128


def _cparams(sem):
    return pltpu.CompilerParams(dimension_semantics=sem, vmem_limit_bytes=VMEM_LIMIT)


def _pick(n, pref):
    t = min(n, pref)
    while n % t:
        t //= 2
    return t


U32 = jnp.uint32
ROW_TILE = D_MODEL // 2 // LANES
DMA_PRIORITIES = 2


def _store_row_tiles(ref, x):
    n = x.shape[0]
    half = D_MODEL // 2
    lo = pltpu.bitcast(x[:, :half].astype(BF16).astype(F32), U32)
    hi = pltpu.bitcast(x[:, half:].astype(BF16).astype(F32), U32)
    w = (lo >> 16) | (hi & jnp.uint32(0xFFFF0000))
    for s in range(ROW_TILE):
        ref[pl.ds(s, n, stride=ROW_TILE), :] = w[:, s * LANES:(s + 1) * LANES]


def _load_row_tiles(ref, first_row, n):
    w = jnp.concatenate(
        [ref[pl.ds(first_row * ROW_TILE + s, n, stride=ROW_TILE), :] for s in range(ROW_TILE)], axis=1)
    lo = pltpu.bitcast(w << 16, F32)
    hi = pltpu.bitcast(w & jnp.uint32(0xFFFF0000), F32)
    return jnp.concatenate([lo, hi], axis=1)


def _mod_kernel(c_ref, w_ref, b_ref, o_ref):
    c = c_ref[...]
    a = c * jax.nn.sigmoid(c)
    o_ref[0] = jnp.dot(a, w_ref[0], preferred_element_type=F32, precision=HIGHEST) + b_ref[0]


def _modulation(c_all, w_ada, b_ada):
    depth, d, n = w_ada.shape
    bn = _pick(n, 1536)
    return pl.pallas_call(
        _mod_kernel,
        grid=(depth, n // bn),
        in_specs=[
            pl.BlockSpec((MOD_ROWS, d), lambda l, j: (0, 0)),
            pl.BlockSpec((1, d, bn), lambda l, j: (l, 0, j)),
            pl.BlockSpec((1, 1, bn), lambda l, j: (l, 0, j)),
        ],
        out_specs=pl.BlockSpec((1, MOD_ROWS, bn), lambda l, j: (l, 0, j)),
        out_shape=jax.ShapeDtypeStruct((depth, MOD_ROWS, n), F32),
        compiler_params=_cparams(("arbitrary", "arbitrary")),
        name="adaln_mod",
    )(c_all, w_ada, b_ada.reshape(depth, 1, n))


def _rmsnorm_mod(x, g, scale, shift):
    y = x * lax.rsqrt(jnp.mean(x * x, axis=-1, keepdims=True) + EPS)
    return (y * g) * (1.0 + scale) + shift


def _head_norm_rope(z, gain, cos, sin_signed, ones_bd):
    z2 = z * z
    hi = z2.astype(BF16)
    lo = (z2 - hi.astype(F32)).astype(BF16)
    ss = (jnp.dot(hi, ones_bd, preferred_element_type=F32)
          + jnp.dot(lo, ones_bd, preferred_element_type=F32))
    y = (z * lax.rsqrt(ss * (1.0 / HEAD_DIM) + EPS)) * gain
    w = z.shape[1]
    quarter = HEAD_DIM // 4
    from_right = pltpu.roll(y, w - quarter, 1)
    from_left = pltpu.roll(y, quarter, 1)
    lane = lax.broadcasted_iota(I32, y.shape, 1)
    rot = jnp.where((lane & quarter) == 0, from_right, from_left)
    return y * cos + rot * sin_signed


def _inproj_kernel(ts_ref, x_ref, mod_ref, g_ref, w_ref, gq_ref, gk_ref, cos_ref, sin_ref, ones_ref,
                   q_ref, k_ref, v_ref, up_ref, us_ref, gt_ref):
    del ts_ref
    x = x_ref[...]
    mod = mod_ref[0]
    h = _rmsnorm_mod(x, g_ref[...], mod[1:2], mod[0:1]).astype(BF16)

    cos = cos_ref[...]
    sin = sin_ref[...]
    zq = jnp.dot(h, w_ref[:, 0:ATTN_DIM], preferred_element_type=F32)
    reps = ATTN_DIM // LANES
    yq = _head_norm_rope(zq, gq_ref[...], jnp.concatenate([cos] * reps, axis=1),
                         jnp.concatenate([sin] * reps, axis=1), ones_ref[...]) * (HEAD_DIM ** -0.5 * LOG2_E)
    for hh in range(N_Q_HEADS):
        q_ref[hh] = yq[:, hh * HEAD_DIM:(hh + 1) * HEAD_DIM].astype(BF16)

    zkv = jnp.dot(h, w_ref[:, OFF_K:OFF_POOL], preferred_element_type=F32)
    yk = _head_norm_rope(zkv[:, :KV_DIM], gk_ref[...], cos, sin, ones_ref[0:KV_DIM, 0:KV_DIM])
    zv = zkv[:, KV_DIM:]
    for j in range(N_KV_HEADS):
        k_ref[j] = yk[:, j * HEAD_DIM:(j + 1) * HEAD_DIM].astype(BF16)
        v_ref[j, 0:HEAD_DIM, :] = zv[:, j * HEAD_DIM:(j + 1) * HEAD_DIM].T.astype(BF16)
        v_ref[j, HEAD_DIM:VT_ROWS, :] = jnp.ones((VT_ROWS - HEAD_DIM, zv.shape[0]), BF16)

    zps = jnp.dot(h, w_ref[:, OFF_POOL:OFF_GATES], preferred_element_type=F32)
    up_ref[...] = zps[:, :POOL_DIM]
    us_ref[...] = zps[:, POOL_DIM:]
    for c in range(3):
        zg = jnp.dot(h, w_ref[:, OFF_GATES + c * D_MODEL:OFF_GATES + (c + 1) * D_MODEL],
                     preferred_element_type=F32)
        gt_ref[:, c * D_MODEL:(c + 1) * D_MODEL] = jax.nn.sigmoid(zg).astype(BF16)


def _inproj(x, mod_l, g1, w_in_bf, gq_t, gk_t, cos_t, sin_t, ones_bd, tile_seq, tm):
    t = x.shape[0]
    const = lambda i, ts: (0, 0)
    row = lambda i, ts: (i, 0)
    grid_spec = pltpu.PrefetchScalarGridSpec(
        num_scalar_prefetch=1,
        grid=(t // tm,),
        in_specs=[
            pl.BlockSpec((tm, D_MODEL), row),
            pl.BlockSpec((1, N_ADA, D_MODEL), lambda i, ts: (ts[i], 0, 0)),
            pl.BlockSpec((1, D_MODEL), const),
            pl.BlockSpec((D_MODEL, IN_DIM), const),
            pl.BlockSpec((1, ATTN_DIM), const),
            pl.BlockSpec((1, KV_DIM), const),
            pl.BlockSpec((tm, LANES), row),
            pl.BlockSpec((tm, LANES), row),
            pl.BlockSpec((ATTN_DIM, ATTN_DIM), const),
        ],
        out_specs=[
            pl.BlockSpec((N_Q_HEADS, tm, HEAD_DIM), lambda i, ts: (0, i, 0)),
            pl.BlockSpec((N_KV_HEADS, tm, HEAD_DIM), lambda i, ts: (0, i, 0)),
            pl.BlockSpec((N_KV_HEADS, VT_ROWS, tm), lambda i, ts: (0, 0, i)),
            pl.BlockSpec((tm, POOL_DIM), row),
            pl.BlockSpec((tm, SSM_DIM), row),
            pl.BlockSpec((tm, 3 * D_MODEL), row),
        ],
    )
    return pl.pallas_call(
        _inproj_kernel,
        grid_spec=grid_spec,
        out_shape=[
            jax.ShapeDtypeStruct((N_Q_HEADS, t, HEAD_DIM), BF16),
            jax.ShapeDtypeStruct((N_KV_HEADS, t, HEAD_DIM), BF16),
            jax.ShapeDtypeStruct((N_KV_HEADS, VT_ROWS, t), BF16),
            jax.ShapeDtypeStruct((t, POOL_DIM), F32),
            jax.ShapeDtypeStruct((t, SSM_DIM), F32),
            jax.ShapeDtypeStruct((t, 3 * D_MODEL), BF16),
        ],
        compiler_params=_cparams(("arbitrary",)),
        name="norm1_inproj",
    )(tile_seq, x, mod_l, g1, w_in_bf, gq_t, gk_t, cos_t, sin_t, ones_bd)


def _flash_kernel(q_ref, k_ref, vt_ref, o_ref, m_sc, acc_sc):
    kb = pl.program_id(3)

    @pl.when(kb == 0)
    def _():
        m_sc[...] = jnp.full(m_sc.shape, -jnp.inf, F32)
        acc_sc[...] = jnp.zeros(acc_sc.shape, F32)

    k = k_ref[0]
    vt = vt_ref[0]
    heads = range(Q_PER_KV)
    qs = [q_ref[hh] for hh in heads]
    m_prev = [m_sc[hh] for hh in heads]
    acc_prev = [acc_sc[hh] for hh in heads]

    def scores(hh):
        return lax.dot_general(k, qs[hh], (((1,), (1,)), ((), ())), preferred_element_type=F32)

    ahead = 2
    s = {hh: scores(hh) for hh in range(ahead)}
    new = []
    for hh in heads:
        if hh + ahead < Q_PER_KV:
            s[hh + ahead] = scores(hh + ahead)
        m_new = jnp.maximum(m_prev[hh], jnp.max(s[hh], axis=0, keepdims=True))
        alpha = jnp.exp2(m_prev[hh] - m_new)
        p = jnp.exp2(s.pop(hh) - m_new).astype(BF16)
        acc_new = acc_prev[hh] * alpha + jnp.dot(vt, p, preferred_element_type=F32)
        new.append((m_new, acc_new))
    for hh, (m_new, acc_new) in enumerate(new):
        m_sc[hh] = m_new
        acc_sc[hh] = acc_new

    @pl.when(kb == pl.num_programs(3) - 1)
    def _():
        for hh in range(Q_PER_KV):
            acc = acc_sc[hh]
            o = acc[:HEAD_DIM] / acc[HEAD_DIM:HEAD_DIM + 1]
            o_ref[:, hh * HEAD_DIM:(hh + 1) * HEAD_DIM] = o.T.astype(BF16)


def _flash(q8, k2, vt2, off, bsz, seq):
    tq = _pick(seq, 512)
    tk = _pick(seq, 2048)
    nq, nk = seq // tq, seq // tk
    assert off % tq == 0 and off % tk == 0
    oq, ok = off // tq, off // tk
    return pl.pallas_call(
        _flash_kernel,
        grid=(N_KV_HEADS, bsz, nq, nk),
        in_specs=[
            pl.BlockSpec((Q_PER_KV, tq, HEAD_DIM), lambda j, b, i, kb: (j, oq + b * nq + i, 0)),
            pl.BlockSpec((1, tk, HEAD_DIM), lambda j, b, i, kb: (j, ok + b * nk + kb, 0)),
            pl.BlockSpec((1, VT_ROWS, tk), lambda j, b, i, kb: (j, 0, ok + b * nk + kb)),
        ],
        out_specs=pl.BlockSpec((tq, Q_PER_KV * HEAD_DIM), lambda j, b, i, kb: (b * nq + i, j)),
        out_shape=jax.ShapeDtypeStruct((bsz * seq, ATTN_DIM), BF16),
        scratch_shapes=[
            pltpu.VMEM((Q_PER_KV, 1, tq), F32),
            pltpu.VMEM((Q_PER_KV, VT_ROWS, tq), F32),
        ],
        compiler_params=_cparams(("arbitrary", "arbitrary", "arbitrary", "arbitrary")),
        name="flash_gqa",
    )(q8, k2, vt2)


def _ssm_tables(a_re, a_im, log_dt, b_re, b_im, c_re, c_im, d_skip):
    tc = SSM_CHUNK
    a = lax.complex(a_re.astype(F32), a_im.astype(F32))
    dt = jnp.exp(log_dt.astype(F32))[..., None]
    adt = a * dt
    a_bar = jnp.exp(adt)
    b_bar = ((a_bar - 1.0) / a)[..., None] * lax.complex(b_re.astype(F32), b_im.astype(F32))
    c_mat = lax.complex(c_re.astype(F32), c_im.astype(F32))
    taus = jnp.arange(tc + 1, dtype=F32)
    pw = jnp.exp(adt[None] * taus[:, None, None, None])

    kern = jnp.real(jnp.einsum('dgpn,tdgn,dgnq->dgtpq', c_mat, pw[:tc], b_bar))
    s_idx = jnp.arange(tc)[:, None]
    t_idx = jnp.arange(tc)[None, :]
    lag_f = jnp.clip(t_idx - s_idx, 0, tc - 1)
    lag_b = jnp.clip(s_idx - t_idx, 0, tc - 1)
    kf = jnp.where((t_idx >= s_idx)[None, :, :, None, None], kern[0][:, lag_f], 0.0)
    kb = jnp.where((s_idx >= t_idx)[None, :, :, None, None], kern[1][:, lag_b], 0.0)
    w_loc = (kf + kb).transpose(0, 1, 4, 2, 3).reshape(SSM_G, SSM_ROW, SSM_ROW)
    w_loc = w_loc + jnp.eye(SSM_ROW, dtype=F32)[None] * jnp.tile(
        d_skip.astype(F32).reshape(SSM_G, 1, SSM_P), (1, tc, 1)).reshape(SSM_G, 1, SSM_ROW)

    def ri(z, axis):
        return jnp.concatenate([jnp.real(z), jnp.imag(z)], axis=axis)

    pf = jnp.einsum('sgn,gnq->gsqn', pw[:tc][::-1, 0], b_bar[0])
    pb = jnp.einsum('sgn,gnq->gsqn', pw[:tc, 1], b_bar[1])
    p_st = jnp.concatenate([ri(pf, -1), ri(pb, -1)], axis=-1).reshape(SSM_G, SSM_ROW, 2 * SSM_STATE)
    qf = jnp.einsum('gpn,tgn->gntp', c_mat[0], pw[1:tc + 1, 0])
    qb = jnp.einsum('gpn,tgn->gntp', c_mat[1], pw[1:tc + 1, 1][::-1])
    q_f = jnp.concatenate([jnp.real(qf), -jnp.imag(qf)], axis=1).reshape(SSM_G, SSM_STATE, SSM_ROW)
    q_b = jnp.concatenate([jnp.real(qb), -jnp.imag(qb)], axis=1).reshape(SSM_G, SSM_STATE, SSM_ROW)
    a16 = pw[tc]
    a1 = jnp.concatenate([jnp.real(a16), jnp.real(a16)], axis=-1).reshape(2, SSM_G * SSM_STATE)
    a2 = jnp.concatenate([-jnp.imag(a16), jnp.imag(a16)], axis=-1).reshape(2, SSM_G * SSM_STATE)
    a_mul = jnp.stack([a1[0], a2[0], a1[1], a2[1]], axis=0)
    return w_loc.astype(BF16), p_st.astype(BF16), q_f.astype(BF16), q_b.astype(BF16), a_mul


def _ssm_state_kernel(u_ref, p_ref, s_ref):
    s = jnp.dot(u_ref[0].astype(BF16), p_ref[0], preferred_element_type=F32)
    s_ref[0] = s[:, :SSM_STATE]
    s_ref[1] = s[:, SSM_STATE:]


def _ssm_states(ug, p_st, tc):
    g, nc, _ = ug.shape
    return pl.pallas_call(
        _ssm_state_kernel,
        grid=(g, nc // tc),
        in_specs=[
            pl.BlockSpec((1, tc, SSM_ROW), lambda gi, i: (gi, i, 0)),
            pl.BlockSpec((1, SSM_ROW, 2 * SSM_STATE), lambda gi, i: (gi, 0, 0)),
        ],
        out_specs=pl.BlockSpec((2, tc, SSM_STATE), lambda gi, i: (0, i, gi)),
        out_shape=jax.ShapeDtypeStruct((2, nc, g * SSM_STATE), F32),
        compiler_params=_cparams(("arbitrary", "arbitrary")),
        name="ssm_chunk_states",
    )(ug, p_st)


def _swap_halves(x):
    parts = [pltpu.roll(x[:, g * LANES:(g + 1) * LANES], LANES // 2, 1) for g in range(x.shape[1] // LANES)]
    return jnp.concatenate(parts, axis=1)


def _ssm_scan_kernel(bt_ref, first_ref, sf_ref, sb_ref, am_ref, xf_ref, zb_ref, st_sc, sfw_sc, sbw_sc, *, tr):
    n = pl.program_id(0)
    del bt_ref

    @pl.when(first_ref[n] == 1)
    def _():
        st_sc[...] = jnp.zeros(st_sc.shape, F32)

    sfw_sc[...] = _swap_halves(sf_ref[0])
    sbw_sc[...] = _swap_halves(sb_ref[0])
    a1f, a2f, a1b, a2b = am_ref[0:1], am_ref[1:2], am_ref[2:3], am_ref[3:4]

    def body(r, carry):
        xf, xfw, zb, zbw = carry
        xf_ref[pl.ds(r, 1), :] = xf
        sf = sf_ref[0, pl.ds(r, 1), :]
        sfw = sfw_sc[pl.ds(r, 1), :]
        nxf = a1f * xf + a2f * xfw + sf
        nxfw = a1f * xfw - a2f * xf + sfw
        rb = tr - 1 - r
        zb_ref[pl.ds(rb, 1), :] = zb
        sb = sb_ref[0, pl.ds(rb, 1), :]
        sbw = sbw_sc[pl.ds(rb, 1), :]
        nzb = a1b * zb + a2b * zbw + sb
        nzbw = a1b * zbw - a2b * zb + sbw
        return nxf, nxfw, nzb, nzbw

    init = (st_sc[0:1], st_sc[1:2], st_sc[2:3], st_sc[3:4])
    xf, xfw, zb, zbw = lax.fori_loop(0, tr, body, init)
    st_sc[0:1] = xf
    st_sc[1:2] = xfw
    st_sc[2:3] = zb
    st_sc[3:4] = zbw


def _ssm_scan(s_st, a_mul, bwd_tile, first_tile, tr):
    _, nc, w = s_st.shape
    grid_spec = pltpu.PrefetchScalarGridSpec(
        num_scalar_prefetch=2,
        grid=(nc // tr,),
        in_specs=[
            pl.BlockSpec((1, tr, w), lambda n, bt, ft: (0, n, 0)),
            pl.BlockSpec((1, tr, w), lambda n, bt, ft: (1, bt[n], 0)),
            pl.BlockSpec((4, w), lambda n, bt, ft: (0, 0)),
        ],
        out_specs=[
            pl.BlockSpec((tr, w), lambda n, bt, ft: (n, 0)),
            pl.BlockSpec((tr, w), lambda n, bt, ft: (bt[n], 0)),
        ],
        scratch_shapes=[pltpu.VMEM((4, w), F32), pltpu.VMEM((tr, w), F32), pltpu.VMEM((tr, w), F32)],
    )
    return pl.pallas_call(
        functools.partial(_ssm_scan_kernel, tr=tr),
        grid_spec=grid_spec,
        out_shape=[jax.ShapeDtypeStruct((nc, w), F32), jax.ShapeDtypeStruct((nc, w), F32)],
        compiler_params=_cparams(("arbitrary",)),
        name="ssm_chunk_scan",
    )(bwd_tile, first_tile, s_st, s_st, a_mul)


def _ssm_out_kernel(u_ref, xf_ref, zb_ref, w_ref, qf_ref, qb_ref, y_ref):
    y = jnp.dot(u_ref[0].astype(BF16), w_ref[0], preferred_element_type=F32)
    y += jnp.dot(xf_ref[...].astype(BF16), qf_ref[0], preferred_element_type=F32)
    y += jnp.dot(zb_ref[...].astype(BF16), qb_ref[0], preferred_element_type=F32)
    y_ref[0] = y


def _ssm_outputs(ug, xf, zb, w_loc, q_f, q_b, tc):
    g, nc, _ = ug.shape
    return pl.pallas_call(
        _ssm_out_kernel,
        grid=(g, nc // tc),
        in_specs=[
            pl.BlockSpec((1, tc, SSM_ROW), lambda gi, i: (gi, i, 0)),
            pl.BlockSpec((tc, SSM_STATE), lambda gi, i: (i, gi)),
            pl.BlockSpec((tc, SSM_STATE), lambda gi, i: (i, gi)),
            pl.BlockSpec((1, SSM_ROW, SSM_ROW), lambda gi, i: (gi, 0, 0)),
            pl.BlockSpec((1, SSM_STATE, SSM_ROW), lambda gi, i: (gi, 0, 0)),
            pl.BlockSpec((1, SSM_STATE, SSM_ROW), lambda gi, i: (gi, 0, 0)),
        ],
        out_specs=pl.BlockSpec((1, tc, SSM_ROW), lambda gi, i: (gi, i, 0)),
        out_shape=jax.ShapeDtypeStruct((g, nc, SSM_ROW), F32),
        compiler_params=_cparams(("arbitrary", "arbitrary")),
        name="ssm_chunk_outputs",
    )(ug, xf, zb, w_loc, q_f, q_b)


def _bidir_ssm(u_ssm, tables, bwd_tile, first_tile, tr):
    w_loc, p_st, q_f, q_b, a_mul = tables
    t = u_ssm.shape[0]
    nc = t // SSM_CHUNK
    ug = u_ssm.reshape(nc, SSM_CHUNK, SSM_G, SSM_P).transpose(2, 0, 1, 3).reshape(SSM_G, nc, SSM_ROW)
    tc = _pick(nc, 512)
    s_st = _ssm_states(ug, p_st, tc)
    xf, zb = _ssm_scan(s_st, a_mul, bwd_tile, first_tile, tr)
    yg = _ssm_outputs(ug, xf, zb, w_loc, q_f, q_b, tc)
    return yg.reshape(SSM_G, nc, SSM_CHUNK, SSM_P).transpose(1, 2, 0, 3).reshape(t, SSM_DIM)


def _pool_mixer(ext_sc, u, pos, seq_len, tm):
    halves = []
    for half in range(2):
        sums = []
        for w in POOL_WINDOWS[2 * half:2 * half + 2]:
            lo = w // 2
            hi = w - lo - 1
            acc = ext_sc[pl.ds(POOL_HALO - lo, tm), half * LANES:(half + 1) * LANES]
            for d in range(-lo + 1, hi + 1):
                acc = acc + ext_sc[pl.ds(POOL_HALO + d, tm), half * LANES:(half + 1) * LANES]
            cnt = (jnp.minimum(pos + hi, seq_len - 1) - jnp.maximum(pos - lo, 0) + 1).astype(F32)
            sums.append(acc / cnt)
        lane = lax.broadcasted_iota(I32, (tm, LANES), 1)
        halves.append(jnp.where(lane < POOL_GROUP, sums[0], sums[1]))
    return jnp.concatenate(halves, axis=1) - u


def _route(h2, wrt_ref, brt_ref):
    tm = h2.shape[0]
    w_hi, w_lo = wrt_ref[0], wrt_ref[1]
    h_hi = h2.astype(BF16)
    h_lo = (h2 - h_hi.astype(F32)).astype(BF16)
    nt = (((1,), (1,)), ((), ()))
    logits = (lax.dot_general(w_hi, h_hi, nt, preferred_element_type=F32)
              + (lax.dot_general(w_hi, h_lo, nt, preferred_element_type=F32)
                 + lax.dot_general(w_lo, h_hi, nt, preferred_element_type=F32)))
    scores = jax.nn.sigmoid(logits)
    choice = scores + brt_ref[...]
    neg = jnp.float32(-jnp.inf)

    c3 = choice.reshape(N_EXPERT_GROUPS, GROUP_SIZE, tm)
    pos3 = lax.broadcasted_iota(I32, c3.shape, 1)
    m1 = jnp.max(c3, axis=1, keepdims=True)
    i1 = jnp.min(jnp.where(c3 == m1, pos3, GROUP_SIZE), axis=1, keepdims=True)
    m2 = jnp.max(jnp.where(pos3 == i1, neg, c3), axis=1, keepdims=True)
    gs = m1 + m2
    gid = lax.broadcasted_iota(I32, gs.shape, 0)
    ahead = jnp.zeros(gs.shape, I32)
    for o in range(N_EXPERT_GROUPS):
        go = gs[o:o + 1]
        ahead = ahead + ((go > gs) | ((go == gs) & (gid > o))).astype(I32)
    c = jnp.where(ahead < TOPK_GROUPS, c3, neg).reshape(N_EXPERTS, tm)

    row = lax.broadcasted_iota(I32, c.shape, 0)
    picks = []
    for _ in range(TOP_K):
        m = jnp.max(c, axis=0, keepdims=True)
        i = jnp.min(jnp.where(c == m, row, N_EXPERTS), axis=0, keepdims=True)
        oh = row == i
        w = jnp.sum(jnp.where(oh, scores, 0.0), axis=0, keepdims=True)
        c = jnp.where(oh, neg, c)
        picks.append((i, w, oh))
    wsum = picks[0][1]
    for _, w, _ in picks[1:]:
        wsum = wsum + w
    return [(i, w / wsum * ROUTED_SCALE, oh) for i, w, oh in picks]


def _post_kernel(ts_ref, tf_ref, tl_ref, tp_ref, tn_ref,
                 x_ref, at_ref, upp_ref, up_ref, upn_ref, ys_ref, gt_ref, mod_ref,
                 wao_ref, pbd_ref, psc_ref, wpo_ref, wgl_ref, wo_ref, g2_ref, wr_ref, br_ref, triu_ref,
                 x1_ref, h2_ref, idx_ref, gate_ref, rank_ref, cnt_ref, ext_sc, carry_sc, *, tm):
    del ts_ref
    i = pl.program_id(0)
    mod = mod_ref[0]

    attn = jnp.dot(at_ref[...], wao_ref[...], preferred_element_type=F32)

    u = up_ref[...]
    ext_sc[0:POOL_HALO] = jnp.where(tf_ref[i] == 1, 0.0, upp_ref[...])
    ext_sc[POOL_HALO:POOL_HALO + tm] = u
    ext_sc[POOL_HALO + tm:2 * POOL_HALO + tm] = jnp.where(tl_ref[i] == 1, 0.0, upn_ref[...])
    pos = tp_ref[i] + lax.broadcasted_iota(I32, (tm, LANES), 0)
    pm = _pool_mixer(ext_sc, u, pos, tn_ref[i], tm)
    pool = jnp.dot(pm.astype(BF16), pbd_ref[...], preferred_element_type=F32) * psc_ref[...]
    pool = jnp.dot(pool.astype(BF16), wpo_ref[...], preferred_element_type=F32)

    z = jnp.dot(jax.nn.gelu(ys_ref[...]).astype(BF16), wgl_ref[...], preferred_element_type=F32)
    ssm = z[:, :D_MODEL] * jax.nn.sigmoid(z[:, D_MODEL:])

    merged = (gt_ref[:, 0:D_MODEL].astype(F32) * attn
              + gt_ref[:, D_MODEL:2 * D_MODEL].astype(F32) * pool
              + gt_ref[:, 2 * D_MODEL:3 * D_MODEL].astype(F32) * ssm)
    mix = jnp.dot(merged.astype(BF16), wo_ref[...], preferred_element_type=F32)
    x1 = x_ref[...] + mod[2:3] * mix
    x1_ref[...] = x1
    h2 = _rmsnorm_mod(x1, g2_ref[...], mod[4:5], mod[3:4])
    _store_row_tiles(h2_ref, h2)

    picks = _route(h2, wr_ref, br_ref)

    @pl.when(i == 0)
    def _():
        carry_sc[...] = jnp.zeros(carry_sc.shape, F32)

    sel = picks[0][2]
    for _, _, oh in picks[1:]:
        sel = sel | oh
    sel_f = sel.astype(F32)
    ranks = jnp.dot(sel_f.astype(BF16), triu_ref[...], preferred_element_type=F32) + carry_sc[...]
    carry = carry_sc[...] + jnp.sum(sel_f, axis=1, keepdims=True)
    carry_sc[...] = carry
    cnt_ref[...] = carry

    idx_ref[...] = jnp.concatenate([ik for ik, _, _ in picks], axis=0)
    gate_ref[...] = jnp.concatenate([gk for _, gk, _ in picks], axis=0)
    rank_ref[...] = jnp.concatenate(
        [jnp.sum(jnp.where(oh, ranks, 0.0), axis=0, keepdims=True).astype(I32) for _, _, oh in picks], axis=0)


def _post(x, attn, u_pool, y_ssm, gates, mod_l, wts, meta, tm):
    t = x.shape[0]
    nh = t // POOL_HALO
    hb = tm // POOL_HALO
    npf = 5
    const = lambda i, *_: (0, 0)
    row = lambda i, *_: (i, 0)
    col = lambda i, *_: (0, i)
    grid_spec = pltpu.PrefetchScalarGridSpec(
        num_scalar_prefetch=npf,
        grid=(t // tm,),
        in_specs=[
            pl.BlockSpec((tm, D_MODEL), row),
            pl.BlockSpec((tm, ATTN_DIM), row),
            pl.BlockSpec((POOL_HALO, POOL_DIM), lambda i, *_: (jnp.maximum(i * hb - 1, 0), 0)),
            pl.BlockSpec((tm, POOL_DIM), row),
            pl.BlockSpec((POOL_HALO, POOL_DIM), lambda i, *_: (jnp.minimum((i + 1) * hb, nh - 1), 0)),
            pl.BlockSpec((tm, SSM_DIM), row),
            pl.BlockSpec((tm, 3 * D_MODEL), row),
            pl.BlockSpec((1, N_ADA, D_MODEL), lambda i, ts, *_: (ts[i], 0, 0)),
            pl.BlockSpec((ATTN_DIM, D_MODEL), const),
            pl.BlockSpec((POOL_DIM, POOL_DIM), const),
            pl.BlockSpec((1, POOL_DIM), const),
            pl.BlockSpec((POOL_DIM, D_MODEL), const),
            pl.BlockSpec((SSM_DIM, 2 * D_MODEL), const),
            pl.BlockSpec((D_MODEL, D_MODEL), const),
            pl.BlockSpec((1, D_MODEL), const),
            pl.BlockSpec((2, N_EXPERTS, D_MODEL), lambda i, *_: (0, 0, 0)),
            pl.BlockSpec((N_EXPERTS, 1), const),
            pl.BlockSpec((tm, tm), const),
        ],
        out_specs=[
            pl.BlockSpec((tm, D_MODEL), row),
            pl.BlockSpec((tm * ROW_TILE, LANES), row),
            pl.BlockSpec((TOP_K, tm), col),
            pl.BlockSpec((TOP_K, tm), col),
            pl.BlockSpec((TOP_K, tm), col),
            pl.BlockSpec((N_EXPERTS, 1), const),
        ],
        scratch_shapes=[pltpu.VMEM((tm + 2 * POOL_HALO, POOL_DIM), F32), pltpu.VMEM((N_EXPERTS, 1), F32)],
    )
    return pl.pallas_call(
        functools.partial(_post_kernel, tm=tm),
        grid_spec=grid_spec,
        out_shape=[
            jax.ShapeDtypeStruct((t, D_MODEL), F32),
            jax.ShapeDtypeStruct((t * ROW_TILE, LANES), U32),
            jax.ShapeDtypeStruct((TOP_K, t), I32),
            jax.ShapeDtypeStruct((TOP_K, t), F32),
            jax.ShapeDtypeStruct((TOP_K, t), I32),
            jax.ShapeDtypeStruct((N_EXPERTS, 1), F32),
        ],
        compiler_params=_cparams(("arbitrary",)),
        name="mixer_merge_route",
    )(*meta, x, attn, u_pool, u_pool, u_pool, y_ssm, gates, mod_l, *wts)


def _dispatch_kernel(lb_ref, has_ref, d_ref, h_hbm, x_hbm, zero_sc, h_sc, sem_in, sem, zsem, *, tn, bm):
    i = pl.program_id(0)
    n = pl.num_programs(0)
    n_copies = tn * TOP_K
    tile_rows = tn * ROW_TILE

    def slot_rows(ref, slot):
        return ref.at[pl.ds(slot * ROW_TILE, ROW_TILE), :]

    def tile_load(tile, s):
        return pltpu.make_async_copy(h_hbm.at[pl.ds(tile * tile_rows, tile_rows), :], h_sc.at[s], sem_in.at[s])

    def pad_loop(act):
        def expert(e, c):
            @pl.when(has_ref[e] == 1)
            def _():
                act(pltpu.make_async_copy(zero_sc, x_hbm.at[pl.ds(lb_ref[e] * ROW_TILE, bm * ROW_TILE), :], zsem))
            return c
        lax.fori_loop(0, N_EXPERTS, expert, 0)

    @pl.when(i == 0)
    def _():
        zero_sc[...] = jnp.zeros(zero_sc.shape, U32)
        pad_loop(lambda cp: cp.start())
        pad_loop(lambda cp: cp.wait())
        tile_load(0, 0).start()

    cur = i % 3

    @pl.when(i + 1 < n)
    def _():
        tile_load(i + 1, (i + 1) % 3).start()

    tile_load(i, cur).wait()

    def body(tok, c):
        src = slot_rows(h_sc.at[cur], tok)
        for k in range(TOP_K):
            pltpu.make_async_copy(src, slot_rows(x_hbm, d_ref[0, 0, tok * TOP_K + k]),
                                  sem.at[cur]).start(priority=k % DMA_PRIORITIES)
        return c
    lax.fori_loop(0, tn, body, 0, unroll=2)

    def wait_tile(s):
        whole = x_hbm.at[pl.ds(0, n_copies * ROW_TILE), :]
        pltpu.make_async_copy(whole, whole, sem.at[s]).wait()

    @pl.when(i > 0)
    def _():
        wait_tile((i + 2) % 3)

    @pl.when(i == n - 1)
    def _():
        wait_tile(cur)


def _dispatch(h2t, dest3, last_block, has_rows, n_rows, tn, bm):
    n = dest3.shape[0]
    grid_spec = pltpu.PrefetchScalarGridSpec(
        num_scalar_prefetch=2,
        grid=(n,),
        in_specs=[
            pl.BlockSpec((1, 1, tn * TOP_K), lambda i, lo, hi: (i, 0, 0), memory_space=pltpu.SMEM),
            pl.BlockSpec(memory_space=pl.ANY),
        ],
        out_specs=pl.BlockSpec(memory_space=pl.ANY),
        scratch_shapes=[pltpu.VMEM((bm * ROW_TILE, LANES), U32), pltpu.VMEM((3, tn * ROW_TILE, LANES), U32),
                        pltpu.SemaphoreType.DMA((3,)), pltpu.SemaphoreType.DMA((3,)),
                        pltpu.SemaphoreType.DMA(())],
    )
    return pl.pallas_call(
        functools.partial(_dispatch_kernel, tn=tn, bm=bm),
        grid_spec=grid_spec,
        out_shape=jax.ShapeDtypeStruct((n_rows * ROW_TILE, LANES), U32),
        compiler_params=_cparams(("arbitrary",)),
        name="moe_dispatch",
    )(last_block, has_rows, dest3, h2t)


def _expert_kernel(be_ref, nu_ref, x_ref, wg_ref, wu_ref, wd_ref, y_ref, wg_sc, wu_sc, wd_sc, *, bm):
    b = pl.program_id(0)
    n_used = nu_ref[0]

    @pl.when(b < n_used)
    def _():
        @pl.when((b == 0) | (be_ref[b] != be_ref[jnp.maximum(b - 1, 0)]))
        def _():
            wg_sc[...] = wg_ref[0, 0].astype(BF16)
            wu_sc[...] = wu_ref[0, 0].astype(BF16)
            wd_sc[...] = wd_ref[0, 0].astype(BF16)

        x = _load_row_tiles(x_ref, 0, bm).astype(BF16)
        g = jnp.dot(x, wg_sc[...], preferred_element_type=F32)
        u = jnp.dot(x, wu_sc[...], preferred_element_type=F32)
        hmid = (g * jax.nn.sigmoid(g) * u).astype(BF16)
        _store_row_tiles(y_ref, jnp.dot(hmid, wd_sc[...], preferred_element_type=F32))

    @pl.when(b >= n_used)
    def _():
        y_ref[...] = jnp.zeros(y_ref.shape, U32)


def _experts(x_buf, block_expert, n_used, w_gate, w_up, w_down, layer, bm):
    nb = x_buf.shape[0] // (bm * ROW_TILE)
    grid_spec = pltpu.PrefetchScalarGridSpec(
        num_scalar_prefetch=2,
        grid=(nb,),
        in_specs=[
            pl.BlockSpec((bm * ROW_TILE, LANES), lambda b, be, nu: (jnp.minimum(b, nu[0] - 1), 0)),
            pl.BlockSpec((1, 1, D_MODEL, D_EXPERT), lambda b, be, nu: (layer, be[b], 0, 0)),
            pl.BlockSpec((1, 1, D_MODEL, D_EXPERT), lambda b, be, nu: (layer, be[b], 0, 0)),
            pl.BlockSpec((1, 1, D_EXPERT, D_MODEL), lambda b, be, nu: (layer, be[b], 0, 0)),
        ],
        out_specs=pl.BlockSpec((bm * ROW_TILE, LANES), lambda b, be, nu: (b, 0)),
        scratch_shapes=[
            pltpu.VMEM((D_MODEL, D_EXPERT), BF16),
            pltpu.VMEM((D_MODEL, D_EXPERT), BF16),
            pltpu.VMEM((D_EXPERT, D_MODEL), BF16),
        ],
    )
    return pl.pallas_call(
        functools.partial(_expert_kernel, bm=bm),
        grid_spec=grid_spec,
        out_shape=jax.ShapeDtypeStruct((nb * bm * ROW_TILE, LANES), U32),
        compiler_params=_cparams(("arbitrary",)),
        name="routed_experts",
    )(block_expert, n_used, x_buf, w_gate, w_up, w_down)


def _combine_kernel(ts_ref, d_ref, dn_ref, y_hbm, x1_ref, h2_ref, gate_ref, mod_ref, wsg_ref, wsu_ref, wsd_ref,
                    o_ref, yg_sc, sem, *, tc):
    del ts_ref
    i = pl.program_id(0)
    n = pl.num_programs(0)
    slot = i % 2
    n_copies = tc * TOP_K

    def row_copy(src, s, row):
        return pltpu.make_async_copy(y_hbm.at[pl.ds(src * ROW_TILE, ROW_TILE), :],
                                     yg_sc.at[s, pl.ds(row * ROW_TILE, ROW_TILE), :], sem.at[s])

    def issue(dest_smem, s):
        def body(tok, c):
            for k in range(TOP_K):
                row_copy(dest_smem[0, 0, tok * TOP_K + k], s, k * tc + tok).start(priority=k % DMA_PRIORITIES)
            return c
        lax.fori_loop(0, tc, body, 0, unroll=2)

    @pl.when(i == 0)
    def _():
        issue(d_ref, 0)

    @pl.when(i + 1 < n)
    def _():
        issue(dn_ref, 1 - slot)

    pltpu.make_async_copy(y_hbm.at[pl.ds(0, n_copies * ROW_TILE), :], yg_sc.at[slot], sem.at[slot]).wait()

    gate = gate_ref[...]
    yg = yg_sc.at[slot]
    routed = gate[:, 0:1] * _load_row_tiles(yg, 0, tc)
    for k in range(1, TOP_K):
        routed = routed + gate[:, k:k + 1] * _load_row_tiles(yg, k * tc, tc)
    hb = _load_row_tiles(h2_ref, 0, tc).astype(BF16)
    g = jnp.dot(hb, wsg_ref[...], preferred_element_type=F32)
    u = jnp.dot(hb, wsu_ref[...], preferred_element_type=F32)
    shared = jnp.dot((g * jax.nn.sigmoid(g) * u).astype(BF16), wsd_ref[...], preferred_element_type=F32)
    o_ref[...] = x1_ref[...] + mod_ref[0][5:6] * (routed + shared)


def _combine(y_buf, dest3, x1, h2, gate, mod_l, ws_gate, ws_up, ws_down, tile_seq, tc):
    t = x1.shape[0]
    n = t // tc
    d_sh = ws_gate.shape[1]
    const = lambda i, ts: (0, 0)
    row = lambda i, ts: (i, 0)
    grid_spec = pltpu.PrefetchScalarGridSpec(
        num_scalar_prefetch=1,
        grid=(n,),
        in_specs=[
            pl.BlockSpec((1, 1, tc * TOP_K), lambda i, ts: (i, 0, 0), memory_space=pltpu.SMEM),
            pl.BlockSpec((1, 1, tc * TOP_K), lambda i, ts: (jnp.minimum(i + 1, n - 1), 0, 0),
                         memory_space=pltpu.SMEM),
            pl.BlockSpec(memory_space=pl.ANY),
            pl.BlockSpec((tc, D_MODEL), row),
            pl.BlockSpec((tc * ROW_TILE, LANES), row),
            pl.BlockSpec((tc, TOP_K), row),
            pl.BlockSpec((1, N_ADA, D_MODEL), lambda i, ts: (ts[i], 0, 0)),
            pl.BlockSpec((D_MODEL, d_sh), const),
            pl.BlockSpec((D_MODEL, d_sh), const),
            pl.BlockSpec((d_sh, D_MODEL), const),
        ],
        out_specs=pl.BlockSpec((tc, D_MODEL), row),
        scratch_shapes=[pltpu.VMEM((2, tc * TOP_K * ROW_TILE, LANES), U32), pltpu.SemaphoreType.DMA((2,))],
    )
    return pl.pallas_call(
        functools.partial(_combine_kernel, tc=tc),
        grid_spec=grid_spec,
        out_shape=jax.ShapeDtypeStruct((t, D_MODEL), F32),
        compiler_params=_cparams(("arbitrary",)),
        name="moe_combine",
    )(tile_seq, dest3, dest3, y_buf, x1, h2, gate, mod_l, ws_gate, ws_up, ws_down)


def _tile_meta(seq_lens, tile):
    seq, first, last, pos0, slen = [], [], [], [], []
    for s, n in enumerate(seq_lens):
        nt = n // tile
        for j in range(nt):
            seq.append(s)
            first.append(int(j == 0))
            last.append(int(j == nt - 1))
            pos0.append(j * tile)
            slen.append(n)
    return tuple(jnp.asarray(np.asarray(a, np.int32)) for a in (seq, first, last, pos0, slen))


def _scan_meta(seq_lens, tr):
    bwd, first = [], []
    base = 0
    for n in seq_lens:
        nt = n // SSM_CHUNK // tr
        for j in range(nt):
            bwd.append(base + nt - 1 - j)
            first.append(int(j == 0))
        base += nt
    return jnp.asarray(np.asarray(bwd, np.int32)), jnp.asarray(np.asarray(first, np.int32))


def _rope_tables(seq_lens):
    quarter = HEAD_DIM // 4
    freqs = ROPE_THETA ** (-jnp.arange(quarter, dtype=F32) / quarter)
    sign = jnp.tile(jnp.concatenate([-jnp.ones((quarter,), F32), jnp.ones((quarter,), F32)]), 2)
    cos_l, sin_l = [], []
    cache = {}
    for n in seq_lens:
        if n not in cache:
            pos = jnp.arange(n)
            ar = (pos // GRID_W).astype(F32)[:, None] * freqs
            ac = (pos % GRID_W).astype(F32)[:, None] * freqs
            ang = jnp.concatenate([ar, ar, ac, ac], axis=-1)
            cache[n] = (jnp.tile(jnp.cos(ang), (1, 2)), jnp.tile(jnp.sin(ang) * sign, (1, 2)))
        cos_l.append(cache[n][0])
        sin_l.append(cache[n][1])
    return jnp.concatenate(cos_l, axis=0), jnp.concatenate(sin_l, axis=0)


def _split_bf16(w):
    hi = w.astype(BF16)
    return jnp.stack([hi, (w - hi.astype(F32)).astype(BF16)], axis=0)


def _block_diag_ones(n, blk):
    r = np.arange(n) // blk
    return jnp.asarray((r[:, None] == r[None, :]).astype(np.float32)).astype(BF16)


def _dest_kernel(idx_ref, rank_ref, ps_ref, d_ref):
    idx = idx_ref[...]
    row_e = lax.broadcasted_iota(I32, (N_EXPERTS, idx.shape[1]), 0)
    starts = [jnp.sum(jnp.where(row_e == idx[k:k + 1], ps_ref[...], 0.0), axis=0, keepdims=True)
              for k in range(TOP_K)]
    d_ref[...] = rank_ref[...] + jnp.concatenate(starts, axis=0).astype(I32)


def _dest_slots(idx, rank, pstart):
    t = idx.shape[1]
    tm = _pick(t, 1024)
    col = lambda i: (0, i)
    return pl.pallas_call(
        _dest_kernel,
        grid=(t // tm,),
        in_specs=[pl.BlockSpec((TOP_K, tm), col), pl.BlockSpec((TOP_K, tm), col),
                  pl.BlockSpec((N_EXPERTS, 1), lambda i: (0, 0))],
        out_specs=pl.BlockSpec((TOP_K, tm), col),
        out_shape=jax.ShapeDtypeStruct((TOP_K, t), I32),
        compiler_params=_cparams(("arbitrary",)),
        name="moe_dest_slots",
    )(idx, rank, pstart)


def _moe_plan(idx, rank, counts, t, bm, nb):
    cnt = counts.reshape(N_EXPERTS).astype(I32)
    padded = (cnt + bm - 1) // bm * bm
    pend = jnp.cumsum(padded)
    pstart = pend - padded
    n_used = (pend[-1] // bm).astype(I32).reshape(1)
    starts = jnp.arange(nb, dtype=I32) * bm
    block_expert = jnp.minimum(
        jnp.sum((pend[None, :] <= starts[:, None]).astype(I32), axis=1), N_EXPERTS - 1).astype(I32)
    dest = _dest_slots(idx, rank, pstart.astype(F32).reshape(N_EXPERTS, 1)).T
    return dest, (pend - bm).astype(I32), (cnt > 0).astype(I32), block_expert, n_used


def kernel(x_prompt, x_sample, c_prompt, c_sample, w_ada, b_ada, norm1_g, w_in, q_norm_g, k_norm_g, w_attn_o, pool_w, pool_scale, w_pool_o, ssm_a_re, ssm_a_im, ssm_log_dt, ssm_b_re, ssm_b_im, ssm_c_re, ssm_c_im, ssm_d, w_glu, w_out, norm2_g, w_router, b_router, w_exp_gate, w_exp_up, w_exp_down, w_sh_gate, w_sh_up, w_sh_down):
    b1, l1, d = x_prompt.shape
    b2, l2, _ = x_sample.shape
    depth = w_in.shape[0]
    assert d == D_MODEL and b1 + b2 <= MOD_ROWS
    seq_lens = [l1] * b1 + [l2] * b2
    t = b1 * l1 + b2 * l2
    lmin = min(l1, l2)

    tm_in = _pick(lmin, 512)
    tm_post = _pick(lmin, 256)
    tc_comb = _pick(lmin, 128)
    tr_scan = _pick(lmin // SSM_CHUNK, 256)
    bm = 256
    nb = -(-(t * TOP_K + N_EXPERTS * (bm - 1)) // bm)

    x = jnp.concatenate([x_prompt.reshape(b1 * l1, d), x_sample.reshape(b2 * l2, d)], axis=0)
    c_all = jnp.concatenate([c_prompt, c_sample, jnp.zeros((MOD_ROWS - b1 - b2, d), F32)], axis=0)
    mod = _modulation(c_all, w_ada, b_ada).reshape(depth, MOD_ROWS, N_ADA, d)

    cos_t, sin_t = _rope_tables(seq_lens)
    ones_bd = _block_diag_ones(ATTN_DIM, HEAD_DIM)
    triu = jnp.asarray(np.triu(np.ones((tm_post, tm_post), np.float32), 1)).astype(BF16)
    meta_in = _tile_meta(seq_lens, tm_in)
    meta_post = _tile_meta(seq_lens, tm_post)
    meta_comb = _tile_meta(seq_lens, tc_comb)
    bwd_tile, first_tile = _scan_meta(seq_lens, tr_scan)

    for l in range(depth):
        q8, k2, v2, u_pool, u_ssm, gates = _inproj(
            x, mod[l], norm1_g[l].reshape(1, d), w_in[l].astype(BF16),
            jnp.tile(q_norm_g[l], N_Q_HEADS).reshape(1, ATTN_DIM),
            jnp.tile(k_norm_g[l], N_KV_HEADS).reshape(1, KV_DIM),
            cos_t, sin_t, ones_bd, meta_in[0], tm_in)
        attn = jnp.concatenate(
            [_flash(q8, k2, v2, 0, b1, l1), _flash(q8, k2, v2, b1 * l1, b2, l2)], axis=0)
        tables = _ssm_tables(ssm_a_re[l], ssm_a_im[l], ssm_log_dt[l], ssm_b_re[l], ssm_b_im[l],
                             ssm_c_re[l], ssm_c_im[l], ssm_d[l])
        y_ssm = _bidir_ssm(u_ssm, tables, bwd_tile, first_tile, tr_scan)

        pool_bd = jax.scipy.linalg.block_diag(*[pool_w[l, g] for g in range(len(POOL_WINDOWS))])
        wts = (w_attn_o[l].astype(BF16), pool_bd.astype(BF16), pool_scale[l].reshape(1, POOL_DIM),
               w_pool_o[l].astype(BF16), w_glu[l].astype(BF16), w_out[l].astype(BF16),
               norm2_g[l].reshape(1, d), _split_bf16(w_router[l].T), b_router[l].reshape(N_EXPERTS, 1), triu)
        x1, h2, idx, gate, rank, counts = _post(x, attn, u_pool, y_ssm, gates, mod[l], wts, meta_post, tm_post)

        dest, last_block, has_rows, block_expert, n_used = _moe_plan(idx, rank, counts, t, bm, nb)
        dest3 = dest.reshape(t // tc_comb, 1, tc_comb * TOP_K)
        x_buf = _dispatch(h2, dest3, last_block, has_rows, nb * bm, tc_comb, bm)
        y_buf = _experts(x_buf, block_expert, n_used, w_exp_gate, w_exp_up, w_exp_down, l, bm)
        gate = gate.T
        x = _combine(y_buf, dest3, x1, h2, gate, mod[l],
                     w_sh_gate[l].astype(BF16), w_sh_up[l].astype(BF16), w_sh_down[l].astype(BF16),
                     meta_comb[0], tc_comb)

    return (x[:b1 * l1].reshape(b1, l1, d), x[b1 * l1:].reshape(b2, l2, d))
```

```python
import functools
import math

import jax
import jax.numpy as jnp
import numpy as np
from jax import lax
from jax.experimental import pallas as pl
from jax.experimental.pallas import tpu as pltpu

F32 = jnp.float32
BF16 = jnp.bfloat16
I32 = jnp.int32
HIGHEST = lax.Precision.HIGHEST

D_MODEL = 1024
GRID_W = 64
HEAD_DIM = 64
N_Q_HEADS = 8
N_KV_HEADS = 2
Q_PER_KV = N_Q_HEADS // N_KV_HEADS
ATTN_DIM = N_Q_HEADS * HEAD_DIM
KV_DIM = N_KV_HEADS * HEAD_DIM
ROPE_THETA = 10000.0
POOL_WINDOWS = (2, 4, 8, 16)
POOL_DIM = 256
POOL_GROUP = 64
POOL_HALO = 8
SSM_DIM = 256
SSM_P = 16
SSM_G = 16
SSM_N = 64
SSM_CHUNK = 16
SSM_ROW = SSM_CHUNK * SSM_P
SSM_STATE = 2 * SSM_N
N_EXPERTS = 256
TOP_K = 8
N_EXPERT_GROUPS = 8
GROUP_SIZE = N_EXPERTS // N_EXPERT_GROUPS
TOPK_GROUPS = 4
D_EXPERT = 256
ROUTED_SCALE = 2.5
N_ADA = 6
EPS = 1e-6
IN_DIM = ATTN_DIM + 2 * KV_DIM + POOL_DIM + SSM_DIM + 3 * D_MODEL
OFF_K = ATTN_DIM
OFF_V = OFF_K + KV_DIM
OFF_POOL = OFF_V + KV_DIM
OFF_SSM = OFF_POOL + POOL_DIM
OFF_GATES = OFF_SSM + SSM_DIM
MOD_ROWS = 8
LOG2_E = math.log2(math.e)
VT_ROWS = HEAD_DIM + 16

V7X_VMEM_BYTES = 64 * 1024 * 1024
VMEM_LIMIT = V7X_VMEM_BYTES - 8 * 1024 * 1024
LANES = 128


def _cparams(sem):
    return pltpu.CompilerParams(dimension_semantics=sem, vmem_limit_bytes=VMEM_LIMIT)


def _pick(n, pref):
    t = min(n, pref)
    while n % t:
        t //= 2
    return t


U32 = jnp.uint32
ROW_TILE = D_MODEL // 2 // LANES
DMA_PRIORITIES = 2


def _store_row_tiles(ref, x):
    n = x.shape[0]
    half = D_MODEL // 2
    lo = pltpu.bitcast(x[:, :half].astype(BF16).astype(F32), U32)
    hi = pltpu.bitcast(x[:, half:].astype(BF16).astype(F32), U32)
    w = (lo >> 16) | (hi & jnp.uint32(0xFFFF0000))
    for s in range(ROW_TILE):
        ref[pl.ds(s, n, stride=ROW_TILE), :] = w[:, s * LANES:(s + 1) * LANES]


def _load_row_tiles(ref, first_row, n):
    w = jnp.concatenate(
        [ref[pl.ds(first_row * ROW_TILE + s, n, stride=ROW_TILE), :] for s in range(ROW_TILE)], axis=1)
    lo = pltpu.bitcast(w << 16, F32)
    hi = pltpu.bitcast(w & jnp.uint32(0xFFFF0000), F32)
    return jnp.concatenate([lo, hi], axis=1)


def _mod_kernel(c_ref, w_ref, b_ref, o_ref):
    c = c_ref[...]
    a = c * jax.nn.sigmoid(c)
    o_ref[0] = jnp.dot(a, w_ref[0], preferred_element_type=F32, precision=HIGHEST) + b_ref[0]


def _modulation(c_all, w_ada, b_ada):
    depth, d, n = w_ada.shape
    bn = _pick(n, 1536)
    return pl.pallas_call(
        _mod_kernel,
        grid=(depth, n // bn),
        in_specs=[
            pl.BlockSpec((MOD_ROWS, d), lambda l, j: (0, 0)),
            pl.BlockSpec((1, d, bn), lambda l, j: (l, 0, j)),
            pl.BlockSpec((1, 1, bn), lambda l, j: (l, 0, j)),
        ],
        out_specs=pl.BlockSpec((1, MOD_ROWS, bn), lambda l, j: (l, 0, j)),
        out_shape=jax.ShapeDtypeStruct((depth, MOD_ROWS, n), F32),
        compiler_params=_cparams(("arbitrary", "arbitrary")),
        name="adaln_mod",
    )(c_all, w_ada, b_ada.reshape(depth, 1, n))


def _rmsnorm_mod(x, g, scale, shift):
    y = x * lax.rsqrt(jnp.mean(x * x, axis=-1, keepdims=True) + EPS)
    return (y * g) * (1.0 + scale) + shift


def _head_norm_rope(z, gain, cos, sin_signed, ones_bd):
    z2 = z * z
    hi = z2.astype(BF16)
    lo = (z2 - hi.astype(F32)).astype(BF16)
    ss = (jnp.dot(hi, ones_bd, preferred_element_type=F32)
          + jnp.dot(lo, ones_bd, preferred_element_type=F32))
    y = (z * lax.rsqrt(ss * (1.0 / HEAD_DIM) + EPS)) * gain
    w = z.shape[1]
    quarter = HEAD_DIM // 4
    from_right = pltpu.roll(y, w - quarter, 1)
    from_left = pltpu.roll(y, quarter, 1)
    lane = lax.broadcasted_iota(I32, y.shape, 1)
    rot = jnp.where((lane & quarter) == 0, from_right, from_left)
    return y * cos + rot * sin_signed


def _inproj_kernel(ts_ref, x_ref, mod_ref, g_ref, w_ref, gq_ref, gk_ref, cos_ref, sin_ref, ones_ref,
                   q_ref, k_ref, v_ref, up_ref, us_ref, gt_ref):
    del ts_ref
    x = x_ref[...]
    mod = mod_ref[0]
    h = _rmsnorm_mod(x, g_ref[...], mod[1:2], mod[0:1]).astype(BF16)

    cos = cos_ref[...]
    sin = sin_ref[...]
    zq = jnp.dot(h, w_ref[:, 0:ATTN_DIM], preferred_element_type=F32)
    reps = ATTN_DIM // LANES
    yq = _head_norm_rope(zq, gq_ref[...], jnp.concatenate([cos] * reps, axis=1),
                         jnp.concatenate([sin] * reps, axis=1), ones_ref[...]) * (HEAD_DIM ** -0.5 * LOG2_E)
    for hh in range(N_Q_HEADS):
        q_ref[hh] = yq[:, hh * HEAD_DIM:(hh + 1) * HEAD_DIM].astype(BF16)

    zkv = jnp.dot(h, w_ref[:, OFF_K:OFF_POOL], preferred_element_type=F32)
    yk = _head_norm_rope(zkv[:, :KV_DIM], gk_ref[...], cos, sin, ones_ref[0:KV_DIM, 0:KV_DIM])
    zv = zkv[:, KV_DIM:]
    for j in range(N_KV_HEADS):
        k_ref[j] = yk[:, j * HEAD_DIM:(j + 1) * HEAD_DIM].astype(BF16)
        v_ref[j, 0:HEAD_DIM, :] = zv[:, j * HEAD_DIM:(j + 1) * HEAD_DIM].T.astype(BF16)
        v_ref[j, HEAD_DIM:VT_ROWS, :] = jnp.ones((VT_ROWS - HEAD_DIM, zv.shape[0]), BF16)

    zps = jnp.dot(h, w_ref[:, OFF_POOL:OFF_GATES], preferred_element_type=F32)
    up_ref[...] = zps[:, :POOL_DIM]
    us_ref[...] = zps[:, POOL_DIM:]
    for c in range(3):
        zg = jnp.dot(h, w_ref[:, OFF_GATES + c * D_MODEL:OFF_GATES + (c + 1) * D_MODEL],
                     preferred_element_type=F32)
        gt_ref[:, c * D_MODEL:(c + 1) * D_MODEL] = jax.nn.sigmoid(zg).astype(BF16)


def _inproj(x, mod_l, g1, w_in_bf, gq_t, gk_t, cos_t, sin_t, ones_bd, tile_seq, tm):
    t = x.shape[0]
    const = lambda i, ts: (0, 0)
    row = lambda i, ts: (i, 0)
    grid_spec = pltpu.PrefetchScalarGridSpec(
        num_scalar_prefetch=1,
        grid=(t // tm,),
        in_specs=[
            pl.BlockSpec((tm, D_MODEL), row),
            pl.BlockSpec((1, N_ADA, D_MODEL), lambda i, ts: (ts[i], 0, 0)),
            pl.BlockSpec((1, D_MODEL), const),
            pl.BlockSpec((D_MODEL, IN_DIM), const),
            pl.BlockSpec((1, ATTN_DIM), const),
            pl.BlockSpec((1, KV_DIM), const),
            pl.BlockSpec((tm, LANES), row),
            pl.BlockSpec((tm, LANES), row),
            pl.BlockSpec((ATTN_DIM, ATTN_DIM), const),
        ],
        out_specs=[
            pl.BlockSpec((N_Q_HEADS, tm, HEAD_DIM), lambda i, ts: (0, i, 0)),
            pl.BlockSpec((N_KV_HEADS, tm, HEAD_DIM), lambda i, ts: (0, i, 0)),
            pl.BlockSpec((N_KV_HEADS, VT_ROWS, tm), lambda i, ts: (0, 0, i)),
            pl.BlockSpec((tm, POOL_DIM), row),
            pl.BlockSpec((tm, SSM_DIM), row),
            pl.BlockSpec((tm, 3 * D_MODEL), row),
        ],
    )
    return pl.pallas_call(
        _inproj_kernel,
        grid_spec=grid_spec,
        out_shape=[
            jax.ShapeDtypeStruct((N_Q_HEADS, t, HEAD_DIM), BF16),
            jax.ShapeDtypeStruct((N_KV_HEADS, t, HEAD_DIM), BF16),
            jax.ShapeDtypeStruct((N_KV_HEADS, VT_ROWS, t), BF16),
            jax.ShapeDtypeStruct((t, POOL_DIM), F32),
            jax.ShapeDtypeStruct((t, SSM_DIM), F32),
            jax.ShapeDtypeStruct((t, 3 * D_MODEL), BF16),
        ],
        compiler_params=_cparams(("arbitrary",)),
        name="norm1_inproj",
    )(tile_seq, x, mod_l, g1, w_in_bf, gq_t, gk_t, cos_t, sin_t, ones_bd)


def _flash_kernel(q_ref, k_ref, vt_ref, o_ref, m_sc, acc_sc):
    kb = pl.program_id(3)

    @pl.when(kb == 0)
    def _():
        m_sc[...] = jnp.full(m_sc.shape, -jnp.inf, F32)
        acc_sc[...] = jnp.zeros(acc_sc.shape, F32)

    k = k_ref[0]
    vt = vt_ref[0]
    heads = range(Q_PER_KV)
    qs = [q_ref[hh] for hh in heads]
    m_prev = [m_sc[hh] for hh in heads]
    acc_prev = [acc_sc[hh] for hh in heads]

    def scores(hh):
        return lax.dot_general(k, qs[hh], (((1,), (1,)), ((), ())), preferred_element_type=F32)

    ahead = 2
    s = {hh: scores(hh) for hh in range(ahead)}
    new = []
    for hh in heads:
        if hh + ahead < Q_PER_KV:
            s[hh + ahead] = scores(hh + ahead)
        m_new = jnp.maximum(m_prev[hh], jnp.max(s[hh], axis=0, keepdims=True))
        alpha = jnp.exp2(m_prev[hh] - m_new)
        p = jnp.exp2(s.pop(hh) - m_new).astype(BF16)
        acc_new = acc_prev[hh] * alpha + jnp.dot(vt, p, preferred_element_type=F32)
        new.append((m_new, acc_new))
    for hh, (m_new, acc_new) in enumerate(new):
        m_sc[hh] = m_new
        acc_sc[hh] = acc_new

    @pl.when(kb == pl.num_programs(3) - 1)
    def _():
        for hh in range(Q_PER_KV):
            acc = acc_sc[hh]
            o = acc[:HEAD_DIM] / acc[HEAD_DIM:HEAD_DIM + 1]
            o_ref[:, hh * HEAD_DIM:(hh + 1) * HEAD_DIM] = o.T.astype(BF16)


def _flash(q8, k2, vt2, off, bsz, seq):
    tq = _pick(seq, 512)
    tk = _pick(seq, 4096)
    nq, nk = seq // tq, seq // tk
    assert off % tq == 0 and off % tk == 0
    oq, ok = off // tq, off // tk
    return pl.pallas_call(
        _flash_kernel,
        grid=(N_KV_HEADS, bsz, nq, nk),
        in_specs=[
            pl.BlockSpec((Q_PER_KV, tq, HEAD_DIM), lambda j, b, i, kb: (j, oq + b * nq + i, 0)),
            pl.BlockSpec((1, tk, HEAD_DIM), lambda j, b, i, kb: (j, ok + b * nk + kb, 0)),
            pl.BlockSpec((1, VT_ROWS, tk), lambda j, b, i, kb: (j, 0, ok + b * nk + kb)),
        ],
        out_specs=pl.BlockSpec((tq, Q_PER_KV * HEAD_DIM), lambda j, b, i, kb: (b * nq + i, j)),
        out_shape=jax.ShapeDtypeStruct((bsz * seq, ATTN_DIM), BF16),
        scratch_shapes=[
            pltpu.VMEM((Q_PER_KV, 1, tq), F32),
            pltpu.VMEM((Q_PER_KV, VT_ROWS, tq), F32),
        ],
        compiler_params=_cparams(("arbitrary", "arbitrary", "arbitrary", "arbitrary")),
        name="flash_gqa",
    )(q8, k2, vt2)


def _ssm_tables(a_re, a_im, log_dt, b_re, b_im, c_re, c_im, d_skip):
    tc = SSM_CHUNK
    a = lax.complex(a_re.astype(F32), a_im.astype(F32))
    dt = jnp.exp(log_dt.astype(F32))[..., None]
    adt = a * dt
    a_bar = jnp.exp(adt)
    b_bar = ((a_bar - 1.0) / a)[..., None] * lax.complex(b_re.astype(F32), b_im.astype(F32))
    c_mat = lax.complex(c_re.astype(F32), c_im.astype(F32))
    taus = jnp.arange(tc + 1, dtype=F32)
    pw = jnp.exp(adt[None] * taus[:, None, None, None])

    kern = jnp.real(jnp.einsum('dgpn,tdgn,dgnq->dgtpq', c_mat, pw[:tc], b_bar))
    s_idx = jnp.arange(tc)[:, None]
    t_idx = jnp.arange(tc)[None, :]
    lag_f = jnp.clip(t_idx - s_idx, 0, tc - 1)
    lag_b = jnp.clip(s_idx - t_idx, 0, tc - 1)
    kf = jnp.where((t_idx >= s_idx)[None, :, :, None, None], kern[0][:, lag_f], 0.0)
    kb = jnp.where((s_idx >= t_idx)[None, :, :, None, None], kern[1][:, lag_b], 0.0)
    w_loc = (kf + kb).transpose(0, 1, 4, 2, 3).reshape(SSM_G, SSM_ROW, SSM_ROW)
    w_loc = w_loc + jnp.eye(SSM_ROW, dtype=F32)[None] * jnp.tile(
        d_skip.astype(F32).reshape(SSM_G, 1, SSM_P), (1, tc, 1)).reshape(SSM_G, 1, SSM_ROW)

    def ri(z, axis):
        return jnp.concatenate([jnp.real(z), jnp.imag(z)], axis=axis)

    pf = jnp.einsum('sgn,gnq->gsqn', pw[:tc][::-1, 0], b_bar[0])
    pb = jnp.einsum('sgn,gnq->gsqn', pw[:tc, 1], b_bar[1])
    p_st = jnp.concatenate([ri(pf, -1), ri(pb, -1)], axis=-1).reshape(SSM_G, SSM_ROW, 2 * SSM_STATE)
    qf = jnp.einsum('gpn,tgn->gntp', c_mat[0], pw[1:tc + 1, 0])
    qb = jnp.einsum('gpn,tgn->gntp', c_mat[1], pw[1:tc + 1, 1][::-1])
    q_f = jnp.concatenate([jnp.real(qf), -jnp.imag(qf)], axis=1).reshape(SSM_G, SSM_STATE, SSM_ROW)
    q_b = jnp.concatenate([jnp.real(qb), -jnp.imag(qb)], axis=1).reshape(SSM_G, SSM_STATE, SSM_ROW)
    a16 = pw[tc]
    a1 = jnp.concatenate([jnp.real(a16), jnp.real(a16)], axis=-1).reshape(2, SSM_G * SSM_STATE)
    a2 = jnp.concatenate([-jnp.imag(a16), jnp.imag(a16)], axis=-1).reshape(2, SSM_G * SSM_STATE)
    a_mul = jnp.stack([a1[0], a2[0], a1[1], a2[1]], axis=0)
    return w_loc.astype(BF16), p_st.astype(BF16), q_f.astype(BF16), q_b.astype(BF16), a_mul


def _ssm_state_kernel(u_ref, p_ref, s_ref):
    s = jnp.dot(u_ref[0].astype(BF16), p_ref[0], preferred_element_type=F32)
    s_ref[0] = s[:, :SSM_STATE]
    s_ref[1] = s[:, SSM_STATE:]


def _ssm_states(ug, p_st, tc):
    g, nc, _ = ug.shape
    return pl.pallas_call(
        _ssm_state_kernel,
        grid=(g, nc // tc),
        in_specs=[
            pl.BlockSpec((1, tc, SSM_ROW), lambda gi, i: (gi, i, 0)),
            pl.BlockSpec((1, SSM_ROW, 2 * SSM_STATE), lambda gi, i: (gi, 0, 0)),
        ],
        out_specs=pl.BlockSpec((2, tc, SSM_STATE), lambda gi, i: (0, i, gi)),
        out_shape=jax.ShapeDtypeStruct((2, nc, g * SSM_STATE), F32),
        compiler_params=_cparams(("arbitrary", "arbitrary")),
        name="ssm_chunk_states",
    )(ug, p_st)


def _swap_halves(x):
    parts = [pltpu.roll(x[:, g * LANES:(g + 1) * LANES], LANES // 2, 1) for g in range(x.shape[1] // LANES)]
    return jnp.concatenate(parts, axis=1)


def _ssm_scan_kernel(bt_ref, first_ref, sf_ref, sb_ref, am_ref, xf_ref, zb_ref, st_sc, sfw_sc, sbw_sc, *, tr):
    n = pl.program_id(0)
    del bt_ref

    @pl.when(first_ref[n] == 1)
    def _():
        st_sc[...] = jnp.zeros(st_sc.shape, F32)

    sfw_sc[...] = _swap_halves(sf_ref[0])
    sbw_sc[...] = _swap_halves(sb_ref[0])
    a1f, a2f, a1b, a2b = am_ref[0:1], am_ref[1:2], am_ref[2:3], am_ref[3:4]

    def body(r, carry):
        xf, xfw, zb, zbw = carry
        xf_ref[pl.ds(r, 1), :] = xf
        sf = sf_ref[0, pl.ds(r, 1), :]
        sfw = sfw_sc[pl.ds(r, 1), :]
        nxf = a1f * xf + a2f * xfw + sf
        nxfw = a1f * xfw - a2f * xf + sfw
        rb = tr - 1 - r
        zb_ref[pl.ds(rb, 1), :] = zb
        sb = sb_ref[0, pl.ds(rb, 1), :]
        sbw = sbw_sc[pl.ds(rb, 1), :]
        nzb = a1b * zb + a2b * zbw + sb
        nzbw = a1b * zbw - a2b * zb + sbw
        return nxf, nxfw, nzb, nzbw

    init = (st_sc[0:1], st_sc[1:2], st_sc[2:3], st_sc[3:4])
    xf, xfw, zb, zbw = lax.fori_loop(0, tr, body, init)
    st_sc[0:1] = xf
    st_sc[1:2] = xfw
    st_sc[2:3] = zb
    st_sc[3:4] = zbw


def _ssm_scan(s_st, a_mul, bwd_tile, first_tile, tr):
    _, nc, w = s_st.shape
    grid_spec = pltpu.PrefetchScalarGridSpec(
        num_scalar_prefetch=2,
        grid=(nc // tr,),
        in_specs=[
            pl.BlockSpec((1, tr, w), lambda n, bt, ft: (0, n, 0)),
            pl.BlockSpec((1, tr, w), lambda n, bt, ft: (1, bt[n], 0)),
            pl.BlockSpec((4, w), lambda n, bt, ft: (0, 0)),
        ],
        out_specs=[
            pl.BlockSpec((tr, w), lambda n, bt, ft: (n, 0)),
            pl.BlockSpec((tr, w), lambda n, bt, ft: (bt[n], 0)),
        ],
        scratch_shapes=[pltpu.VMEM((4, w), F32), pltpu.VMEM((tr, w), F32), pltpu.VMEM((tr, w), F32)],
    )
    return pl.pallas_call(
        functools.partial(_ssm_scan_kernel, tr=tr),
        grid_spec=grid_spec,
        out_shape=[jax.ShapeDtypeStruct((nc, w), F32), jax.ShapeDtypeStruct((nc, w), F32)],
        compiler_params=_cparams(("arbitrary",)),
        name="ssm_chunk_scan",
    )(bwd_tile, first_tile, s_st, s_st, a_mul)


def _ssm_out_kernel(u_ref, xf_ref, zb_ref, w_ref, qf_ref, qb_ref, y_ref):
    y = jnp.dot(u_ref[0].astype(BF16), w_ref[0], preferred_element_type=F32)
    y += jnp.dot(xf_ref[...].astype(BF16), qf_ref[0], preferred_element_type=F32)
    y += jnp.dot(zb_ref[...].astype(BF16), qb_ref[0], preferred_element_type=F32)
    y_ref[0] = y


def _ssm_outputs(ug, xf, zb, w_loc, q_f, q_b, tc):
    g, nc, _ = ug.shape
    return pl.pallas_call(
        _ssm_out_kernel,
        grid=(g, nc // tc),
        in_specs=[
            pl.BlockSpec((1, tc, SSM_ROW), lambda gi, i: (gi, i, 0)),
            pl.BlockSpec((tc, SSM_STATE), lambda gi, i: (i, gi)),
            pl.BlockSpec((tc, SSM_STATE), lambda gi, i: (i, gi)),
            pl.BlockSpec((1, SSM_ROW, SSM_ROW), lambda gi, i: (gi, 0, 0)),
            pl.BlockSpec((1, SSM_STATE, SSM_ROW), lambda gi, i: (gi, 0, 0)),
            pl.BlockSpec((1, SSM_STATE, SSM_ROW), lambda gi, i: (gi, 0, 0)),
        ],
        out_specs=pl.BlockSpec((1, tc, SSM_ROW), lambda gi, i: (gi, i, 0)),
        out_shape=jax.ShapeDtypeStruct((g, nc, SSM_ROW), F32),
        compiler_params=_cparams(("arbitrary", "arbitrary")),
        name="ssm_chunk_outputs",
    )(ug, xf, zb, w_loc, q_f, q_b)


def _bidir_ssm(u_ssm, tables, bwd_tile, first_tile, tr):
    w_loc, p_st, q_f, q_b, a_mul = tables
    t = u_ssm.shape[0]
    nc = t // SSM_CHUNK
    ug = u_ssm.reshape(nc, SSM_CHUNK, SSM_G, SSM_P).transpose(2, 0, 1, 3).reshape(SSM_G, nc, SSM_ROW)
    tc = _pick(nc, 512)
    s_st = _ssm_states(ug, p_st, tc)
    xf, zb = _ssm_scan(s_st, a_mul, bwd_tile, first_tile, tr)
    yg = _ssm_outputs(ug, xf, zb, w_loc, q_f, q_b, tc)
    return yg.reshape(SSM_G, nc, SSM_CHUNK, SSM_P).transpose(1, 2, 0, 3).reshape(t, SSM_DIM)


def _pool_mixer(ext_sc, u, pos, seq_len, tm):
    halves = []
    for half in range(2):
        sums = []
        for w in POOL_WINDOWS[2 * half:2 * half + 2]:
            lo = w // 2
            hi = w - lo - 1
            acc = ext_sc[pl.ds(POOL_HALO - lo, tm), half * LANES:(half + 1) * LANES]
            for d in range(-lo + 1, hi + 1):
                acc = acc + ext_sc[pl.ds(POOL_HALO + d, tm), half * LANES:(half + 1) * LANES]
            cnt = (jnp.minimum(pos + hi, seq_len - 1) - jnp.maximum(pos - lo, 0) + 1).astype(F32)
            sums.append(acc / cnt)
        lane = lax.broadcasted_iota(I32, (tm, LANES), 1)
        halves.append(jnp.where(lane < POOL_GROUP, sums[0], sums[1]))
    return jnp.concatenate(halves, axis=1) - u


def _route(h2, wrt_ref, brt_ref):
    tm = h2.shape[0]
    w_hi, w_lo = wrt_ref[0], wrt_ref[1]
    h_hi = h2.astype(BF16)
    h_lo = (h2 - h_hi.astype(F32)).astype(BF16)
    nt = (((1,), (1,)), ((), ()))
    logits = (lax.dot_general(w_hi, h_hi, nt, preferred_element_type=F32)
              + (lax.dot_general(w_hi, h_lo, nt, preferred_element_type=F32)
                 + lax.dot_general(w_lo, h_hi, nt, preferred_element_type=F32)))
    scores = jax.nn.sigmoid(logits)
    choice = scores + brt_ref[...]
    neg = jnp.float32(-jnp.inf)

    c3 = choice.reshape(N_EXPERT_GROUPS, GROUP_SIZE, tm)
    pos3 = lax.broadcasted_iota(I32, c3.shape, 1)
    m1 = jnp.max(c3, axis=1, keepdims=True)
    i1 = jnp.min(jnp.where(c3 == m1, pos3, GROUP_SIZE), axis=1, keepdims=True)
    m2 = jnp.max(jnp.where(pos3 == i1, neg, c3), axis=1, keepdims=True)
    gs = m1 + m2
    gid = lax.broadcasted_iota(I32, gs.shape, 0)
    ahead = jnp.zeros(gs.shape, I32)
    for o in range(N_EXPERT_GROUPS):
        go = gs[o:o + 1]
        ahead = ahead + ((go > gs) | ((go == gs) & (gid > o))).astype(I32)
    c = jnp.where(ahead < TOPK_GROUPS, c3, neg).reshape(N_EXPERTS, tm)

    row = lax.broadcasted_iota(I32, c.shape, 0)
    picks = []
    for _ in range(TOP_K):
        m = jnp.max(c, axis=0, keepdims=True)
        i = jnp.min(jnp.where(c == m, row, N_EXPERTS), axis=0, keepdims=True)
        oh = row == i
        w = jnp.sum(jnp.where(oh, scores, 0.0), axis=0, keepdims=True)
        c = jnp.where(oh, neg, c)
        picks.append((i, w, oh))
    wsum = picks[0][1]
    for _, w, _ in picks[1:]:
        wsum = wsum + w
    return [(i, w / wsum * ROUTED_SCALE, oh) for i, w, oh in picks]


def _post_kernel(ts_ref, tf_ref, tl_ref, tp_ref, tn_ref,
                 x_ref, at_ref, upp_ref, up_ref, upn_ref, ys_ref, gt_ref, mod_ref,
                 wao_ref, pbd_ref, psc_ref, wpo_ref, wgl_ref, wo_ref, g2_ref, wr_ref, br_ref, triu_ref,
                 x1_ref, h2_ref, idx_ref, gate_ref, rank_ref, cnt_ref, ext_sc, carry_sc, *, tm):
    del ts_ref
    i = pl.program_id(0)
    mod = mod_ref[0]

    attn = jnp.dot(at_ref[...], wao_ref[...], preferred_element_type=F32)

    u = up_ref[...]
    ext_sc[0:POOL_HALO] = jnp.where(tf_ref[i] == 1, 0.0, upp_ref[...])
    ext_sc[POOL_HALO:POOL_HALO + tm] = u
    ext_sc[POOL_HALO + tm:2 * POOL_HALO + tm] = jnp.where(tl_ref[i] == 1, 0.0, upn_ref[...])
    pos = tp_ref[i] + lax.broadcasted_iota(I32, (tm, LANES), 0)
    pm = _pool_mixer(ext_sc, u, pos, tn_ref[i], tm)
    pool = jnp.dot(pm.astype(BF16), pbd_ref[...], preferred_element_type=F32) * psc_ref[...]
    pool = jnp.dot(pool.astype(BF16), wpo_ref[...], preferred_element_type=F32)

    z = jnp.dot(jax.nn.gelu(ys_ref[...]).astype(BF16), wgl_ref[...], preferred_element_type=F32)
    ssm = z[:, :D_MODEL] * jax.nn.sigmoid(z[:, D_MODEL:])

    merged = (gt_ref[:, 0:D_MODEL].astype(F32) * attn
              + gt_ref[:, D_MODEL:2 * D_MODEL].astype(F32) * pool
              + gt_ref[:, 2 * D_MODEL:3 * D_MODEL].astype(F32) * ssm)
    mix = jnp.dot(merged.astype(BF16), wo_ref[...], preferred_element_type=F32)
    x1 = x_ref[...] + mod[2:3] * mix
    x1_ref[...] = x1
    h2 = _rmsnorm_mod(x1, g2_ref[...], mod[4:5], mod[3:4])
    _store_row_tiles(h2_ref, h2)

    picks = _route(h2, wr_ref, br_ref)

    @pl.when(i == 0)
    def _():
        carry_sc[...] = jnp.zeros(carry_sc.shape, F32)

    sel = picks[0][2]
    for _, _, oh in picks[1:]:
        sel = sel | oh
    sel_f = sel.astype(F32)
    ranks = jnp.dot(sel_f.astype(BF16), triu_ref[...], preferred_element_type=F32) + carry_sc[...]
    carry = carry_sc[...] + jnp.sum(sel_f, axis=1, keepdims=True)
    carry_sc[...] = carry
    cnt_ref[...] = carry

    idx_ref[...] = jnp.concatenate([ik for ik, _, _ in picks], axis=0)
    gate_ref[...] = jnp.concatenate([gk for _, gk, _ in picks], axis=0)
    rank_ref[...] = jnp.concatenate(
        [jnp.sum(jnp.where(oh, ranks, 0.0), axis=0, keepdims=True).astype(I32) for _, _, oh in picks], axis=0)


def _post(x, attn, u_pool, y_ssm, gates, mod_l, wts, meta, tm):
    t = x.shape[0]
    nh = t // POOL_HALO
    hb = tm // POOL_HALO
    npf = 5
    const = lambda i, *_: (0, 0)
    row = lambda i, *_: (i, 0)
    col = lambda i, *_: (0, i)
    grid_spec = pltpu.PrefetchScalarGridSpec(
        num_scalar_prefetch=npf,
        grid=(t // tm,),
        in_specs=[
            pl.BlockSpec((tm, D_MODEL), row),
            pl.BlockSpec((tm, ATTN_DIM), row),
            pl.BlockSpec((POOL_HALO, POOL_DIM), lambda i, *_: (jnp.maximum(i * hb - 1, 0), 0)),
            pl.BlockSpec((tm, POOL_DIM), row),
            pl.BlockSpec((POOL_HALO, POOL_DIM), lambda i, *_: (jnp.minimum((i + 1) * hb, nh - 1), 0)),
            pl.BlockSpec((tm, SSM_DIM), row),
            pl.BlockSpec((tm, 3 * D_MODEL), row),
            pl.BlockSpec((1, N_ADA, D_MODEL), lambda i, ts, *_: (ts[i], 0, 0)),
            pl.BlockSpec((ATTN_DIM, D_MODEL), const),
            pl.BlockSpec((POOL_DIM, POOL_DIM), const),
            pl.BlockSpec((1, POOL_DIM), const),
            pl.BlockSpec((POOL_DIM, D_MODEL), const),
            pl.BlockSpec((SSM_DIM, 2 * D_MODEL), const),
            pl.BlockSpec((D_MODEL, D_MODEL), const),
            pl.BlockSpec((1, D_MODEL), const),
            pl.BlockSpec((2, N_EXPERTS, D_MODEL), lambda i, *_: (0, 0, 0)),
            pl.BlockSpec((N_EXPERTS, 1), const),
            pl.BlockSpec((tm, tm), const),
        ],
        out_specs=[
            pl.BlockSpec((tm, D_MODEL), row),
            pl.BlockSpec((tm * ROW_TILE, LANES), row),
            pl.BlockSpec((TOP_K, tm), col),
            pl.BlockSpec((TOP_K, tm), col),
            pl.BlockSpec((TOP_K, tm), col),
            pl.BlockSpec((N_EXPERTS, 1), const),
        ],
        scratch_shapes=[pltpu.VMEM((tm + 2 * POOL_HALO, POOL_DIM), F32), pltpu.VMEM((N_EXPERTS, 1), F32)],
    )
    return pl.pallas_call(
        functools.partial(_post_kernel, tm=tm),
        grid_spec=grid_spec,
        out_shape=[
            jax.ShapeDtypeStruct((t, D_MODEL), F32),
            jax.ShapeDtypeStruct((t * ROW_TILE, LANES), U32),
            jax.ShapeDtypeStruct((TOP_K, t), I32),
            jax.ShapeDtypeStruct((TOP_K, t), F32),
            jax.ShapeDtypeStruct((TOP_K, t), I32),
            jax.ShapeDtypeStruct((N_EXPERTS, 1), F32),
        ],
        compiler_params=_cparams(("arbitrary",)),
        name="mixer_merge_route",
    )(*meta, x, attn, u_pool, u_pool, u_pool, y_ssm, gates, mod_l, *wts)


def _dispatch_kernel(lb_ref, has_ref, d_ref, h_hbm, x_hbm, zero_sc, h_sc, sem_in, sem, zsem, *, tn, bm):
    i = pl.program_id(0)
    n = pl.num_programs(0)
    n_copies = tn * TOP_K
    tile_rows = tn * ROW_TILE

    def slot_rows(ref, slot):
        return ref.at[pl.ds(slot * ROW_TILE, ROW_TILE), :]

    def tile_load(tile, s):
        return pltpu.make_async_copy(h_hbm.at[pl.ds(tile * tile_rows, tile_rows), :], h_sc.at[s], sem_in.at[s])

    def pad_loop(act):
        def expert(e, c):
            @pl.when(has_ref[e] == 1)
            def _():
                act(pltpu.make_async_copy(zero_sc, x_hbm.at[pl.ds(lb_ref[e] * ROW_TILE, bm * ROW_TILE), :], zsem))
            return c
        lax.fori_loop(0, N_EXPERTS, expert, 0)

    @pl.when(i == 0)
    def _():
        zero_sc[...] = jnp.zeros(zero_sc.shape, U32)
        pad_loop(lambda cp: cp.start())
        pad_loop(lambda cp: cp.wait())
        tile_load(0, 0).start()

    cur = i % 3

    @pl.when(i + 1 < n)
    def _():
        tile_load(i + 1, (i + 1) % 3).start()

    tile_load(i, cur).wait()

    def body(tok, c):
        src = slot_rows(h_sc.at[cur], tok)
        for k in range(TOP_K):
            pltpu.make_async_copy(src, slot_rows(x_hbm, d_ref[0, 0, tok * TOP_K + k]),
                                  sem.at[cur]).start(priority=k % DMA_PRIORITIES)
        return c
    lax.fori_loop(0, tn, body, 0, unroll=2)

    def wait_tile(s):
        whole = x_hbm.at[pl.ds(0, n_copies * ROW_TILE), :]
        pltpu.make_async_copy(whole, whole, sem.at[s]).wait()

    @pl.when(i > 0)
    def _():
        wait_tile((i + 2) % 3)

    @pl.when(i == n - 1)
    def _():
        wait_tile(cur)


def _dispatch(h2t, dest3, last_block, has_rows, n_rows, tn, bm):
    n = dest3.shape[0]
    grid_spec = pltpu.PrefetchScalarGridSpec(
        num_scalar_prefetch=2,
        grid=(n,),
        in_specs=[
            pl.BlockSpec((1, 1, tn * TOP_K), lambda i, lo, hi: (i, 0, 0), memory_space=pltpu.SMEM),
            pl.BlockSpec(memory_space=pl.ANY),
        ],
        out_specs=pl.BlockSpec(memory_space=pl.ANY),
        scratch_shapes=[pltpu.VMEM((bm * ROW_TILE, LANES), U32), pltpu.VMEM((3, tn * ROW_TILE, LANES), U32),
                        pltpu.SemaphoreType.DMA((3,)), pltpu.SemaphoreType.DMA((3,)),
                        pltpu.SemaphoreType.DMA(())],
    )
    return pl.pallas_call(
        functools.partial(_dispatch_kernel, tn=tn, bm=bm),
        grid_spec=grid_spec,
        out_shape=jax.ShapeDtypeStruct((n_rows * ROW_TILE, LANES), U32),
        compiler_params=_cparams(("arbitrary",)),
        name="moe_dispatch",
    )(last_block, has_rows, dest3, h2t)


def _expert_kernel(be_ref, nu_ref, x_ref, wg_ref, wu_ref, wd_ref, y_ref, wg_sc, wu_sc, wd_sc, *, bm):
    b = pl.program_id(0)
    n_used = nu_ref[0]

    @pl.when(b < n_used)
    def _():
        @pl.when((b == 0) | (be_ref[b] != be_ref[jnp.maximum(b - 1, 0)]))
        def _():
            wg_sc[...] = wg_ref[0, 0].astype(BF16)
            wu_sc[...] = wu_ref[0, 0].astype(BF16)
            wd_sc[...] = wd_ref[0, 0].astype(BF16)

        x = _load_row_tiles(x_ref, 0, bm).astype(BF16)
        g = jnp.dot(x, wg_sc[...], preferred_element_type=F32)
        u = jnp.dot(x, wu_sc[...], preferred_element_type=F32)
        hmid = (g * jax.nn.sigmoid(g) * u).astype(BF16)
        _store_row_tiles(y_ref, jnp.dot(hmid, wd_sc[...], preferred_element_type=F32))

    @pl.when(b >= n_used)
    def _():
        y_ref[...] = jnp.zeros(y_ref.shape, U32)


def _experts(x_buf, block_expert, n_used, w_gate, w_up, w_down, layer, bm):
    nb = x_buf.shape[0] // (bm * ROW_TILE)
    grid_spec = pltpu.PrefetchScalarGridSpec(
        num_scalar_prefetch=2,
        grid=(nb,),
        in_specs=[
            pl.BlockSpec((bm * ROW_TILE, LANES), lambda b, be, nu: (jnp.minimum(b, nu[0] - 1), 0)),
            pl.BlockSpec((1, 1, D_MODEL, D_EXPERT), lambda b, be, nu: (layer, be[b], 0, 0)),
            pl.BlockSpec((1, 1, D_MODEL, D_EXPERT), lambda b, be, nu: (layer, be[b], 0, 0)),
            pl.BlockSpec((1, 1, D_EXPERT, D_MODEL), lambda b, be, nu: (layer, be[b], 0, 0)),
        ],
        out_specs=pl.BlockSpec((bm * ROW_TILE, LANES), lambda b, be, nu: (b, 0)),
        scratch_shapes=[
            pltpu.VMEM((D_MODEL, D_EXPERT), BF16),
            pltpu.VMEM((D_MODEL, D_EXPERT), BF16),
            pltpu.VMEM((D_EXPERT, D_MODEL), BF16),
        ],
    )
    return pl.pallas_call(
        functools.partial(_expert_kernel, bm=bm),
        grid_spec=grid_spec,
        out_shape=jax.ShapeDtypeStruct((nb * bm * ROW_TILE, LANES), U32),
        compiler_params=_cparams(("arbitrary",)),
        name="routed_experts",
    )(block_expert, n_used, x_buf, w_gate, w_up, w_down)


def _combine_kernel(ts_ref, d_ref, dn_ref, y_hbm, x1_ref, h2_ref, gate_ref, mod_ref, wsg_ref, wsu_ref, wsd_ref,
                    o_ref, yg_sc, sem, *, tc):
    del ts_ref
    i = pl.program_id(0)
    n = pl.num_programs(0)
    slot = i % 2
    n_copies = tc * TOP_K

    def row_copy(src, s, row):
        return pltpu.make_async_copy(y_hbm.at[pl.ds(src * ROW_TILE, ROW_TILE), :],
                                     yg_sc.at[s, pl.ds(row * ROW_TILE, ROW_TILE), :], sem.at[s])

    def issue(dest_smem, s):
        def body(tok, c):
            for k in range(TOP_K):
                row_copy(dest_smem[0, 0, tok * TOP_K + k], s, k * tc + tok).start(priority=k % DMA_PRIORITIES)
            return c
        lax.fori_loop(0, tc, body, 0, unroll=2)

    @pl.when(i == 0)
    def _():
        issue(d_ref, 0)

    @pl.when(i + 1 < n)
    def _():
        issue(dn_ref, 1 - slot)

    pltpu.make_async_copy(y_hbm.at[pl.ds(0, n_copies * ROW_TILE), :], yg_sc.at[slot], sem.at[slot]).wait()

    gate = gate_ref[...]
    yg = yg_sc.at[slot]
    routed = gate[:, 0:1] * _load_row_tiles(yg, 0, tc)
    for k in range(1, TOP_K):
        routed = routed + gate[:, k:k + 1] * _load_row_tiles(yg, k * tc, tc)
    hb = _load_row_tiles(h2_ref, 0, tc).astype(BF16)
    g = jnp.dot(hb, wsg_ref[...], preferred_element_type=F32)
    u = jnp.dot(hb, wsu_ref[...], preferred_element_type=F32)
    shared = jnp.dot((g * jax.nn.sigmoid(g) * u).astype(BF16), wsd_ref[...], preferred_element_type=F32)
    o_ref[...] = x1_ref[...] + mod_ref[0][5:6] * (routed + shared)


def _combine(y_buf, dest3, x1, h2, gate, mod_l, ws_gate, ws_up, ws_down, tile_seq, tc):
    t = x1.shape[0]
    n = t // tc
    d_sh = ws_gate.shape[1]
    const = lambda i, ts: (0, 0)
    row = lambda i, ts: (i, 0)
    grid_spec = pltpu.PrefetchScalarGridSpec(
        num_scalar_prefetch=1,
        grid=(n,),
        in_specs=[
            pl.BlockSpec((1, 1, tc * TOP_K), lambda i, ts: (i, 0, 0), memory_space=pltpu.SMEM),
            pl.BlockSpec((1, 1, tc * TOP_K), lambda i, ts: (jnp.minimum(i + 1, n - 1), 0, 0),
                         memory_space=pltpu.SMEM),
            pl.BlockSpec(memory_space=pl.ANY),
            pl.BlockSpec((tc, D_MODEL), row),
            pl.BlockSpec((tc * ROW_TILE, LANES), row),
            pl.BlockSpec((tc, TOP_K), row),
            pl.BlockSpec((1, N_ADA, D_MODEL), lambda i, ts: (ts[i], 0, 0)),
            pl.BlockSpec((D_MODEL, d_sh), const),
            pl.BlockSpec((D_MODEL, d_sh), const),
            pl.BlockSpec((d_sh, D_MODEL), const),
        ],
        out_specs=pl.BlockSpec((tc, D_MODEL), row),
        scratch_shapes=[pltpu.VMEM((2, tc * TOP_K * ROW_TILE, LANES), U32), pltpu.SemaphoreType.DMA((2,))],
    )
    return pl.pallas_call(
        functools.partial(_combine_kernel, tc=tc),
        grid_spec=grid_spec,
        out_shape=jax.ShapeDtypeStruct((t, D_MODEL), F32),
        compiler_params=_cparams(("arbitrary",)),
        name="moe_combine",
    )(tile_seq, dest3, dest3, y_buf, x1, h2, gate, mod_l, ws_gate, ws_up, ws_down)


def _tile_meta(seq_lens, tile):
    seq, first, last, pos0, slen = [], [], [], [], []
    for s, n in enumerate(seq_lens):
        nt = n // tile
        for j in range(nt):
            seq.append(s)
            first.append(int(j == 0))
            last.append(int(j == nt - 1))
            pos0.append(j * tile)
            slen.append(n)
    return tuple(jnp.asarray(np.asarray(a, np.int32)) for a in (seq, first, last, pos0, slen))


def _scan_meta(seq_lens, tr):
    bwd, first = [], []
    base = 0
    for n in seq_lens:
        nt = n // SSM_CHUNK // tr
        for j in range(nt):
            bwd.append(base + nt - 1 - j)
            first.append(int(j == 0))
        base += nt
    return jnp.asarray(np.asarray(bwd, np.int32)), jnp.asarray(np.asarray(first, np.int32))


def _rope_tables(seq_lens):
    quarter = HEAD_DIM // 4
    freqs = ROPE_THETA ** (-jnp.arange(quarter, dtype=F32) / quarter)
    sign = jnp.tile(jnp.concatenate([-jnp.ones((quarter,), F32), jnp.ones((quarter,), F32)]), 2)
    cos_l, sin_l = [], []
    cache = {}
    for n in seq_lens:
        if n not in cache:
            pos = jnp.arange(n)
            ar = (pos // GRID_W).astype(F32)[:, None] * freqs
            ac = (pos % GRID_W).astype(F32)[:, None] * freqs
            ang = jnp.concatenate([ar, ar, ac, ac], axis=-1)
            cache[n] = (jnp.tile(jnp.cos(ang), (1, 2)), jnp.tile(jnp.sin(ang) * sign, (1, 2)))
        cos_l.append(cache[n][0])
        sin_l.append(cache[n][1])
    return jnp.concatenate(cos_l, axis=0), jnp.concatenate(sin_l, axis=0)


def _split_bf16(w):
    hi = w.astype(BF16)
    return jnp.stack([hi, (w - hi.astype(F32)).astype(BF16)], axis=0)


def _block_diag_ones(n, blk):
    r = np.arange(n) // blk
    return jnp.asarray((r[:, None] == r[None, :]).astype(np.float32)).astype(BF16)


def _dest_kernel(idx_ref, rank_ref, ps_ref, d_ref):
    idx = idx_ref[...]
    row_e = lax.broadcasted_iota(I32, (N_EXPERTS, idx.shape[1]), 0)
    starts = [jnp.sum(jnp.where(row_e == idx[k:k + 1], ps_ref[...], 0.0), axis=0, keepdims=True)
              for k in range(TOP_K)]
    d_ref[...] = rank_ref[...] + jnp.concatenate(starts, axis=0).astype(I32)


def _dest_slots(idx, rank, pstart):
    t = idx.shape[1]
    tm = _pick(t, 1024)
    col = lambda i: (0, i)
    return pl.pallas_call(
        _dest_kernel,
        grid=(t // tm,),
        in_specs=[pl.BlockSpec((TOP_K, tm), col), pl.BlockSpec((TOP_K, tm), col),
                  pl.BlockSpec((N_EXPERTS, 1), lambda i: (0, 0))],
        out_specs=pl.BlockSpec((TOP_K, tm), col),
        out_shape=jax.ShapeDtypeStruct((TOP_K, t), I32),
        compiler_params=_cparams(("arbitrary",)),
        name="moe_dest_slots",
    )(idx, rank, pstart)


def _moe_plan(idx, rank, counts, t, bm, nb):
    cnt = counts.reshape(N_EXPERTS).astype(I32)
    padded = (cnt + bm - 1) // bm * bm
    pend = jnp.cumsum(padded)
    pstart = pend - padded
    n_used = (pend[-1] // bm).astype(I32).reshape(1)
    starts = jnp.arange(nb, dtype=I32) * bm
    block_expert = jnp.minimum(
        jnp.sum((pend[None, :] <= starts[:, None]).astype(I32), axis=1), N_EXPERTS - 1).astype(I32)
    dest = _dest_slots(idx, rank, pstart.astype(F32).reshape(N_EXPERTS, 1)).T
    return dest, (pend - bm).astype(I32), (cnt > 0).astype(I32), block_expert, n_used


def kernel(x_prompt, x_sample, c_prompt, c_sample, w_ada, b_ada, norm1_g, w_in, q_norm_g, k_norm_g, w_attn_o, pool_w, pool_scale, w_pool_o, ssm_a_re, ssm_a_im, ssm_log_dt, ssm_b_re, ssm_b_im, ssm_c_re, ssm_c_im, ssm_d, w_glu, w_out, norm2_g, w_router, b_router, w_exp_gate, w_exp_up, w_exp_down, w_sh_gate, w_sh_up, w_sh_down):
    b1, l1, d = x_prompt.shape
    b2, l2, _ = x_sample.shape
    depth = w_in.shape[0]
    assert d == D_MODEL and b1 + b2 <= MOD_ROWS
    seq_lens = [l1] * b1 + [l2] * b2
    t = b1 * l1 + b2 * l2
    lmin = min(l1, l2)

    tm_in = _pick(lmin, 512)
    tm_post = _pick(lmin, 256)
    tc_comb = _pick(lmin, 128)
    tr_scan = _pick(lmin // SSM_CHUNK, 256)
    bm = 512
    nb = -(-(t * TOP_K + N_EXPERTS * (bm - 1)) // bm)

    x = jnp.concatenate([x_prompt.reshape(b1 * l1, d), x_sample.reshape(b2 * l2, d)], axis=0)
    c_all = jnp.concatenate([c_prompt, c_sample, jnp.zeros((MOD_ROWS - b1 - b2, d), F32)], axis=0)
    mod = _modulation(c_all, w_ada, b_ada).reshape(depth, MOD_ROWS, N_ADA, d)

    cos_t, sin_t = _rope_tables(seq_lens)
    ones_bd = _block_diag_ones(ATTN_DIM, HEAD_DIM)
    triu = jnp.asarray(np.triu(np.ones((tm_post, tm_post), np.float32), 1)).astype(BF16)
    meta_in = _tile_meta(seq_lens, tm_in)
    meta_post = _tile_meta(seq_lens, tm_post)
    meta_comb = _tile_meta(seq_lens, tc_comb)
    bwd_tile, first_tile = _scan_meta(seq_lens, tr_scan)

    for l in range(depth):
        q8, k2, v2, u_pool, u_ssm, gates = _inproj(
            x, mod[l], norm1_g[l].reshape(1, d), w_in[l].astype(BF16),
            jnp.tile(q_norm_g[l], N_Q_HEADS).reshape(1, ATTN_DIM),
            jnp.tile(k_norm_g[l], N_KV_HEADS).reshape(1, KV_DIM),
            cos_t, sin_t, ones_bd, meta_in[0], tm_in)
        attn = jnp.concatenate(
            [_flash(q8, k2, v2, 0, b1, l1), _flash(q8, k2, v2, b1 * l1, b2, l2)], axis=0)
        tables = _ssm_tables(ssm_a_re[l], ssm_a_im[l], ssm_log_dt[l], ssm_b_re[l], ssm_b_im[l],
                             ssm_c_re[l], ssm_c_im[l], ssm_d[l])
        y_ssm = _bidir_ssm(u_ssm, tables, bwd_tile, first_tile, tr_scan)

        pool_bd = jax.scipy.linalg.block_diag(*[pool_w[l, g] for g in range(len(POOL_WINDOWS))])
        wts = (w_attn_o[l].astype(BF16), pool_bd.astype(BF16), pool_scale[l].reshape(1, POOL_DIM),
               w_pool_o[l].astype(BF16), w_glu[l].astype(BF16), w_out[l].astype(BF16),
               norm2_g[l].reshape(1, d), _split_bf16(w_router[l].T), b_router[l].reshape(N_EXPERTS, 1), triu)
        x1, h2, idx, gate, rank, counts = _post(x, attn, u_pool, y_ssm, gates, mod[l], wts, meta_post, tm_post)

        dest, last_block, has_rows, block_expert, n_used = _moe_plan(idx, rank, counts, t, bm, nb)
        dest3 = dest.reshape(t // tc_comb, 1, tc_comb * TOP_K)
        x_buf = _dispatch(h2, dest3, last_block, has_rows, nb * bm, tc_comb, bm)
        y_buf = _experts(x_buf, block_expert, n_used, w_exp_gate, w_exp_up, w_exp_down, l, bm)
        gate = gate.T
        x = _combine(y_buf, dest3, x1, h2, gate, mod[l],
                     w_sh_gate[l].astype(BF16), w_sh_up[l].astype(BF16), w_sh_down[l].astype(BF16),
                     meta_comb[0], tc_comb)

    return (x[:b1 * l1].reshape(b1, l1, d), x[b1 * l1:].reshape(b2, l2, d))
```

```python
import functools
import math

import jax
import jax.numpy as jnp
import numpy as np
from jax import lax
from jax.experimental import pallas as pl
from jax.experimental.pallas import tpu as pltpu

F32 = jnp.float32
BF16 = jnp.bfloat16
I32 = jnp.int32
HIGHEST = lax.Precision.HIGHEST

D_MODEL = 1024
GRID_W = 64
HEAD_DIM = 64
N_Q_HEADS = 8
N_KV_HEADS = 2
Q_PER_KV = N_Q_HEADS // N_KV_HEADS
ATTN_DIM = N_Q_HEADS * HEAD_DIM
KV_DIM = N_KV_HEADS * HEAD_DIM
ROPE_THETA = 10000.0
POOL_WINDOWS = (2, 4, 8, 16)
POOL_DIM = 256
POOL_GROUP = 64
POOL_HALO = 8
SSM_DIM = 256
SSM_P = 16
SSM_G = 16
SSM_N = 64
SSM_CHUNK = 16
SSM_ROW = SSM_CHUNK * SSM_P
SSM_STATE = 2 * SSM_N
N_EXPERTS = 256
TOP_K = 8
N_EXPERT_GROUPS = 8
GROUP_SIZE = N_EXPERTS // N_EXPERT_GROUPS
TOPK_GROUPS = 4
D_EXPERT = 256
ROUTED_SCALE = 2.5
N_ADA = 6
EPS = 1e-6
IN_DIM = ATTN_DIM + 2 * KV_DIM + POOL_DIM + SSM_DIM + 3 * D_MODEL
OFF_K = ATTN_DIM
OFF_V = OFF_K + KV_DIM
OFF_POOL = OFF_V + KV_DIM
OFF_SSM = OFF_POOL + POOL_DIM
OFF_GATES = OFF_SSM + SSM_DIM
MOD_ROWS = 8
LOG2_E = math.log2(math.e)
VT_ROWS = HEAD_DIM + 16

V7X_VMEM_BYTES = 64 * 1024 * 1024
VMEM_LIMIT = V7X_VMEM_BYTES - 8 * 1024 * 1024
LANES = 128


def _cparams(sem):
    return pltpu.CompilerParams(dimension_semantics=sem, vmem_limit_bytes=VMEM_LIMIT)


def _pick(n, pref):
    t = min(n, pref)
    while n % t:
        t //= 2
    return t


U32 = jnp.uint32
ROW_TILE = D_MODEL // 2 // LANES
DMA_PRIORITIES = 2


def _store_row_tiles(ref, x):
    n = x.shape[0]
    half = D_MODEL // 2
    lo = pltpu.bitcast(x[:, :half].astype(BF16).astype(F32), U32)
    hi = pltpu.bitcast(x[:, half:].astype(BF16).astype(F32), U32)
    w = (lo >> 16) | (hi & jnp.uint32(0xFFFF0000))
    for s in range(ROW_TILE):
        ref[pl.ds(s, n, stride=ROW_TILE), :] = w[:, s * LANES:(s + 1) * LANES]


def _load_row_tiles(ref, first_row, n):
    w = jnp.concatenate(
        [ref[pl.ds(first_row * ROW_TILE + s, n, stride=ROW_TILE), :] for s in range(ROW_TILE)], axis=1)
    lo = pltpu.bitcast(w << 16, F32)
    hi = pltpu.bitcast(w & jnp.uint32(0xFFFF0000), F32)
    return jnp.concatenate([lo, hi], axis=1)


def _mod_kernel(c_ref, w_ref, b_ref, o_ref):
    c = c_ref[...]
    a = c * jax.nn.sigmoid(c)
    o_ref[0] = jnp.dot(a, w_ref[0], preferred_element_type=F32, precision=HIGHEST) + b_ref[0]


def _modulation(c_all, w_ada, b_ada):
    depth, d, n = w_ada.shape
    bn = _pick(n, 1536)
    return pl.pallas_call(
        _mod_kernel,
        grid=(depth, n // bn),
        in_specs=[
            pl.BlockSpec((MOD_ROWS, d), lambda l, j: (0, 0)),
            pl.BlockSpec((1, d, bn), lambda l, j: (l, 0, j)),
            pl.BlockSpec((1, 1, bn), lambda l, j: (l, 0, j)),
        ],
        out_specs=pl.BlockSpec((1, MOD_ROWS, bn), lambda l, j: (l, 0, j)),
        out_shape=jax.ShapeDtypeStruct((depth, MOD_ROWS, n), F32),
        compiler_params=_cparams(("arbitrary", "arbitrary")),
        name="adaln_mod",
    )(c_all, w_ada, b_ada.reshape(depth, 1, n))


def _rmsnorm_mod(x, g, scale, shift):
    y = x * lax.rsqrt(jnp.mean(x * x, axis=-1, keepdims=True) + EPS)
    return (y * g) * (1.0 + scale) + shift


def _head_norm_rope(z, gain, cos, sin_signed, ones_bd):
    z2 = z * z
    hi = z2.astype(BF16)
    lo = (z2 - hi.astype(F32)).astype(BF16)
    ss = (jnp.dot(hi, ones_bd, preferred_element_type=F32)
          + jnp.dot(lo, ones_bd, preferred_element_type=F32))
    y = (z * lax.rsqrt(ss * (1.0 / HEAD_DIM) + EPS)) * gain
    w = z.shape[1]
    quarter = HEAD_DIM // 4
    from_right = pltpu.roll(y, w - quarter, 1)
    from_left = pltpu.roll(y, quarter, 1)
    lane = lax.broadcasted_iota(I32, y.shape, 1)
    rot = jnp.where((lane & quarter) == 0, from_right, from_left)
    return y * cos + rot * sin_signed


def _inproj_kernel(ts_ref, x_ref, mod_ref, g_ref, w_ref, gq_ref, gk_ref, cos_ref, sin_ref, ones_ref,
                   q_ref, k_ref, v_ref, up_ref, us_ref, gt_ref):
    del ts_ref
    x = x_ref[...]
    mod = mod_ref[0]
    h = _rmsnorm_mod(x, g_ref[...], mod[1:2], mod[0:1]).astype(BF16)

    cos = cos_ref[...]
    sin = sin_ref[...]
    zq = jnp.dot(h, w_ref[:, 0:ATTN_DIM], preferred_element_type=F32)
    reps = ATTN_DIM // LANES
    yq = _head_norm_rope(zq, gq_ref[...], jnp.concatenate([cos] * reps, axis=1),
                         jnp.concatenate([sin] * reps, axis=1), ones_ref[...]) * (HEAD_DIM ** -0.5 * LOG2_E)
    for hh in range(N_Q_HEADS):
        q_ref[hh] = yq[:, hh * HEAD_DIM:(hh + 1) * HEAD_DIM].astype(BF16)

    zkv = jnp.dot(h, w_ref[:, OFF_K:OFF_POOL], preferred_element_type=F32)
    yk = _head_norm_rope(zkv[:, :KV_DIM], gk_ref[...], cos, sin, ones_ref[0:KV_DIM, 0:KV_DIM])
    zv = zkv[:, KV_DIM:]
    for j in range(N_KV_HEADS):
        k_ref[j] = yk[:, j * HEAD_DIM:(j + 1) * HEAD_DIM].astype(BF16)
        v_ref[j, 0:HEAD_DIM, :] = zv[:, j * HEAD_DIM:(j + 1) * HEAD_DIM].T.astype(BF16)
        v_ref[j, HEAD_DIM:VT_ROWS, :] = jnp.ones((VT_ROWS - HEAD_DIM, zv.shape[0]), BF16)

    zps = jnp.dot(h, w_ref[:, OFF_POOL:OFF_GATES], preferred_element_type=F32)
    up_ref[...] = zps[:, :POOL_DIM]
    us_ref[...] = zps[:, POOL_DIM:]
    for c in range(3):
        zg = jnp.dot(h, w_ref[:, OFF_GATES + c * D_MODEL:OFF_GATES + (c + 1) * D_MODEL],
                     preferred_element_type=F32)
        gt_ref[:, c * D_MODEL:(c + 1) * D_MODEL] = jax.nn.sigmoid(zg).astype(BF16)


def _inproj(x, mod_l, g1, w_in_bf, gq_t, gk_t, cos_t, sin_t, ones_bd, tile_seq, tm):
    t = x.shape[0]
    const = lambda i, ts: (0, 0)
    row = lambda i, ts: (i, 0)
    grid_spec = pltpu.PrefetchScalarGridSpec(
        num_scalar_prefetch=1,
        grid=(t // tm,),
        in_specs=[
            pl.BlockSpec((tm, D_MODEL), row),
            pl.BlockSpec((1, N_ADA, D_MODEL), lambda i, ts: (ts[i], 0, 0)),
            pl.BlockSpec((1, D_MODEL), const),
            pl.BlockSpec((D_MODEL, IN_DIM), const),
            pl.BlockSpec((1, ATTN_DIM), const),
            pl.BlockSpec((1, KV_DIM), const),
            pl.BlockSpec((tm, LANES), row),
            pl.BlockSpec((tm, LANES), row),
            pl.BlockSpec((ATTN_DIM, ATTN_DIM), const),
        ],
        out_specs=[
            pl.BlockSpec((N_Q_HEADS, tm, HEAD_DIM), lambda i, ts: (0, i, 0)),
            pl.BlockSpec((N_KV_HEADS, tm, HEAD_DIM), lambda i, ts: (0, i, 0)),
            pl.BlockSpec((N_KV_HEADS, VT_ROWS, tm), lambda i, ts: (0, 0, i)),
            pl.BlockSpec((tm, POOL_DIM), row),
            pl.BlockSpec((tm, SSM_DIM), row),
            pl.BlockSpec((tm, 3 * D_MODEL), row),
        ],
    )
    return pl.pallas_call(
        _inproj_kernel,
        grid_spec=grid_spec,
        out_shape=[
            jax.ShapeDtypeStruct((N_Q_HEADS, t, HEAD_DIM), BF16),
            jax.ShapeDtypeStruct((N_KV_HEADS, t, HEAD_DIM), BF16),
            jax.ShapeDtypeStruct((N_KV_HEADS, VT_ROWS, t), BF16),
            jax.ShapeDtypeStruct((t, POOL_DIM), F32),
            jax.ShapeDtypeStruct((t, SSM_DIM), F32),
            jax.ShapeDtypeStruct((t, 3 * D_MODEL), BF16),
        ],
        compiler_params=_cparams(("arbitrary",)),
        name="norm1_inproj",
    )(tile_seq, x, mod_l, g1, w_in_bf, gq_t, gk_t, cos_t, sin_t, ones_bd)


def _flash_kernel(q_ref, k_ref, vt_ref, o_ref, m_sc, acc_sc):
    kb = pl.program_id(3)

    @pl.when(kb == 0)
    def _():
        m_sc[...] = jnp.full(m_sc.shape, -jnp.inf, F32)
        acc_sc[...] = jnp.zeros(acc_sc.shape, F32)

    k = k_ref[0]
    vt = vt_ref[0]
    heads = range(Q_PER_KV)
    qs = [q_ref[hh] for hh in heads]
    m_prev = [m_sc[hh] for hh in heads]
    acc_prev = [acc_sc[hh] for hh in heads]

    def scores(hh):
        return lax.dot_general(k, qs[hh], (((1,), (1,)), ((), ())), preferred_element_type=F32)

    ahead = 2
    s = {hh: scores(hh) for hh in range(ahead)}
    new = []
    for hh in heads:
        if hh + ahead < Q_PER_KV:
            s[hh + ahead] = scores(hh + ahead)
        m_new = jnp.maximum(m_prev[hh], jnp.max(s[hh], axis=0, keepdims=True))
        alpha = jnp.exp2(m_prev[hh] - m_new)
        p = jnp.exp2(s.pop(hh) - m_new).astype(BF16)
        acc_new = acc_prev[hh] * alpha + jnp.dot(vt, p, preferred_element_type=F32)
        new.append((m_new, acc_new))
    for hh, (m_new, acc_new) in enumerate(new):
        m_sc[hh] = m_new
        acc_sc[hh] = acc_new

    @pl.when(kb == pl.num_programs(3) - 1)
    def _():
        for hh in range(Q_PER_KV):
            acc = acc_sc[hh]
            o = acc[:HEAD_DIM] / acc[HEAD_DIM:HEAD_DIM + 1]
            o_ref[:, hh * HEAD_DIM:(hh + 1) * HEAD_DIM] = o.T.astype(BF16)


def _flash(q8, k2, vt2, off, bsz, seq):
    tq = _pick(seq, 512)
    tk = _pick(seq, 4096)
    nq, nk = seq // tq, seq // tk
    assert off % tq == 0 and off % tk == 0
    oq, ok = off // tq, off // tk
    return pl.pallas_call(
        _flash_kernel,
        grid=(N_KV_HEADS, bsz, nq, nk),
        in_specs=[
            pl.BlockSpec((Q_PER_KV, tq, HEAD_DIM), lambda j, b, i, kb: (j, oq + b * nq + i, 0)),
            pl.BlockSpec((1, tk, HEAD_DIM), lambda j, b, i, kb: (j, ok + b * nk + kb, 0)),
            pl.BlockSpec((1, VT_ROWS, tk), lambda j, b, i, kb: (j, 0, ok + b * nk + kb)),
        ],
        out_specs=pl.BlockSpec((tq, Q_PER_KV * HEAD_DIM), lambda j, b, i, kb: (b * nq + i, j)),
        out_shape=jax.ShapeDtypeStruct((bsz * seq, ATTN_DIM), BF16),
        scratch_shapes=[
            pltpu.VMEM((Q_PER_KV, 1, tq), F32),
            pltpu.VMEM((Q_PER_KV, VT_ROWS, tq), F32),
        ],
        compiler_params=_cparams(("arbitrary", "arbitrary", "arbitrary", "arbitrary")),
        name="flash_gqa",
    )(q8, k2, vt2)


def _ssm_tables(a_re, a_im, log_dt, b_re, b_im, c_re, c_im, d_skip):
    tc = SSM_CHUNK
    a = lax.complex(a_re.astype(F32), a_im.astype(F32))
    dt = jnp.exp(log_dt.astype(F32))[..., None]
    adt = a * dt
    a_bar = jnp.exp(adt)
    b_bar = ((a_bar - 1.0) / a)[..., None] * lax.complex(b_re.astype(F32), b_im.astype(F32))
    c_mat = lax.complex(c_re.astype(F32), c_im.astype(F32))
    taus = jnp.arange(tc + 1, dtype=F32)
    pw = jnp.exp(adt[None] * taus[:, None, None, None])

    kern = jnp.real(jnp.einsum('dgpn,tdgn,dgnq->dgtpq', c_mat, pw[:tc], b_bar))
    s_idx = jnp.arange(tc)[:, None]
    t_idx = jnp.arange(tc)[None, :]
    lag_f = jnp.clip(t_idx - s_idx, 0, tc - 1)
    lag_b = jnp.clip(s_idx - t_idx, 0, tc - 1)
    kf = jnp.where((t_idx >= s_idx)[None, :, :, None, None], kern[0][:, lag_f], 0.0)
    kb = jnp.where((s_idx >= t_idx)[None, :, :, None, None], kern[1][:, lag_b], 0.0)
    w_loc = (kf + kb).transpose(0, 1, 4, 2, 3).reshape(SSM_G, SSM_ROW, SSM_ROW)
    w_loc = w_loc + jnp.eye(SSM_ROW, dtype=F32)[None] * jnp.tile(
        d_skip.astype(F32).reshape(SSM_G, 1, SSM_P), (1, tc, 1)).reshape(SSM_G, 1, SSM_ROW)

    def ri(z, axis):
        return jnp.concatenate([jnp.real(z), jnp.imag(z)], axis=axis)

    pf = jnp.einsum('sgn,gnq->gsqn', pw[:tc][::-1, 0], b_bar[0])
    pb = jnp.einsum('sgn,gnq->gsqn', pw[:tc, 1], b_bar[1])
    p_st = jnp.concatenate([ri(pf, -1), ri(pb, -1)], axis=-1).reshape(SSM_G, SSM_ROW, 2 * SSM_STATE)
    qf = jnp.einsum('gpn,tgn->gntp', c_mat[0], pw[1:tc + 1, 0])
    qb = jnp.einsum('gpn,tgn->gntp', c_mat[1], pw[1:tc + 1, 1][::-1])
    q_f = jnp.concatenate([jnp.real(qf), -jnp.imag(qf)], axis=1).reshape(SSM_G, SSM_STATE, SSM_ROW)
    q_b = jnp.concatenate([jnp.real(qb), -jnp.imag(qb)], axis=1).reshape(SSM_G, SSM_STATE, SSM_ROW)
    a16 = pw[tc]
    a1 = jnp.concatenate([jnp.real(a16), jnp.real(a16)], axis=-1).reshape(2, SSM_G * SSM_STATE)
    a2 = jnp.concatenate([-jnp.imag(a16), jnp.imag(a16)], axis=-1).reshape(2, SSM_G * SSM_STATE)
    a_mul = jnp.stack([a1[0], a2[0], a1[1], a2[1]], axis=0)
    return w_loc.astype(BF16), p_st.astype(BF16), q_f.astype(BF16), q_b.astype(BF16), a_mul


def _ssm_state_kernel(u_ref, p_ref, s_ref):
    s = jnp.dot(u_ref[0].astype(BF16), p_ref[0], preferred_element_type=F32)
    s_ref[0] = s[:, :SSM_STATE]
    s_ref[1] = s[:, SSM_STATE:]


def _ssm_states(ug, p_st, tc):
    g, nc, _ = ug.shape
    return pl.pallas_call(
        _ssm_state_kernel,
        grid=(g, nc // tc),
        in_specs=[
            pl.BlockSpec((1, tc, SSM_ROW), lambda gi, i: (gi, i, 0)),
            pl.BlockSpec((1, SSM_ROW, 2 * SSM_STATE), lambda gi, i: (gi, 0, 0)),
        ],
        out_specs=pl.BlockSpec((2, tc, SSM_STATE), lambda gi, i: (0, i, gi)),
        out_shape=jax.ShapeDtypeStruct((2, nc, g * SSM_STATE), F32),
        compiler_params=_cparams(("arbitrary", "arbitrary")),
        name="ssm_chunk_states",
    )(ug, p_st)


def _swap_halves(x):
    parts = [pltpu.roll(x[:, g * LANES:(g + 1) * LANES], LANES // 2, 1) for g in range(x.shape[1] // LANES)]
    return jnp.concatenate(parts, axis=1)


def _ssm_scan_kernel(bt_ref, first_ref, sf_ref, sb_ref, am_ref, xf_ref, zb_ref, st_sc, sfw_sc, sbw_sc, *, tr):
    n = pl.program_id(0)
    del bt_ref

    @pl.when(first_ref[n] == 1)
    def _():
        st_sc[...] = jnp.zeros(st_sc.shape, F32)

    sfw_sc[...] = _swap_halves(sf_ref[0])
    sbw_sc[...] = _swap_halves(sb_ref[0])
    a1f, a2f, a1b, a2b = am_ref[0:1], am_ref[1:2], am_ref[2:3], am_ref[3:4]

    def body(r, carry):
        xf, xfw, zb, zbw = carry
        xf_ref[pl.ds(r, 1), :] = xf
        sf = sf_ref[0, pl.ds(r, 1), :]
        sfw = sfw_sc[pl.ds(r, 1), :]
        nxf = a1f * xf + a2f * xfw + sf
        nxfw = a1f * xfw - a2f * xf + sfw
        rb = tr - 1 - r
        zb_ref[pl.ds(rb, 1), :] = zb
        sb = sb_ref[0, pl.ds(rb, 1), :]
        sbw = sbw_sc[pl.ds(rb, 1), :]
        nzb = a1b * zb + a2b * zbw + sb
        nzbw = a1b * zbw - a2b * zb + sbw
        return nxf, nxfw, nzb, nzbw

    init = (st_sc[0:1], st_sc[1:2], st_sc[2:3], st_sc[3:4])
    xf, xfw, zb, zbw = lax.fori_loop(0, tr, body, init)
    st_sc[0:1] = xf
    st_sc[1:2] = xfw
    st_sc[2:3] = zb
    st_sc[3:4] = zbw


def _ssm_scan(s_st, a_mul, bwd_tile, first_tile, tr):
    _, nc, w = s_st.shape
    grid_spec = pltpu.PrefetchScalarGridSpec(
        num_scalar_prefetch=2,
        grid=(nc // tr,),
        in_specs=[
            pl.BlockSpec((1, tr, w), lambda n, bt, ft: (0, n, 0)),
            pl.BlockSpec((1, tr, w), lambda n, bt, ft: (1, bt[n], 0)),
            pl.BlockSpec((4, w), lambda n, bt, ft: (0, 0)),
        ],
        out_specs=[
            pl.BlockSpec((tr, w), lambda n, bt, ft: (n, 0)),
            pl.BlockSpec((tr, w), lambda n, bt, ft: (bt[n], 0)),
        ],
        scratch_shapes=[pltpu.VMEM((4, w), F32), pltpu.VMEM((tr, w), F32), pltpu.VMEM((tr, w), F32)],
    )
    return pl.pallas_call(
        functools.partial(_ssm_scan_kernel, tr=tr),
        grid_spec=grid_spec,
        out_shape=[jax.ShapeDtypeStruct((nc, w), F32), jax.ShapeDtypeStruct((nc, w), F32)],
        compiler_params=_cparams(("arbitrary",)),
        name="ssm_chunk_scan",
    )(bwd_tile, first_tile, s_st, s_st, a_mul)


def _ssm_out_kernel(u_ref, xf_ref, zb_ref, w_ref, qf_ref, qb_ref, y_ref):
    y = jnp.dot(u_ref[0].astype(BF16), w_ref[0], preferred_element_type=F32)
    y += jnp.dot(xf_ref[...].astype(BF16), qf_ref[0], preferred_element_type=F32)
    y += jnp.dot(zb_ref[...].astype(BF16), qb_ref[0], preferred_element_type=F32)
    y_ref[0] = y


def _ssm_outputs(ug, xf, zb, w_loc, q_f, q_b, tc):
    g, nc, _ = ug.shape
    return pl.pallas_call(
        _ssm_out_kernel,
        grid=(g, nc // tc),
        in_specs=[
            pl.BlockSpec((1, tc, SSM_ROW), lambda gi, i: (gi, i, 0)),
            pl.BlockSpec((tc, SSM_STATE), lambda gi, i: (i, gi)),
            pl.BlockSpec((tc, SSM_STATE), lambda gi, i: (i, gi)),
            pl.BlockSpec((1, SSM_ROW, SSM_ROW), lambda gi, i: (gi, 0, 0)),
            pl.BlockSpec((1, SSM_STATE, SSM_ROW), lambda gi, i: (gi, 0, 0)),
            pl.BlockSpec((1, SSM_STATE, SSM_ROW), lambda gi, i: (gi, 0, 0)),
        ],
        out_specs=pl.BlockSpec((1, tc, SSM_ROW), lambda gi, i: (gi, i, 0)),
        out_shape=jax.ShapeDtypeStruct((g, nc, SSM_ROW), F32),
        compiler_params=_cparams(("arbitrary", "arbitrary")),
        name="ssm_chunk_outputs",
    )(ug, xf, zb, w_loc, q_f, q_b)


def _bidir_ssm(u_ssm, tables, bwd_tile, first_tile, tr):
    w_loc, p_st, q_f, q_b, a_mul = tables
    t = u_ssm.shape[0]
    nc = t // SSM_CHUNK
    ug = u_ssm.reshape(nc, SSM_CHUNK, SSM_G, SSM_P).transpose(2, 0, 1, 3).reshape(SSM_G, nc, SSM_ROW)
    tc = _pick(nc, 512)
    s_st = _ssm_states(ug, p_st, tc)
    xf, zb = _ssm_scan(s_st, a_mul, bwd_tile, first_tile, tr)
    yg = _ssm_outputs(ug, xf, zb, w_loc, q_f, q_b, tc)
    return yg.reshape(SSM_G, nc, SSM_CHUNK, SSM_P).transpose(1, 2, 0, 3).reshape(t, SSM_DIM)


def _pool_mixer(ext_sc, u, pos, seq_len, tm):
    halves = []
    for half in range(2):
        sums = []
        for w in POOL_WINDOWS[2 * half:2 * half + 2]:
            lo = w // 2
            hi = w - lo - 1
            acc = ext_sc[pl.ds(POOL_HALO - lo, tm), half * LANES:(half + 1) * LANES]
            for d in range(-lo + 1, hi + 1):
                acc = acc + ext_sc[pl.ds(POOL_HALO + d, tm), half * LANES:(half + 1) * LANES]
            cnt = (jnp.minimum(pos + hi, seq_len - 1) - jnp.maximum(pos - lo, 0) + 1).astype(F32)
            sums.append(acc / cnt)
        lane = lax.broadcasted_iota(I32, (tm, LANES), 1)
        halves.append(jnp.where(lane < POOL_GROUP, sums[0], sums[1]))
    return jnp.concatenate(halves, axis=1) - u


def _route(h2, wrt_ref, brt_ref):
    tm = h2.shape[0]
    w_hi, w_lo = wrt_ref[0], wrt_ref[1]
    h_hi = h2.astype(BF16)
    h_lo = (h2 - h_hi.astype(F32)).astype(BF16)
    nt = (((1,), (1,)), ((), ()))
    logits = (lax.dot_general(w_hi, h_hi, nt, preferred_element_type=F32)
              + (lax.dot_general(w_hi, h_lo, nt, preferred_element_type=F32)
                 + lax.dot_general(w_lo, h_hi, nt, preferred_element_type=F32)))
    scores = jax.nn.sigmoid(logits)
    choice = scores + brt_ref[...]
    neg = jnp.float32(-jnp.inf)

    c3 = choice.reshape(N_EXPERT_GROUPS, GROUP_SIZE, tm)
    pos3 = lax.broadcasted_iota(I32, c3.shape, 1)
    m1 = jnp.max(c3, axis=1, keepdims=True)
    i1 = jnp.min(jnp.where(c3 == m1, pos3, GROUP_SIZE), axis=1, keepdims=True)
    m2 = jnp.max(jnp.where(pos3 == i1, neg, c3), axis=1, keepdims=True)
    gs = m1 + m2
    gid = lax.broadcasted_iota(I32, gs.shape, 0)
    ahead = jnp.zeros(gs.shape, I32)
    for o in range(N_EXPERT_GROUPS):
        go = gs[o:o + 1]
        ahead = ahead + ((go > gs) | ((go == gs) & (gid > o))).astype(I32)
    c = jnp.where(ahead < TOPK_GROUPS, c3, neg).reshape(N_EXPERTS, tm)

    row = lax.broadcasted_iota(I32, c.shape, 0)
    picks = []
    for _ in range(TOP_K):
        m = jnp.max(c, axis=0, keepdims=True)
        i = jnp.min(jnp.where(c == m, row, N_EXPERTS), axis=0, keepdims=True)
        oh = row == i
        w = jnp.sum(jnp.where(oh, scores, 0.0), axis=0, keepdims=True)
        c = jnp.where(oh, neg, c)
        picks.append((i, w, oh))
    wsum = picks[0][1]
    for _, w, _ in picks[1:]:
        wsum = wsum + w
    return [(i, w / wsum * ROUTED_SCALE, oh) for i, w, oh in picks]


def _post_kernel(ts_ref, tf_ref, tl_ref, tp_ref, tn_ref,
                 x_ref, at_ref, upp_ref, up_ref, upn_ref, ys_ref, gt_ref, mod_ref,
                 wao_ref, pbd_ref, psc_ref, wpo_ref, wgl_ref, wo_ref, g2_ref, wr_ref, br_ref, triu_ref,
                 x1_ref, h2_ref, idx_ref, gate_ref, rank_ref, cnt_ref, ext_sc, carry_sc, *, tm):
    del ts_ref
    i = pl.program_id(0)
    mod = mod_ref[0]

    attn = jnp.dot(at_ref[...], wao_ref[...], preferred_element_type=F32)

    u = up_ref[...]
    ext_sc[0:POOL_HALO] = jnp.where(tf_ref[i] == 1, 0.0, upp_ref[...])
    ext_sc[POOL_HALO:POOL_HALO + tm] = u
    ext_sc[POOL_HALO + tm:2 * POOL_HALO + tm] = jnp.where(tl_ref[i] == 1, 0.0, upn_ref[...])
    pos = tp_ref[i] + lax.broadcasted_iota(I32, (tm, LANES), 0)
    pm = _pool_mixer(ext_sc, u, pos, tn_ref[i], tm)
    pool = jnp.dot(pm.astype(BF16), pbd_ref[...], preferred_element_type=F32) * psc_ref[...]
    pool = jnp.dot(pool.astype(BF16), wpo_ref[...], preferred_element_type=F32)

    z = jnp.dot(jax.nn.gelu(ys_ref[...]).astype(BF16), wgl_ref[...], preferred_element_type=F32)
    ssm = z[:, :D_MODEL] * jax.nn.sigmoid(z[:, D_MODEL:])

    merged = (gt_ref[:, 0:D_MODEL].astype(F32) * attn
              + gt_ref[:, D_MODEL:2 * D_MODEL].astype(F32) * pool
              + gt_ref[:, 2 * D_MODEL:3 * D_MODEL].astype(F32) * ssm)
    mix = jnp.dot(merged.astype(BF16), wo_ref[...], preferred_element_type=F32)
    x1 = x_ref[...] + mod[2:3] * mix
    x1_ref[...] = x1
    h2 = _rmsnorm_mod(x1, g2_ref[...], mod[4:5], mod[3:4])
    _store_row_tiles(h2_ref, h2)

    picks = _route(h2, wr_ref, br_ref)

    @pl.when(i == 0)
    def _():
        carry_sc[...] = jnp.zeros(carry_sc.shape, F32)

    sel = picks[0][2]
    for _, _, oh in picks[1:]:
        sel = sel | oh
    sel_f = sel.astype(F32)
    ranks = jnp.dot(sel_f.astype(BF16), triu_ref[...], preferred_element_type=F32) + carry_sc[...]
    carry = carry_sc[...] + jnp.sum(sel_f, axis=1, keepdims=True)
    carry_sc[...] = carry
    cnt_ref[...] = carry

    idx_ref[...] = jnp.concatenate([ik for ik, _, _ in picks], axis=0)
    gate_ref[...] = jnp.concatenate([gk for _, gk, _ in picks], axis=0)
    rank_ref[...] = jnp.concatenate(
        [jnp.sum(jnp.where(oh, ranks, 0.0), axis=0, keepdims=True).astype(I32) for _, _, oh in picks], axis=0)


def _post(x, attn, u_pool, y_ssm, gates, mod_l, wts, meta, tm):
    t = x.shape[0]
    nh = t // POOL_HALO
    hb = tm // POOL_HALO
    npf = 5
    const = lambda i, *_: (0, 0)
    row = lambda i, *_: (i, 0)
    col = lambda i, *_: (0, i)
    grid_spec = pltpu.PrefetchScalarGridSpec(
        num_scalar_prefetch=npf,
        grid=(t // tm,),
        in_specs=[
            pl.BlockSpec((tm, D_MODEL), row),
            pl.BlockSpec((tm, ATTN_DIM), row),
            pl.BlockSpec((POOL_HALO, POOL_DIM), lambda i, *_: (jnp.maximum(i * hb - 1, 0), 0)),
            pl.BlockSpec((tm, POOL_DIM), row),
            pl.BlockSpec((POOL_HALO, POOL_DIM), lambda i, *_: (jnp.minimum((i + 1) * hb, nh - 1), 0)),
            pl.BlockSpec((tm, SSM_DIM), row),
            pl.BlockSpec((tm, 3 * D_MODEL), row),
            pl.BlockSpec((1, N_ADA, D_MODEL), lambda i, ts, *_: (ts[i], 0, 0)),
            pl.BlockSpec((ATTN_DIM, D_MODEL), const),
            pl.BlockSpec((POOL_DIM, POOL_DIM), const),
            pl.BlockSpec((1, POOL_DIM), const),
            pl.BlockSpec((POOL_DIM, D_MODEL), const),
            pl.BlockSpec((SSM_DIM, 2 * D_MODEL), const),
            pl.BlockSpec((D_MODEL, D_MODEL), const),
            pl.BlockSpec((1, D_MODEL), const),
            pl.BlockSpec((2, N_EXPERTS, D_MODEL), lambda i, *_: (0, 0, 0)),
            pl.BlockSpec((N_EXPERTS, 1), const),
            pl.BlockSpec((tm, tm), const),
        ],
        out_specs=[
            pl.BlockSpec((tm, D_MODEL), row),
            pl.BlockSpec((tm * ROW_TILE, LANES), row),
            pl.BlockSpec((TOP_K, tm), col),
            pl.BlockSpec((TOP_K, tm), col),
            pl.BlockSpec((TOP_K, tm), col),
            pl.BlockSpec((N_EXPERTS, 1), const),
        ],
        scratch_shapes=[pltpu.VMEM((tm + 2 * POOL_HALO, POOL_DIM), F32), pltpu.VMEM((N_EXPERTS, 1), F32)],
    )
    return pl.pallas_call(
        functools.partial(_post_kernel, tm=tm),
        grid_spec=grid_spec,
        out_shape=[
            jax.ShapeDtypeStruct((t, D_MODEL), F32),
            jax.ShapeDtypeStruct((t * ROW_TILE, LANES), U32),
            jax.ShapeDtypeStruct((TOP_K, t), I32),
            jax.ShapeDtypeStruct((TOP_K, t), F32),
            jax.ShapeDtypeStruct((TOP_K, t), I32),
            jax.ShapeDtypeStruct((N_EXPERTS, 1), F32),
        ],
        compiler_params=_cparams(("arbitrary",)),
        name="mixer_merge_route",
    )(*meta, x, attn, u_pool, u_pool, u_pool, y_ssm, gates, mod_l, *wts)


def _dispatch_kernel(lb_ref, has_ref, d_ref, h_hbm, x_hbm, zero_sc, h_sc, sem_in, sem, zsem, *, tn, bm):
    i = pl.program_id(0)
    n = pl.num_programs(0)
    n_copies = tn * TOP_K
    tile_rows = tn * ROW_TILE

    def slot_rows(ref, slot):
        return ref.at[pl.ds(slot * ROW_TILE, ROW_TILE), :]

    def tile_load(tile, s):
        return pltpu.make_async_copy(h_hbm.at[pl.ds(tile * tile_rows, tile_rows), :], h_sc.at[s], sem_in.at[s])

    def pad_loop(act):
        def expert(e, c):
            @pl.when(has_ref[e] == 1)
            def _():
                act(pltpu.make_async_copy(zero_sc, x_hbm.at[pl.ds(lb_ref[e] * ROW_TILE, bm * ROW_TILE), :], zsem))
            return c
        lax.fori_loop(0, N_EXPERTS, expert, 0)

    @pl.when(i == 0)
    def _():
        zero_sc[...] = jnp.zeros(zero_sc.shape, U32)
        pad_loop(lambda cp: cp.start())
        pad_loop(lambda cp: cp.wait())
        tile_load(0, 0).start()

    cur = i % 3

    @pl.when(i + 1 < n)
    def _():
        tile_load(i + 1, (i + 1) % 3).start()

    tile_load(i, cur).wait()

    def body(tok, c):
        src = slot_rows(h_sc.at[cur], tok)
        for k in range(TOP_K):
            pltpu.make_async_copy(src, slot_rows(x_hbm, d_ref[0, 0, tok * TOP_K + k]),
                                  sem.at[cur]).start(priority=k % DMA_PRIORITIES)
        return c
    lax.fori_loop(0, tn, body, 0, unroll=2)

    def wait_tile(s):
        whole = x_hbm.at[pl.ds(0, n_copies * ROW_TILE), :]
        pltpu.make_async_copy(whole, whole, sem.at[s]).wait()

    @pl.when(i > 0)
    def _():
        wait_tile((i + 2) % 3)

    @pl.when(i == n - 1)
    def _():
        wait_tile(cur)


def _dispatch(h2t, dest3, last_block, has_rows, n_rows, tn, bm):
    n = dest3.shape[0]
    grid_spec = pltpu.PrefetchScalarGridSpec(
        num_scalar_prefetch=2,
        grid=(n,),
        in_specs=[
            pl.BlockSpec((1, 1, tn * TOP_K), lambda i, lo, hi: (i, 0, 0), memory_space=pltpu.SMEM),
            pl.BlockSpec(memory_space=pl.ANY),
        ],
        out_specs=pl.BlockSpec(memory_space=pl.ANY),
        scratch_shapes=[pltpu.VMEM((bm * ROW_TILE, LANES), U32), pltpu.VMEM((3, tn * ROW_TILE, LANES), U32),
                        pltpu.SemaphoreType.DMA((3,)), pltpu.SemaphoreType.DMA((3,)),
                        pltpu.SemaphoreType.DMA(())],
    )
    return pl.pallas_call(
        functools.partial(_dispatch_kernel, tn=tn, bm=bm),
        grid_spec=grid_spec,
        out_shape=jax.ShapeDtypeStruct((n_rows * ROW_TILE, LANES), U32),
        compiler_params=_cparams(("arbitrary",)),
        name="moe_dispatch",
    )(last_block, has_rows, dest3, h2t)


def _expert_kernel(be_ref, nu_ref, x_ref, wg_ref, wu_ref, wd_ref, y_ref, wg_sc, wu_sc, wd_sc, *, bm):
    b = pl.program_id(0)
    n_used = nu_ref[0]

    @pl.when(b < n_used)
    def _():
        @pl.when((b == 0) | (be_ref[b] != be_ref[jnp.maximum(b - 1, 0)]))
        def _():
            wg_sc[...] = wg_ref[0, 0].astype(BF16)
            wu_sc[...] = wu_ref[0, 0].astype(BF16)
            wd_sc[...] = wd_ref[0, 0].astype(BF16)

        x = _load_row_tiles(x_ref, 0, bm).astype(BF16)
        g = jnp.dot(x, wg_sc[...], preferred_element_type=F32)
        u = jnp.dot(x, wu_sc[...], preferred_element_type=F32)
        hmid = (g * jax.nn.sigmoid(g) * u).astype(BF16)
        _store_row_tiles(y_ref, jnp.dot(hmid, wd_sc[...], preferred_element_type=F32))

    @pl.when(b >= n_used)
    def _():
        y_ref[...] = jnp.zeros(y_ref.shape, U32)


def _experts(x_buf, block_expert, n_used, w_gate, w_up, w_down, layer, bm):
    nb = x_buf.shape[0] // (bm * ROW_TILE)
    grid_spec = pltpu.PrefetchScalarGridSpec(
        num_scalar_prefetch=2,
        grid=(nb,),
        in_specs=[
            pl.BlockSpec((bm * ROW_TILE, LANES), lambda b, be, nu: (jnp.minimum(b, nu[0] - 1), 0)),
            pl.BlockSpec((1, 1, D_MODEL, D_EXPERT), lambda b, be, nu: (layer, be[b], 0, 0)),
            pl.BlockSpec((1, 1, D_MODEL, D_EXPERT), lambda b, be, nu: (layer, be[b], 0, 0)),
            pl.BlockSpec((1, 1, D_EXPERT, D_MODEL), lambda b, be, nu: (layer, be[b], 0, 0)),
        ],
        out_specs=pl.BlockSpec((bm * ROW_TILE, LANES), lambda b, be, nu: (b, 0)),
        scratch_shapes=[
            pltpu.VMEM((D_MODEL, D_EXPERT), BF16),
            pltpu.VMEM((D_MODEL, D_EXPERT), BF16),
            pltpu.VMEM((D_EXPERT, D_MODEL), BF16),
        ],
    )
    return pl.pallas_call(
        functools.partial(_expert_kernel, bm=bm),
        grid_spec=grid_spec,
        out_shape=jax.ShapeDtypeStruct((nb * bm * ROW_TILE, LANES), U32),
        compiler_params=_cparams(("arbitrary",)),
        name="routed_experts",
    )(block_expert, n_used, x_buf, w_gate, w_up, w_down)


def _combine_kernel(ts_ref, d_ref, dn_ref, y_hbm, x1_ref, h2_ref, gate_ref, mod_ref, wsg_ref, wsu_ref, wsd_ref,
                    o_ref, yg_sc, sem, *, tc):
    del ts_ref
    i = pl.program_id(0)
    n = pl.num_programs(0)
    slot = i % 2
    n_copies = tc * TOP_K

    def row_copy(src, s, row):
        return pltpu.make_async_copy(y_hbm.at[pl.ds(src * ROW_TILE, ROW_TILE), :],
                                     yg_sc.at[s, pl.ds(row * ROW_TILE, ROW_TILE), :], sem.at[s])

    def issue(dest_smem, s):
        def body(tok, c):
            for k in range(TOP_K):
                row_copy(dest_smem[0, 0, tok * TOP_K + k], s, k * tc + tok).start(priority=k % DMA_PRIORITIES)
            return c
        lax.fori_loop(0, tc, body, 0, unroll=2)

    @pl.when(i == 0)
    def _():
        issue(d_ref, 0)

    @pl.when(i + 1 < n)
    def _():
        issue(dn_ref, 1 - slot)

    pltpu.make_async_copy(y_hbm.at[pl.ds(0, n_copies * ROW_TILE), :], yg_sc.at[slot], sem.at[slot]).wait()

    gate = gate_ref[...]
    yg = yg_sc.at[slot]
    routed = gate[:, 0:1] * _load_row_tiles(yg, 0, tc)
    for k in range(1, TOP_K):
        routed = routed + gate[:, k:k + 1] * _load_row_tiles(yg, k * tc, tc)
    hb = _load_row_tiles(h2_ref, 0, tc).astype(BF16)
    g = jnp.dot(hb, wsg_ref[...], preferred_element_type=F32)
    u = jnp.dot(hb, wsu_ref[...], preferred_element_type=F32)
    shared = jnp.dot((g * jax.nn.sigmoid(g) * u).astype(BF16), wsd_ref[...], preferred_element_type=F32)
    o_ref[...] = x1_ref[...] + mod_ref[0][5:6] * (routed + shared)


def _combine(y_buf, dest3, x1, h2, gate, mod_l, ws_gate, ws_up, ws_down, tile_seq, tc):
    t = x1.shape[0]
    n = t // tc
    d_sh = ws_gate.shape[1]
    const = lambda i, ts: (0, 0)
    row = lambda i, ts: (i, 0)
    grid_spec = pltpu.PrefetchScalarGridSpec(
        num_scalar_prefetch=1,
        grid=(n,),
        in_specs=[
            pl.BlockSpec((1, 1, tc * TOP_K), lambda i, ts: (i, 0, 0), memory_space=pltpu.SMEM),
            pl.BlockSpec((1, 1, tc * TOP_K), lambda i, ts: (jnp.minimum(i + 1, n - 1), 0, 0),
                         memory_space=pltpu.SMEM),
            pl.BlockSpec(memory_space=pl.ANY),
            pl.BlockSpec((tc, D_MODEL), row),
            pl.BlockSpec((tc * ROW_TILE, LANES), row),
            pl.BlockSpec((tc, TOP_K), row),
            pl.BlockSpec((1, N_ADA, D_MODEL), lambda i, ts: (ts[i], 0, 0)),
            pl.BlockSpec((D_MODEL, d_sh), const),
            pl.BlockSpec((D_MODEL, d_sh), const),
            pl.BlockSpec((d_sh, D_MODEL), const),
        ],
        out_specs=pl.BlockSpec((tc, D_MODEL), row),
        scratch_shapes=[pltpu.VMEM((2, tc * TOP_K * ROW_TILE, LANES), U32), pltpu.SemaphoreType.DMA((2,))],
    )
    return pl.pallas_call(
        functools.partial(_combine_kernel, tc=tc),
        grid_spec=grid_spec,
        out_shape=jax.ShapeDtypeStruct((t, D_MODEL), F32),
        compiler_params=_cparams(("arbitrary",)),
        name="moe_combine",
    )(tile_seq, dest3, dest3, y_buf, x1, h2, gate, mod_l, ws_gate, ws_up, ws_down)


def _tile_meta(seq_lens, tile):
    seq, first, last, pos0, slen = [], [], [], [], []
    for s, n in enumerate(seq_lens):
        nt = n // tile
        for j in range(nt):
            seq.append(s)
            first.append(int(j == 0))
            last.append(int(j == nt - 1))
            pos0.append(j * tile)
            slen.append(n)
    return tuple(jnp.asarray(np.asarray(a, np.int32)) for a in (seq, first, last, pos0, slen))


def _scan_meta(seq_lens, tr):
    bwd, first = [], []
    base = 0
    for n in seq_lens:
        nt = n // SSM_CHUNK // tr
        for j in range(nt):
            bwd.append(base + nt - 1 - j)
            first.append(int(j == 0))
        base += nt
    return jnp.asarray(np.asarray(bwd, np.int32)), jnp.asarray(np.asarray(first, np.int32))


def _rope_tables(seq_lens):
    quarter = HEAD_DIM // 4
    freqs = ROPE_THETA ** (-jnp.arange(quarter, dtype=F32) / quarter)
    sign = jnp.tile(jnp.concatenate([-jnp.ones((quarter,), F32), jnp.ones((quarter,), F32)]), 2)
    cos_l, sin_l = [], []
    cache = {}
    for n in seq_lens:
        if n not in cache:
            pos = jnp.arange(n)
            ar = (pos // GRID_W).astype(F32)[:, None] * freqs
            ac = (pos % GRID_W).astype(F32)[:, None] * freqs
            ang = jnp.concatenate([ar, ar, ac, ac], axis=-1)
            cache[n] = (jnp.tile(jnp.cos(ang), (1, 2)), jnp.tile(jnp.sin(ang) * sign, (1, 2)))
        cos_l.append(cache[n][0])
        sin_l.append(cache[n][1])
    return jnp.concatenate(cos_l, axis=0), jnp.concatenate(sin_l, axis=0)


def _split_bf16(w):
    hi = w.astype(BF16)
    return jnp.stack([hi, (w - hi.astype(F32)).astype(BF16)], axis=0)


def _block_diag_ones(n, blk):
    r = np.arange(n) // blk
    return jnp.asarray((r[:, None] == r[None, :]).astype(np.float32)).astype(BF16)


def _dest_kernel(idx_ref, rank_ref, ps_ref, d_ref):
    idx = idx_ref[...]
    row_e = lax.broadcasted_iota(I32, (N_EXPERTS, idx.shape[1]), 0)
    starts = [jnp.sum(jnp.where(row_e == idx[k:k + 1], ps_ref[...], 0.0), axis=0, keepdims=True)
              for k in range(TOP_K)]
    d_ref[...] = rank_ref[...] + jnp.concatenate(starts, axis=0).astype(I32)


def _dest_slots(idx, rank, pstart):
    t = idx.shape[1]
    tm = _pick(t, 1024)
    col = lambda i: (0, i)
    return pl.pallas_call(
        _dest_kernel,
        grid=(t // tm,),
        in_specs=[pl.BlockSpec((TOP_K, tm), col), pl.BlockSpec((TOP_K, tm), col),
                  pl.BlockSpec((N_EXPERTS, 1), lambda i: (0, 0))],
        out_specs=pl.BlockSpec((TOP_K, tm), col),
        out_shape=jax.ShapeDtypeStruct((TOP_K, t), I32),
        compiler_params=_cparams(("arbitrary",)),
        name="moe_dest_slots",
    )(idx, rank, pstart)


def _moe_plan(idx, rank, counts, t, bm, nb):
    cnt = counts.reshape(N_EXPERTS).astype(I32)
    padded = (cnt + bm - 1) // bm * bm
    pend = jnp.cumsum(padded)
    pstart = pend - padded
    n_used = (pend[-1] // bm).astype(I32).reshape(1)
    starts = jnp.arange(nb, dtype=I32) * bm
    block_expert = jnp.minimum(
        jnp.sum((pend[None, :] <= starts[:, None]).astype(I32), axis=1), N_EXPERTS - 1).astype(I32)
    dest = _dest_slots(idx, rank, pstart.astype(F32).reshape(N_EXPERTS, 1)).T
    return dest, (pend - bm).astype(I32), (cnt > 0).astype(I32), block_expert, n_used


def kernel(x_prompt, x_sample, c_prompt, c_sample, w_ada, b_ada, norm1_g, w_in, q_norm_g, k_norm_g, w_attn_o, pool_w, pool_scale, w_pool_o, ssm_a_re, ssm_a_im, ssm_log_dt, ssm_b_re, ssm_b_im, ssm_c_re, ssm_c_im, ssm_d, w_glu, w_out, norm2_g, w_router, b_router, w_exp_gate, w_exp_up, w_exp_down, w_sh_gate, w_sh_up, w_sh_down):
    b1, l1, d = x_prompt.shape
    b2, l2, _ = x_sample.shape
    depth = w_in.shape[0]
    assert d == D_MODEL and b1 + b2 <= MOD_ROWS
    seq_lens = [l1] * b1 + [l2] * b2
    t = b1 * l1 + b2 * l2
    lmin = min(l1, l2)

    tm_in = _pick(lmin, 512)
    tm_post = _pick(lmin, 256)
    tc_comb = _pick(lmin, 256)
    tr_scan = _pick(lmin // SSM_CHUNK, 256)
    bm = 512
    nb = -(-(t * TOP_K + N_EXPERTS * (bm - 1)) // bm)

    x = jnp.concatenate([x_prompt.reshape(b1 * l1, d), x_sample.reshape(b2 * l2, d)], axis=0)
    c_all = jnp.concatenate([c_prompt, c_sample, jnp.zeros((MOD_ROWS - b1 - b2, d), F32)], axis=0)
    mod = _modulation(c_all, w_ada, b_ada).reshape(depth, MOD_ROWS, N_ADA, d)

    cos_t, sin_t = _rope_tables(seq_lens)
    ones_bd = _block_diag_ones(ATTN_DIM, HEAD_DIM)
    triu = jnp.asarray(np.triu(np.ones((tm_post, tm_post), np.float32), 1)).astype(BF16)
    meta_in = _tile_meta(seq_lens, tm_in)
    meta_post = _tile_meta(seq_lens, tm_post)
    meta_comb = _tile_meta(seq_lens, tc_comb)
    bwd_tile, first_tile = _scan_meta(seq_lens, tr_scan)

    for l in range(depth):
        q8, k2, v2, u_pool, u_ssm, gates = _inproj(
            x, mod[l], norm1_g[l].reshape(1, d), w_in[l].astype(BF16),
            jnp.tile(q_norm_g[l], N_Q_HEADS).reshape(1, ATTN_DIM),
            jnp.tile(k_norm_g[l], N_KV_HEADS).reshape(1, KV_DIM),
            cos_t, sin_t, ones_bd, meta_in[0], tm_in)
        attn = jnp.concatenate(
            [_flash(q8, k2, v2, 0, b1, l1), _flash(q8, k2, v2, b1 * l1, b2, l2)], axis=0)
        tables = _ssm_tables(ssm_a_re[l], ssm_a_im[l], ssm_log_dt[l], ssm_b_re[l], ssm_b_im[l],
                             ssm_c_re[l], ssm_c_im[l], ssm_d[l])
        y_ssm = _bidir_ssm(u_ssm, tables, bwd_tile, first_tile, tr_scan)

        pool_bd = jax.scipy.linalg.block_diag(*[pool_w[l, g] for g in range(len(POOL_WINDOWS))])
        wts = (w_attn_o[l].astype(BF16), pool_bd.astype(BF16), pool_scale[l].reshape(1, POOL_DIM),
               w_pool_o[l].astype(BF16), w_glu[l].astype(BF16), w_out[l].astype(BF16),
               norm2_g[l].reshape(1, d), _split_bf16(w_router[l].T), b_router[l].reshape(N_EXPERTS, 1), triu)
        x1, h2, idx, gate, rank, counts = _post(x, attn, u_pool, y_ssm, gates, mod[l], wts, meta_post, tm_post)

        dest, last_block, has_rows, block_expert, n_used = _moe_plan(idx, rank, counts, t, bm, nb)
        dest3 = dest.reshape(t // tc_comb, 1, tc_comb * TOP_K)
        x_buf = _dispatch(h2, dest3, last_block, has_rows, nb * bm, tc_comb, bm)
        y_buf = _experts(x_buf, block_expert, n_used, w_exp_gate, w_exp_up, w_exp_down, l, bm)
        gate = gate.T
        x = _combine(y_buf, dest3, x1, h2, gate, mod[l],
                     w_sh_gate[l].astype(BF16), w_sh_up[l].astype(BF16), w_sh_down[l].astype(BF16),
                     meta_comb[0], tc_comb)

    return (x[:b1 * l1].reshape(b1, l1, d), x[b1 * l1:].reshape(b2, l2, d))
```

```python
import functools
import math

import jax
import jax.numpy as jnp
import numpy as np
from jax import lax
from jax.experimental import pallas as pl
from jax.experimental.pallas import tpu as pltpu

F32 = jnp.float32
BF16 = jnp.bfloat16
I32 = jnp.int32
HIGHEST = lax.Precision.HIGHEST

D_MODEL = 1024
GRID_W = 64
HEAD_DIM = 64
N_Q_HEADS = 8
N_KV_HEADS = 2
Q_PER_KV = N_Q_HEADS // N_KV_HEADS
ATTN_DIM = N_Q_HEADS * HEAD_DIM
KV_DIM = N_KV_HEADS * HEAD_DIM
ROPE_THETA = 10000.0
POOL_WINDOWS = (2, 4, 8, 16)
POOL_DIM = 256
POOL_GROUP = 64
POOL_HALO = 8
SSM_DIM = 256
SSM_P = 16
SSM_G = 16
SSM_N = 64
SSM_CHUNK = 16
SSM_ROW = SSM_CHUNK * SSM_P
SSM_STATE = 2 * SSM_N
N_EXPERTS = 256
TOP_K = 8
N_EXPERT_GROUPS = 8
GROUP_SIZE = N_EXPERTS // N_EXPERT_GROUPS
TOPK_GROUPS = 4
D_EXPERT = 256
ROUTED_SCALE = 2.5
N_ADA = 6
EPS = 1e-6
IN_DIM = ATTN_DIM + 2 * KV_DIM + POOL_DIM + SSM_DIM + 3 * D_MODEL
OFF_K = ATTN_DIM
OFF_V = OFF_K + KV_DIM
OFF_POOL = OFF_V + KV_DIM
OFF_SSM = OFF_POOL + POOL_DIM
OFF_GATES = OFF_SSM + SSM_DIM
MOD_ROWS = 8
LOG2_E = math.log2(math.e)
VT_ROWS = HEAD_DIM + 16

V7X_VMEM_BYTES = 64 * 1024 * 1024
VMEM_LIMIT = V7X_VMEM_BYTES - 8 * 1024 * 1024
LANES = 128


def _cparams(sem):
    return pltpu.CompilerParams(dimension_semantics=sem, vmem_limit_bytes=VMEM_LIMIT)


def _pick(n, pref):
    t = min(n, pref)
    while n % t:
        t //= 2
    return t


U32 = jnp.uint32
ROW_TILE = D_MODEL // 2 // LANES
DMA_PRIORITIES = 2


def _store_row_tiles(ref, x):
    n = x.shape[0]
    half = D_MODEL // 2
    lo = pltpu.bitcast(x[:, :half].astype(BF16).astype(F32), U32)
    hi = pltpu.bitcast(x[:, half:].astype(BF16).astype(F32), U32)
    w = (lo >> 16) | (hi & jnp.uint32(0xFFFF0000))
    for s in range(ROW_TILE):
        ref[pl.ds(s, n, stride=ROW_TILE), :] = w[:, s * LANES:(s + 1) * LANES]


def _load_row_tiles(ref, first_row, n):
    w = jnp.concatenate(
        [ref[pl.ds(first_row * ROW_TILE + s, n, stride=ROW_TILE), :] for s in range(ROW_TILE)], axis=1)
    lo = pltpu.bitcast(w << 16, F32)
    hi = pltpu.bitcast(w & jnp.uint32(0xFFFF0000), F32)
    return jnp.concatenate([lo, hi], axis=1)


def _mod_kernel(c_ref, w_ref, b_ref, o_ref):
    c = c_ref[...]
    a = c * jax.nn.sigmoid(c)
    o_ref[0] = jnp.dot(a, w_ref[0], preferred_element_type=F32, precision=HIGHEST) + b_ref[0]


def _modulation(c_all, w_ada, b_ada):
    depth, d, n = w_ada.shape
    bn = _pick(n, 1536)
    return pl.pallas_call(
        _mod_kernel,
        grid=(depth, n // bn),
        in_specs=[
            pl.BlockSpec((MOD_ROWS, d), lambda l, j: (0, 0)),
            pl.BlockSpec((1, d, bn), lambda l, j: (l, 0, j)),
            pl.BlockSpec((1, 1, bn), lambda l, j: (l, 0, j)),
        ],
        out_specs=pl.BlockSpec((1, MOD_ROWS, bn), lambda l, j: (l, 0, j)),
        out_shape=jax.ShapeDtypeStruct((depth, MOD_ROWS, n), F32),
        compiler_params=_cparams(("arbitrary", "arbitrary")),
        name="adaln_mod",
    )(c_all, w_ada, b_ada.reshape(depth, 1, n))


def _rmsnorm_mod(x, g, scale, shift):
    y = x * lax.rsqrt(jnp.mean(x * x, axis=-1, keepdims=True) + EPS)
    return (y * g) * (1.0 + scale) + shift


def _head_norm_rope(z, gain, cos, sin_signed, ones_bd):
    z2 = z * z
    hi = z2.astype(BF16)
    lo = (z2 - hi.astype(F32)).astype(BF16)
    ss = (jnp.dot(hi, ones_bd, preferred_element_type=F32)
          + jnp.dot(lo, ones_bd, preferred_element_type=F32))
    y = (z * lax.rsqrt(ss * (1.0 / HEAD_DIM) + EPS)) * gain
    w = z.shape[1]
    quarter = HEAD_DIM // 4
    from_right = pltpu.roll(y, w - quarter, 1)
    from_left = pltpu.roll(y, quarter, 1)
    lane = lax.broadcasted_iota(I32, y.shape, 1)
    rot = jnp.where((lane & quarter) == 0, from_right, from_left)
    return y * cos + rot * sin_signed


def _inproj_kernel(ts_ref, x_ref, mod_ref, g_ref, w_ref, gq_ref, gk_ref, cos_ref, sin_ref, ones_ref,
                   q_ref, k_ref, v_ref, up_ref, us_ref, gt_ref):
    del ts_ref
    x = x_ref[...]
    mod = mod_ref[0]
    h = _rmsnorm_mod(x, g_ref[...], mod[1:2], mod[0:1]).astype(BF16)

    cos = cos_ref[...]
    sin = sin_ref[...]
    zq = jnp.dot(h, w_ref[:, 0:ATTN_DIM], preferred_element_type=F32)
    reps = ATTN_DIM // LANES
    yq = _head_norm_rope(zq, gq_ref[...], jnp.concatenate([cos] * reps, axis=1),
                         jnp.concatenate([sin] * reps, axis=1), ones_ref[...]) * (HEAD_DIM ** -0.5 * LOG2_E)
    for hh in range(N_Q_HEADS):
        q_ref[hh] = yq[:, hh * HEAD_DIM:(hh + 1) * HEAD_DIM].astype(BF16)

    zkv = jnp.dot(h, w_ref[:, OFF_K:OFF_POOL], preferred_element_type=F32)
    yk = _head_norm_rope(zkv[:, :KV_DIM], gk_ref[...], cos, sin, ones_ref[0:KV_DIM, 0:KV_DIM])
    zv = zkv[:, KV_DIM:]
    for j in range(N_KV_HEADS):
        k_ref[j] = yk[:, j * HEAD_DIM:(j + 1) * HEAD_DIM].astype(BF16)
        v_ref[j, 0:HEAD_DIM, :] = zv[:, j * HEAD_DIM:(j + 1) * HEAD_DIM].T.astype(BF16)
        v_ref[j, HEAD_DIM:VT_ROWS, :] = jnp.ones((VT_ROWS - HEAD_DIM, zv.shape[0]), BF16)

    zps = jnp.dot(h, w_ref[:, OFF_POOL:OFF_GATES], preferred_element_type=F32)
    up_ref[...] = zps[:, :POOL_DIM]
    us_ref[...] = zps[:, POOL_DIM:]
    for c in range(3):
        zg = jnp.dot(h, w_ref[:, OFF_GATES + c * D_MODEL:OFF_GATES + (c + 1) * D_MODEL],
                     preferred_element_type=F32)
        gt_ref[:, c * D_MODEL:(c + 1) * D_MODEL] = jax.nn.sigmoid(zg).astype(BF16)


def _inproj(x, mod_l, g1, w_in_bf, gq_t, gk_t, cos_t, sin_t, ones_bd, tile_seq, tm):
    t = x.shape[0]
    const = lambda i, ts: (0, 0)
    row = lambda i, ts: (i, 0)
    grid_spec = pltpu.PrefetchScalarGridSpec(
        num_scalar_prefetch=1,
        grid=(t // tm,),
        in_specs=[
            pl.BlockSpec((tm, D_MODEL), row),
            pl.BlockSpec((1, N_ADA, D_MODEL), lambda i, ts: (ts[i], 0, 0)),
            pl.BlockSpec((1, D_MODEL), const),
            pl.BlockSpec((D_MODEL, IN_DIM), const),
            pl.BlockSpec((1, ATTN_DIM), const),
            pl.BlockSpec((1, KV_DIM), const),
            pl.BlockSpec((tm, LANES), row),
            pl.BlockSpec((tm, LANES), row),
            pl.BlockSpec((ATTN_DIM, ATTN_DIM), const),
        ],
        out_specs=[
            pl.BlockSpec((N_Q_HEADS, tm, HEAD_DIM), lambda i, ts: (0, i, 0)),
            pl.BlockSpec((N_KV_HEADS, tm, HEAD_DIM), lambda i, ts: (0, i, 0)),
            pl.BlockSpec((N_KV_HEADS, VT_ROWS, tm), lambda i, ts: (0, 0, i)),
            pl.BlockSpec((tm, POOL_DIM), row),
            pl.BlockSpec((tm, SSM_DIM), row),
            pl.BlockSpec((tm, 3 * D_MODEL), row),
        ],
    )
    return pl.pallas_call(
        _inproj_kernel,
        grid_spec=grid_spec,
        out_shape=[
            jax.ShapeDtypeStruct((N_Q_HEADS, t, HEAD_DIM), BF16),
            jax.ShapeDtypeStruct((N_KV_HEADS, t, HEAD_DIM), BF16),
            jax.ShapeDtypeStruct((N_KV_HEADS, VT_ROWS, t), BF16),
            jax.ShapeDtypeStruct((t, POOL_DIM), F32),
            jax.ShapeDtypeStruct((t, SSM_DIM), F32),
            jax.ShapeDtypeStruct((t, 3 * D_MODEL), BF16),
        ],
        compiler_params=_cparams(("arbitrary",)),
        name="norm1_inproj",
    )(tile_seq, x, mod_l, g1, w_in_bf, gq_t, gk_t, cos_t, sin_t, ones_bd)


def _flash_kernel(q_ref, k_ref, vt_ref, o_ref, m_sc, acc_sc):
    kb = pl.program_id(3)

    @pl.when(kb == 0)
    def _():
        m_sc[...] = jnp.full(m_sc.shape, -jnp.inf, F32)
        acc_sc[...] = jnp.zeros(acc_sc.shape, F32)

    k = k_ref[0]
    vt = vt_ref[0]
    heads = range(Q_PER_KV)
    qs = [q_ref[hh] for hh in heads]
    m_prev = [m_sc[hh] for hh in heads]
    acc_prev = [acc_sc[hh] for hh in heads]

    def scores(hh):
        return lax.dot_general(k, qs[hh], (((1,), (1,)), ((), ())), preferred_element_type=F32)

    ahead = 2
    s = {hh: scores(hh) for hh in range(ahead)}
    new = []
    for hh in heads:
        if hh + ahead < Q_PER_KV:
            s[hh + ahead] = scores(hh + ahead)
        m_new = jnp.maximum(m_prev[hh], jnp.max(s[hh], axis=0, keepdims=True))
        alpha = jnp.exp2(m_prev[hh] - m_new)
        p = jnp.exp2(s.pop(hh) - m_new).astype(BF16)
        acc_new = acc_prev[hh] * alpha + jnp.dot(vt, p, preferred_element_type=F32)
        new.append((m_new, acc_new))
    for hh, (m_new, acc_new) in enumerate(new):
        m_sc[hh] = m_new
        acc_sc[hh] = acc_new

    @pl.when(kb == pl.num_programs(3) - 1)
    def _():
        for hh in range(Q_PER_KV):
            acc = acc_sc[hh]
            o = acc[:HEAD_DIM] / acc[HEAD_DIM:HEAD_DIM + 1]
            o_ref[:, hh * HEAD_DIM:(hh + 1) * HEAD_DIM] = o.T.astype(BF16)


def _flash_into_kernel(q_ref, k_ref, vt_ref, prev_ref, o_ref, m_sc, acc_sc):
    del prev_ref
    _flash_kernel(q_ref, k_ref, vt_ref, o_ref, m_sc, acc_sc)


def _flash(q8, k2, vt2, off, bsz, seq, out=None):
    t = q8.shape[1]
    tq = _pick(seq, 512)
    tk = _pick(seq, 4096)
    nq, nk = seq // tq, seq // tk
    assert off % tq == 0 and off % tk == 0
    oq, ok = off // tq, off // tk
    in_specs = [
        pl.BlockSpec((Q_PER_KV, tq, HEAD_DIM), lambda j, b, i, kb: (j, oq + b * nq + i, 0)),
        pl.BlockSpec((1, tk, HEAD_DIM), lambda j, b, i, kb: (j, ok + b * nk + kb, 0)),
        pl.BlockSpec((1, VT_ROWS, tk), lambda j, b, i, kb: (j, 0, ok + b * nk + kb)),
    ]
    args = (q8, k2, vt2)
    if out is not None:
        in_specs.append(pl.BlockSpec(memory_space=pl.ANY))
        args = args + (out,)
    return pl.pallas_call(
        _flash_kernel if out is None else _flash_into_kernel,
        grid=(N_KV_HEADS, bsz, nq, nk),
        in_specs=in_specs,
        out_specs=pl.BlockSpec((tq, Q_PER_KV * HEAD_DIM), lambda j, b, i, kb: (oq + b * nq + i, j)),
        out_shape=jax.ShapeDtypeStruct((t, ATTN_DIM), BF16),
        scratch_shapes=[
            pltpu.VMEM((Q_PER_KV, 1, tq), F32),
            pltpu.VMEM((Q_PER_KV, VT_ROWS, tq), F32),
        ],
        input_output_aliases={} if out is None else {3: 0},
        compiler_params=_cparams(("arbitrary", "arbitrary", "arbitrary", "arbitrary")),
        name="flash_gqa",
    )(*args)


def _ssm_tables(a_re, a_im, log_dt, b_re, b_im, c_re, c_im, d_skip):
    tc = SSM_CHUNK
    a = lax.complex(a_re.astype(F32), a_im.astype(F32))
    dt = jnp.exp(log_dt.astype(F32))[..., None]
    adt = a * dt
    a_bar = jnp.exp(adt)
    b_bar = ((a_bar - 1.0) / a)[..., None] * lax.complex(b_re.astype(F32), b_im.astype(F32))
    c_mat = lax.complex(c_re.astype(F32), c_im.astype(F32))
    taus = jnp.arange(tc + 1, dtype=F32)
    pw = jnp.exp(adt[None] * taus[:, None, None, None])

    kern = jnp.real(jnp.einsum('dgpn,tdgn,dgnq->dgtpq', c_mat, pw[:tc], b_bar))
    s_idx = jnp.arange(tc)[:, None]
    t_idx = jnp.arange(tc)[None, :]
    lag_f = jnp.clip(t_idx - s_idx, 0, tc - 1)
    lag_b = jnp.clip(s_idx - t_idx, 0, tc - 1)
    kf = jnp.where((t_idx >= s_idx)[None, :, :, None, None], kern[0][:, lag_f], 0.0)
    kb = jnp.where((s_idx >= t_idx)[None, :, :, None, None], kern[1][:, lag_b], 0.0)
    w_loc = (kf + kb).transpose(0, 1, 4, 2, 3).reshape(SSM_G, SSM_ROW, SSM_ROW)
    w_loc = w_loc + jnp.eye(SSM_ROW, dtype=F32)[None] * jnp.tile(
        d_skip.astype(F32).reshape(SSM_G, 1, SSM_P), (1, tc, 1)).reshape(SSM_G, 1, SSM_ROW)

    def ri(z, axis):
        return jnp.concatenate([jnp.real(z), jnp.imag(z)], axis=axis)

    pf = jnp.einsum('sgn,gnq->gsqn', pw[:tc][::-1, 0], b_bar[0])
    pb = jnp.einsum('sgn,gnq->gsqn', pw[:tc, 1], b_bar[1])
    p_st = jnp.concatenate([ri(pf, -1), ri(pb, -1)], axis=-1).reshape(SSM_G, SSM_ROW, 2 * SSM_STATE)
    qf = jnp.einsum('gpn,tgn->gntp', c_mat[0], pw[1:tc + 1, 0])
    qb = jnp.einsum('gpn,tgn->gntp', c_mat[1], pw[1:tc + 1, 1][::-1])
    q_f = jnp.concatenate([jnp.real(qf), -jnp.imag(qf)], axis=1).reshape(SSM_G, SSM_STATE, SSM_ROW)
    q_b = jnp.concatenate([jnp.real(qb), -jnp.imag(qb)], axis=1).reshape(SSM_G, SSM_STATE, SSM_ROW)
    a16 = pw[tc]
    a1 = jnp.concatenate([jnp.real(a16), jnp.real(a16)], axis=-1).reshape(2, SSM_G * SSM_STATE)
    a2 = jnp.concatenate([-jnp.imag(a16), jnp.imag(a16)], axis=-1).reshape(2, SSM_G * SSM_STATE)
    a_mul = jnp.stack([a1[0], a2[0], a1[1], a2[1]], axis=0)
    return w_loc.astype(BF16), p_st.astype(BF16), q_f.astype(BF16), q_b.astype(BF16), a_mul


def _ssm_state_kernel(u_ref, p_ref, s_ref):
    s = jnp.dot(u_ref[0].astype(BF16), p_ref[0], preferred_element_type=F32)
    s_ref[0] = s[:, :SSM_STATE]
    s_ref[1] = s[:, SSM_STATE:]


def _ssm_states(ug, p_st, tc):
    g, nc, _ = ug.shape
    return pl.pallas_call(
        _ssm_state_kernel,
        grid=(g, nc // tc),
        in_specs=[
            pl.BlockSpec((1, tc, SSM_ROW), lambda gi, i: (gi, i, 0)),
            pl.BlockSpec((1, SSM_ROW, 2 * SSM_STATE), lambda gi, i: (gi, 0, 0)),
        ],
        out_specs=pl.BlockSpec((2, tc, SSM_STATE), lambda gi, i: (0, i, gi)),
        out_shape=jax.ShapeDtypeStruct((2, nc, g * SSM_STATE), F32),
        compiler_params=_cparams(("arbitrary", "arbitrary")),
        name="ssm_chunk_states",
    )(ug, p_st)


def _swap_halves(x):
    parts = [pltpu.roll(x[:, g * LANES:(g + 1) * LANES], LANES // 2, 1) for g in range(x.shape[1] // LANES)]
    return jnp.concatenate(parts, axis=1)


def _ssm_scan_kernel(bt_ref, first_ref, sf_ref, sb_ref, am_ref, xf_ref, zb_ref, st_sc, sfw_sc, sbw_sc, *, tr):
    n = pl.program_id(0)
    del bt_ref

    @pl.when(first_ref[n] == 1)
    def _():
        st_sc[...] = jnp.zeros(st_sc.shape, F32)

    sfw_sc[...] = _swap_halves(sf_ref[0])
    sbw_sc[...] = _swap_halves(sb_ref[0])
    a1f, a2f, a1b, a2b = am_ref[0:1], am_ref[1:2], am_ref[2:3], am_ref[3:4]

    def body(r, carry):
        xf, xfw, zb, zbw = carry
        xf_ref[pl.ds(r, 1), :] = xf
        sf = sf_ref[0, pl.ds(r, 1), :]
        sfw = sfw_sc[pl.ds(r, 1), :]
        nxf = a1f * xf + a2f * xfw + sf
        nxfw = a1f * xfw - a2f * xf + sfw
        rb = tr - 1 - r
        zb_ref[pl.ds(rb, 1), :] = zb
        sb = sb_ref[0, pl.ds(rb, 1), :]
        sbw = sbw_sc[pl.ds(rb, 1), :]
        nzb = a1b * zb + a2b * zbw + sb
        nzbw = a1b * zbw - a2b * zb + sbw
        return nxf, nxfw, nzb, nzbw

    init = (st_sc[0:1], st_sc[1:2], st_sc[2:3], st_sc[3:4])
    xf, xfw, zb, zbw = lax.fori_loop(0, tr, body, init)
    st_sc[0:1] = xf
    st_sc[1:2] = xfw
    st_sc[2:3] = zb
    st_sc[3:4] = zbw


def _ssm_scan(s_st, a_mul, bwd_tile, first_tile, tr):
    _, nc, w = s_st.shape
    grid_spec = pltpu.PrefetchScalarGridSpec(
        num_scalar_prefetch=2,
        grid=(nc // tr,),
        in_specs=[
            pl.BlockSpec((1, tr, w), lambda n, bt, ft: (0, n, 0)),
            pl.BlockSpec((1, tr, w), lambda n, bt, ft: (1, bt[n], 0)),
            pl.BlockSpec((4, w), lambda n, bt, ft: (0, 0)),
        ],
        out_specs=[
            pl.BlockSpec((tr, w), lambda n, bt, ft: (n, 0)),
            pl.BlockSpec((tr, w), lambda n, bt, ft: (bt[n], 0)),
        ],
        scratch_shapes=[pltpu.VMEM((4, w), F32), pltpu.VMEM((tr, w), F32), pltpu.VMEM((tr, w), F32)],
    )
    return pl.pallas_call(
        functools.partial(_ssm_scan_kernel, tr=tr),
        grid_spec=grid_spec,
        out_shape=[jax.ShapeDtypeStruct((nc, w), F32), jax.ShapeDtypeStruct((nc, w), F32)],
        compiler_params=_cparams(("arbitrary",)),
        name="ssm_chunk_scan",
    )(bwd_tile, first_tile, s_st, s_st, a_mul)


def _ssm_out_kernel(u_ref, xf_ref, zb_ref, w_ref, qf_ref, qb_ref, y_ref):
    y = jnp.dot(u_ref[0].astype(BF16), w_ref[0], preferred_element_type=F32)
    y += jnp.dot(xf_ref[...].astype(BF16), qf_ref[0], preferred_element_type=F32)
    y += jnp.dot(zb_ref[...].astype(BF16), qb_ref[0], preferred_element_type=F32)
    y_ref[0] = y


def _ssm_outputs(ug, xf, zb, w_loc, q_f, q_b, tc):
    g, nc, _ = ug.shape
    return pl.pallas_call(
        _ssm_out_kernel,
        grid=(g, nc // tc),
        in_specs=[
            pl.BlockSpec((1, tc, SSM_ROW), lambda gi, i: (gi, i, 0)),
            pl.BlockSpec((tc, SSM_STATE), lambda gi, i: (i, gi)),
            pl.BlockSpec((tc, SSM_STATE), lambda gi, i: (i, gi)),
            pl.BlockSpec((1, SSM_ROW, SSM_ROW), lambda gi, i: (gi, 0, 0)),
            pl.BlockSpec((1, SSM_STATE, SSM_ROW), lambda gi, i: (gi, 0, 0)),
            pl.BlockSpec((1, SSM_STATE, SSM_ROW), lambda gi, i: (gi, 0, 0)),
        ],
        out_specs=pl.BlockSpec((1, tc, SSM_ROW), lambda gi, i: (gi, i, 0)),
        out_shape=jax.ShapeDtypeStruct((g, nc, SSM_ROW), F32),
        compiler_params=_cparams(("arbitrary", "arbitrary")),
        name="ssm_chunk_outputs",
    )(ug, xf, zb, w_loc, q_f, q_b)


def _bidir_ssm(u_ssm, tables, bwd_tile, first_tile, tr):
    w_loc, p_st, q_f, q_b, a_mul = tables
    t = u_ssm.shape[0]
    nc = t // SSM_CHUNK
    ug = u_ssm.reshape(nc, SSM_CHUNK, SSM_G, SSM_P).transpose(2, 0, 1, 3).reshape(SSM_G, nc, SSM_ROW)
    tc = _pick(nc, 512)
    s_st = _ssm_states(ug, p_st, tc)
    xf, zb = _ssm_scan(s_st, a_mul, bwd_tile, first_tile, tr)
    yg = _ssm_outputs(ug, xf, zb, w_loc, q_f, q_b, tc)
    return yg.reshape(SSM_G, nc, SSM_CHUNK, SSM_P).transpose(1, 2, 0, 3).reshape(t, SSM_DIM)


def _pool_mixer(ext_sc, u, pos, seq_len, tm):
    halves = []
    for half in range(2):
        sums = []
        for w in POOL_WINDOWS[2 * half:2 * half + 2]:
            lo = w // 2
            hi = w - lo - 1
            acc = ext_sc[pl.ds(POOL_HALO - lo, tm), half * LANES:(half + 1) * LANES]
            for d in range(-lo + 1, hi + 1):
                acc = acc + ext_sc[pl.ds(POOL_HALO + d, tm), half * LANES:(half + 1) * LANES]
            cnt = (jnp.minimum(pos + hi, seq_len - 1) - jnp.maximum(pos - lo, 0) + 1).astype(F32)
            sums.append(acc / cnt)
        lane = lax.broadcasted_iota(I32, (tm, LANES), 1)
        halves.append(jnp.where(lane < POOL_GROUP, sums[0], sums[1]))
    return jnp.concatenate(halves, axis=1) - u


def _route(h2, wrt_ref, brt_ref):
    tm = h2.shape[0]
    w_hi, w_lo = wrt_ref[0], wrt_ref[1]
    h_hi = h2.astype(BF16)
    h_lo = (h2 - h_hi.astype(F32)).astype(BF16)
    nt = (((1,), (1,)), ((), ()))
    logits = (lax.dot_general(w_hi, h_hi, nt, preferred_element_type=F32)
              + (lax.dot_general(w_hi, h_lo, nt, preferred_element_type=F32)
                 + lax.dot_general(w_lo, h_hi, nt, preferred_element_type=F32)))
    scores = jax.nn.sigmoid(logits)
    choice = scores + brt_ref[...]
    neg = jnp.float32(-jnp.inf)

    c3 = choice.reshape(N_EXPERT_GROUPS, GROUP_SIZE, tm)
    pos3 = lax.broadcasted_iota(I32, c3.shape, 1)
    m1 = jnp.max(c3, axis=1, keepdims=True)
    i1 = jnp.min(jnp.where(c3 == m1, pos3, GROUP_SIZE), axis=1, keepdims=True)
    m2 = jnp.max(jnp.where(pos3 == i1, neg, c3), axis=1, keepdims=True)
    gs = m1 + m2
    gid = lax.broadcasted_iota(I32, gs.shape, 0)
    ahead = jnp.zeros(gs.shape, I32)
    for o in range(N_EXPERT_GROUPS):
        go = gs[o:o + 1]
        ahead = ahead + ((go > gs) | ((go == gs) & (gid > o))).astype(I32)
    c = jnp.where(ahead < TOPK_GROUPS, c3, neg).reshape(N_EXPERTS, tm)

    row = lax.broadcasted_iota(I32, c.shape, 0)
    picks = []
    for _ in range(TOP_K):
        m = jnp.max(c, axis=0, keepdims=True)
        i = jnp.min(jnp.where(c == m, row, N_EXPERTS), axis=0, keepdims=True)
        oh = row == i
        w = jnp.sum(jnp.where(oh, scores, 0.0), axis=0, keepdims=True)
        c = jnp.where(oh, neg, c)
        picks.append((i, w, oh))
    wsum = picks[0][1]
    for _, w, _ in picks[1:]:
        wsum = wsum + w
    return [(i, w / wsum * ROUTED_SCALE, oh) for i, w, oh in picks]


def _post_kernel(ts_ref, tf_ref, tl_ref, tp_ref, tn_ref,
                 x_ref, at_ref, upp_ref, up_ref, upn_ref, ys_ref, gt_ref, mod_ref,
                 wao_ref, pbd_ref, psc_ref, wpo_ref, wgl_ref, wo_ref, g2_ref, wr_ref, br_ref, triu_ref,
                 x1_ref, h2_ref, idx_ref, gate_ref, rank_ref, cnt_ref, ext_sc, carry_sc, *, tm):
    del ts_ref
    i = pl.program_id(0)
    mod = mod_ref[0]

    attn = jnp.dot(at_ref[...], wao_ref[...], preferred_element_type=F32)

    u = up_ref[...]
    ext_sc[0:POOL_HALO] = jnp.where(tf_ref[i] == 1, 0.0, upp_ref[...])
    ext_sc[POOL_HALO:POOL_HALO + tm] = u
    ext_sc[POOL_HALO + tm:2 * POOL_HALO + tm] = jnp.where(tl_ref[i] == 1, 0.0, upn_ref[...])
    pos = tp_ref[i] + lax.broadcasted_iota(I32, (tm, LANES), 0)
    pm = _pool_mixer(ext_sc, u, pos, tn_ref[i], tm)
    pool = jnp.dot(pm.astype(BF16), pbd_ref[...], preferred_element_type=F32) * psc_ref[...]
    pool = jnp.dot(pool.astype(BF16), wpo_ref[...], preferred_element_type=F32)

    z = jnp.dot(jax.nn.gelu(ys_ref[...]).astype(BF16), wgl_ref[...], preferred_element_type=F32)
    ssm = z[:, :D_MODEL] * jax.nn.sigmoid(z[:, D_MODEL:])

    merged = (gt_ref[:, 0:D_MODEL].astype(F32) * attn
              + gt_ref[:, D_MODEL:2 * D_MODEL].astype(F32) * pool
              + gt_ref[:, 2 * D_MODEL:3 * D_MODEL].astype(F32) * ssm)
    mix = jnp.dot(merged.astype(BF16), wo_ref[...], preferred_element_type=F32)
    x1 = x_ref[...] + mod[2:3] * mix
    x1_ref[...] = x1
    h2 = _rmsnorm_mod(x1, g2_ref[...], mod[4:5], mod[3:4])
    _store_row_tiles(h2_ref, h2)

    picks = _route(h2, wr_ref, br_ref)

    @pl.when(i == 0)
    def _():
        carry_sc[...] = jnp.zeros(carry_sc.shape, F32)

    sel = picks[0][2]
    for _, _, oh in picks[1:]:
        sel = sel | oh
    sel_f = sel.astype(F32)
    ranks = jnp.dot(sel_f.astype(BF16), triu_ref[...], preferred_element_type=F32) + carry_sc[...]
    carry = carry_sc[...] + jnp.sum(sel_f, axis=1, keepdims=True)
    carry_sc[...] = carry
    cnt_ref[...] = carry

    idx_ref[...] = jnp.concatenate([ik for ik, _, _ in picks], axis=0)
    gate_ref[...] = jnp.concatenate([gk for _, gk, _ in picks], axis=0)
    rank_ref[...] = jnp.concatenate(
        [jnp.sum(jnp.where(oh, ranks, 0.0), axis=0, keepdims=True).astype(I32) for _, _, oh in picks], axis=0)


def _post(x, attn, u_pool, y_ssm, gates, mod_l, wts, meta, tm):
    t = x.shape[0]
    nh = t // POOL_HALO
    hb = tm // POOL_HALO
    npf = 5
    const = lambda i, *_: (0, 0)
    row = lambda i, *_: (i, 0)
    col = lambda i, *_: (0, i)
    grid_spec = pltpu.PrefetchScalarGridSpec(
        num_scalar_prefetch=npf,
        grid=(t // tm,),
        in_specs=[
            pl.BlockSpec((tm, D_MODEL), row),
            pl.BlockSpec((tm, ATTN_DIM), row),
            pl.BlockSpec((POOL_HALO, POOL_DIM), lambda i, *_: (jnp.maximum(i * hb - 1, 0), 0)),
            pl.BlockSpec((tm, POOL_DIM), row),
            pl.BlockSpec((POOL_HALO, POOL_DIM), lambda i, *_: (jnp.minimum((i + 1) * hb, nh - 1), 0)),
            pl.BlockSpec((tm, SSM_DIM), row),
            pl.BlockSpec((tm, 3 * D_MODEL), row),
            pl.BlockSpec((1, N_ADA, D_MODEL), lambda i, ts, *_: (ts[i], 0, 0)),
            pl.BlockSpec((ATTN_DIM, D_MODEL), const),
            pl.BlockSpec((POOL_DIM, POOL_DIM), const),
            pl.BlockSpec((1, POOL_DIM), const),
            pl.BlockSpec((POOL_DIM, D_MODEL), const),
            pl.BlockSpec((SSM_DIM, 2 * D_MODEL), const),
            pl.BlockSpec((D_MODEL, D_MODEL), const),
            pl.BlockSpec((1, D_MODEL), const),
            pl.BlockSpec((2, N_EXPERTS, D_MODEL), lambda i, *_: (0, 0, 0)),
            pl.BlockSpec((N_EXPERTS, 1), const),
            pl.BlockSpec((tm, tm), const),
        ],
        out_specs=[
            pl.BlockSpec((tm, D_MODEL), row),
            pl.BlockSpec((tm * ROW_TILE, LANES), row),
            pl.BlockSpec((TOP_K, tm), col),
            pl.BlockSpec((TOP_K, tm), col),
            pl.BlockSpec((TOP_K, tm), col),
            pl.BlockSpec((N_EXPERTS, 1), const),
        ],
        scratch_shapes=[pltpu.VMEM((tm + 2 * POOL_HALO, POOL_DIM), F32), pltpu.VMEM((N_EXPERTS, 1), F32)],
    )
    return pl.pallas_call(
        functools.partial(_post_kernel, tm=tm),
        grid_spec=grid_spec,
        out_shape=[
            jax.ShapeDtypeStruct((t, D_MODEL), F32),
            jax.ShapeDtypeStruct((t * ROW_TILE, LANES), U32),
            jax.ShapeDtypeStruct((TOP_K, t), I32),
            jax.ShapeDtypeStruct((TOP_K, t), F32),
            jax.ShapeDtypeStruct((TOP_K, t), I32),
            jax.ShapeDtypeStruct((N_EXPERTS, 1), F32),
        ],
        compiler_params=_cparams(("arbitrary",)),
        name="mixer_merge_route",
    )(*meta, x, attn, u_pool, u_pool, u_pool, y_ssm, gates, mod_l, *wts)


def _dispatch_kernel(lb_ref, has_ref, d_ref, h_hbm, x_hbm, zero_sc, h_sc, sem_in, sem, zsem, *, tn, bm):
    i = pl.program_id(0)
    n = pl.num_programs(0)
    n_copies = tn * TOP_K
    tile_rows = tn * ROW_TILE

    def slot_rows(ref, slot):
        return ref.at[pl.ds(slot * ROW_TILE, ROW_TILE), :]

    def tile_load(tile, s):
        return pltpu.make_async_copy(h_hbm.at[pl.ds(tile * tile_rows, tile_rows), :], h_sc.at[s], sem_in.at[s])

    def pad_loop(act):
        def expert(e, c):
            @pl.when(has_ref[e] == 1)
            def _():
                act(pltpu.make_async_copy(zero_sc, x_hbm.at[pl.ds(lb_ref[e] * ROW_TILE, bm * ROW_TILE), :], zsem))
            return c
        lax.fori_loop(0, N_EXPERTS, expert, 0)

    @pl.when(i == 0)
    def _():
        zero_sc[...] = jnp.zeros(zero_sc.shape, U32)
        pad_loop(lambda cp: cp.start())
        pad_loop(lambda cp: cp.wait())
        tile_load(0, 0).start()

    cur = i % 3

    @pl.when(i + 1 < n)
    def _():
        tile_load(i + 1, (i + 1) % 3).start()

    tile_load(i, cur).wait()

    def body(tok, c):
        src = slot_rows(h_sc.at[cur], tok)
        for k in range(TOP_K):
            pltpu.make_async_copy(src, slot_rows(x_hbm, d_ref[0, 0, tok * TOP_K + k]),
                                  sem.at[cur]).start(priority=k % DMA_PRIORITIES)
        return c
    lax.fori_loop(0, tn, body, 0, unroll=2)

    def wait_tile(s):
        whole = x_hbm.at[pl.ds(0, n_copies * ROW_TILE), :]
        pltpu.make_async_copy(whole, whole, sem.at[s]).wait()

    @pl.when(i > 0)
    def _():
        wait_tile((i + 2) % 3)

    @pl.when(i == n - 1)
    def _():
        wait_tile(cur)


def _dispatch(h2t, dest3, last_block, has_rows, n_rows, tn, bm):
    n = dest3.shape[0]
    grid_spec = pltpu.PrefetchScalarGridSpec(
        num_scalar_prefetch=2,
        grid=(n,),
        in_specs=[
            pl.BlockSpec((1, 1, tn * TOP_K), lambda i, lo, hi: (i, 0, 0), memory_space=pltpu.SMEM),
            pl.BlockSpec(memory_space=pl.ANY),
        ],
        out_specs=pl.BlockSpec(memory_space=pl.ANY),
        scratch_shapes=[pltpu.VMEM((bm * ROW_TILE, LANES), U32), pltpu.VMEM((3, tn * ROW_TILE, LANES), U32),
                        pltpu.SemaphoreType.DMA((3,)), pltpu.SemaphoreType.DMA((3,)),
                        pltpu.SemaphoreType.DMA(())],
    )
    return pl.pallas_call(
        functools.partial(_dispatch_kernel, tn=tn, bm=bm),
        grid_spec=grid_spec,
        out_shape=jax.ShapeDtypeStruct((n_rows * ROW_TILE, LANES), U32),
        compiler_params=_cparams(("arbitrary",)),
        name="moe_dispatch",
    )(last_block, has_rows, dest3, h2t)


def _expert_kernel(be_ref, nu_ref, x_ref, wg_ref, wu_ref, wd_ref, y_ref, wg_sc, wu_sc, wd_sc, *, bm):
    b = pl.program_id(0)
    n_used = nu_ref[0]

    @pl.when(b < n_used)
    def _():
        @pl.when((b == 0) | (be_ref[b] != be_ref[jnp.maximum(b - 1, 0)]))
        def _():
            wg_sc[...] = wg_ref[0, 0].astype(BF16)
            wu_sc[...] = wu_ref[0, 0].astype(BF16)
            wd_sc[...] = wd_ref[0, 0].astype(BF16)

        x = _load_row_tiles(x_ref, 0, bm).astype(BF16)
        g = jnp.dot(x, wg_sc[...], preferred_element_type=F32)
        u = jnp.dot(x, wu_sc[...], preferred_element_type=F32)
        hmid = (g * jax.nn.sigmoid(g) * u).astype(BF16)
        _store_row_tiles(y_ref, jnp.dot(hmid, wd_sc[...], preferred_element_type=F32))

    @pl.when(b >= n_used)
    def _():
        y_ref[...] = jnp.zeros(y_ref.shape, U32)


def _experts(x_buf, block_expert, n_used, w_gate, w_up, w_down, layer, bm):
    nb = x_buf.shape[0] // (bm * ROW_TILE)
    grid_spec = pltpu.PrefetchScalarGridSpec(
        num_scalar_prefetch=2,
        grid=(nb,),
        in_specs=[
            pl.BlockSpec((bm * ROW_TILE, LANES), lambda b, be, nu: (jnp.minimum(b, nu[0] - 1), 0)),
            pl.BlockSpec((1, 1, D_MODEL, D_EXPERT), lambda b, be, nu: (layer, be[b], 0, 0)),
            pl.BlockSpec((1, 1, D_MODEL, D_EXPERT), lambda b, be, nu: (layer, be[b], 0, 0)),
            pl.BlockSpec((1, 1, D_EXPERT, D_MODEL), lambda b, be, nu: (layer, be[b], 0, 0)),
        ],
        out_specs=pl.BlockSpec((bm * ROW_TILE, LANES), lambda b, be, nu: (b, 0)),
        scratch_shapes=[
            pltpu.VMEM((D_MODEL, D_EXPERT), BF16),
            pltpu.VMEM((D_MODEL, D_EXPERT), BF16),
            pltpu.VMEM((D_EXPERT, D_MODEL), BF16),
        ],
    )
    return pl.pallas_call(
        functools.partial(_expert_kernel, bm=bm),
        grid_spec=grid_spec,
        out_shape=jax.ShapeDtypeStruct((nb * bm * ROW_TILE, LANES), U32),
        compiler_params=_cparams(("arbitrary",)),
        name="routed_experts",
    )(block_expert, n_used, x_buf, w_gate, w_up, w_down)


def _combine_kernel(ts_ref, d_ref, dn_ref, y_hbm, x1_ref, h2_ref, gate_ref, mod_ref, wsg_ref, wsu_ref, wsd_ref,
                    o_ref, yg_sc, sem, *, tc):
    del ts_ref
    i = pl.program_id(0)
    n = pl.num_programs(0)
    slot = i % 2
    n_copies = tc * TOP_K

    def row_copy(src, s, row):
        return pltpu.make_async_copy(y_hbm.at[pl.ds(src * ROW_TILE, ROW_TILE), :],
                                     yg_sc.at[s, pl.ds(row * ROW_TILE, ROW_TILE), :], sem.at[s])

    def issue(dest_smem, s):
        def body(tok, c):
            for k in range(TOP_K):
                row_copy(dest_smem[0, 0, tok * TOP_K + k], s, k * tc + tok).start(priority=k % DMA_PRIORITIES)
            return c
        lax.fori_loop(0, tc, body, 0, unroll=2)

    @pl.when(i == 0)
    def _():
        issue(d_ref, 0)

    @pl.when(i + 1 < n)
    def _():
        issue(dn_ref, 1 - slot)

    pltpu.make_async_copy(y_hbm.at[pl.ds(0, n_copies * ROW_TILE), :], yg_sc.at[slot], sem.at[slot]).wait()

    gate = gate_ref[...]
    yg = yg_sc.at[slot]
    routed = gate[:, 0:1] * _load_row_tiles(yg, 0, tc)
    for k in range(1, TOP_K):
        routed = routed + gate[:, k:k + 1] * _load_row_tiles(yg, k * tc, tc)
    hb = _load_row_tiles(h2_ref, 0, tc).astype(BF16)
    g = jnp.dot(hb, wsg_ref[...], preferred_element_type=F32)
    u = jnp.dot(hb, wsu_ref[...], preferred_element_type=F32)
    shared = jnp.dot((g * jax.nn.sigmoid(g) * u).astype(BF16), wsd_ref[...], preferred_element_type=F32)
    o_ref[...] = x1_ref[...] + mod_ref[0][5:6] * (routed + shared)


def _combine(y_buf, dest3, x1, h2, gate, mod_l, ws_gate, ws_up, ws_down, tile_seq, tc):
    t = x1.shape[0]
    n = t // tc
    d_sh = ws_gate.shape[1]
    const = lambda i, ts: (0, 0)
    row = lambda i, ts: (i, 0)
    grid_spec = pltpu.PrefetchScalarGridSpec(
        num_scalar_prefetch=1,
        grid=(n,),
        in_specs=[
            pl.BlockSpec((1, 1, tc * TOP_K), lambda i, ts: (i, 0, 0), memory_space=pltpu.SMEM),
            pl.BlockSpec((1, 1, tc * TOP_K), lambda i, ts: (jnp.minimum(i + 1, n - 1), 0, 0),
                         memory_space=pltpu.SMEM),
            pl.BlockSpec(memory_space=pl.ANY),
            pl.BlockSpec((tc, D_MODEL), row),
            pl.BlockSpec((tc * ROW_TILE, LANES), row),
            pl.BlockSpec((tc, TOP_K), row),
            pl.BlockSpec((1, N_ADA, D_MODEL), lambda i, ts: (ts[i], 0, 0)),
            pl.BlockSpec((D_MODEL, d_sh), const),
            pl.BlockSpec((D_MODEL, d_sh), const),
            pl.BlockSpec((d_sh, D_MODEL), const),
        ],
        out_specs=pl.BlockSpec((tc, D_MODEL), row),
        scratch_shapes=[pltpu.VMEM((2, tc * TOP_K * ROW_TILE, LANES), U32), pltpu.SemaphoreType.DMA((2,))],
    )
    return pl.pallas_call(
        functools.partial(_combine_kernel, tc=tc),
        grid_spec=grid_spec,
        out_shape=jax.ShapeDtypeStruct((t, D_MODEL), F32),
        compiler_params=_cparams(("arbitrary",)),
        name="moe_combine",
    )(tile_seq, dest3, dest3, y_buf, x1, h2, gate, mod_l, ws_gate, ws_up, ws_down)


def _tile_meta(seq_lens, tile):
    seq, first, last, pos0, slen = [], [], [], [], []
    for s, n in enumerate(seq_lens):
        nt = n // tile
        for j in range(nt):
            seq.append(s)
            first.append(int(j == 0))
            last.append(int(j == nt - 1))
            pos0.append(j * tile)
            slen.append(n)
    return tuple(jnp.asarray(np.asarray(a, np.int32)) for a in (seq, first, last, pos0, slen))


def _scan_meta(seq_lens, tr):
    bwd, first = [], []
    base = 0
    for n in seq_lens:
        nt = n // SSM_CHUNK // tr
        for j in range(nt):
            bwd.append(base + nt - 1 - j)
            first.append(int(j == 0))
        base += nt
    return jnp.asarray(np.asarray(bwd, np.int32)), jnp.asarray(np.asarray(first, np.int32))


def _rope_tables(seq_lens):
    quarter = HEAD_DIM // 4
    freqs = ROPE_THETA ** (-jnp.arange(quarter, dtype=F32) / quarter)
    sign = jnp.tile(jnp.concatenate([-jnp.ones((quarter,), F32), jnp.ones((quarter,), F32)]), 2)
    cos_l, sin_l = [], []
    cache = {}
    for n in seq_lens:
        if n not in cache:
            pos = jnp.arange(n)
            ar = (pos // GRID_W).astype(F32)[:, None] * freqs
            ac = (pos % GRID_W).astype(F32)[:, None] * freqs
            ang = jnp.concatenate([ar, ar, ac, ac], axis=-1)
            cache[n] = (jnp.tile(jnp.cos(ang), (1, 2)), jnp.tile(jnp.sin(ang) * sign, (1, 2)))
        cos_l.append(cache[n][0])
        sin_l.append(cache[n][1])
    return jnp.concatenate(cos_l, axis=0), jnp.concatenate(sin_l, axis=0)


def _split_bf16(w):
    hi = w.astype(BF16)
    return jnp.stack([hi, (w - hi.astype(F32)).astype(BF16)], axis=0)


def _block_diag_ones(n, blk):
    r = np.arange(n) // blk
    return jnp.asarray((r[:, None] == r[None, :]).astype(np.float32)).astype(BF16)


def _dest_kernel(idx_ref, rank_ref, ps_ref, d_ref):
    idx = idx_ref[...]
    row_e = lax.broadcasted_iota(I32, (N_EXPERTS, idx.shape[1]), 0)
    starts = [jnp.sum(jnp.where(row_e == idx[k:k + 1], ps_ref[...], 0.0), axis=0, keepdims=True)
              for k in range(TOP_K)]
    d_ref[...] = rank_ref[...] + jnp.concatenate(starts, axis=0).astype(I32)


def _dest_slots(idx, rank, pstart):
    t = idx.shape[1]
    tm = _pick(t, 1024)
    col = lambda i: (0, i)
    return pl.pallas_call(
        _dest_kernel,
        grid=(t // tm,),
        in_specs=[pl.BlockSpec((TOP_K, tm), col), pl.BlockSpec((TOP_K, tm), col),
                  pl.BlockSpec((N_EXPERTS, 1), lambda i: (0, 0))],
        out_specs=pl.BlockSpec((TOP_K, tm), col),
        out_shape=jax.ShapeDtypeStruct((TOP_K, t), I32),
        compiler_params=_cparams(("arbitrary",)),
        name="moe_dest_slots",
    )(idx, rank, pstart)


def _moe_plan(idx, rank, counts, t, bm, nb):
    cnt = counts.reshape(N_EXPERTS).astype(I32)
    padded = (cnt + bm - 1) // bm * bm
    pend = jnp.cumsum(padded)
    pstart = pend - padded
    n_used = (pend[-1] // bm).astype(I32).reshape(1)
    starts = jnp.arange(nb, dtype=I32) * bm
    block_expert = jnp.minimum(
        jnp.sum((pend[None, :] <= starts[:, None]).astype(I32), axis=1), N_EXPERTS - 1).astype(I32)
    dest = _dest_slots(idx, rank, pstart.astype(F32).reshape(N_EXPERTS, 1)).T
    return dest, (pend - bm).astype(I32), (cnt > 0).astype(I32), block_expert, n_used


def kernel(x_prompt, x_sample, c_prompt, c_sample, w_ada, b_ada, norm1_g, w_in, q_norm_g, k_norm_g, w_attn_o, pool_w, pool_scale, w_pool_o, ssm_a_re, ssm_a_im, ssm_log_dt, ssm_b_re, ssm_b_im, ssm_c_re, ssm_c_im, ssm_d, w_glu, w_out, norm2_g, w_router, b_router, w_exp_gate, w_exp_up, w_exp_down, w_sh_gate, w_sh_up, w_sh_down):
    b1, l1, d = x_prompt.shape
    b2, l2, _ = x_sample.shape
    depth = w_in.shape[0]
    assert d == D_MODEL and b1 + b2 <= MOD_ROWS
    seq_lens = [l1] * b1 + [l2] * b2
    t = b1 * l1 + b2 * l2
    lmin = min(l1, l2)

    tm_in = _pick(lmin, 512)
    tm_post = _pick(lmin, 256)
    tc_comb = _pick(lmin, 256)
    tr_scan = _pick(lmin // SSM_CHUNK, 256)
    bm = 512
    nb = -(-(t * TOP_K + N_EXPERTS * (bm - 1)) // bm)

    x = jnp.concatenate([x_prompt.reshape(b1 * l1, d), x_sample.reshape(b2 * l2, d)], axis=0)
    c_all = jnp.concatenate([c_prompt, c_sample, jnp.zeros((MOD_ROWS - b1 - b2, d), F32)], axis=0)
    mod = _modulation(c_all, w_ada, b_ada).reshape(depth, MOD_ROWS, N_ADA, d)

    cos_t, sin_t = _rope_tables(seq_lens)
    ones_bd = _block_diag_ones(ATTN_DIM, HEAD_DIM)
    triu = jnp.asarray(np.triu(np.ones((tm_post, tm_post), np.float32), 1)).astype(BF16)
    meta_in = _tile_meta(seq_lens, tm_in)
    meta_post = _tile_meta(seq_lens, tm_post)
    meta_comb = _tile_meta(seq_lens, tc_comb)
    bwd_tile, first_tile = _scan_meta(seq_lens, tr_scan)

    for l in range(depth):
        q8, k2, v2, u_pool, u_ssm, gates = _inproj(
            x, mod[l], norm1_g[l].reshape(1, d), w_in[l].astype(BF16),
            jnp.tile(q_norm_g[l], N_Q_HEADS).reshape(1, ATTN_DIM),
            jnp.tile(k_norm_g[l], N_KV_HEADS).reshape(1, KV_DIM),
            cos_t, sin_t, ones_bd, meta_in[0], tm_in)
        attn = _flash(q8, k2, v2, b1 * l1, b2, l2, out=_flash(q8, k2, v2, 0, b1, l1))
        tables = _ssm_tables(ssm_a_re[l], ssm_a_im[l], ssm_log_dt[l], ssm_b_re[l], ssm_b_im[l],
                             ssm_c_re[l], ssm_c_im[l], ssm_d[l])
        y_ssm = _bidir_ssm(u_ssm, tables, bwd_tile, first_tile, tr_scan)

        pool_bd = jax.scipy.linalg.block_diag(*[pool_w[l, g] for g in range(len(POOL_WINDOWS))])
        wts = (w_attn_o[l].astype(BF16), pool_bd.astype(BF16), pool_scale[l].reshape(1, POOL_DIM),
               w_pool_o[l].astype(BF16), w_glu[l].astype(BF16), w_out[l].astype(BF16),
               norm2_g[l].reshape(1, d), _split_bf16(w_router[l].T), b_router[l].reshape(N_EXPERTS, 1), triu)
        x1, h2, idx, gate, rank, counts = _post(x, attn, u_pool, y_ssm, gates, mod[l], wts, meta_post, tm_post)

        dest, last_block, has_rows, block_expert, n_used = _moe_plan(idx, rank, counts, t, bm, nb)
        dest3 = dest.reshape(t // tc_comb, 1, tc_comb * TOP_K)
        x_buf = _dispatch(h2, dest3, last_block, has_rows, nb * bm, tc_comb, bm)
        y_buf = _experts(x_buf, block_expert, n_used, w_exp_gate, w_exp_up, w_exp_down, l, bm)
        gate = gate.T
        x = _combine(y_buf, dest3, x1, h2, gate, mod[l],
                     w_sh_gate[l].astype(BF16), w_sh_up[l].astype(BF16), w_sh_down[l].astype(BF16),
                     meta_comb[0], tc_comb)

    return (x[:b1 * l1].reshape(b1, l1, d), x[b1 * l1:].reshape(b2, l2, d))
```
